```python
import jax, jax.numpy as jnp
from jax import lax
import numpy as np

D_MODEL = 2048
BATCH = 2
SEQ = 4096
DEPTH = 1

MLA_HEADS = 8
MLA_Q_RANK = 512
MLA_KV_RANK = 256
MLA_NOPE = 128
MLA_ROPE = 64
MLA_V = 128
ROPE_THETA = 10000.0
Q_BLOCK = 128
GLA_HEADS = 4
GLA_DK = D_MODEL // 2 // GLA_HEADS
GLA_DV = D_MODEL // GLA_HEADS
GLA_GATE_RANK = 16
GLA_TAU = 16.0
GLA_CHUNK = 64
D_FF = 5632
CONV_WIDTH = 3
EPS = 1e-6

IN_SIZES = (MLA_Q_RANK, MLA_KV_RANK, MLA_ROPE,
            GLA_HEADS * GLA_DK, GLA_HEADS * GLA_DK, GLA_HEADS * GLA_DV,
            GLA_GATE_RANK, GLA_HEADS * GLA_DV,
            D_MODEL, D_MODEL)
IN_OFFSETS = tuple(int(v) for v in np.cumsum(IN_SIZES)[:-1])
D_IN = int(sum(IN_SIZES))

kernel_name = 'hybrid_mla_gla_convffn_block'


def rmsnorm(x, w):
    x32 = x.astype(jnp.float32)
    y = x32 * lax.rsqrt(jnp.mean(x32 * x32, axis=-1, keepdims=True) + EPS)
    return (y * w.astype(jnp.float32)).astype(x.dtype)


def rope_tables(positions):
    inv = 1.0 / (ROPE_THETA ** (jnp.arange(0, MLA_ROPE, 2, dtype=jnp.float32) / MLA_ROPE))
    ang = positions.astype(jnp.float32)[..., None] * inv
    return jnp.cos(ang), jnp.sin(ang)


def apply_rope(x, cos, sin):
    x32 = x.astype(jnp.float32)
    x1, x2 = jnp.split(x32, 2, axis=-1)
    out = jnp.concatenate([x1 * cos - x2 * sin, x2 * cos + x1 * sin], axis=-1)
    return out.astype(x.dtype)


def mla_branch(q_lat, kv_lat, k_rope, positions, g_q, w_q_up, g_kv, w_kv_up):
    B, S, _ = q_lat.shape
    q = (rmsnorm(q_lat, g_q) @ w_q_up).reshape(B, S, MLA_HEADS, MLA_NOPE + MLA_ROPE)
    kv = (rmsnorm(kv_lat, g_kv) @ w_kv_up).reshape(B, S, MLA_HEADS, MLA_NOPE + MLA_V)
    q_nope, q_pe = q[..., :MLA_NOPE], q[..., MLA_NOPE:]
    k_nope, v = kv[..., :MLA_NOPE], kv[..., MLA_NOPE:]
    cos, sin = rope_tables(positions)
    q_pe = apply_rope(q_pe, cos[:, :, None, :], sin[:, :, None, :])
    k_pe = apply_rope(k_rope, cos, sin)
    scale = (MLA_NOPE + MLA_ROPE) ** -0.5
    nb = S // Q_BLOCK
    key_idx = jnp.arange(S)

    def blocks(t):
        return jnp.moveaxis(t.reshape((B, nb, Q_BLOCK) + t.shape[2:]), 1, 0)

    def attend(args):
        qn, qp, start = args
        s = (jnp.einsum('bqhd,bkhd->bhqk', qn, k_nope).astype(jnp.float32)
             + jnp.einsum('bqhr,bkr->bhqk', qp, k_pe).astype(jnp.float32)) * scale
        q_idx = start + jnp.arange(Q_BLOCK)
        mask = key_idx[None, :] <= q_idx[:, None]
        s = jnp.where(mask, s, -jnp.inf)
        p = jax.nn.softmax(s, axis=-1).astype(v.dtype)
        return jnp.einsum('bhqk,bkhd->bqhd', p, v)

    starts = jnp.arange(nb) * Q_BLOCK
    o = lax.map(attend, (blocks(q_nope), blocks(q_pe), starts))
    return jnp.moveaxis(o, 0, 1).reshape(B, S, MLA_HEADS * MLA_V)


def gla_branch(q, k, v, gate_lr, out_gate, w_gate_up, b_gate, g_out):
    B, S, _ = q.shape
    dtype = q.dtype
    nc = S // GLA_CHUNK
    log_a = jax.nn.log_sigmoid((gate_lr @ w_gate_up + b_gate).astype(jnp.float32)) / GLA_TAU

    def heads(t, d):
        return t.reshape(B, nc, GLA_CHUNK, GLA_HEADS, d).transpose(1, 0, 3, 2, 4).astype(jnp.float32)

    qc = heads(q * (GLA_DK ** -0.5), GLA_DK)
    kc = heads(k, GLA_DK)
    vc = heads(v, GLA_DV)
    gc = heads(log_a, GLA_DK)
    causal = jnp.tril(jnp.ones((GLA_CHUNK, GLA_CHUNK), dtype=bool))[:, :, None]

    def step(state, inp):
        qi, ki, vi, gi = inp
        b = jnp.cumsum(gi, axis=-2)
        inter = jnp.einsum('bhcd,bhde->bhce', qi * jnp.exp(b), state)
        diff = b[:, :, :, None, :] - b[:, :, None, :, :]
        decay = jnp.where(causal, jnp.exp(jnp.where(causal, diff, 0.0)), 0.0)
        attn = jnp.einsum('bhid,bhjd,bhijd->bhij', qi, ki, decay)
        intra = jnp.einsum('bhij,bhje->bhie', attn, vi)
        b_last = b[:, :, -1:, :]
        state = (jnp.exp(b_last[:, :, 0, :])[..., None] * state
                 + jnp.einsum('bhcd,bhce->bhde', ki * jnp.exp(b_last - b), vi))
        return state, inter + intra

    s0 = jnp.zeros((B, GLA_HEADS, GLA_DK, GLA_DV), jnp.float32)
    _, o = lax.scan(step, s0, (qc, kc, vc, gc))
    o = o.transpose(1, 0, 3, 2, 4).reshape(B, S, GLA_HEADS, GLA_DV)
    o = rmsnorm(o, g_out).reshape(B, S, GLA_HEADS * GLA_DV).astype(dtype)
    return o * jax.nn.silu(out_gate)


def conv_ffn(h, w_up, conv_w, conv_b, w_down):
    S = h.shape[1]
    u = h @ w_up
    up = jnp.pad(u, ((0, 0), (CONV_WIDTH - 1, 0), (0, 0)))
    acc = up[:, 0:S] * conv_w[0]
    for i in range(1, CONV_WIDTH):
        acc = acc + up[:, i:i + S] * conv_w[i]
    acc = acc + conv_b
    a, val = jnp.split(acc, 2, axis=-1)
    return (jax.nn.gelu(a, approximate=True) * val) @ w_down


def setup_inputs(seed: int = 0) -> dict:
    key = jax.random.key(seed)
    ks = jax.random.split(key, 24)

    def dense(k, shape):
        return jax.random.normal(k, (DEPTH,) + shape, jnp.float32) * (shape[0] ** -0.5)

    def gain(k, n):
        return 1.0 + 0.02 * jax.random.normal(k, (DEPTH, n), jnp.float32)

    def bias(k, n):
        return 0.01 * jax.random.normal(k, (DEPTH, n), jnp.float32)

    return {
        'x': jax.random.normal(ks[0], (BATCH, SEQ, D_MODEL), jnp.float32),
        'c': jax.random.normal(ks[1], (BATCH, D_MODEL), jnp.float32),
        'positions': jnp.broadcast_to(jnp.arange(SEQ, dtype=jnp.int32), (BATCH, SEQ)),
        'w_ada': dense(ks[2], (D_MODEL, 6 * D_MODEL)),
        'b_ada': bias(ks[3], 6 * D_MODEL),
        'g_pre_mix': gain(ks[4], D_MODEL),
        'w_in': dense(ks[5], (D_MODEL, D_IN)),
        'g_q_lat': gain(ks[6], MLA_Q_RANK),
        'w_q_up': dense(ks[7], (MLA_Q_RANK, MLA_HEADS * (MLA_NOPE + MLA_ROPE))),
        'g_kv_lat': gain(ks[8], MLA_KV_RANK),
        'w_kv_up': dense(ks[9], (MLA_KV_RANK, MLA_HEADS * (MLA_NOPE + MLA_V))),
        'w_gla_gate_up': dense(ks[10], (GLA_GATE_RANK, GLA_HEADS * GLA_DK)),
        'b_gla_gate': bias(ks[11], GLA_HEADS * GLA_DK),
        'g_gla_out': gain(ks[12], GLA_DV),
        'w_branch_a': dense(ks[13], (MLA_HEADS * MLA_V, D_MODEL)),
        'w_branch_b': dense(ks[14], (GLA_HEADS * GLA_DV, D_MODEL)),
        'w_mix_out': dense(ks[15], (D_MODEL, D_MODEL)),
        'g_post_mix': gain(ks[16], D_MODEL),
        'g_pre_ffn': gain(ks[17], D_MODEL),
        'w_ffn_up': dense(ks[18], (D_MODEL, 2 * D_FF)),
        'conv_w': jax.random.normal(ks[19], (DEPTH, CONV_WIDTH, 2 * D_FF), jnp.float32) * (CONV_WIDTH ** -0.5),
        'conv_b': bias(ks[20], 2 * D_FF),
        'w_ffn_down': dense(ks[21], (D_FF, D_MODEL)),
        'g_post_ffn': gain(ks[22], D_MODEL),
    }


def reference(x, c, positions, w_ada, b_ada, g_pre_mix, w_in, g_q_lat, w_q_up, g_kv_lat,
              w_kv_up, w_gla_gate_up, b_gla_gate, g_gla_out, w_branch_a, w_branch_b,
              w_mix_out, g_post_mix, g_pre_ffn, w_ffn_up, conv_w, conv_b, w_ffn_down,
              g_post_ffn):
    c_act = jax.nn.silu(c)
    for l in range(DEPTH):
        mod = (c_act @ w_ada[l] + b_ada[l])[:, None, :]
        shift_m, scale_m, gate_m, shift_f, scale_f, gate_f = jnp.split(mod, 6, axis=-1)

        h = rmsnorm(x, g_pre_mix[l]) * (1 + scale_m) + shift_m
        z = h @ w_in[l]
        (q_lat, kv_lat, k_rope, gq, gk, gv, g_lr, g_og, gate_a, gate_b) = jnp.split(z, IN_OFFSETS, axis=-1)
        y_a = mla_branch(q_lat, kv_lat, k_rope, positions, g_q_lat[l], w_q_up[l],
                         g_kv_lat[l], w_kv_up[l]) @ w_branch_a[l]
        y_b = gla_branch(gq, gk, gv, g_lr, g_og, w_gla_gate_up[l], b_gla_gate[l],
                         g_gla_out[l]) @ w_branch_b[l]
        merged = jax.nn.sigmoid(gate_a) * y_a + jax.nn.sigmoid(gate_b) * y_b
        y = merged @ w_mix_out[l]
        x = x + gate_m * rmsnorm(y, g_post_mix[l])

        h = rmsnorm(x, g_pre_ffn[l]) * (1 + scale_f) + shift_f
        f = conv_ffn(h, w_ffn_up[l], conv_w[l], conv_b[l], w_ffn_down[l])
        x = x + gate_f * rmsnorm(f, g_post_ffn[l])
    return x
```

```python
import functools

import jax
import jax.numpy as jnp
import numpy as np
from jax import lax
from jax.experimental import pallas as pl
from jax.experimental.pallas import tpu as pltpu

F32 = jnp.float32
BF16 = jnp.bfloat16

D_MODEL = 2048
MLA_HEADS = 8
MLA_Q_RANK = 512
MLA_KV_RANK = 256
MLA_NOPE = 128
MLA_ROPE = 64
MLA_V = 128
ROPE_THETA = 10000.0
GLA_HEADS = 4
GLA_DK = 256
GLA_DV = 512
GLA_GATE_RANK = 16
GLA_TAU = 16.0
D_FF = 5632
CONV_WIDTH = 3
EPS = 1e-6

LANES = 128
SUBLANES = 8
MLA_QK_PAD = 256

Z_QLAT = 0
Z_KVLAT = 512
Z_KROPE = 768
Z_GLR = 896
Z_GQ = 1024
Z_GK = 2048
Z_GV = 3072
Z_OG = 5120
Z_GATE_A = 7168
Z_GATE_B = 9216
Z_WIDTH = 11264

GLA_CHUNK = 128
GLA_LEVELS = 7

MIB = 1024 * 1024


def _params(semantics, vmem_mib):
    return pltpu.CompilerParams(dimension_semantics=semantics,
                                vmem_limit_bytes=vmem_mib * MIB)


def _rms(x, g):
    return x * lax.rsqrt(jnp.mean(x * x, axis=-1, keepdims=True) + EPS) * g


def _ada_kernel(c_ref, w_ref, b_ref, o_ref):
    c = c_ref[...]
    ca = c * jax.nn.sigmoid(c)
    o_ref[...] = jnp.dot(ca.astype(BF16), w_ref[...].astype(BF16),
                         preferred_element_type=F32) + b_ref[...]


def _ada(c_pad, w, b):
    n = w.shape[1]
    tn = 1024
    return pl.pallas_call(
        _ada_kernel,
        grid=(n // tn,),
        in_specs=[pl.BlockSpec((SUBLANES, D_MODEL), lambda j: (0, 0)),
                  pl.BlockSpec((D_MODEL, tn), lambda j: (0, j)),
                  pl.BlockSpec((1, tn), lambda j: (0, j))],
        out_specs=pl.BlockSpec((SUBLANES, tn), lambda j: (0, j)),
        out_shape=jax.ShapeDtypeStruct((SUBLANES, n), F32),
        compiler_params=_params(("arbitrary",), 32),
        name="ada",
    )(c_pad, w, b)


def _norm_mod_kernel(x_ref, g_ref, sc_ref, sh_ref, o_ref):
    y = _rms(x_ref[...], g_ref[...])
    o_ref[...] = (y * (1.0 + sc_ref[0]) + sh_ref[0]).astype(o_ref.dtype)


def _norm_mod(x2, g, scale, shift, seq):
    t = x2.shape[0]
    tm = 512
    tpb = seq // tm
    return pl.pallas_call(
        _norm_mod_kernel,
        grid=(t // tm,),
        in_specs=[pl.BlockSpec((tm, D_MODEL), lambda i: (i, 0)),
                  pl.BlockSpec((1, D_MODEL), lambda i: (0, 0)),
                  pl.BlockSpec((1, 1, D_MODEL), lambda i: (i // tpb, 0, 0)),
                  pl.BlockSpec((1, 1, D_MODEL), lambda i: (i // tpb, 0, 0))],
        out_specs=pl.BlockSpec((tm, D_MODEL), lambda i: (i, 0)),
        out_shape=jax.ShapeDtypeStruct((t, D_MODEL), BF16),
        compiler_params=_params(("arbitrary",), 32),
        name="norm_mod",
    )(x2, g, scale, shift)


def _mm_kernel(a_ref, w_ref, o_ref):
    o_ref[...] = jnp.dot(a_ref[...], w_ref[...],
                         preferred_element_type=F32).astype(o_ref.dtype)


def _mm(a, w, out_dtype, tm, tn, name):
    m, k = a.shape
    n = w.shape[1]
    return pl.pallas_call(
        _mm_kernel,
        grid=(n // tn, m // tm),
        in_specs=[pl.BlockSpec((tm, k), lambda j, i: (i, 0)),
                  pl.BlockSpec((k, tn), lambda j, i: (0, j))],
        out_specs=pl.BlockSpec((tm, tn), lambda j, i: (i, j)),
        out_shape=jax.ShapeDtypeStruct((m, n), out_dtype),
        compiler_params=_params(("arbitrary", "arbitrary"), 48),
        name=name,
    )(a, w)


def _mla_proj_kernel(ql_ref, kvl_ref, kr_ref, pos_ref, inv_ref, sgn_ref, cm_ref,
                     gq_ref, gkv_ref, wq_ref, wkv_ref, q_ref, k_ref, v_ref):
    ang = pos_ref[...].astype(F32) * inv_ref[...]
    cosm = jnp.cos(ang) * cm_ref[...]
    sinm = jnp.sin(ang) * sgn_ref[...]

    def rope(r):
        return r * cosm + pltpu.roll(r, MLA_ROPE, 1) * sinm

    scale = (MLA_NOPE + MLA_ROPE) ** -0.5
    qn = _rms(ql_ref[...].astype(F32), gq_ref[...]).astype(BF16)
    for h in range(MLA_HEADS):
        r = jnp.dot(qn, wq_ref[h], preferred_element_type=F32) * scale
        q_ref[0, h, :, 0:MLA_NOPE] = r[:, 0:MLA_NOPE].astype(BF16)
        q_ref[0, h, :, MLA_NOPE:MLA_QK_PAD] = rope(r[:, MLA_NOPE:MLA_QK_PAD]).astype(BF16)

    kvn = _rms(kvl_ref[...].astype(F32), gkv_ref[...]).astype(BF16)
    kv = jnp.dot(kvn, wkv_ref[...], preferred_element_type=F32).astype(BF16)
    kpe = rope(kr_ref[...].astype(F32)).astype(BF16)
    hw = MLA_NOPE + MLA_V
    for h in range(MLA_HEADS):
        k_ref[0, h, :, 0:MLA_NOPE] = kv[:, h * hw:h * hw + MLA_NOPE]
        k_ref[0, h, :, MLA_NOPE:MLA_QK_PAD] = kpe
        v_ref[0, h] = kv[:, h * hw + MLA_NOPE:(h + 1) * hw]


def _mla_proj(z, pos, inv, sgn, cm, gq, gkv, wq, wkv, batch, seq):
    tm = 512
    tpb = seq // tm
    qk_shape = jax.ShapeDtypeStruct((batch, MLA_HEADS, seq, MLA_QK_PAD), BF16)
    v_shape = jax.ShapeDtypeStruct((batch, MLA_HEADS, seq, MLA_V), BF16)
    const2 = lambda i: (0, 0)
    qk_spec = pl.BlockSpec((1, MLA_HEADS, tm, MLA_QK_PAD), lambda i: (i // tpb, 0, i % tpb, 0))
    return pl.pallas_call(
        _mla_proj_kernel,
        grid=(batch * tpb,),
        in_specs=[pl.BlockSpec((tm, MLA_Q_RANK), lambda i: (i, Z_QLAT // MLA_Q_RANK)),
                  pl.BlockSpec((tm, MLA_KV_RANK), lambda i: (i, Z_KVLAT // MLA_KV_RANK)),
                  pl.BlockSpec((tm, LANES), lambda i: (i, Z_KROPE // LANES)),
                  pl.BlockSpec((tm, 1), lambda i: (i, 0)),
                  pl.BlockSpec((1, LANES), const2),
                  pl.BlockSpec((1, LANES), const2),
                  pl.BlockSpec((1, LANES), const2),
                  pl.BlockSpec((1, MLA_Q_RANK), const2),
                  pl.BlockSpec((1, MLA_KV_RANK), const2),
                  pl.BlockSpec((MLA_HEADS, MLA_Q_RANK, MLA_QK_PAD), lambda i: (0, 0, 0)),
                  pl.BlockSpec((MLA_KV_RANK, MLA_HEADS * (MLA_NOPE + MLA_V)), const2)],
        out_specs=[qk_spec, qk_spec,
                   pl.BlockSpec((1, MLA_HEADS, tm, MLA_V), lambda i: (i // tpb, 0, i % tpb, 0))],
        out_shape=[qk_shape, qk_shape, v_shape],
        compiler_params=_params(("arbitrary",), 48),
        name="mla_proj",
    )(z, z, z, pos, inv, sgn, cm, gq, gkv, wq, wkv)


def _attn_kernel(q_ref, k_ref, v_ref, o_ref, m_sc, l_sc, acc_sc, *, tq):
    qi = pl.program_id(2)
    q = q_ref[0, 0]
    m_sc[...] = jnp.full(m_sc.shape, -jnp.inf, F32)
    l_sc[...] = jnp.zeros(l_sc.shape, F32)
    acc_sc[...] = jnp.zeros(acc_sc.shape, F32)

    def step(kb, masked):
        ks = pl.multiple_of(kb * tq, tq)
        k = k_ref[0, 0, pl.ds(ks, tq), :]
        v = v_ref[0, 0, pl.ds(ks, tq), :]
        s = lax.dot_general(q, k, (((1,), (1,)), ((), ())), preferred_element_type=F32)
        if masked:
            row = lax.broadcasted_iota(jnp.int32, s.shape, 0)
            col = lax.broadcasted_iota(jnp.int32, s.shape, 1)
            s = jnp.where(col <= row, s, -jnp.inf)
        m_prev = m_sc[...]
        m_new = jnp.maximum(m_prev, jnp.max(s, axis=-1, keepdims=True))
        alpha = jnp.exp(m_prev - m_new)
        p = jnp.exp(s - m_new)
        l_sc[...] = alpha * l_sc[...] + jnp.sum(p, axis=-1, keepdims=True)
        acc_sc[...] = alpha * acc_sc[...] + jnp.dot(p.astype(BF16), v,
                                                    preferred_element_type=F32)
        m_sc[...] = m_new

    def body(kb, carry):
        step(kb, False)
        return carry

    lax.fori_loop(0, qi, body, 0)
    step(qi, True)
    o_ref[...] = (acc_sc[...] / l_sc[...]).astype(o_ref.dtype)


def _attn(q, k, v, batch, seq):
    tq = 512
    nq = seq // tq
    return pl.pallas_call(
        functools.partial(_attn_kernel, tq=tq),
        grid=(batch, MLA_HEADS, nq),
        in_specs=[pl.BlockSpec((1, 1, tq, MLA_QK_PAD), lambda b, h, i: (b, h, i, 0)),
                  pl.BlockSpec((1, 1, seq, MLA_QK_PAD), lambda b, h, i: (b, h, 0, 0)),
                  pl.BlockSpec((1, 1, seq, MLA_V), lambda b, h, i: (b, h, 0, 0))],
        out_specs=pl.BlockSpec((tq, MLA_V), lambda b, h, i: (b * nq + i, h)),
        out_shape=jax.ShapeDtypeStruct((batch * seq, MLA_HEADS * MLA_V), BF16),
        scratch_shapes=[pltpu.VMEM((tq, 1), F32), pltpu.VMEM((tq, 1), F32),
                        pltpu.VMEM((tq, MLA_V), F32)],
        compiler_params=_params(("arbitrary", "arbitrary", "arbitrary"), 32),
        name="attn",
    )(q, k, v)


def _gla_constants():
    c = GLA_CHUNK
    idx = np.arange(c)
    w = np.zeros((GLA_LEVELS + 2, c, c), np.float32)
    masks = np.zeros((GLA_LEVELS + 1, c, c), np.float32)
    for l in range(GLA_LEVELS):
        half = 1 << l
        blk = idx // (2 * half)
        mid = blk * 2 * half + half - 1
        upper = (idx % (2 * half)) >= half
        t = idx[None, :]
        up_rows = (t > mid[:, None]) & (t <= idx[:, None])
        lo_rows = (t > idx[:, None]) & (t <= mid[:, None])
        w[l] = np.where(upper[:, None], up_rows, lo_rows)
        masks[l] = (blk[:, None] == blk[None, :]) & upper[:, None] & (~upper[None, :])
    w[GLA_LEVELS] = idx[None, :] <= idx[:, None]
    w[GLA_LEVELS + 1] = idx[None, :] > idx[:, None]
    masks[GLA_LEVELS] = np.eye(c)
    return w.reshape(-1, c), masks


def _gla_kernel(q_ref, k_ref, v_ref, lr_ref, og_ref, wg_ref, bg_ref, go_ref, ws_ref, mask_ref,
                o_ref, st_ref):
    c = GLA_CHUNK
    nt = (((1,), (1,)), ((), ()))

    @pl.when(pl.program_id(2) == 0)
    def _():
        st_ref[...] = jnp.zeros(st_ref.shape, F32)

    x = jnp.dot(lr_ref[...].astype(F32), wg_ref[...], preferred_element_type=F32,
                precision=lax.Precision.HIGHEST) + bg_ref[...]
    log_a = -(jnp.maximum(-x, 0.0) + jnp.log1p(jnp.exp(-jnp.abs(x)))) * (1.0 / GLA_TAU)
    expo = jnp.dot(ws_ref[...], log_a, preferred_element_type=F32,
                   precision=lax.Precision.HIGHEST)
    p = jnp.exp(expo)

    q = q_ref[...].astype(F32) * (GLA_DK ** -0.5)
    k = k_ref[...].astype(F32)
    v = v_ref[...]
    attn = mask_ref[GLA_LEVELS] * lax.dot_general(q.astype(BF16), k.astype(BF16), nt,
                                                  preferred_element_type=F32)
    for l in range(GLA_LEVELS):
        pl_ = p[l * c:(l + 1) * c]
        s = lax.dot_general((q * pl_).astype(BF16), (k * pl_).astype(BF16), nt,
                            preferred_element_type=F32)
        attn = attn + mask_ref[l] * s
    eb = p[GLA_LEVELS * c:(GLA_LEVELS + 1) * c]
    ebl = p[(GLA_LEVELS + 1) * c:(GLA_LEVELS + 2) * c]
    st = st_ref[...]
    o = (lax.dot_general((q * eb).astype(BF16), st.astype(BF16), nt,
                         preferred_element_type=F32)
         + jnp.dot(attn.astype(BF16), v, preferred_element_type=F32))
    v_t = v.astype(F32).T.astype(BF16)
    st_ref[...] = st * eb[c - 1:c, :] + jnp.dot(v_t, (k * ebl).astype(BF16),
                                                preferred_element_type=F32)
    y = _rms(o, go_ref[...])
    og = og_ref[...].astype(F32)
    o_ref[...] = (y * (og * jax.nn.sigmoid(og))).astype(o_ref.dtype)


def _gla(z, wg, bg, go, batch, seq):
    c = GLA_CHUNK
    nc = seq // c
    ws, masks = _gla_constants()
    row = lambda b, h, i: b * nc + i
    return pl.pallas_call(
        _gla_kernel,
        grid=(batch, GLA_HEADS, nc),
        in_specs=[pl.BlockSpec((c, GLA_DK), lambda b, h, i: (row(b, h, i), Z_GQ // GLA_DK + h)),
                  pl.BlockSpec((c, GLA_DK), lambda b, h, i: (row(b, h, i), Z_GK // GLA_DK + h)),
                  pl.BlockSpec((c, GLA_DV), lambda b, h, i: (row(b, h, i), Z_GV // GLA_DV + h)),
                  pl.BlockSpec((c, LANES), lambda b, h, i: (row(b, h, i), Z_GLR // LANES)),
                  pl.BlockSpec((c, GLA_DV), lambda b, h, i: (row(b, h, i), Z_OG // GLA_DV + h)),
                  pl.BlockSpec((LANES, GLA_DK), lambda b, h, i: (0, h)),
                  pl.BlockSpec((1, GLA_DK), lambda b, h, i: (0, h)),
                  pl.BlockSpec((1, GLA_DV), lambda b, h, i: (0, 0)),
                  pl.BlockSpec(ws.shape, lambda b, h, i: (0, 0)),
                  pl.BlockSpec(masks.shape, lambda b, h, i: (0, 0, 0))],
        out_specs=pl.BlockSpec((c, GLA_DV), lambda b, h, i: (row(b, h, i), h)),
        out_shape=jax.ShapeDtypeStruct((batch * seq, GLA_HEADS * GLA_DV), BF16),
        scratch_shapes=[pltpu.VMEM((GLA_DV, GLA_DK), F32)],
        compiler_params=_params(("arbitrary", "arbitrary", "arbitrary"), 32),
        name="gla",
    )(z, z, z, z, z, wg, bg, go, jnp.asarray(ws), jnp.asarray(masks))


def _mix1_kernel(oa_ref, ob_ref, wa_ref, wb_ref, ga_ref, gb_ref, o_ref):
    ya = jnp.dot(oa_ref[...], wa_ref[...], preferred_element_type=F32)
    yb = jnp.dot(ob_ref[...], wb_ref[...], preferred_element_type=F32)
    o_ref[...] = (jax.nn.sigmoid(ga_ref[...].astype(F32)) * ya
                  + jax.nn.sigmoid(gb_ref[...].astype(F32)) * yb).astype(o_ref.dtype)


def _mix1(oa, ob, wa, wb, z):
    t = oa.shape[0]
    tm, tn = 1024, 512
    return pl.pallas_call(
        _mix1_kernel,
        grid=(D_MODEL // tn, t // tm),
        in_specs=[pl.BlockSpec((tm, oa.shape[1]), lambda j, i: (i, 0)),
                  pl.BlockSpec((tm, ob.shape[1]), lambda j, i: (i, 0)),
                  pl.BlockSpec((wa.shape[0], tn), lambda j, i: (0, j)),
                  pl.BlockSpec((wb.shape[0], tn), lambda j, i: (0, j)),
                  pl.BlockSpec((tm, tn), lambda j, i: (i, Z_GATE_A // tn + j)),
                  pl.BlockSpec((tm, tn), lambda j, i: (i, Z_GATE_B // tn + j))],
        out_specs=pl.BlockSpec((tm, tn), lambda j, i: (i, j)),
        out_shape=jax.ShapeDtypeStruct((t, D_MODEL), BF16),
        compiler_params=_params(("arbitrary", "arbitrary"), 48),
        name="mix1",
    )(oa, ob, wa, wb, z, z)


def _mix2_kernel(m_ref, w_ref, x_ref, gm_ref, gpost_ref, gpre_ref, sc_ref, sh_ref,
                 x1_ref, h2_ref):
    y = jnp.dot(m_ref[...], w_ref[...], preferred_element_type=F32)
    x1 = x_ref[...] + gm_ref[0] * _rms(y, gpost_ref[...])
    x1_ref[...] = x1
    h2_ref[...] = (_rms(x1, gpre_ref[...]) * (1.0 + sc_ref[0]) + sh_ref[0]).astype(h2_ref.dtype)


def _mix2(merged, w, x2, gate_m, g_post, g_pre, scale_f, shift_f, seq):
    t = merged.shape[0]
    tm = 256
    tpb = seq // tm
    row = pl.BlockSpec((tm, D_MODEL), lambda i: (i, 0))
    vec = pl.BlockSpec((1, D_MODEL), lambda i: (0, 0))
    per_batch = pl.BlockSpec((1, 1, D_MODEL), lambda i: (i // tpb, 0, 0))
    return pl.pallas_call(
        _mix2_kernel,
        grid=(t // tm,),
        in_specs=[row, pl.BlockSpec((D_MODEL, D_MODEL), lambda i: (0, 0)), row,
                  per_batch, vec, vec, per_batch, per_batch],
        out_specs=[row, row],
        out_shape=[jax.ShapeDtypeStruct((t, D_MODEL), F32),
                   jax.ShapeDtypeStruct((t, D_MODEL), BF16)],
        compiler_params=_params(("arbitrary",), 48),
        name="mix2",
    )(merged, w, x2, gate_m, g_post, g_pre, scale_f, shift_f)


def _ffn_up_kernel(h_ref, wa_ref, wv_ref, cwa_ref, cwv_ref, cba_ref, cbv_ref, o_ref,
                   halo_a, halo_v, *, tiles_per_batch):
    i = pl.program_id(1)
    tm = h_ref.shape[0]

    @pl.when(i % tiles_per_batch == 0)
    def _():
        halo_a[...] = jnp.zeros(halo_a.shape, F32)
        halo_v[...] = jnp.zeros(halo_v.shape, F32)

    h = h_ref[...]

    def conv(w_ref, cw_ref, cb_ref, halo):
        u = jnp.dot(h, w_ref[...], preferred_element_type=F32)
        cw = cw_ref[...]
        w0, w1, w2 = cw[0:1], cw[1:2], cw[2:3]
        body = (u * w2 + pltpu.roll(u, 1, 0) * w1 + pltpu.roll(u, 2, 0) * w0) + cb_ref[...]
        u8 = u[0:SUBLANES]
        prev8 = halo[...]
        r = lax.broadcasted_iota(jnp.int32, u8.shape, 0)
        s1 = jnp.where(r < 1, pltpu.roll(prev8, 1, 0), pltpu.roll(u8, 1, 0))
        s2 = jnp.where(r < 2, pltpu.roll(prev8, 2, 0), pltpu.roll(u8, 2, 0))
        top = (u8 * w2 + s1 * w1 + s2 * w0) + cb_ref[...]
        halo[...] = u[tm - SUBLANES:tm]
        return body, top

    a, a_top = conv(wa_ref, cwa_ref, cba_ref, halo_a)
    val, val_top = conv(wv_ref, cwv_ref, cbv_ref, halo_v)
    o_ref[...] = (jax.nn.gelu(a, approximate=True) * val).astype(o_ref.dtype)
    o_ref[0:SUBLANES, :] = (jax.nn.gelu(a_top, approximate=True) * val_top).astype(o_ref.dtype)


def _ffn_up(h2, w_up, conv_w, conv_b, seq):
    t = h2.shape[0]
    tm, tn = 1024, 512
    nj = D_FF // tn
    return pl.pallas_call(
        functools.partial(_ffn_up_kernel, tiles_per_batch=seq // tm),
        grid=(nj, t // tm),
        in_specs=[pl.BlockSpec((tm, D_MODEL), lambda j, i: (i, 0)),
                  pl.BlockSpec((D_MODEL, tn), lambda j, i: (0, j)),
                  pl.BlockSpec((D_MODEL, tn), lambda j, i: (0, nj + j)),
                  pl.BlockSpec((CONV_WIDTH, tn), lambda j, i: (0, j)),
                  pl.BlockSpec((CONV_WIDTH, tn), lambda j, i: (0, nj + j)),
                  pl.BlockSpec((1, tn), lambda j, i: (0, j)),
                  pl.BlockSpec((1, tn), lambda j, i: (0, nj + j))],
        out_specs=pl.BlockSpec((tm, tn), lambda j, i: (i, j)),
        out_shape=jax.ShapeDtypeStruct((t, D_FF), BF16),
        scratch_shapes=[pltpu.VMEM((SUBLANES, tn), F32), pltpu.VMEM((SUBLANES, tn), F32)],
        compiler_params=_params(("arbitrary", "arbitrary"), 48),
        name="ffn_up",
    )(h2, w_up, w_up, conv_w, conv_w, conv_b, conv_b)


def _ffn_down_kernel(g_ref, w_ref, x1_ref, gf_ref, gpost_ref, o_ref, acc_ref):
    kk = pl.program_id(1)

    @pl.when(kk == 0)
    def _():
        acc_ref[...] = jnp.zeros(acc_ref.shape, F32)

    acc_ref[...] += jnp.dot(g_ref[...], w_ref[...], preferred_element_type=F32)

    @pl.when(kk == pl.num_programs(1) - 1)
    def _():
        o_ref[...] = x1_ref[...] + gf_ref[0] * _rms(acc_ref[...], gpost_ref[...])


def _ffn_down(g, w, x1, gate_f, g_post, seq):
    t = g.shape[0]
    tm, tk = 512, 512
    tpb = seq // tm
    return pl.pallas_call(
        _ffn_down_kernel,
        grid=(t // tm, D_FF // tk),
        in_specs=[pl.BlockSpec((tm, tk), lambda i, kk: (i, kk)),
                  pl.BlockSpec((tk, D_MODEL), lambda i, kk: (kk, 0)),
                  pl.BlockSpec((tm, D_MODEL), lambda i, kk: (i, 0)),
                  pl.BlockSpec((1, 1, D_MODEL), lambda i, kk: (i // tpb, 0, 0)),
                  pl.BlockSpec((1, D_MODEL), lambda i, kk: (0, 0))],
        out_specs=pl.BlockSpec((tm, D_MODEL), lambda i, kk: (i, 0)),
        out_shape=jax.ShapeDtypeStruct((t, D_MODEL), F32),
        scratch_shapes=[pltpu.VMEM((tm, D_MODEL), F32)],
        compiler_params=_params(("arbitrary", "arbitrary"), 48),
        name="ffn_down",
    )(g, w, x1, gate_f, g_post)


def _swap_halves(w):
    half = w.shape[-1] // 2
    return jnp.concatenate([w[..., half:], w[..., :half]], axis=-1)


def _prep_w_in(w_in):
    offs = np.cumsum((0, MLA_Q_RANK, MLA_KV_RANK, MLA_ROPE, 1024, 1024, 2048, GLA_GATE_RANK,
                      2048, D_MODEL, D_MODEL))
    part = lambda n: w_in[:, offs[n]:offs[n + 1]]
    k_rope = part(2)
    zeros = jnp.zeros((D_MODEL, LANES - GLA_GATE_RANK), w_in.dtype)
    cols = [part(0), part(1), k_rope, _swap_halves(k_rope), part(6), zeros,
            part(3), part(4), part(5), part(7), part(8), part(9)]
    return jnp.concatenate(cols, axis=1).astype(BF16)


def _prep_w_q(w_q_up):
    w = w_q_up.reshape(MLA_Q_RANK, MLA_HEADS, MLA_NOPE + MLA_ROPE)
    pe = w[:, :, MLA_NOPE:]
    w = jnp.concatenate([w, _swap_halves(pe)], axis=-1)
    return jnp.transpose(w, (1, 0, 2)).astype(BF16)


def kernel(x, c, positions, w_ada, b_ada, g_pre_mix, w_in, g_q_lat, w_q_up, g_kv_lat, w_kv_up,
           w_gla_gate_up, b_gla_gate, g_gla_out, w_branch_a, w_branch_b, w_mix_out, g_post_mix,
           g_pre_ffn, w_ffn_up, conv_w, conv_b, w_ffn_down, g_post_ffn):
    batch, seq, _ = x.shape
    depth = w_ada.shape[0]
    t = batch * seq
    row = lambda v: v.reshape(1, -1)

    inv = 1.0 / (ROPE_THETA ** (jnp.arange(0, MLA_ROPE, 2, dtype=F32) / MLA_ROPE))
    zeros64 = jnp.zeros((MLA_ROPE,), F32)
    ones32 = jnp.ones((MLA_ROPE // 2,), F32)
    inv128 = row(jnp.concatenate([inv, inv, zeros64]))
    sgn128 = row(jnp.concatenate([-ones32, ones32, zeros64]))
    cm128 = row(jnp.concatenate([ones32, ones32, zeros64]))
    pos = positions.reshape(t, 1)

    c_pad = jnp.zeros((SUBLANES, D_MODEL), F32).at[:batch].set(c)
    x2 = x.reshape(t, D_MODEL)
    for l in range(depth):
        mod = _ada(c_pad, w_ada[l], row(b_ada[l]))[:batch]
        shift_m, scale_m, gate_m, shift_f, scale_f, gate_f = (
            m.reshape(batch, 1, D_MODEL) for m in jnp.split(mod, 6, axis=-1))

        h = _norm_mod(x2, row(g_pre_mix[l]), scale_m, shift_m, seq)
        z = _mm(h, _prep_w_in(w_in[l]), BF16, 1024, 1024, "mm_in")

        q, k, v = _mla_proj(z, pos, inv128, sgn128, cm128, row(g_q_lat[l]), row(g_kv_lat[l]),
                            _prep_w_q(w_q_up[l]), w_kv_up[l].astype(BF16), batch, seq)
        o_a = _attn(q, k, v, batch, seq)

        wg = jnp.zeros((LANES, GLA_HEADS * GLA_DK), F32).at[:GLA_GATE_RANK].set(w_gla_gate_up[l])
        o_b = _gla(z, wg, row(b_gla_gate[l]), row(g_gla_out[l]), batch, seq)

        merged = _mix1(o_a, o_b, w_branch_a[l].astype(BF16), w_branch_b[l].astype(BF16), z)
        x1, h2 = _mix2(merged, w_mix_out[l].astype(BF16), x2, gate_m, row(g_post_mix[l]),
                       row(g_pre_ffn[l]), scale_f, shift_f, seq)

        g = _ffn_up(h2, w_ffn_up[l].astype(BF16), conv_w[l], row(conv_b[l]), seq)
        x2 = _ffn_down(g, w_ffn_down[l].astype(BF16), x1, gate_f, row(g_post_ffn[l]), seq)
    return x2.reshape(batch, seq, D_MODEL)
```

```python
import functools

import jax
import jax.numpy as jnp
import numpy as np
from jax import lax
from jax.experimental import pallas as pl
from jax.experimental.pallas import tpu as pltpu

F32 = jnp.float32
BF16 = jnp.bfloat16

D_MODEL = 2048
MLA_HEADS = 8
MLA_Q_RANK = 512
MLA_KV_RANK = 256
MLA_NOPE = 128
MLA_ROPE = 64
MLA_V = 128
ROPE_THETA = 10000.0
GLA_HEADS = 4
GLA_DK = 256
GLA_DV = 512
GLA_GATE_RANK = 16
GLA_TAU = 16.0
D_FF = 5632
CONV_WIDTH = 3
EPS = 1e-6

LANES = 128
SUBLANES = 8
MLA_QK_PAD = 256

Z_QLAT = 0
Z_KVLAT = 512
Z_KROPE = 768
Z_GLR = 896
Z_GQ = 1024
Z_GV = 2048
Z_OG = 4096
Z_GATE_A = 6144
Z_GATE_B = 8192
Z_GK = 10240
Z_WIDTH = 11264

GLA_CHUNK = 128
GLA_LEVELS = 7

MIB = 1024 * 1024


def _params(semantics, vmem_mib):
    return pltpu.CompilerParams(dimension_semantics=semantics,
                                vmem_limit_bytes=vmem_mib * MIB)


def _rms(x, g):
    return x * lax.rsqrt(jnp.mean(x * x, axis=-1, keepdims=True) + EPS) * g


def _ada_kernel(c_ref, w_ref, b_ref, o_ref):
    c = c_ref[...]
    ca = c * jax.nn.sigmoid(c)
    o_ref[...] = jnp.dot(ca.astype(BF16), w_ref[...].astype(BF16),
                         preferred_element_type=F32) + b_ref[...]


def _ada(c_pad, w, b):
    n = w.shape[1]
    tn = 1024
    return pl.pallas_call(
        _ada_kernel,
        grid=(n // tn,),
        in_specs=[pl.BlockSpec((SUBLANES, D_MODEL), lambda j: (0, 0)),
                  pl.BlockSpec((D_MODEL, tn), lambda j: (0, j)),
                  pl.BlockSpec((1, tn), lambda j: (0, j))],
        out_specs=pl.BlockSpec((SUBLANES, tn), lambda j: (0, j)),
        out_shape=jax.ShapeDtypeStruct((SUBLANES, n), F32),
        compiler_params=_params(("arbitrary",), 32),
        name="ada",
    )(c_pad, w, b)


def _norm_mod_kernel(x_ref, g_ref, sc_ref, sh_ref, o_ref):
    y = _rms(x_ref[...], g_ref[...])
    o_ref[...] = (y * (1.0 + sc_ref[0]) + sh_ref[0]).astype(o_ref.dtype)


def _norm_mod(x2, g, scale, shift, seq):
    t = x2.shape[0]
    tm = 512
    tpb = seq // tm
    return pl.pallas_call(
        _norm_mod_kernel,
        grid=(t // tm,),
        in_specs=[pl.BlockSpec((tm, D_MODEL), lambda i: (i, 0)),
                  pl.BlockSpec((1, D_MODEL), lambda i: (0, 0)),
                  pl.BlockSpec((1, 1, D_MODEL), lambda i: (i // tpb, 0, 0)),
                  pl.BlockSpec((1, 1, D_MODEL), lambda i: (i // tpb, 0, 0))],
        out_specs=pl.BlockSpec((tm, D_MODEL), lambda i: (i, 0)),
        out_shape=jax.ShapeDtypeStruct((t, D_MODEL), BF16),
        compiler_params=_params(("arbitrary",), 32),
        name="norm_mod",
    )(x2, g, scale, shift)


def _mm_kernel(a_ref, w_ref, o_ref):
    o_ref[...] = jnp.dot(a_ref[...], w_ref[...],
                         preferred_element_type=F32).astype(o_ref.dtype)


def _mm(a, w, out_dtype, tm, tn, name):
    m, k = a.shape
    n = w.shape[1]
    return pl.pallas_call(
        _mm_kernel,
        grid=(n // tn, m // tm),
        in_specs=[pl.BlockSpec((tm, k), lambda j, i: (i, 0)),
                  pl.BlockSpec((k, tn), lambda j, i: (0, j))],
        out_specs=pl.BlockSpec((tm, tn), lambda j, i: (i, j)),
        out_shape=jax.ShapeDtypeStruct((m, n), out_dtype),
        compiler_params=_params(("arbitrary", "arbitrary"), 48),
        name=name,
    )(a, w)


def _mla_proj_kernel(ql_ref, kvl_ref, kr_ref, pos_ref, inv_ref, sgn_ref, cm_ref,
                     gq_ref, gkv_ref, wq_ref, wkv_ref, q_ref, k_ref, v_ref):
    ang = pos_ref[...].astype(F32) * inv_ref[...]
    cosm = jnp.cos(ang) * cm_ref[...]
    sinm = jnp.sin(ang) * sgn_ref[...]

    def rope(r):
        return r * cosm + pltpu.roll(r, MLA_ROPE, 1) * sinm

    scale = (MLA_NOPE + MLA_ROPE) ** -0.5 * np.log2(np.e)
    qn =_rms(ql_ref[...].astype(F32), gq_ref[...]).astype(BF16)
    for h in range(MLA_HEADS):
        r = jnp.dot(qn, wq_ref[h], preferred_element_type=F32) * scale
        q_ref[0, h, :, 0:MLA_NOPE] = r[:, 0:MLA_NOPE].astype(BF16)
        q_ref[0, h, :, MLA_NOPE:MLA_QK_PAD] = rope(r[:, MLA_NOPE:MLA_QK_PAD]).astype(BF16)

    kvn = _rms(kvl_ref[...].astype(F32), gkv_ref[...]).astype(BF16)
    kv = jnp.dot(kvn, wkv_ref[...], preferred_element_type=F32).astype(BF16)
    kpe = rope(kr_ref[...].astype(F32)).astype(BF16)
    hw = MLA_NOPE + MLA_V
    for h in range(MLA_HEADS):
        k_ref[0, h, :, 0:MLA_NOPE] = kv[:, h * hw:h * hw + MLA_NOPE]
        k_ref[0, h, :, MLA_NOPE:MLA_QK_PAD] = kpe
        v_ref[0, h] = kv[:, h * hw + MLA_NOPE:(h + 1) * hw]


def _mla_proj(z, pos, inv, sgn, cm, gq, gkv, wq, wkv, batch, seq):
    tm = 512
    tpb = seq // tm
    qk_shape = jax.ShapeDtypeStruct((batch, MLA_HEADS, seq, MLA_QK_PAD), BF16)
    v_shape = jax.ShapeDtypeStruct((batch, MLA_HEADS, seq, MLA_V), BF16)
    const2 = lambda i: (0, 0)
    qk_spec = pl.BlockSpec((1, MLA_HEADS, tm, MLA_QK_PAD), lambda i: (i // tpb, 0, i % tpb, 0))
    return pl.pallas_call(
        _mla_proj_kernel,
        grid=(batch * tpb,),
        in_specs=[pl.BlockSpec((tm, MLA_Q_RANK), lambda i: (i, Z_QLAT // MLA_Q_RANK)),
                  pl.BlockSpec((tm, MLA_KV_RANK), lambda i: (i, Z_KVLAT // MLA_KV_RANK)),
                  pl.BlockSpec((tm, LANES), lambda i: (i, Z_KROPE // LANES)),
                  pl.BlockSpec((tm, 1), lambda i: (i, 0)),
                  pl.BlockSpec((1, LANES), const2),
                  pl.BlockSpec((1, LANES), const2),
                  pl.BlockSpec((1, LANES), const2),
                  pl.BlockSpec((1, MLA_Q_RANK), const2),
                  pl.BlockSpec((1, MLA_KV_RANK), const2),
                  pl.BlockSpec((MLA_HEADS, MLA_Q_RANK, MLA_QK_PAD), lambda i: (0, 0, 0)),
                  pl.BlockSpec((MLA_KV_RANK, MLA_HEADS * (MLA_NOPE + MLA_V)), const2)],
        out_specs=[qk_spec, qk_spec,
                   pl.BlockSpec((1, MLA_HEADS, tm, MLA_V), lambda i: (i // tpb, 0, i % tpb, 0))],
        out_shape=[qk_shape, qk_shape, v_shape],
        compiler_params=_params(("arbitrary",), 48),
        name="mla_proj",
    )(z, z, z, pos, inv, sgn, cm, gq, gkv, wq, wkv)


ATTN_BLOCK = 1024
ATTN_DIAG_CHAINS = 2


def _attn_kernel(q_ref, k_ref, v_ref, o_ref, m_sc, l_sc, acc_sc):
    blk = ATTN_BLOCK
    nt = (((1,), (1,)), ((), ()))
    qi = pl.program_id(2)
    m_sc[...] = jnp.full(m_sc.shape, -jnp.inf, F32)
    l_sc[...] = jnp.zeros(l_sc.shape, F32)
    acc_sc[...] = jnp.zeros(acc_sc.shape, F32)

    def update(rows, s, v):
        m_prev = m_sc[rows]
        m_new = jnp.maximum(m_prev, jnp.max(s, axis=-1, keepdims=True))
        alpha = jnp.exp2(m_prev - m_new)
        ps = [jnp.exp2(s[:, c * LANES:(c + 1) * LANES] - m_new)
              for c in range(s.shape[1] // LANES)]
        l_sc[rows] = alpha * l_sc[rows] + functools.reduce(lambda a, b: a + b, ps)
        p = jnp.concatenate(ps, axis=1).astype(BF16)
        acc_sc[rows] = alpha * acc_sc[rows] + jnp.dot(p, v, preferred_element_type=F32)
        m_sc[rows] = m_new

    def body(kb, carry):
        ks = pl.multiple_of(kb * blk, blk)
        k = k_ref[0, 0, pl.ds(ks, blk), :]
        v = v_ref[0, 0, pl.ds(ks, blk), :]
        s = lax.dot_general(q_ref[0, 0], k, nt, preferred_element_type=F32)
        update(slice(0, blk), s, v)
        return carry

    lax.fori_loop(0, qi, body, 0)

    ks = pl.multiple_of(qi * blk, blk)
    sub = blk // ATTN_DIAG_CHAINS
    for r in range(ATTN_DIAG_CHAINS):
        n = (r + 1) * sub
        k = k_ref[0, 0, pl.ds(ks, n), :]
        v = v_ref[0, 0, pl.ds(ks, n), :]
        q = q_ref[0, 0, r * sub:(r + 1) * sub, :]
        s = lax.dot_general(q, k, nt, preferred_element_type=F32)
        row = lax.broadcasted_iota(jnp.int32, s.shape, 0) + r * sub
        col = lax.broadcasted_iota(jnp.int32, s.shape, 1)
        update(slice(r * sub, n), jnp.where(col <= row, s, -jnp.inf), v)
    l = jnp.sum(l_sc[...], axis=-1, keepdims=True)
    o_ref[...] = (acc_sc[...] / l).astype(o_ref.dtype)


def _attn(q, k, v, batch, seq):
    blk = ATTN_BLOCK
    nq = seq // blk
    return pl.pallas_call(
        _attn_kernel,
        grid=(batch, MLA_HEADS, nq),
        in_specs=[pl.BlockSpec((1, 1, blk, MLA_QK_PAD), lambda b, h, i: (b, h, i, 0)),
                  pl.BlockSpec((1, 1, seq, MLA_QK_PAD), lambda b, h, i: (b, h, 0, 0)),
                  pl.BlockSpec((1, 1, seq, MLA_V), lambda b, h, i: (b, h, 0, 0))],
        out_specs=pl.BlockSpec((blk, MLA_V), lambda b, h, i: (b * nq + i, h)),
        out_shape=jax.ShapeDtypeStruct((batch * seq, MLA_HEADS * MLA_V), BF16),
        scratch_shapes=[pltpu.VMEM((blk, LANES), F32), pltpu.VMEM((blk, LANES), F32),
                        pltpu.VMEM((blk, MLA_V), F32)],
        compiler_params=_params(("arbitrary", "arbitrary", "arbitrary"), 32),
        name="attn",
    )(q, k, v)


def _gla_constants():
    c = GLA_CHUNK
    idx = np.arange(c)
    w = np.zeros((GLA_LEVELS + 2, c, c), np.float32)
    masks = np.zeros((GLA_LEVELS + 1, c, c), np.float32)
    for l in range(GLA_LEVELS):
        half = 1 << l
        blk = idx // (2 * half)
        mid = blk * 2 * half + half - 1
        upper = (idx % (2 * half)) >= half
        t = idx[None, :]
        up_rows = (t > mid[:, None]) & (t <= idx[:, None])
        lo_rows = (t > idx[:, None]) & (t <= mid[:, None])
        w[l] = np.where(upper[:, None], up_rows, lo_rows)
        masks[l] = (blk[:, None] == blk[None, :]) & upper[:, None] & (~upper[None, :])
    w[GLA_LEVELS] = idx[None, :] <= idx[:, None]
    w[GLA_LEVELS + 1] = idx[None, :] > idx[:, None]
    masks[GLA_LEVELS] = np.eye(c)
    w = w.reshape(-1, c)
    return np.concatenate([w, w], axis=1), masks


def _split_bf16(x):
    hi = x.astype(BF16)
    return hi, (x - hi.astype(F32)).astype(BF16)


def _gla_kernel(q_ref, k_ref, v_ref, lr_ref, og_ref, wg_ref, bg_ref, go_ref, ws_ref, mask_ref,
                o_ref, st_ref):
    c = GLA_CHUNK
    dk, dv = GLA_DK, GLA_DV
    nt = (((1,), (1,)), ((), ()))

    @pl.when(pl.program_id(1) == 0)
    def _():
        st_ref[...] = jnp.zeros(st_ref.shape, F32)

    lr = lr_ref[...]
    for h in range(GLA_HEADS):
        x = (jnp.dot(lr, wg_ref[:, h * dk:(h + 1) * dk], preferred_element_type=F32)
             + bg_ref[:, h * dk:(h + 1) * dk])
        log_a = -(jnp.maximum(-x, 0.0) + jnp.log1p(jnp.exp(-jnp.abs(x)))) * (1.0 / GLA_TAU)
        expo = jnp.dot(ws_ref[...], jnp.concatenate(_split_bf16(log_a), axis=0),
                       preferred_element_type=F32)
        p = jnp.exp(expo)
        pb = p.astype(BF16)

        q = q_ref[:, h * dk:(h + 1) * dk] * (dk ** -0.5)
        k = k_ref[:, h * dk:(h + 1) * dk]
        v = v_ref[:, h * dv:(h + 1) * dv]
        attn = mask_ref[GLA_LEVELS] * lax.dot_general(q, k, nt, preferred_element_type=F32)
        for l in range(GLA_LEVELS):
            pl_ = pb[l * c:(l + 1) * c]
            s = lax.dot_general(q * pl_, k * pl_, nt, preferred_element_type=F32)
            attn = attn + mask_ref[l] * s
        eb = pb[GLA_LEVELS * c:(GLA_LEVELS + 1) * c]
        ebl = pb[(GLA_LEVELS + 1) * c:(GLA_LEVELS + 2) * c]
        st = st_ref[h]
        o = (lax.dot_general(q * eb, st.astype(BF16), nt, preferred_element_type=F32)
             + jnp.dot(attn.astype(BF16), v, preferred_element_type=F32))
        v_t = v.astype(F32).T.astype(BF16)
        decay = p[(GLA_LEVELS + 1) * c - 1:(GLA_LEVELS + 1) * c]
        st_ref[h] = st * decay + jnp.dot(v_t, k * ebl, preferred_element_type=F32)
        y = _rms(o, go_ref[...])
        og = og_ref[:, h * dv:(h + 1) * dv].astype(F32)
        o_ref[:, h * dv:(h + 1) * dv] = (y * (og * jax.nn.sigmoid(og))).astype(o_ref.dtype)


def _gla(z, wg, bg, go, batch, seq):
    c = GLA_CHUNK
    nc = seq // c
    ws, masks = _gla_constants()
    hk, hv = GLA_HEADS * GLA_DK, GLA_HEADS * GLA_DV
    const2 = lambda b, i: (0, 0)
    return pl.pallas_call(
        _gla_kernel,
        grid=(batch, nc),
        in_specs=[pl.BlockSpec((c, hk), lambda b, i: (b * nc + i, Z_GQ // hk)),
                  pl.BlockSpec((c, hk), lambda b, i: (b * nc + i, Z_GK // hk)),
                  pl.BlockSpec((c, hv), lambda b, i: (b * nc + i, Z_GV // hv)),
                  pl.BlockSpec((c, LANES), lambda b, i: (b * nc + i, Z_GLR // LANES)),
                  pl.BlockSpec((c, hv), lambda b, i: (b * nc + i, Z_OG // hv)),
                  pl.BlockSpec((LANES, hk), const2),
                  pl.BlockSpec((1, hk), const2),
                  pl.BlockSpec((1, GLA_DV), const2),
                  pl.BlockSpec(ws.shape, const2),
                  pl.BlockSpec(masks.shape, lambda b, i: (0, 0, 0))],
        out_specs=pl.BlockSpec((c, hv), lambda b, i: (b * nc + i, 0)),
        out_shape=jax.ShapeDtypeStruct((batch * seq, hv), BF16),
        scratch_shapes=[pltpu.VMEM((GLA_HEADS, GLA_DV, GLA_DK), F32)],
        compiler_params=_params(("arbitrary", "arbitrary"), 48),
        name="gla",
    )(z, z, z, z, z, wg, bg, go, jnp.asarray(ws, BF16), jnp.asarray(masks))


def _mix1_kernel(oa_ref, ob_ref, wa_ref, wb_ref, ga_ref, gb_ref, o_ref):
    ya = jnp.dot(oa_ref[...], wa_ref[...], preferred_element_type=F32)
    yb = jnp.dot(ob_ref[...], wb_ref[...], preferred_element_type=F32)
    o_ref[...] = (jax.nn.sigmoid(ga_ref[...].astype(F32)) * ya
                  + jax.nn.sigmoid(gb_ref[...].astype(F32)) * yb).astype(o_ref.dtype)


def _mix1(oa, ob, wa, wb, z):
    t = oa.shape[0]
    tm, tn = 1024, 512
    return pl.pallas_call(
        _mix1_kernel,
        grid=(D_MODEL // tn, t // tm),
        in_specs=[pl.BlockSpec((tm, oa.shape[1]), lambda j, i: (i, 0)),
                  pl.BlockSpec((tm, ob.shape[1]), lambda j, i: (i, 0)),
                  pl.BlockSpec((wa.shape[0], tn), lambda j, i: (0, j)),
                  pl.BlockSpec((wb.shape[0], tn), lambda j, i: (0, j)),
                  pl.BlockSpec((tm, tn), lambda j, i: (i, Z_GATE_A // tn + j)),
                  pl.BlockSpec((tm, tn), lambda j, i: (i, Z_GATE_B // tn + j))],
        out_specs=pl.BlockSpec((tm, tn), lambda j, i: (i, j)),
        out_shape=jax.ShapeDtypeStruct((t, D_MODEL), BF16),
        compiler_params=_params(("arbitrary", "arbitrary"), 48),
        name="mix1",
    )(oa, ob, wa, wb, z, z)


def _mix2_kernel(m_ref, w_ref, x_ref, gm_ref, gpost_ref, gpre_ref, sc_ref, sh_ref,
                 x1_ref, h2_ref):
    y = jnp.dot(m_ref[...], w_ref[...], preferred_element_type=F32)
    x1 = x_ref[...] + gm_ref[0] * _rms(y, gpost_ref[...])
    x1_ref[...] = x1
    h2_ref[...] = (_rms(x1, gpre_ref[...]) * (1.0 + sc_ref[0]) + sh_ref[0]).astype(h2_ref.dtype)


def _mix2(merged, w, x2, gate_m, g_post, g_pre, scale_f, shift_f, seq):
    t = merged.shape[0]
    tm = 256
    tpb = seq // tm
    row = pl.BlockSpec((tm, D_MODEL), lambda i: (i, 0))
    vec = pl.BlockSpec((1, D_MODEL), lambda i: (0, 0))
    per_batch = pl.BlockSpec((1, 1, D_MODEL), lambda i: (i // tpb, 0, 0))
    return pl.pallas_call(
        _mix2_kernel,
        grid=(t // tm,),
        in_specs=[row, pl.BlockSpec((D_MODEL, D_MODEL), lambda i: (0, 0)), row,
                  per_batch, vec, vec, per_batch, per_batch],
        out_specs=[row, row],
        out_shape=[jax.ShapeDtypeStruct((t, D_MODEL), F32),
                   jax.ShapeDtypeStruct((t, D_MODEL), BF16)],
        compiler_params=_params(("arbitrary",), 48),
        name="mix2",
    )(merged, w, x2, gate_m, g_post, g_pre, scale_f, shift_f)


def _ffn_up_kernel(h_ref, wa_ref, wv_ref, cwa_ref, cwv_ref, cba_ref, cbv_ref, o_ref,
                   halo_a, halo_v, *, tiles_per_batch):
    i = pl.program_id(1)
    tm = h_ref.shape[0]

    @pl.when(i % tiles_per_batch == 0)
    def _():
        halo_a[...] = jnp.zeros(halo_a.shape, F32)
        halo_v[...] = jnp.zeros(halo_v.shape, F32)

    h = h_ref[...]

    def conv(w_ref, cw_ref, cb_ref, halo):
        u = jnp.dot(h, w_ref[...], preferred_element_type=F32)
        cw = cw_ref[...]
        w0, w1, w2 = cw[0:1], cw[1:2], cw[2:3]
        body = (u * w2 + pltpu.roll(u, 1, 0) * w1 + pltpu.roll(u, 2, 0) * w0) + cb_ref[...]
        u8 = u[0:SUBLANES]
        prev8 = halo[...]
        r = lax.broadcasted_iota(jnp.int32, u8.shape, 0)
        s1 = jnp.where(r < 1, pltpu.roll(prev8, 1, 0), pltpu.roll(u8, 1, 0))
        s2 = jnp.where(r < 2, pltpu.roll(prev8, 2, 0), pltpu.roll(u8, 2, 0))
        top = (u8 * w2 + s1 * w1 + s2 * w0) + cb_ref[...]
        halo[...] = u[tm - SUBLANES:tm]
        return body, top

    a, a_top = conv(wa_ref, cwa_ref, cba_ref, halo_a)
    val, val_top = conv(wv_ref, cwv_ref, cbv_ref, halo_v)
    o_ref[...] = (jax.nn.gelu(a, approximate=True) * val).astype(o_ref.dtype)
    o_ref[0:SUBLANES, :] = (jax.nn.gelu(a_top, approximate=True) * val_top).astype(o_ref.dtype)


def _ffn_up(h2, w_up, conv_w, conv_b, seq):
    t = h2.shape[0]
    tm, tn = 1024, 512
    nj = D_FF // tn
    return pl.pallas_call(
        functools.partial(_ffn_up_kernel, tiles_per_batch=seq // tm),
        grid=(nj, t // tm),
        in_specs=[pl.BlockSpec((tm, D_MODEL), lambda j, i: (i, 0)),
                  pl.BlockSpec((D_MODEL, tn), lambda j, i: (0, j)),
                  pl.BlockSpec((D_MODEL, tn), lambda j, i: (0, nj + j)),
                  pl.BlockSpec((CONV_WIDTH, tn), lambda j, i: (0, j)),
                  pl.BlockSpec((CONV_WIDTH, tn), lambda j, i: (0, nj + j)),
                  pl.BlockSpec((1, tn), lambda j, i: (0, j)),
                  pl.BlockSpec((1, tn), lambda j, i: (0, nj + j))],
        out_specs=pl.BlockSpec((tm, tn), lambda j, i: (i, j)),
        out_shape=jax.ShapeDtypeStruct((t, D_FF), BF16),
        scratch_shapes=[pltpu.VMEM((SUBLANES, tn), F32), pltpu.VMEM((SUBLANES, tn), F32)],
        compiler_params=_params(("arbitrary", "arbitrary"), 48),
        name="ffn_up",
    )(h2, w_up, w_up, conv_w, conv_w, conv_b, conv_b)


def _ffn_down_kernel(g_ref, w_ref, x1_ref, gf_ref, gpost_ref, o_ref, acc_ref):
    kk = pl.program_id(1)

    @pl.when(kk == 0)
    def _():
        acc_ref[...] = jnp.zeros(acc_ref.shape, F32)

    acc_ref[...] += jnp.dot(g_ref[...], w_ref[...], preferred_element_type=F32)

    @pl.when(kk == pl.num_programs(1) - 1)
    def _():
        o_ref[...] = x1_ref[...] + gf_ref[0] * _rms(acc_ref[...], gpost_ref[...])


def _ffn_down(g, w, x1, gate_f, g_post, seq):
    t = g.shape[0]
    tm, tk = 512, 512
    tpb = seq // tm
    return pl.pallas_call(
        _ffn_down_kernel,
        grid=(t // tm, D_FF // tk),
        in_specs=[pl.BlockSpec((tm, tk), lambda i, kk: (i, kk)),
                  pl.BlockSpec((tk, D_MODEL), lambda i, kk: (kk, 0)),
                  pl.BlockSpec((tm, D_MODEL), lambda i, kk: (i, 0)),
                  pl.BlockSpec((1, 1, D_MODEL), lambda i, kk: (i // tpb, 0, 0)),
                  pl.BlockSpec((1, D_MODEL), lambda i, kk: (0, 0))],
        out_specs=pl.BlockSpec((tm, D_MODEL), lambda i, kk: (i, 0)),
        out_shape=jax.ShapeDtypeStruct((t, D_MODEL), F32),
        scratch_shapes=[pltpu.VMEM((tm, D_MODEL), F32)],
        compiler_params=_params(("arbitrary", "arbitrary"), 48),
        name="ffn_down",
    )(g, w, x1, gate_f, g_post)


def _swap_halves(w):
    half = w.shape[-1] // 2
    return jnp.concatenate([w[..., half:], w[..., :half]], axis=-1)


def _prep_w_in(w_in):
    offs = np.cumsum((0, MLA_Q_RANK, MLA_KV_RANK, MLA_ROPE, 1024, 1024, 2048, GLA_GATE_RANK,
                      2048, D_MODEL, D_MODEL))
    part = lambda n: w_in[:, offs[n]:offs[n + 1]]
    k_rope = part(2)
    zeros = jnp.zeros((D_MODEL, LANES - 2 * GLA_GATE_RANK), w_in.dtype)
    cols = [part(0), part(1), k_rope, _swap_halves(k_rope), part(6), part(6), zeros,
            part(3), part(5), part(7), part(8), part(9), part(4)]
    return jnp.concatenate(cols, axis=1).astype(BF16)


def _prep_w_q(w_q_up):
    w = w_q_up.reshape(MLA_Q_RANK, MLA_HEADS, MLA_NOPE + MLA_ROPE)
    pe = w[:, :, MLA_NOPE:]
    w = jnp.concatenate([w, _swap_halves(pe)], axis=-1)
    return jnp.transpose(w, (1, 0, 2)).astype(BF16)


def kernel(x, c, positions, w_ada, b_ada, g_pre_mix, w_in, g_q_lat, w_q_up, g_kv_lat, w_kv_up,
           w_gla_gate_up, b_gla_gate, g_gla_out, w_branch_a, w_branch_b, w_mix_out, g_post_mix,
           g_pre_ffn, w_ffn_up, conv_w, conv_b, w_ffn_down, g_post_ffn):
    batch, seq, _ = x.shape
    depth = w_ada.shape[0]
    t = batch * seq
    row = lambda v: v.reshape(1, -1)

    inv = 1.0 / (ROPE_THETA ** (jnp.arange(0, MLA_ROPE, 2, dtype=F32) / MLA_ROPE))
    zeros64 = jnp.zeros((MLA_ROPE,), F32)
    ones32 = jnp.ones((MLA_ROPE // 2,), F32)
    inv128 = row(jnp.concatenate([inv, inv, zeros64]))
    sgn128 = row(jnp.concatenate([-ones32, ones32, zeros64]))
    cm128 = row(jnp.concatenate([ones32, ones32, zeros64]))
    pos = positions.reshape(t, 1)

    c_pad = jnp.zeros((SUBLANES, D_MODEL), F32).at[:batch].set(c)
    x2 = x.reshape(t, D_MODEL)
    for l in range(depth):
        mod = _ada(c_pad, w_ada[l], row(b_ada[l]))[:batch]
        shift_m, scale_m, gate_m, shift_f, scale_f, gate_f = (
            m.reshape(batch, 1, D_MODEL) for m in jnp.split(mod, 6, axis=-1))

        h = _norm_mod(x2, row(g_pre_mix[l]), scale_m, shift_m, seq)
        z = _mm(h, _prep_w_in(w_in[l]), BF16, 1024, 1024, "mm_in")

        q, k, v = _mla_proj(z, pos, inv128, sgn128, cm128, row(g_q_lat[l]), row(g_kv_lat[l]),
                            _prep_w_q(w_q_up[l]), w_kv_up[l].astype(BF16), batch, seq)
        o_a = _attn(q, k, v, batch, seq)

        wg_hi, wg_lo = _split_bf16(w_gla_gate_up[l])
        wg = jnp.zeros((LANES, GLA_HEADS * GLA_DK), BF16)
        wg = wg.at[:GLA_GATE_RANK].set(wg_hi).at[GLA_GATE_RANK:2 * GLA_GATE_RANK].set(wg_lo)
        o_b = _gla(z, wg, row(b_gla_gate[l]), row(g_gla_out[l]), batch, seq)

        merged = _mix1(o_a, o_b, w_branch_a[l].astype(BF16), w_branch_b[l].astype(BF16), z)
        x1, h2 = _mix2(merged, w_mix_out[l].astype(BF16), x2, gate_m, row(g_post_mix[l]),
                       row(g_pre_ffn[l]), scale_f, shift_f, seq)

        g = _ffn_up(h2, w_ffn_up[l].astype(BF16), conv_w[l], row(conv_b[l]), seq)
        x2 = _ffn_down(g, w_ffn_down[l].astype(BF16), x1, gate_f, row(g_post_ffn[l]), seq)
    return x2.reshape(batch, seq, D_MODEL)
```

```python
import functools

import jax
import jax.numpy as jnp
import numpy as np
from jax import lax
from jax.experimental import pallas as pl
from jax.experimental.pallas import tpu as pltpu

F32 = jnp.float32
BF16 = jnp.bfloat16

D_MODEL = 2048
MLA_HEADS = 8
MLA_Q_RANK = 512
MLA_KV_RANK = 256
MLA_NOPE = 128
MLA_ROPE = 64
MLA_V = 128
ROPE_THETA = 10000.0
GLA_HEADS = 4
GLA_DK = 256
GLA_DV = 512
GLA_GATE_RANK = 16
GLA_TAU = 16.0
D_FF = 5632
CONV_WIDTH = 3
EPS = 1e-6

LANES = 128
SUBLANES = 8
MLA_QK_PAD = 256

Z_QLAT = 0
Z_KVLAT = 512
Z_KROPE = 768
Z_GLR = 896
Z_GQ = 1024
Z_GV = 2048
Z_OG = 4096
Z_GATE_A = 6144
Z_GATE_B = 8192
Z_GK = 10240
Z_WIDTH = 11264

GLA_CHUNK = 128
GLA_LEVELS = 7

MIB = 1024 * 1024


def _params(semantics, vmem_mib):
    return pltpu.CompilerParams(dimension_semantics=semantics,
                                vmem_limit_bytes=vmem_mib * MIB)


def _rms(x, g):
    return x * lax.rsqrt(jnp.mean(x * x, axis=-1, keepdims=True) + EPS) * g


def _ada_kernel(c_ref, w_ref, b_ref, o_ref):
    c = c_ref[...]
    ca = c * jax.nn.sigmoid(c)
    o_ref[...] = jnp.dot(ca.astype(BF16), w_ref[...].astype(BF16),
                         preferred_element_type=F32) + b_ref[...]


def _ada(c_pad, w, b):
    n = w.shape[1]
    tn = 1024
    return pl.pallas_call(
        _ada_kernel,
        grid=(n // tn,),
        in_specs=[pl.BlockSpec((SUBLANES, D_MODEL), lambda j: (0, 0)),
                  pl.BlockSpec((D_MODEL, tn), lambda j: (0, j)),
                  pl.BlockSpec((1, tn), lambda j: (0, j))],
        out_specs=pl.BlockSpec((SUBLANES, tn), lambda j: (0, j)),
        out_shape=jax.ShapeDtypeStruct((SUBLANES, n), F32),
        compiler_params=_params(("arbitrary",), 32),
        name="ada",
    )(c_pad, w, b)


def _norm_mod_kernel(x_ref, g_ref, sc_ref, sh_ref, o_ref):
    y = _rms(x_ref[...], g_ref[...])
    o_ref[...] = (y * (1.0 + sc_ref[0]) + sh_ref[0]).astype(o_ref.dtype)


def _norm_mod(x2, g, scale, shift, seq):
    t = x2.shape[0]
    tm = 512
    tpb = seq // tm
    return pl.pallas_call(
        _norm_mod_kernel,
        grid=(t // tm,),
        in_specs=[pl.BlockSpec((tm, D_MODEL), lambda i: (i, 0)),
                  pl.BlockSpec((1, D_MODEL), lambda i: (0, 0)),
                  pl.BlockSpec((1, 1, D_MODEL), lambda i: (i // tpb, 0, 0)),
                  pl.BlockSpec((1, 1, D_MODEL), lambda i: (i // tpb, 0, 0))],
        out_specs=pl.BlockSpec((tm, D_MODEL), lambda i: (i, 0)),
        out_shape=jax.ShapeDtypeStruct((t, D_MODEL), BF16),
        compiler_params=_params(("arbitrary",), 32),
        name="norm_mod",
    )(x2, g, scale, shift)


def _mm_kernel(a_ref, w_ref, o_ref):
    o_ref[...] = jnp.dot(a_ref[...], w_ref[...],
                         preferred_element_type=F32).astype(o_ref.dtype)


def _mm(a, w, out_dtype, tm, tn, name):
    m, k = a.shape
    n = w.shape[1]
    return pl.pallas_call(
        _mm_kernel,
        grid=(n // tn, m // tm),
        in_specs=[pl.BlockSpec((tm, k), lambda j, i: (i, 0)),
                  pl.BlockSpec((k, tn), lambda j, i: (0, j))],
        out_specs=pl.BlockSpec((tm, tn), lambda j, i: (i, j)),
        out_shape=jax.ShapeDtypeStruct((m, n), out_dtype),
        compiler_params=_params(("arbitrary", "arbitrary"), 48),
        name=name,
    )(a, w)


def _mla_proj_kernel(ql_ref, kvl_ref, kr_ref, pos_ref, inv_ref, sgn_ref, cm_ref,
                     gq_ref, gkv_ref, wq_ref, wkv_ref, q_ref, k_ref, v_ref):
    ang = pos_ref[...].astype(F32) * inv_ref[...]
    cosm = jnp.cos(ang) * cm_ref[...]
    sinm = jnp.sin(ang) * sgn_ref[...]

    def rope(r):
        return r * cosm + pltpu.roll(r, MLA_ROPE, 1) * sinm

    scale = (MLA_NOPE + MLA_ROPE) ** -0.5 * np.log2(np.e)
    qn =_rms(ql_ref[...].astype(F32), gq_ref[...]).astype(BF16)
    for h in range(MLA_HEADS):
        r = jnp.dot(qn, wq_ref[h], preferred_element_type=F32) * scale
        q_ref[0, h, :, 0:MLA_NOPE] = r[:, 0:MLA_NOPE].astype(BF16)
        q_ref[0, h, :, MLA_NOPE:MLA_QK_PAD] = rope(r[:, MLA_NOPE:MLA_QK_PAD]).astype(BF16)

    kvn = _rms(kvl_ref[...].astype(F32), gkv_ref[...]).astype(BF16)
    kv = jnp.dot(kvn, wkv_ref[...], preferred_element_type=F32).astype(BF16)
    kpe = rope(kr_ref[...].astype(F32)).astype(BF16)
    hw = MLA_NOPE + MLA_V
    for h in range(MLA_HEADS):
        k_ref[0, h, :, 0:MLA_NOPE] = kv[:, h * hw:h * hw + MLA_NOPE]
        k_ref[0, h, :, MLA_NOPE:MLA_QK_PAD] = kpe
        v_ref[0, h] = kv[:, h * hw + MLA_NOPE:(h + 1) * hw]


def _mla_proj(z, pos, inv, sgn, cm, gq, gkv, wq, wkv, batch, seq):
    tm = 512
    tpb = seq // tm
    qk_shape = jax.ShapeDtypeStruct((batch, MLA_HEADS, seq, MLA_QK_PAD), BF16)
    v_shape = jax.ShapeDtypeStruct((batch, MLA_HEADS, seq, MLA_V), BF16)
    const2 = lambda i: (0, 0)
    qk_spec = pl.BlockSpec((1, MLA_HEADS, tm, MLA_QK_PAD), lambda i: (i // tpb, 0, i % tpb, 0))
    return pl.pallas_call(
        _mla_proj_kernel,
        grid=(batch * tpb,),
        in_specs=[pl.BlockSpec((tm, MLA_Q_RANK), lambda i: (i, Z_QLAT // MLA_Q_RANK)),
                  pl.BlockSpec((tm, MLA_KV_RANK), lambda i: (i, Z_KVLAT // MLA_KV_RANK)),
                  pl.BlockSpec((tm, LANES), lambda i: (i, Z_KROPE // LANES)),
                  pl.BlockSpec((tm, 1), lambda i: (i, 0)),
                  pl.BlockSpec((1, LANES), const2),
                  pl.BlockSpec((1, LANES), const2),
                  pl.BlockSpec((1, LANES), const2),
                  pl.BlockSpec((1, MLA_Q_RANK), const2),
                  pl.BlockSpec((1, MLA_KV_RANK), const2),
                  pl.BlockSpec((MLA_HEADS, MLA_Q_RANK, MLA_QK_PAD), lambda i: (0, 0, 0)),
                  pl.BlockSpec((MLA_KV_RANK, MLA_HEADS * (MLA_NOPE + MLA_V)), const2)],
        out_specs=[qk_spec, qk_spec,
                   pl.BlockSpec((1, MLA_HEADS, tm, MLA_V), lambda i: (i // tpb, 0, i % tpb, 0))],
        out_shape=[qk_shape, qk_shape, v_shape],
        compiler_params=_params(("arbitrary",), 48),
        name="mla_proj",
    )(z, z, z, pos, inv, sgn, cm, gq, gkv, wq, wkv)


ATTN_BLOCK = 1024
ATTN_DIAG_CHAINS = 2


def _attn_kernel(q_ref, k_ref, v_ref, o_ref, m_sc, l_sc, acc_sc):
    blk = ATTN_BLOCK
    nt = (((1,), (1,)), ((), ()))
    qi = pl.program_id(2)
    m_sc[...] = jnp.full(m_sc.shape, -jnp.inf, F32)
    l_sc[...] = jnp.zeros(l_sc.shape, F32)
    acc_sc[...] = jnp.zeros(acc_sc.shape, F32)

    def update(rows, s, v):
        m_prev = m_sc[rows]
        m_new = jnp.maximum(m_prev, jnp.max(s, axis=-1, keepdims=True))
        alpha = jnp.exp2(m_prev - m_new)
        ps = [jnp.exp2(s[:, c * LANES:(c + 1) * LANES] - m_new)
              for c in range(s.shape[1] // LANES)]
        l_sc[rows] = alpha * l_sc[rows] + functools.reduce(lambda a, b: a + b, ps)
        p = jnp.concatenate(ps, axis=1).astype(BF16)
        acc_sc[rows] = alpha * acc_sc[rows] + jnp.dot(p, v, preferred_element_type=F32)
        m_sc[rows] = m_new

    def body(kb, carry):
        ks = pl.multiple_of(kb * blk, blk)
        k = k_ref[0, 0, pl.ds(ks, blk), :]
        v = v_ref[0, 0, pl.ds(ks, blk), :]
        s = lax.dot_general(q_ref[0, 0], k, nt, preferred_element_type=F32)
        update(slice(0, blk), s, v)
        return carry

    lax.fori_loop(0, qi, body, 0)

    ks = pl.multiple_of(qi * blk, blk)
    sub = blk // ATTN_DIAG_CHAINS
    for r in range(ATTN_DIAG_CHAINS):
        n = (r + 1) * sub
        k = k_ref[0, 0, pl.ds(ks, n), :]
        v = v_ref[0, 0, pl.ds(ks, n), :]
        q = q_ref[0, 0, r * sub:(r + 1) * sub, :]
        s = lax.dot_general(q, k, nt, preferred_element_type=F32)
        row = lax.broadcasted_iota(jnp.int32, s.shape, 0) + r * sub
        col = lax.broadcasted_iota(jnp.int32, s.shape, 1)
        update(slice(r * sub, n), jnp.where(col <= row, s, -jnp.inf), v)
    l = jnp.sum(l_sc[...], axis=-1, keepdims=True)
    o_ref[...] = (acc_sc[...] / l).astype(o_ref.dtype)


def _attn(q, k, v, batch, seq):
    blk = ATTN_BLOCK
    nq = seq // blk
    return pl.pallas_call(
        _attn_kernel,
        grid=(batch, MLA_HEADS, nq),
        in_specs=[pl.BlockSpec((1, 1, blk, MLA_QK_PAD), lambda b, h, i: (b, h, i, 0)),
                  pl.BlockSpec((1, 1, seq, MLA_QK_PAD), lambda b, h, i: (b, h, 0, 0)),
                  pl.BlockSpec((1, 1, seq, MLA_V), lambda b, h, i: (b, h, 0, 0))],
        out_specs=pl.BlockSpec((blk, MLA_V), lambda b, h, i: (b * nq + i, h)),
        out_shape=jax.ShapeDtypeStruct((batch * seq, MLA_HEADS * MLA_V), BF16),
        scratch_shapes=[pltpu.VMEM((blk, LANES), F32), pltpu.VMEM((blk, LANES), F32),
                        pltpu.VMEM((blk, MLA_V), F32)],
        compiler_params=_params(("arbitrary", "arbitrary", "arbitrary"), 32),
        name="attn",
    )(q, k, v)


def _gla_constants():
    c = GLA_CHUNK
    idx = np.arange(c)
    w = np.zeros((GLA_LEVELS + 2, c, c), np.float32)
    masks = np.zeros((GLA_LEVELS + 1, c, c), np.float32)
    for l in range(GLA_LEVELS):
        half = 1 << l
        blk = idx // (2 * half)
        mid = blk * 2 * half + half - 1
        upper = (idx % (2 * half)) >= half
        t = idx[None, :]
        up_rows = (t > mid[:, None]) & (t <= idx[:, None])
        lo_rows = (t > idx[:, None]) & (t <= mid[:, None])
        w[l] = np.where(upper[:, None], up_rows, lo_rows)
        masks[l] = (blk[:, None] == blk[None, :]) & upper[:, None] & (~upper[None, :])
    w[GLA_LEVELS] = idx[None, :] <= idx[:, None]
    w[GLA_LEVELS + 1] = idx[None, :] > idx[:, None]
    masks[GLA_LEVELS] = np.eye(c)
    w = w.reshape(-1, c)
    return np.concatenate([w, w], axis=1), masks


def _split_bf16(x):
    hi = x.astype(BF16)
    return hi, (x - hi.astype(F32)).astype(BF16)


def _gla_kernel(q_ref, k_ref, v_ref, lr_ref, og_ref, wg_ref, bg_ref, go_ref, ws_ref, mask_ref,
                o_ref, st_ref):
    c = GLA_CHUNK
    dk, dv = GLA_DK, GLA_DV
    nt = (((1,), (1,)), ((), ()))

    @pl.when(pl.program_id(1) == 0)
    def _():
        st_ref[...] = jnp.zeros(st_ref.shape, F32)

    lr = lr_ref[...]
    for h in range(GLA_HEADS):
        x = (jnp.dot(lr, wg_ref[:, h * dk:(h + 1) * dk], preferred_element_type=F32)
             + bg_ref[:, h * dk:(h + 1) * dk])
        log_a = -(jnp.maximum(-x, 0.0) + jnp.log1p(jnp.exp(-jnp.abs(x)))) * (1.0 / GLA_TAU)
        expo = jnp.dot(ws_ref[...], jnp.concatenate(_split_bf16(log_a), axis=0),
                       preferred_element_type=F32)
        p = jnp.exp(expo)
        pb = p.astype(BF16)

        q = q_ref[:, h * dk:(h + 1) * dk] * (dk ** -0.5)
        k = k_ref[:, h * dk:(h + 1) * dk]
        v = v_ref[:, h * dv:(h + 1) * dv]
        attn = mask_ref[GLA_LEVELS] * lax.dot_general(q, k, nt, preferred_element_type=F32)
        for l in range(GLA_LEVELS):
            pl_ = pb[l * c:(l + 1) * c]
            s = lax.dot_general(q * pl_, k * pl_, nt, preferred_element_type=F32)
            attn = attn + mask_ref[l] * s
        eb = pb[GLA_LEVELS * c:(GLA_LEVELS + 1) * c]
        ebl = pb[(GLA_LEVELS + 1) * c:(GLA_LEVELS + 2) * c]
        st = st_ref[h]
        o = (lax.dot_general(q * eb, st.astype(BF16), nt, preferred_element_type=F32)
             + jnp.dot(attn.astype(BF16), v, preferred_element_type=F32))
        v_t = v.astype(F32).T.astype(BF16)
        decay = p[(GLA_LEVELS + 1) * c - 1:(GLA_LEVELS + 1) * c]
        st_ref[h] = st * decay + jnp.dot(v_t, k * ebl, preferred_element_type=F32)
        y = _rms(o, go_ref[...])
        og = og_ref[:, h * dv:(h + 1) * dv].astype(F32)
        o_ref[:, h * dv:(h + 1) * dv] = (y * (og * jax.nn.sigmoid(og))).astype(o_ref.dtype)


def _gla(z, wg, bg, go, batch, seq):
    c = GLA_CHUNK
    nc = seq // c
    ws, masks = _gla_constants()
    hk, hv = GLA_HEADS * GLA_DK, GLA_HEADS * GLA_DV
    const2 = lambda b, i: (0, 0)
    return pl.pallas_call(
        _gla_kernel,
        grid=(batch, nc),
        in_specs=[pl.BlockSpec((c, hk), lambda b, i: (b * nc + i, Z_GQ // hk)),
                  pl.BlockSpec((c, hk), lambda b, i: (b * nc + i, Z_GK // hk)),
                  pl.BlockSpec((c, hv), lambda b, i: (b * nc + i, Z_GV // hv)),
                  pl.BlockSpec((c, LANES), lambda b, i: (b * nc + i, Z_GLR // LANES)),
                  pl.BlockSpec((c, hv), lambda b, i: (b * nc + i, Z_OG // hv)),
                  pl.BlockSpec((LANES, hk), const2),
                  pl.BlockSpec((1, hk), const2),
                  pl.BlockSpec((1, GLA_DV), const2),
                  pl.BlockSpec(ws.shape, const2),
                  pl.BlockSpec(masks.shape, lambda b, i: (0, 0, 0))],
        out_specs=pl.BlockSpec((c, hv), lambda b, i: (b * nc + i, 0)),
        out_shape=jax.ShapeDtypeStruct((batch * seq, hv), BF16),
        scratch_shapes=[pltpu.VMEM((GLA_HEADS, GLA_DV, GLA_DK), F32)],
        compiler_params=_params(("arbitrary", "arbitrary"), 48),
        name="gla",
    )(z, z, z, z, z, wg, bg, go, jnp.asarray(ws, BF16), jnp.asarray(masks))


def _mix1_kernel(oa_ref, ob_ref, wa_ref, wb_ref, ga_ref, gb_ref, o_ref):
    ya = jnp.dot(oa_ref[...], wa_ref[...], preferred_element_type=F32)
    yb = jnp.dot(ob_ref[...], wb_ref[...], preferred_element_type=F32)
    o_ref[...] = (jax.nn.sigmoid(ga_ref[...].astype(F32)) * ya
                  + jax.nn.sigmoid(gb_ref[...].astype(F32)) * yb).astype(o_ref.dtype)


def _mix1(oa, ob, wa, wb, z):
    t = oa.shape[0]
    tm, tn = 1024, 512
    return pl.pallas_call(
        _mix1_kernel,
        grid=(D_MODEL // tn, t // tm),
        in_specs=[pl.BlockSpec((tm, oa.shape[1]), lambda j, i: (i, 0)),
                  pl.BlockSpec((tm, ob.shape[1]), lambda j, i: (i, 0)),
                  pl.BlockSpec((wa.shape[0], tn), lambda j, i: (0, j)),
                  pl.BlockSpec((wb.shape[0], tn), lambda j, i: (0, j)),
                  pl.BlockSpec((tm, tn), lambda j, i: (i, Z_GATE_A // tn + j)),
                  pl.BlockSpec((tm, tn), lambda j, i: (i, Z_GATE_B // tn + j))],
        out_specs=pl.BlockSpec((tm, tn), lambda j, i: (i, j)),
        out_shape=jax.ShapeDtypeStruct((t, D_MODEL), BF16),
        compiler_params=_params(("arbitrary", "arbitrary"), 48),
        name="mix1",
    )(oa, ob, wa, wb, z, z)


def _mix2_kernel(m_ref, w_ref, x_ref, gm_ref, gpost_ref, gpre_ref, sc_ref, sh_ref,
                 x1_ref, h2_ref):
    y = jnp.dot(m_ref[...], w_ref[...], preferred_element_type=F32)
    x1 = x_ref[...] + gm_ref[0] * _rms(y, gpost_ref[...])
    x1_ref[...] = x1
    h2_ref[...] = (_rms(x1, gpre_ref[...]) * (1.0 + sc_ref[0]) + sh_ref[0]).astype(h2_ref.dtype)


def _mix2(merged, w, x2, gate_m, g_post, g_pre, scale_f, shift_f, seq):
    t = merged.shape[0]
    tm = 256
    tpb = seq // tm
    row = pl.BlockSpec((tm, D_MODEL), lambda i: (i, 0))
    vec = pl.BlockSpec((1, D_MODEL), lambda i: (0, 0))
    per_batch = pl.BlockSpec((1, 1, D_MODEL), lambda i: (i // tpb, 0, 0))
    return pl.pallas_call(
        _mix2_kernel,
        grid=(t // tm,),
        in_specs=[row, pl.BlockSpec((D_MODEL, D_MODEL), lambda i: (0, 0)), row,
                  per_batch, vec, vec, per_batch, per_batch],
        out_specs=[row, row],
        out_shape=[jax.ShapeDtypeStruct((t, D_MODEL), F32),
                   jax.ShapeDtypeStruct((t, D_MODEL), BF16)],
        compiler_params=_params(("arbitrary",), 48),
        name="mix2",
    )(merged, w, x2, gate_m, g_post, g_pre, scale_f, shift_f)


def _ffn_up_kernel(h_ref, wa_ref, wv_ref, cwa_ref, cwv_ref, cba_ref, cbv_ref, o_ref,
                   wa_bf, wv_bf, halo_a, halo_v, *, tiles_per_batch):
    i = pl.program_id(1)
    tm = h_ref.shape[0]

    @pl.when(i == 0)
    def _():
        wa_bf[...] = wa_ref[...].astype(BF16)
        wv_bf[...] = wv_ref[...].astype(BF16)

    @pl.when(i % tiles_per_batch == 0)
    def _():
        halo_a[...] = jnp.zeros(halo_a.shape, F32)
        halo_v[...] = jnp.zeros(halo_v.shape, F32)

    h = h_ref[...]

    def conv(w_ref, cw_ref, cb_ref, halo):
        u = jnp.dot(h, w_ref[...], preferred_element_type=F32)
        cw = cw_ref[...]
        w0, w1, w2 = cw[0:1], cw[1:2], cw[2:3]
        body = (u * w2 + pltpu.roll(u, 1, 0) * w1 + pltpu.roll(u, 2, 0) * w0) + cb_ref[...]
        u8 = u[0:SUBLANES]
        prev8 = halo[...]
        r = lax.broadcasted_iota(jnp.int32, u8.shape, 0)
        s1 = jnp.where(r < 1, pltpu.roll(prev8, 1, 0), pltpu.roll(u8, 1, 0))
        s2 = jnp.where(r < 2, pltpu.roll(prev8, 2, 0), pltpu.roll(u8, 2, 0))
        top = (u8 * w2 + s1 * w1 + s2 * w0) + cb_ref[...]
        halo[...] = u[tm - SUBLANES:tm]
        return body, top

    a, a_top = conv(wa_bf, cwa_ref, cba_ref, halo_a)
    val, val_top = conv(wv_bf, cwv_ref, cbv_ref, halo_v)
    o_ref[...] = (jax.nn.gelu(a, approximate=True) * val).astype(o_ref.dtype)
    o_ref[0:SUBLANES, :] = (jax.nn.gelu(a_top, approximate=True) * val_top).astype(o_ref.dtype)


def _ffn_up(h2, w_up, conv_w, conv_b, seq):
    t = h2.shape[0]
    tm, tn = 1024, 512
    nj = D_FF // tn
    return pl.pallas_call(
        functools.partial(_ffn_up_kernel, tiles_per_batch=seq // tm),
        grid=(nj, t // tm),
        in_specs=[pl.BlockSpec((tm, D_MODEL), lambda j, i: (i, 0)),
                  pl.BlockSpec((D_MODEL, tn), lambda j, i: (0, j)),
                  pl.BlockSpec((D_MODEL, tn), lambda j, i: (0, nj + j)),
                  pl.BlockSpec((CONV_WIDTH, tn), lambda j, i: (0, j)),
                  pl.BlockSpec((CONV_WIDTH, tn), lambda j, i: (0, nj + j)),
                  pl.BlockSpec((1, tn), lambda j, i: (0, j)),
                  pl.BlockSpec((1, tn), lambda j, i: (0, nj + j))],
        out_specs=pl.BlockSpec((tm, tn), lambda j, i: (i, j)),
        out_shape=jax.ShapeDtypeStruct((t, D_FF), BF16),
        scratch_shapes=[pltpu.VMEM((D_MODEL, tn), BF16), pltpu.VMEM((D_MODEL, tn), BF16),
                        pltpu.VMEM((SUBLANES, tn), F32), pltpu.VMEM((SUBLANES, tn), F32)],
        compiler_params=_params(("arbitrary", "arbitrary"), 56),
        name="ffn_up",
    )(h2, w_up, w_up, conv_w, conv_w, conv_b, conv_b)


def _ffn_down_kernel(g_ref, w_ref, x1_ref, gf_ref, gpost_ref, o_ref):
    f = jnp.dot(g_ref[...], w_ref[...], preferred_element_type=F32)
    o_ref[...] = x1_ref[...] + gf_ref[0] * _rms(f, gpost_ref[...])


def _ffn_down(g, w, x1, gate_f, g_post, seq):
    t = g.shape[0]
    tm = 256
    tpb = seq // tm
    return pl.pallas_call(
        _ffn_down_kernel,
        grid=(t // tm,),
        in_specs=[pl.BlockSpec((tm, D_FF), lambda i: (i, 0)),
                  pl.BlockSpec((D_FF, D_MODEL), lambda i: (0, 0), pipeline_mode=pl.Buffered(1)),
                  pl.BlockSpec((tm, D_MODEL), lambda i: (i, 0)),
                  pl.BlockSpec((1, 1, D_MODEL), lambda i: (i // tpb, 0, 0)),
                  pl.BlockSpec((1, D_MODEL), lambda i: (0, 0))],
        out_specs=pl.BlockSpec((tm, D_MODEL), lambda i: (i, 0)),
        out_shape=jax.ShapeDtypeStruct((t, D_MODEL), F32),
        compiler_params=_params(("arbitrary",), 56),
        name="ffn_down",
    )(g, w, x1, gate_f, g_post)


def _swap_halves(w):
    half = w.shape[-1] // 2
    return jnp.concatenate([w[..., half:], w[..., :half]], axis=-1)


_SRC_KROPE = MLA_Q_RANK + MLA_KV_RANK
_SRC_GQ = _SRC_KROPE + MLA_ROPE
_SRC_GK = _SRC_GQ + GLA_HEADS * GLA_DK
_SRC_GV = _SRC_GK + GLA_HEADS * GLA_DK
_SRC_GLR = _SRC_GV + GLA_HEADS * GLA_DV
_SRC_OG = _SRC_GLR + GLA_GATE_RANK
_SRC_GATE_A = _SRC_OG + GLA_HEADS * GLA_DV
_SRC_GATE_B = _SRC_GATE_A + D_MODEL
D_IN = _SRC_GATE_B + D_MODEL
_W_IN_MOVES = (
    (Z_QLAT, 0, _SRC_KROPE),
    (Z_KROPE, _SRC_KROPE, MLA_ROPE),
    (Z_KROPE + MLA_ROPE, _SRC_KROPE + MLA_ROPE // 2, MLA_ROPE // 2),
    (Z_KROPE + 3 * MLA_ROPE // 2, _SRC_KROPE, MLA_ROPE // 2),
    (Z_GLR, _SRC_GLR, GLA_GATE_RANK),
    (Z_GLR + GLA_GATE_RANK, _SRC_GLR, GLA_GATE_RANK),
    (Z_GQ, _SRC_GQ, GLA_HEADS * GLA_DK),
    (Z_GV, _SRC_GV, GLA_HEADS * GLA_DV),
    (Z_OG, _SRC_OG, GLA_HEADS * GLA_DV),
    (Z_GATE_A, _SRC_GATE_A, D_MODEL),
    (Z_GATE_B, _SRC_GATE_B, D_MODEL),
    (Z_GK, _SRC_GK, GLA_HEADS * GLA_DK),
)


def _prep_w_in_kernel(w_ref, o_ref):
    pad0 = Z_GLR + 2 * GLA_GATE_RANK
    o_ref[:, pad0:Z_GQ] = jnp.zeros((o_ref.shape[0], Z_GQ - pad0), o_ref.dtype)
    for dst, src, width in _W_IN_MOVES:
        o_ref[:, dst:dst + width] = w_ref[:, src:src + width].astype(o_ref.dtype)


def _prep_w_in(w_in):
    tk = 256
    return pl.pallas_call(
        _prep_w_in_kernel,
        grid=(D_MODEL // tk,),
        in_specs=[pl.BlockSpec((tk, D_IN), lambda i: (i, 0))],
        out_specs=pl.BlockSpec((tk, Z_WIDTH), lambda i: (i, 0)),
        out_shape=jax.ShapeDtypeStruct((D_MODEL, Z_WIDTH), BF16),
        compiler_params=_params(("arbitrary",), 48),
        name="prep_w_in",
    )(w_in)


def _prep_w_q(w_q_up):
    w = w_q_up.reshape(MLA_Q_RANK, MLA_HEADS, MLA_NOPE + MLA_ROPE)
    pe = w[:, :, MLA_NOPE:]
    w = jnp.concatenate([w, _swap_halves(pe)], axis=-1)
    return jnp.transpose(w, (1, 0, 2)).astype(BF16)


def kernel(x, c, positions, w_ada, b_ada, g_pre_mix, w_in, g_q_lat, w_q_up, g_kv_lat, w_kv_up,
           w_gla_gate_up, b_gla_gate, g_gla_out, w_branch_a, w_branch_b, w_mix_out, g_post_mix,
           g_pre_ffn, w_ffn_up, conv_w, conv_b, w_ffn_down, g_post_ffn):
    batch, seq, _ = x.shape
    depth = w_ada.shape[0]
    t = batch * seq
    row = lambda v: v.reshape(1, -1)

    inv = 1.0 / (ROPE_THETA ** (jnp.arange(0, MLA_ROPE, 2, dtype=F32) / MLA_ROPE))
    zeros64 = jnp.zeros((MLA_ROPE,), F32)
    ones32 = jnp.ones((MLA_ROPE // 2,), F32)
    inv128 = row(jnp.concatenate([inv, inv, zeros64]))
    sgn128 = row(jnp.concatenate([-ones32, ones32, zeros64]))
    cm128 = row(jnp.concatenate([ones32, ones32, zeros64]))
    pos = positions.reshape(t, 1)

    c_pad = jnp.zeros((SUBLANES, D_MODEL), F32).at[:batch].set(c)
    x2 = x.reshape(t, D_MODEL)
    for l in range(depth):
        mod = _ada(c_pad, w_ada[l], row(b_ada[l]))[:batch]
        shift_m, scale_m, gate_m, shift_f, scale_f, gate_f = (
            m.reshape(batch, 1, D_MODEL) for m in jnp.split(mod, 6, axis=-1))

        h = _norm_mod(x2, row(g_pre_mix[l]), scale_m, shift_m, seq)
        z = _mm(h, _prep_w_in(w_in[l]), BF16, 1024, 1024, "mm_in")

        q, k, v = _mla_proj(z, pos, inv128, sgn128, cm128, row(g_q_lat[l]), row(g_kv_lat[l]),
                            _prep_w_q(w_q_up[l]), w_kv_up[l].astype(BF16), batch, seq)
        o_a = _attn(q, k, v, batch, seq)

        wg_hi, wg_lo = _split_bf16(w_gla_gate_up[l])
        wg = jnp.zeros((LANES, GLA_HEADS * GLA_DK), BF16)
        wg = wg.at[:GLA_GATE_RANK].set(wg_hi).at[GLA_GATE_RANK:2 * GLA_GATE_RANK].set(wg_lo)
        o_b = _gla(z, wg, row(b_gla_gate[l]), row(g_gla_out[l]), batch, seq)

        merged = _mix1(o_a, o_b, w_branch_a[l].astype(BF16), w_branch_b[l].astype(BF16), z)
        x1, h2 = _mix2(merged, w_mix_out[l].astype(BF16), x2, gate_m, row(g_post_mix[l]),
                       row(g_pre_ffn[l]), scale_f, shift_f, seq)

        g = _ffn_up(h2, w_ffn_up[l], conv_w[l], row(conv_b[l]), seq)
        x2 = _ffn_down(g, w_ffn_down[l].astype(BF16), x1, gate_f, row(g_post_ffn[l]), seq)
    return x2.reshape(batch, seq, D_MODEL)
```

```python
import functools

import jax
import jax.numpy as jnp
import numpy as np
from jax import lax
from jax.experimental import pallas as pl
from jax.experimental.pallas import tpu as pltpu

F32 = jnp.float32
BF16 = jnp.bfloat16

D_MODEL = 2048
MLA_HEADS = 8
MLA_Q_RANK = 512
MLA_KV_RANK = 256
MLA_NOPE = 128
MLA_ROPE = 64
MLA_V = 128
ROPE_THETA = 10000.0
GLA_HEADS = 4
GLA_DK = 256
GLA_DV = 512
GLA_GATE_RANK = 16
GLA_TAU = 16.0
D_FF = 5632
CONV_WIDTH = 3
EPS = 1e-6

LANES = 128
SUBLANES = 8
MLA_QK_PAD = 256

Z_QLAT = 0
Z_KVLAT = 512
Z_KROPE = 768
Z_GLR = 896
Z_GQ = 1024
Z_GV = 2048
Z_OG = 4096
Z_GATE_A = 6144
Z_GATE_B = 8192
Z_GK = 10240
Z_WIDTH = 11264

GLA_CHUNK = 128
GLA_LEVELS = 7

MIB = 1024 * 1024


def _params(semantics, vmem_mib):
    return pltpu.CompilerParams(dimension_semantics=semantics,
                                vmem_limit_bytes=vmem_mib * MIB)


def _rms(x, g):
    return x * lax.rsqrt(jnp.mean(x * x, axis=-1, keepdims=True) + EPS) * g


def _ada_kernel(c_ref, w_ref, b_ref, o_ref):
    c = c_ref[...]
    ca = c * jax.nn.sigmoid(c)
    o_ref[...] = jnp.dot(ca.astype(BF16), w_ref[...].astype(BF16),
                         preferred_element_type=F32) + b_ref[...]


def _ada(c_pad, w, b):
    n = w.shape[1]
    tn = 1024
    return pl.pallas_call(
        _ada_kernel,
        grid=(n // tn,),
        in_specs=[pl.BlockSpec((SUBLANES, D_MODEL), lambda j: (0, 0)),
                  pl.BlockSpec((D_MODEL, tn), lambda j: (0, j)),
                  pl.BlockSpec((1, tn), lambda j: (0, j))],
        out_specs=pl.BlockSpec((SUBLANES, tn), lambda j: (0, j)),
        out_shape=jax.ShapeDtypeStruct((SUBLANES, n), F32),
        compiler_params=_params(("arbitrary",), 32),
        name="ada",
    )(c_pad, w, b)


def _norm_mod_kernel(x_ref, g_ref, sc_ref, sh_ref, o_ref):
    y = _rms(x_ref[...], g_ref[...])
    o_ref[...] = (y * (1.0 + sc_ref[0]) + sh_ref[0]).astype(o_ref.dtype)


def _norm_mod(x2, g, scale, shift, seq):
    t = x2.shape[0]
    tm = 512
    tpb = seq // tm
    return pl.pallas_call(
        _norm_mod_kernel,
        grid=(t // tm,),
        in_specs=[pl.BlockSpec((tm, D_MODEL), lambda i: (i, 0)),
                  pl.BlockSpec((1, D_MODEL), lambda i: (0, 0)),
                  pl.BlockSpec((1, 1, D_MODEL), lambda i: (i // tpb, 0, 0)),
                  pl.BlockSpec((1, 1, D_MODEL), lambda i: (i // tpb, 0, 0))],
        out_specs=pl.BlockSpec((tm, D_MODEL), lambda i: (i, 0)),
        out_shape=jax.ShapeDtypeStruct((t, D_MODEL), BF16),
        compiler_params=_params(("arbitrary",), 32),
        name="norm_mod",
    )(x2, g, scale, shift)


def _mm_kernel(a_ref, w_ref, o_ref):
    o_ref[...] = lax.dot_general(a_ref[...], w_ref[...], (((1,), (1,)), ((), ())),
                                 preferred_element_type=F32).astype(o_ref.dtype)


def _mm(a, w_t, out_dtype, tm, tn, name):
    m, k = a.shape
    n = w_t.shape[0]
    return pl.pallas_call(
        _mm_kernel,
        grid=(n // tn, m // tm),
        in_specs=[pl.BlockSpec((tm, k), lambda j, i: (i, 0)),
                  pl.BlockSpec((tn, k), lambda j, i: (j, 0))],
        out_specs=pl.BlockSpec((tm, tn), lambda j, i: (i, j)),
        out_shape=jax.ShapeDtypeStruct((m, n), out_dtype),
        compiler_params=_params(("arbitrary", "arbitrary"), 48),
        name=name,
    )(a, w_t)


def _mla_proj_kernel(ql_ref, kvl_ref, kr_ref, pos_ref, inv_ref, sgn_ref, cm_ref,
                     gq_ref, gkv_ref, wq_ref, wkv_ref, q_ref, k_ref, v_ref):
    ang = pos_ref[...].astype(F32) * inv_ref[...]
    cosm = jnp.cos(ang) * cm_ref[...]
    sinm = jnp.sin(ang) * sgn_ref[...]

    def rope(r):
        return r * cosm + pltpu.roll(r, MLA_ROPE, 1) * sinm

    scale = (MLA_NOPE + MLA_ROPE) ** -0.5 * np.log2(np.e)
    qn =_rms(ql_ref[...].astype(F32), gq_ref[...]).astype(BF16)
    for h in range(MLA_HEADS):
        r = jnp.dot(qn, wq_ref[h], preferred_element_type=F32) * scale
        q_ref[0, h, :, 0:MLA_NOPE] = r[:, 0:MLA_NOPE].astype(BF16)
        q_ref[0, h, :, MLA_NOPE:MLA_QK_PAD] = rope(r[:, MLA_NOPE:MLA_QK_PAD]).astype(BF16)

    kvn = _rms(kvl_ref[...].astype(F32), gkv_ref[...]).astype(BF16)
    kv = jnp.dot(kvn, wkv_ref[...], preferred_element_type=F32).astype(BF16)
    kpe = rope(kr_ref[...].astype(F32)).astype(BF16)
    hw = MLA_NOPE + MLA_V
    for h in range(MLA_HEADS):
        k_ref[0, h, :, 0:MLA_NOPE] = kv[:, h * hw:h * hw + MLA_NOPE]
        k_ref[0, h, :, MLA_NOPE:MLA_QK_PAD] = kpe
        v_ref[0, h] = kv[:, h * hw + MLA_NOPE:(h + 1) * hw]


def _mla_proj(z, pos, inv, sgn, cm, gq, gkv, wq, wkv, batch, seq):
    tm = 512
    tpb = seq // tm
    qk_shape = jax.ShapeDtypeStruct((batch, MLA_HEADS, seq, MLA_QK_PAD), BF16)
    v_shape = jax.ShapeDtypeStruct((batch, MLA_HEADS, seq, MLA_V), BF16)
    const2 = lambda i: (0, 0)
    qk_spec = pl.BlockSpec((1, MLA_HEADS, tm, MLA_QK_PAD), lambda i: (i // tpb, 0, i % tpb, 0))
    return pl.pallas_call(
        _mla_proj_kernel,
        grid=(batch * tpb,),
        in_specs=[pl.BlockSpec((tm, MLA_Q_RANK), lambda i: (i, Z_QLAT // MLA_Q_RANK)),
                  pl.BlockSpec((tm, MLA_KV_RANK), lambda i: (i, Z_KVLAT // MLA_KV_RANK)),
                  pl.BlockSpec((tm, LANES), lambda i: (i, Z_KROPE // LANES)),
                  pl.BlockSpec((tm, 1), lambda i: (i, 0)),
                  pl.BlockSpec((1, LANES), const2),
                  pl.BlockSpec((1, LANES), const2),
                  pl.BlockSpec((1, LANES), const2),
                  pl.BlockSpec((1, MLA_Q_RANK), const2),
                  pl.BlockSpec((1, MLA_KV_RANK), const2),
                  pl.BlockSpec((MLA_HEADS, MLA_Q_RANK, MLA_QK_PAD), lambda i: (0, 0, 0)),
                  pl.BlockSpec((MLA_KV_RANK, MLA_HEADS * (MLA_NOPE + MLA_V)), const2)],
        out_specs=[qk_spec, qk_spec,
                   pl.BlockSpec((1, MLA_HEADS, tm, MLA_V), lambda i: (i // tpb, 0, i % tpb, 0))],
        out_shape=[qk_shape, qk_shape, v_shape],
        compiler_params=_params(("arbitrary",), 48),
        name="mla_proj",
    )(z, z, z, pos, inv, sgn, cm, gq, gkv, wq, wkv)


ATTN_BLOCK = 1024
ATTN_DIAG_CHAINS = 2


def _attn_kernel(q_ref, k_ref, v_ref, o_ref, m_sc, l_sc, acc_sc):
    blk = ATTN_BLOCK
    nt = (((1,), (1,)), ((), ()))
    qi = pl.program_id(2)
    m_sc[...] = jnp.full(m_sc.shape, -jnp.inf, F32)
    l_sc[...] = jnp.zeros(l_sc.shape, F32)
    acc_sc[...] = jnp.zeros(acc_sc.shape, F32)

    def update(rows, s, v):
        m_prev = m_sc[rows]
        m_new = jnp.maximum(m_prev, jnp.max(s, axis=-1, keepdims=True))
        alpha = jnp.exp2(m_prev - m_new)
        ps = [jnp.exp2(s[:, c * LANES:(c + 1) * LANES] - m_new)
              for c in range(s.shape[1] // LANES)]
        l_sc[rows] = alpha * l_sc[rows] + functools.reduce(lambda a, b: a + b, ps)
        p = jnp.concatenate(ps, axis=1).astype(BF16)
        acc_sc[rows] = alpha * acc_sc[rows] + jnp.dot(p, v, preferred_element_type=F32)
        m_sc[rows] = m_new

    def body(kb, carry):
        ks = pl.multiple_of(kb * blk, blk)
        k = k_ref[0, 0, pl.ds(ks, blk), :]
        v = v_ref[0, 0, pl.ds(ks, blk), :]
        s = lax.dot_general(q_ref[0, 0], k, nt, preferred_element_type=F32)
        update(slice(0, blk), s, v)
        return carry

    lax.fori_loop(0, qi, body, 0)

    ks = pl.multiple_of(qi * blk, blk)
    sub = blk // ATTN_DIAG_CHAINS
    for r in range(ATTN_DIAG_CHAINS):
        n = (r + 1) * sub
        k = k_ref[0, 0, pl.ds(ks, n), :]
        v = v_ref[0, 0, pl.ds(ks, n), :]
        q = q_ref[0, 0, r * sub:(r + 1) * sub, :]
        s = lax.dot_general(q, k, nt, preferred_element_type=F32)
        row = lax.broadcasted_iota(jnp.int32, s.shape, 0) + r * sub
        col = lax.broadcasted_iota(jnp.int32, s.shape, 1)
        update(slice(r * sub, n), jnp.where(col <= row, s, -jnp.inf), v)
    l = jnp.sum(l_sc[...], axis=-1, keepdims=True)
    o_ref[...] = (acc_sc[...] / l).astype(o_ref.dtype)


def _attn(q, k, v, batch, seq):
    blk = ATTN_BLOCK
    nq = seq // blk
    return pl.pallas_call(
        _attn_kernel,
        grid=(batch, MLA_HEADS, nq),
        in_specs=[pl.BlockSpec((1, 1, blk, MLA_QK_PAD), lambda b, h, i: (b, h, i, 0)),
                  pl.BlockSpec((1, 1, seq, MLA_QK_PAD), lambda b, h, i: (b, h, 0, 0)),
                  pl.BlockSpec((1, 1, seq, MLA_V), lambda b, h, i: (b, h, 0, 0))],
        out_specs=pl.BlockSpec((blk, MLA_V), lambda b, h, i: (b * nq + i, h)),
        out_shape=jax.ShapeDtypeStruct((batch * seq, MLA_HEADS * MLA_V), BF16),
        scratch_shapes=[pltpu.VMEM((blk, LANES), F32), pltpu.VMEM((blk, LANES), F32),
                        pltpu.VMEM((blk, MLA_V), F32)],
        compiler_params=_params(("arbitrary", "arbitrary", "arbitrary"), 32),
        name="attn",
    )(q, k, v)


def _gla_constants():
    c = GLA_CHUNK
    idx = np.arange(c)
    w = np.zeros((GLA_LEVELS + 2, c, c), np.float32)
    masks = np.zeros((GLA_LEVELS + 1, c, c), np.float32)
    for l in range(GLA_LEVELS):
        half = 1 << l
        blk = idx // (2 * half)
        mid = blk * 2 * half + half - 1
        upper = (idx % (2 * half)) >= half
        t = idx[None, :]
        up_rows = (t > mid[:, None]) & (t <= idx[:, None])
        lo_rows = (t > idx[:, None]) & (t <= mid[:, None])
        w[l] = np.where(upper[:, None], up_rows, lo_rows)
        masks[l] = (blk[:, None] == blk[None, :]) & upper[:, None] & (~upper[None, :])
    w[GLA_LEVELS] = idx[None, :] <= idx[:, None]
    w[GLA_LEVELS + 1] = idx[None, :] > idx[:, None]
    masks[GLA_LEVELS] = np.eye(c)
    w = w.reshape(-1, c)
    return np.concatenate([w, w], axis=1), masks


def _split_bf16(x):
    hi = x.astype(BF16)
    return hi, (x - hi.astype(F32)).astype(BF16)


def _gla_kernel(q_ref, k_ref, v_ref, lr_ref, og_ref, wg_ref, bg_ref, go_ref, ws_ref, mask_ref,
                o_ref, st_ref):
    c = GLA_CHUNK
    dk, dv = GLA_DK, GLA_DV
    nt = (((1,), (1,)), ((), ()))

    @pl.when(pl.program_id(1) == 0)
    def _():
        st_ref[...] = jnp.zeros(st_ref.shape, F32)

    lr = lr_ref[...]
    for h in range(GLA_HEADS):
        x = (jnp.dot(lr, wg_ref[:, h * dk:(h + 1) * dk], preferred_element_type=F32)
             + bg_ref[:, h * dk:(h + 1) * dk])
        log_a = -(jnp.maximum(-x, 0.0) + jnp.log1p(jnp.exp(-jnp.abs(x)))) * (1.0 / GLA_TAU)
        expo = jnp.dot(ws_ref[...], jnp.concatenate(_split_bf16(log_a), axis=0),
                       preferred_element_type=F32)
        p = jnp.exp(expo)
        pb = p.astype(BF16)

        q = q_ref[:, h * dk:(h + 1) * dk] * (dk ** -0.5)
        k = k_ref[:, h * dk:(h + 1) * dk]
        v = v_ref[:, h * dv:(h + 1) * dv]
        attn = mask_ref[GLA_LEVELS] * lax.dot_general(q, k, nt, preferred_element_type=F32)
        for l in range(GLA_LEVELS):
            pl_ = pb[l * c:(l + 1) * c]
            s = lax.dot_general(q * pl_, k * pl_, nt, preferred_element_type=F32)
            attn = attn + mask_ref[l] * s
        eb = pb[GLA_LEVELS * c:(GLA_LEVELS + 1) * c]
        ebl = pb[(GLA_LEVELS + 1) * c:(GLA_LEVELS + 2) * c]
        st = st_ref[h]
        o = (lax.dot_general(q * eb, st.astype(BF16), nt, preferred_element_type=F32)
             + jnp.dot(attn.astype(BF16), v, preferred_element_type=F32))
        v_t = v.astype(F32).T.astype(BF16)
        decay = p[(GLA_LEVELS + 1) * c - 1:(GLA_LEVELS + 1) * c]
        st_ref[h] = st * decay + jnp.dot(v_t, k * ebl, preferred_element_type=F32)
        y = _rms(o, go_ref[...])
        og = og_ref[:, h * dv:(h + 1) * dv].astype(F32)
        o_ref[:, h * dv:(h + 1) * dv] = (y * (og * jax.nn.sigmoid(og))).astype(o_ref.dtype)


def _gla(z, wg, bg, go, batch, seq):
    c = GLA_CHUNK
    nc = seq // c
    ws, masks = _gla_constants()
    hk, hv = GLA_HEADS * GLA_DK, GLA_HEADS * GLA_DV
    const2 = lambda b, i: (0, 0)
    return pl.pallas_call(
        _gla_kernel,
        grid=(batch, nc),
        in_specs=[pl.BlockSpec((c, hk), lambda b, i: (b * nc + i, Z_GQ // hk)),
                  pl.BlockSpec((c, hk), lambda b, i: (b * nc + i, Z_GK // hk)),
                  pl.BlockSpec((c, hv), lambda b, i: (b * nc + i, Z_GV // hv)),
                  pl.BlockSpec((c, LANES), lambda b, i: (b * nc + i, Z_GLR // LANES)),
                  pl.BlockSpec((c, hv), lambda b, i: (b * nc + i, Z_OG // hv)),
                  pl.BlockSpec((LANES, hk), const2),
                  pl.BlockSpec((1, hk), const2),
                  pl.BlockSpec((1, GLA_DV), const2),
                  pl.BlockSpec(ws.shape, const2),
                  pl.BlockSpec(masks.shape, lambda b, i: (0, 0, 0))],
        out_specs=pl.BlockSpec((c, hv), lambda b, i: (b * nc + i, 0)),
        out_shape=jax.ShapeDtypeStruct((batch * seq, hv), BF16),
        scratch_shapes=[pltpu.VMEM((GLA_HEADS, GLA_DV, GLA_DK), F32)],
        compiler_params=_params(("arbitrary", "arbitrary"), 48),
        name="gla",
    )(z, z, z, z, z, wg, bg, go, jnp.asarray(ws, BF16), jnp.asarray(masks))


def _mix1_kernel(oa_ref, ob_ref, wa_ref, wb_ref, ga_ref, gb_ref, o_ref):
    ya = jnp.dot(oa_ref[...], wa_ref[...], preferred_element_type=F32)
    yb = jnp.dot(ob_ref[...], wb_ref[...], preferred_element_type=F32)
    o_ref[...] = (jax.nn.sigmoid(ga_ref[...].astype(F32)) * ya
                  + jax.nn.sigmoid(gb_ref[...].astype(F32)) * yb).astype(o_ref.dtype)


def _mix1(oa, ob, wa, wb, z):
    t = oa.shape[0]
    tm, tn = 1024, 512
    return pl.pallas_call(
        _mix1_kernel,
        grid=(D_MODEL // tn, t // tm),
        in_specs=[pl.BlockSpec((tm, oa.shape[1]), lambda j, i: (i, 0)),
                  pl.BlockSpec((tm, ob.shape[1]), lambda j, i: (i, 0)),
                  pl.BlockSpec((wa.shape[0], tn), lambda j, i: (0, j)),
                  pl.BlockSpec((wb.shape[0], tn), lambda j, i: (0, j)),
                  pl.BlockSpec((tm, tn), lambda j, i: (i, Z_GATE_A // tn + j)),
                  pl.BlockSpec((tm, tn), lambda j, i: (i, Z_GATE_B // tn + j))],
        out_specs=pl.BlockSpec((tm, tn), lambda j, i: (i, j)),
        out_shape=jax.ShapeDtypeStruct((t, D_MODEL), BF16),
        compiler_params=_params(("arbitrary", "arbitrary"), 48),
        name="mix1",
    )(oa, ob, wa, wb, z, z)


def _mix2_kernel(m_ref, w_ref, x_ref, gm_ref, gpost_ref, gpre_ref, sc_ref, sh_ref,
                 x1_ref, h2_ref):
    y = jnp.dot(m_ref[...], w_ref[...], preferred_element_type=F32)
    x1 = x_ref[...] + gm_ref[0] * _rms(y, gpost_ref[...])
    x1_ref[...] = x1
    h2_ref[...] = (_rms(x1, gpre_ref[...]) * (1.0 + sc_ref[0]) + sh_ref[0]).astype(h2_ref.dtype)


def _mix2(merged, w, x2, gate_m, g_post, g_pre, scale_f, shift_f, seq):
    t = merged.shape[0]
    tm = 256
    tpb = seq // tm
    row = pl.BlockSpec((tm, D_MODEL), lambda i: (i, 0))
    vec = pl.BlockSpec((1, D_MODEL), lambda i: (0, 0))
    per_batch = pl.BlockSpec((1, 1, D_MODEL), lambda i: (i // tpb, 0, 0))
    return pl.pallas_call(
        _mix2_kernel,
        grid=(t // tm,),
        in_specs=[row, pl.BlockSpec((D_MODEL, D_MODEL), lambda i: (0, 0)), row,
                  per_batch, vec, vec, per_batch, per_batch],
        out_specs=[row, row],
        out_shape=[jax.ShapeDtypeStruct((t, D_MODEL), F32),
                   jax.ShapeDtypeStruct((t, D_MODEL), BF16)],
        compiler_params=_params(("arbitrary",), 48),
        name="mix2",
    )(merged, w, x2, gate_m, g_post, g_pre, scale_f, shift_f)


def _ffn_up_kernel(h_ref, wa_ref, wv_ref, cwa_ref, cwv_ref, cba_ref, cbv_ref, o_ref,
                   wa_bf, wv_bf, halo_a, halo_v, *, tiles_per_batch):
    i = pl.program_id(1)
    tm = h_ref.shape[0]

    @pl.when(i == 0)
    def _():
        wa_bf[...] = wa_ref[...].astype(BF16)
        wv_bf[...] = wv_ref[...].astype(BF16)

    @pl.when(i % tiles_per_batch == 0)
    def _():
        halo_a[...] = jnp.zeros(halo_a.shape, F32)
        halo_v[...] = jnp.zeros(halo_v.shape, F32)

    h = h_ref[...]

    def conv(w_ref, cw_ref, cb_ref, halo):
        u = jnp.dot(h, w_ref[...], preferred_element_type=F32)
        cw = cw_ref[...]
        w0, w1, w2 = cw[0:1], cw[1:2], cw[2:3]
        body = (u * w2 + pltpu.roll(u, 1, 0) * w1 + pltpu.roll(u, 2, 0) * w0) + cb_ref[...]
        u8 = u[0:SUBLANES]
        prev8 = halo[...]
        r = lax.broadcasted_iota(jnp.int32, u8.shape, 0)
        s1 = jnp.where(r < 1, pltpu.roll(prev8, 1, 0), pltpu.roll(u8, 1, 0))
        s2 = jnp.where(r < 2, pltpu.roll(prev8, 2, 0), pltpu.roll(u8, 2, 0))
        top = (u8 * w2 + s1 * w1 + s2 * w0) + cb_ref[...]
        halo[...] = u[tm - SUBLANES:tm]
        return body, top

    a, a_top = conv(wa_bf, cwa_ref, cba_ref, halo_a)
    val, val_top = conv(wv_bf, cwv_ref, cbv_ref, halo_v)
    o_ref[...] = (jax.nn.gelu(a, approximate=True) * val).astype(o_ref.dtype)
    o_ref[0:SUBLANES, :] = (jax.nn.gelu(a_top, approximate=True) * val_top).astype(o_ref.dtype)


def _ffn_up(h2, w_up, conv_w, conv_b, seq):
    t = h2.shape[0]
    tm, tn = 1024, 512
    nj = D_FF // tn
    return pl.pallas_call(
        functools.partial(_ffn_up_kernel, tiles_per_batch=seq // tm),
        grid=(nj, t // tm),
        in_specs=[pl.BlockSpec((tm, D_MODEL), lambda j, i: (i, 0)),
                  pl.BlockSpec((D_MODEL, tn), lambda j, i: (0, j)),
                  pl.BlockSpec((D_MODEL, tn), lambda j, i: (0, nj + j)),
                  pl.BlockSpec((CONV_WIDTH, tn), lambda j, i: (0, j)),
                  pl.BlockSpec((CONV_WIDTH, tn), lambda j, i: (0, nj + j)),
                  pl.BlockSpec((1, tn), lambda j, i: (0, j)),
                  pl.BlockSpec((1, tn), lambda j, i: (0, nj + j))],
        out_specs=pl.BlockSpec((tm, tn), lambda j, i: (i, j)),
        out_shape=jax.ShapeDtypeStruct((t, D_FF), BF16),
        scratch_shapes=[pltpu.VMEM((D_MODEL, tn), BF16), pltpu.VMEM((D_MODEL, tn), BF16),
                        pltpu.VMEM((SUBLANES, tn), F32), pltpu.VMEM((SUBLANES, tn), F32)],
        compiler_params=_params(("arbitrary", "arbitrary"), 56),
        name="ffn_up",
    )(h2, w_up, w_up, conv_w, conv_w, conv_b, conv_b)


def _ffn_down_kernel(g_ref, w_ref, x1_ref, gf_ref, gpost_ref, o_ref):
    f = jnp.dot(g_ref[...], w_ref[...], preferred_element_type=F32)
    o_ref[...] = x1_ref[...] + gf_ref[0] * _rms(f, gpost_ref[...])


def _ffn_down(g, w, x1, gate_f, g_post, seq):
    t = g.shape[0]
    tm = 256
    tpb = seq // tm
    return pl.pallas_call(
        _ffn_down_kernel,
        grid=(t // tm,),
        in_specs=[pl.BlockSpec((tm, D_FF), lambda i: (i, 0)),
                  pl.BlockSpec((D_FF, D_MODEL), lambda i: (0, 0), pipeline_mode=pl.Buffered(1)),
                  pl.BlockSpec((tm, D_MODEL), lambda i: (i, 0)),
                  pl.BlockSpec((1, 1, D_MODEL), lambda i: (i // tpb, 0, 0)),
                  pl.BlockSpec((1, D_MODEL), lambda i: (0, 0))],
        out_specs=pl.BlockSpec((tm, D_MODEL), lambda i: (i, 0)),
        out_shape=jax.ShapeDtypeStruct((t, D_MODEL), F32),
        compiler_params=_params(("arbitrary",), 56),
        name="ffn_down",
    )(g, w, x1, gate_f, g_post)


def _swap_halves(w):
    half = w.shape[-1] // 2
    return jnp.concatenate([w[..., half:], w[..., :half]], axis=-1)


_SRC_KROPE = MLA_Q_RANK + MLA_KV_RANK
_SRC_GQ = _SRC_KROPE + MLA_ROPE
_SRC_GK = _SRC_GQ + GLA_HEADS * GLA_DK
_SRC_GV = _SRC_GK + GLA_HEADS * GLA_DK
_SRC_GLR = _SRC_GV + GLA_HEADS * GLA_DV
_SRC_OG = _SRC_GLR + GLA_GATE_RANK
_SRC_GATE_A = _SRC_OG + GLA_HEADS * GLA_DV
_SRC_GATE_B = _SRC_GATE_A + D_MODEL
D_IN = _SRC_GATE_B + D_MODEL
_W_IN_MOVES = (
    (Z_QLAT, 0, _SRC_KROPE),
    (Z_KROPE, _SRC_KROPE, MLA_ROPE),
    (Z_KROPE + MLA_ROPE, _SRC_KROPE + MLA_ROPE // 2, MLA_ROPE // 2),
    (Z_KROPE + 3 * MLA_ROPE // 2, _SRC_KROPE, MLA_ROPE // 2),
    (Z_GLR, _SRC_GLR, GLA_GATE_RANK),
    (Z_GLR + GLA_GATE_RANK, _SRC_GLR, GLA_GATE_RANK),
    (Z_GQ, _SRC_GQ, GLA_HEADS * GLA_DK),
    (Z_GV, _SRC_GV, GLA_HEADS * GLA_DV),
    (Z_OG, _SRC_OG, GLA_HEADS * GLA_DV),
    (Z_GATE_A, _SRC_GATE_A, D_MODEL),
    (Z_GATE_B, _SRC_GATE_B, D_MODEL),
    (Z_GK, _SRC_GK, GLA_HEADS * GLA_DK),
)


def _prep_w_in_kernel(w_ref, o_ref):
    pad0 = Z_GLR + 2 * GLA_GATE_RANK
    o_ref[pad0:Z_GQ, :] = jnp.zeros((Z_GQ - pad0, o_ref.shape[1]), o_ref.dtype)
    for dst, src, width in _W_IN_MOVES:
        o_ref[dst:dst + width, :] = w_ref[src:src + width, :].astype(o_ref.dtype)


def _prep_w_in(w_in_t):
    tk = 256
    return pl.pallas_call(
        _prep_w_in_kernel,
        grid=(D_MODEL // tk,),
        in_specs=[pl.BlockSpec((D_IN, tk), lambda i: (0, i))],
        out_specs=pl.BlockSpec((Z_WIDTH, tk), lambda i: (0, i)),
        out_shape=jax.ShapeDtypeStruct((Z_WIDTH, D_MODEL), BF16),
        compiler_params=_params(("arbitrary",), 48),
        name="prep_w_in",
    )(w_in_t)


def _prep_w_q(w_q_up):
    w = w_q_up.reshape(MLA_Q_RANK, MLA_HEADS, MLA_NOPE + MLA_ROPE)
    pe = w[:, :, MLA_NOPE:]
    w = jnp.concatenate([w, _swap_halves(pe)], axis=-1)
    return jnp.transpose(w, (1, 0, 2)).astype(BF16)


def kernel(x, c, positions, w_ada, b_ada, g_pre_mix, w_in, g_q_lat, w_q_up, g_kv_lat, w_kv_up,
           w_gla_gate_up, b_gla_gate, g_gla_out, w_branch_a, w_branch_b, w_mix_out, g_post_mix,
           g_pre_ffn, w_ffn_up, conv_w, conv_b, w_ffn_down, g_post_ffn):
    batch, seq, _ = x.shape
    depth = w_ada.shape[0]
    t = batch * seq
    row = lambda v: v.reshape(1, -1)

    inv = 1.0 / (ROPE_THETA ** (jnp.arange(0, MLA_ROPE, 2, dtype=F32) / MLA_ROPE))
    zeros64 = jnp.zeros((MLA_ROPE,), F32)
    ones32 = jnp.ones((MLA_ROPE // 2,), F32)
    inv128 = row(jnp.concatenate([inv, inv, zeros64]))
    sgn128 = row(jnp.concatenate([-ones32, ones32, zeros64]))
    cm128 = row(jnp.concatenate([ones32, ones32, zeros64]))
    pos = positions.reshape(t, 1)

    c_pad = jnp.zeros((SUBLANES, D_MODEL), F32).at[:batch].set(c)
    x2 = x.reshape(t, D_MODEL)
    for l in range(depth):
        mod = _ada(c_pad, w_ada[l], row(b_ada[l]))[:batch]
        shift_m, scale_m, gate_m, shift_f, scale_f, gate_f = (
            m.reshape(batch, 1, D_MODEL) for m in jnp.split(mod, 6, axis=-1))

        h = _norm_mod(x2, row(g_pre_mix[l]), scale_m, shift_m, seq)
        z = _mm(h, _prep_w_in(w_in[l].T), BF16, 1024, 1024, "mm_in")

        q, k, v = _mla_proj(z, pos, inv128, sgn128, cm128, row(g_q_lat[l]), row(g_kv_lat[l]),
                            _prep_w_q(w_q_up[l]), w_kv_up[l].astype(BF16), batch, seq)
        o_a = _attn(q, k, v, batch, seq)

        wg_hi, wg_lo = _split_bf16(w_gla_gate_up[l])
        wg = jnp.zeros((LANES, GLA_HEADS * GLA_DK), BF16)
        wg = wg.at[:GLA_GATE_RANK].set(wg_hi).at[GLA_GATE_RANK:2 * GLA_GATE_RANK].set(wg_lo)
        o_b = _gla(z, wg, row(b_gla_gate[l]), row(g_gla_out[l]), batch, seq)

        merged = _mix1(o_a, o_b, w_branch_a[l].astype(BF16), w_branch_b[l].astype(BF16), z)
        x1, h2 = _mix2(merged, w_mix_out[l].astype(BF16), x2, gate_m, row(g_post_mix[l]),
                       row(g_pre_ffn[l]), scale_f, shift_f, seq)

        g = _ffn_up(h2, w_ffn_up[l], conv_w[l], row(conv_b[l]), seq)
        x2 = _ffn_down(g, w_ffn_down[l].astype(BF16), x1, gate_f, row(g_post_ffn[l]), seq)
    return x2.reshape(batch, seq, D_MODEL)
```

```python
import functools

import jax
import jax.numpy as jnp
import numpy as np
from jax import lax
from jax.experimental import pallas as pl
from jax.experimental.pallas import tpu as pltpu

F32 = jnp.float32
BF16 = jnp.bfloat16

D_MODEL = 2048
MLA_HEADS = 8
MLA_Q_RANK = 512
MLA_KV_RANK = 256
MLA_NOPE = 128
MLA_ROPE = 64
MLA_V = 128
ROPE_THETA = 10000.0
GLA_HEADS = 4
GLA_DK = 256
GLA_DV = 512
GLA_GATE_RANK = 16
GLA_TAU = 16.0
D_FF = 5632
CONV_WIDTH = 3
EPS = 1e-6

LANES = 128
SUBLANES = 8
MLA_QK_PAD = 256

Z_QLAT = 0
Z_KVLAT = 512
Z_KROPE = 768
Z_GLR = 896
Z_GQ = 1024
Z_GV = 2048
Z_OG = 4096
Z_GATE_A = 6144
Z_GATE_B = 8192
Z_GK = 10240
Z_WIDTH = 11264

GLA_CHUNK = 128
GLA_LEVELS = 7

MIB = 1024 * 1024


def _params(semantics, vmem_mib):
    return pltpu.CompilerParams(dimension_semantics=semantics,
                                vmem_limit_bytes=vmem_mib * MIB)


def _rms(x, g):
    return x * lax.rsqrt(jnp.mean(x * x, axis=-1, keepdims=True) + EPS) * g


def _ada_kernel(c_ref, w_ref, b_ref, o_ref):
    c = c_ref[...]
    ca = c * jax.nn.sigmoid(c)
    o_ref[...] = jnp.dot(ca.astype(BF16), w_ref[...].astype(BF16),
                         preferred_element_type=F32) + b_ref[...]


def _ada(c_pad, w, b):
    n = w.shape[1]
    tn = 1024
    return pl.pallas_call(
        _ada_kernel,
        grid=(n // tn,),
        in_specs=[pl.BlockSpec((SUBLANES, D_MODEL), lambda j: (0, 0)),
                  pl.BlockSpec((D_MODEL, tn), lambda j: (0, j)),
                  pl.BlockSpec((1, tn), lambda j: (0, j))],
        out_specs=pl.BlockSpec((SUBLANES, tn), lambda j: (0, j)),
        out_shape=jax.ShapeDtypeStruct((SUBLANES, n), F32),
        compiler_params=_params(("arbitrary",), 32),
        name="ada",
    )(c_pad, w, b)


MM_IN_NORM_ROWS = 256


def _mm_in_kernel(x_ref, g_ref, sc_ref, sh_ref, w_ref, o_ref, h_sc):
    @pl.when(pl.program_id(1) == 0)
    def _():
        for r0 in range(0, x_ref.shape[0], MM_IN_NORM_ROWS):
            rows = slice(r0, r0 + MM_IN_NORM_ROWS)
            y = _rms(x_ref[rows], g_ref[...])
            h_sc[rows] = (y * (1.0 + sc_ref[0]) + sh_ref[0]).astype(h_sc.dtype)

    o_ref[...] = lax.dot_general(h_sc[...], w_ref[...], (((1,), (1,)), ((), ())),
                                 preferred_element_type=F32).astype(o_ref.dtype)


def _mm_in(x2, g, scale, shift, w_t, seq):
    t = x2.shape[0]
    n = w_t.shape[0]
    tm, tn = 1024, 1024
    tpb = seq // tm
    per_batch = pl.BlockSpec((1, 1, D_MODEL), lambda i, j: (i // tpb, 0, 0))
    return pl.pallas_call(
        _mm_in_kernel,
        grid=(t // tm, n // tn),
        in_specs=[pl.BlockSpec((tm, D_MODEL), lambda i, j: (i, 0)),
                  pl.BlockSpec((1, D_MODEL), lambda i, j: (0, 0)),
                  per_batch, per_batch,
                  pl.BlockSpec((tn, D_MODEL), lambda i, j: (j, 0))],
        out_specs=pl.BlockSpec((tm, tn), lambda i, j: (i, j)),
        out_shape=jax.ShapeDtypeStruct((t, n), BF16),
        scratch_shapes=[pltpu.VMEM((tm, D_MODEL), BF16)],
        compiler_params=_params(("arbitrary", "arbitrary"), 48),
        name="mm_in",
    )(x2, g, scale, shift, w_t)


def _mla_proj_kernel(ql_ref, kvl_ref, kr_ref, pos_ref, inv_ref, sgn_ref, cm_ref,
                     gq_ref, gkv_ref, wq_ref, wkv_ref, q_ref, k_ref, v_ref):
    ang = pos_ref[...].astype(F32) * inv_ref[...]
    cosm = jnp.cos(ang) * cm_ref[...]
    sinm = jnp.sin(ang) * sgn_ref[...]

    def rope(r):
        return r * cosm + pltpu.roll(r, MLA_ROPE, 1) * sinm

    scale = (MLA_NOPE + MLA_ROPE) ** -0.5 * np.log2(np.e)
    qn =_rms(ql_ref[...].astype(F32), gq_ref[...]).astype(BF16)
    for h in range(MLA_HEADS):
        r = jnp.dot(qn, wq_ref[h], preferred_element_type=F32) * scale
        q_ref[0, h, :, 0:MLA_NOPE] = r[:, 0:MLA_NOPE].astype(BF16)
        q_ref[0, h, :, MLA_NOPE:MLA_QK_PAD] = rope(r[:, MLA_NOPE:MLA_QK_PAD]).astype(BF16)

    kvn = _rms(kvl_ref[...].astype(F32), gkv_ref[...]).astype(BF16)
    kv = jnp.dot(kvn, wkv_ref[...], preferred_element_type=F32).astype(BF16)
    kpe = rope(kr_ref[...].astype(F32)).astype(BF16)
    hw = MLA_NOPE + MLA_V
    for h in range(MLA_HEADS):
        k_ref[0, h, :, 0:MLA_NOPE] = kv[:, h * hw:h * hw + MLA_NOPE]
        k_ref[0, h, :, MLA_NOPE:MLA_QK_PAD] = kpe
        v_ref[0, h] = kv[:, h * hw + MLA_NOPE:(h + 1) * hw]


def _mla_proj(z, pos, inv, sgn, cm, gq, gkv, wq, wkv, batch, seq):
    tm = 512
    tpb = seq // tm
    qk_shape = jax.ShapeDtypeStruct((batch, MLA_HEADS, seq, MLA_QK_PAD), BF16)
    v_shape = jax.ShapeDtypeStruct((batch, MLA_HEADS, seq, MLA_V), BF16)
    const2 = lambda i: (0, 0)
    qk_spec = pl.BlockSpec((1, MLA_HEADS, tm, MLA_QK_PAD), lambda i: (i // tpb, 0, i % tpb, 0))
    return pl.pallas_call(
        _mla_proj_kernel,
        grid=(batch * tpb,),
        in_specs=[pl.BlockSpec((tm, MLA_Q_RANK), lambda i: (i, Z_QLAT // MLA_Q_RANK)),
                  pl.BlockSpec((tm, MLA_KV_RANK), lambda i: (i, Z_KVLAT // MLA_KV_RANK)),
                  pl.BlockSpec((tm, LANES), lambda i: (i, Z_KROPE // LANES)),
                  pl.BlockSpec((tm, 1), lambda i: (i, 0)),
                  pl.BlockSpec((1, LANES), const2),
                  pl.BlockSpec((1, LANES), const2),
                  pl.BlockSpec((1, LANES), const2),
                  pl.BlockSpec((1, MLA_Q_RANK), const2),
                  pl.BlockSpec((1, MLA_KV_RANK), const2),
                  pl.BlockSpec((MLA_HEADS, MLA_Q_RANK, MLA_QK_PAD), lambda i: (0, 0, 0)),
                  pl.BlockSpec((MLA_KV_RANK, MLA_HEADS * (MLA_NOPE + MLA_V)), const2)],
        out_specs=[qk_spec, qk_spec,
                   pl.BlockSpec((1, MLA_HEADS, tm, MLA_V), lambda i: (i // tpb, 0, i % tpb, 0))],
        out_shape=[qk_shape, qk_shape, v_shape],
        compiler_params=_params(("arbitrary",), 48),
        name="mla_proj",
    )(z, z, z, pos, inv, sgn, cm, gq, gkv, wq, wkv)


ATTN_BLOCK = 1024
ATTN_DIAG_CHAINS = 2


def _attn_kernel(q_ref, k_ref, v_ref, o_ref, m_sc, l_sc, acc_sc):
    blk = ATTN_BLOCK
    nt = (((1,), (1,)), ((), ()))
    qi = pl.program_id(2)
    m_sc[...] = jnp.full(m_sc.shape, -jnp.inf, F32)
    l_sc[...] = jnp.zeros(l_sc.shape, F32)
    acc_sc[...] = jnp.zeros(acc_sc.shape, F32)

    def update(rows, s, v):
        m_prev = m_sc[rows]
        m_new = jnp.maximum(m_prev, jnp.max(s, axis=-1, keepdims=True))
        alpha = jnp.exp2(m_prev - m_new)
        ps = [jnp.exp2(s[:, c * LANES:(c + 1) * LANES] - m_new)
              for c in range(s.shape[1] // LANES)]
        l_sc[rows] = alpha * l_sc[rows] + functools.reduce(lambda a, b: a + b, ps)
        p = jnp.concatenate(ps, axis=1).astype(BF16)
        acc_sc[rows] = alpha * acc_sc[rows] + jnp.dot(p, v, preferred_element_type=F32)
        m_sc[rows] = m_new

    def body(kb, carry):
        ks = pl.multiple_of(kb * blk, blk)
        k = k_ref[0, 0, pl.ds(ks, blk), :]
        v = v_ref[0, 0, pl.ds(ks, blk), :]
        s = lax.dot_general(q_ref[0, 0], k, nt, preferred_element_type=F32)
        update(slice(0, blk), s, v)
        return carry

    lax.fori_loop(0, qi, body, 0)

    ks = pl.multiple_of(qi * blk, blk)
    sub = blk // ATTN_DIAG_CHAINS
    for r in range(ATTN_DIAG_CHAINS):
        n = (r + 1) * sub
        k = k_ref[0, 0, pl.ds(ks, n), :]
        v = v_ref[0, 0, pl.ds(ks, n), :]
        q = q_ref[0, 0, r * sub:(r + 1) * sub, :]
        s = lax.dot_general(q, k, nt, preferred_element_type=F32)
        row = lax.broadcasted_iota(jnp.int32, s.shape, 0) + r * sub
        col = lax.broadcasted_iota(jnp.int32, s.shape, 1)
        update(slice(r * sub, n), jnp.where(col <= row, s, -jnp.inf), v)
    l = jnp.sum(l_sc[...], axis=-1, keepdims=True)
    o_ref[...] = (acc_sc[...] / l).astype(o_ref.dtype)


def _attn(q, k, v, batch, seq):
    blk = ATTN_BLOCK
    nq = seq // blk
    return pl.pallas_call(
        _attn_kernel,
        grid=(batch, MLA_HEADS, nq),
        in_specs=[pl.BlockSpec((1, 1, blk, MLA_QK_PAD), lambda b, h, i: (b, h, i, 0)),
                  pl.BlockSpec((1, 1, seq, MLA_QK_PAD), lambda b, h, i: (b, h, 0, 0)),
                  pl.BlockSpec((1, 1, seq, MLA_V), lambda b, h, i: (b, h, 0, 0))],
        out_specs=pl.BlockSpec((blk, MLA_V), lambda b, h, i: (b * nq + i, h)),
        out_shape=jax.ShapeDtypeStruct((batch * seq, MLA_HEADS * MLA_V), BF16),
        scratch_shapes=[pltpu.VMEM((blk, LANES), F32), pltpu.VMEM((blk, LANES), F32),
                        pltpu.VMEM((blk, MLA_V), F32)],
        compiler_params=_params(("arbitrary", "arbitrary", "arbitrary"), 32),
        name="attn",
    )(q, k, v)


def _gla_constants():
    c = GLA_CHUNK
    idx = np.arange(c)
    w = np.zeros((GLA_LEVELS + 2, c, c), np.float32)
    masks = np.zeros((GLA_LEVELS + 1, c, c), np.float32)
    for l in range(GLA_LEVELS):
        half = 1 << l
        blk = idx // (2 * half)
        mid = blk * 2 * half + half - 1
        upper = (idx % (2 * half)) >= half
        t = idx[None, :]
        up_rows = (t > mid[:, None]) & (t <= idx[:, None])
        lo_rows = (t > idx[:, None]) & (t <= mid[:, None])
        w[l] = np.where(upper[:, None], up_rows, lo_rows)
        masks[l] = (blk[:, None] == blk[None, :]) & upper[:, None] & (~upper[None, :])
    w[GLA_LEVELS] = idx[None, :] <= idx[:, None]
    w[GLA_LEVELS + 1] = idx[None, :] > idx[:, None]
    masks[GLA_LEVELS] = np.eye(c)
    w = w.reshape(-1, c)
    return np.concatenate([w, w], axis=1), masks


def _split_bf16(x):
    hi = x.astype(BF16)
    return hi, (x - hi.astype(F32)).astype(BF16)


def _gla_kernel(q_ref, k_ref, v_ref, lr_ref, og_ref, wg_ref, bg_ref, go_ref, ws_ref, mask_ref,
                o_ref, st_ref):
    c = GLA_CHUNK
    dk, dv = GLA_DK, GLA_DV
    nt = (((1,), (1,)), ((), ()))

    @pl.when(pl.program_id(1) == 0)
    def _():
        st_ref[...] = jnp.zeros(st_ref.shape, F32)

    lr = lr_ref[...]
    for h in range(GLA_HEADS):
        x = (jnp.dot(lr, wg_ref[:, h * dk:(h + 1) * dk], preferred_element_type=F32)
             + bg_ref[:, h * dk:(h + 1) * dk])
        log_a = -(jnp.maximum(-x, 0.0) + jnp.log1p(jnp.exp(-jnp.abs(x)))) * (1.0 / GLA_TAU)
        expo = jnp.dot(ws_ref[...], jnp.concatenate(_split_bf16(log_a), axis=0),
                       preferred_element_type=F32)
        p = jnp.exp(expo)
        pb = p.astype(BF16)

        q = q_ref[:, h * dk:(h + 1) * dk] * (dk ** -0.5)
        k = k_ref[:, h * dk:(h + 1) * dk]
        v = v_ref[:, h * dv:(h + 1) * dv]
        attn = mask_ref[GLA_LEVELS] * lax.dot_general(q, k, nt, preferred_element_type=F32)
        for l in range(GLA_LEVELS):
            pl_ = pb[l * c:(l + 1) * c]
            s = lax.dot_general(q * pl_, k * pl_, nt, preferred_element_type=F32)
            attn = attn + mask_ref[l] * s
        eb = pb[GLA_LEVELS * c:(GLA_LEVELS + 1) * c]
        ebl = pb[(GLA_LEVELS + 1) * c:(GLA_LEVELS + 2) * c]
        st = st_ref[h]
        o = (lax.dot_general(q * eb, st.astype(BF16), nt, preferred_element_type=F32)
             + jnp.dot(attn.astype(BF16), v, preferred_element_type=F32))
        v_t = v.astype(F32).T.astype(BF16)
        decay = p[(GLA_LEVELS + 1) * c - 1:(GLA_LEVELS + 1) * c]
        st_ref[h] = st * decay + jnp.dot(v_t, k * ebl, preferred_element_type=F32)
        y = _rms(o, go_ref[...])
        og = og_ref[:, h * dv:(h + 1) * dv].astype(F32)
        o_ref[:, h * dv:(h + 1) * dv] = (y * (og * jax.nn.sigmoid(og))).astype(o_ref.dtype)


def _gla(z, wg, bg, go, batch, seq):
    c = GLA_CHUNK
    nc = seq // c
    ws, masks = _gla_constants()
    hk, hv = GLA_HEADS * GLA_DK, GLA_HEADS * GLA_DV
    const2 = lambda b, i: (0, 0)
    return pl.pallas_call(
        _gla_kernel,
        grid=(batch, nc),
        in_specs=[pl.BlockSpec((c, hk), lambda b, i: (b * nc + i, Z_GQ // hk)),
                  pl.BlockSpec((c, hk), lambda b, i: (b * nc + i, Z_GK // hk)),
                  pl.BlockSpec((c, hv), lambda b, i: (b * nc + i, Z_GV // hv)),
                  pl.BlockSpec((c, LANES), lambda b, i: (b * nc + i, Z_GLR // LANES)),
                  pl.BlockSpec((c, hv), lambda b, i: (b * nc + i, Z_OG // hv)),
                  pl.BlockSpec((LANES, hk), const2),
                  pl.BlockSpec((1, hk), const2),
                  pl.BlockSpec((1, GLA_DV), const2),
                  pl.BlockSpec(ws.shape, const2),
                  pl.BlockSpec(masks.shape, lambda b, i: (0, 0, 0))],
        out_specs=pl.BlockSpec((c, hv), lambda b, i: (b * nc + i, 0)),
        out_shape=jax.ShapeDtypeStruct((batch * seq, hv), BF16),
        scratch_shapes=[pltpu.VMEM((GLA_HEADS, GLA_DV, GLA_DK), F32)],
        compiler_params=_params(("arbitrary", "arbitrary"), 48),
        name="gla",
    )(z, z, z, z, z, wg, bg, go, jnp.asarray(ws, BF16), jnp.asarray(masks))


def _mix1_kernel(oa_ref, ob_ref, wa_ref, wb_ref, ga_ref, gb_ref, o_ref, wa_bf, wb_bf):
    @pl.when(pl.program_id(1) == 0)
    def _():
        wa_bf[...] = wa_ref[...].astype(BF16)
        wb_bf[...] = wb_ref[...].astype(BF16)

    ya = jnp.dot(oa_ref[...], wa_bf[...], preferred_element_type=F32)
    yb = jnp.dot(ob_ref[...], wb_bf[...], preferred_element_type=F32)
    o_ref[...] = (jax.nn.sigmoid(ga_ref[...].astype(F32)) * ya
                  + jax.nn.sigmoid(gb_ref[...].astype(F32)) * yb).astype(o_ref.dtype)


def _mix1(oa, ob, wa, wb, z):
    t = oa.shape[0]
    tm, tn = 1024, 512
    return pl.pallas_call(
        _mix1_kernel,
        grid=(D_MODEL // tn, t // tm),
        in_specs=[pl.BlockSpec((tm, oa.shape[1]), lambda j, i: (i, 0)),
                  pl.BlockSpec((tm, ob.shape[1]), lambda j, i: (i, 0)),
                  pl.BlockSpec((wa.shape[0], tn), lambda j, i: (0, j)),
                  pl.BlockSpec((wb.shape[0], tn), lambda j, i: (0, j)),
                  pl.BlockSpec((tm, tn), lambda j, i: (i, Z_GATE_A // tn + j)),
                  pl.BlockSpec((tm, tn), lambda j, i: (i, Z_GATE_B // tn + j))],
        out_specs=pl.BlockSpec((tm, tn), lambda j, i: (i, j)),
        out_shape=jax.ShapeDtypeStruct((t, D_MODEL), BF16),
        scratch_shapes=[pltpu.VMEM((wa.shape[0], tn), BF16), pltpu.VMEM((wb.shape[0], tn), BF16)],
        compiler_params=_params(("arbitrary", "arbitrary"), 48),
        name="mix1",
    )(oa, ob, wa, wb, z, z)


def _mix2_kernel(m_ref, w_ref, x_ref, gm_ref, gpost_ref, gpre_ref, sc_ref, sh_ref,
                 x1_ref, h2_ref):
    for r0 in range(0, m_ref.shape[0], MIX2_SUBTILE):
        rows = slice(r0, r0 + MIX2_SUBTILE)
        y = jnp.dot(m_ref[rows], w_ref[...], preferred_element_type=F32)
        x1 = x_ref[rows] + gm_ref[0] * _rms(y, gpost_ref[...])
        x1_ref[rows] = x1
        h2_ref[rows] = (_rms(x1, gpre_ref[...]) * (1.0 + sc_ref[0])
                        + sh_ref[0]).astype(h2_ref.dtype)


MIX2_SUBTILE = 256


def _mix2(merged, w, x2, gate_m, g_post, g_pre, scale_f, shift_f, seq):
    t = merged.shape[0]
    tm = 512
    tpb = seq // tm
    row = pl.BlockSpec((tm, D_MODEL), lambda i: (i, 0))
    vec = pl.BlockSpec((1, D_MODEL), lambda i: (0, 0))
    per_batch = pl.BlockSpec((1, 1, D_MODEL), lambda i: (i // tpb, 0, 0))
    return pl.pallas_call(
        _mix2_kernel,
        grid=(t // tm,),
        in_specs=[row, pl.BlockSpec((D_MODEL, D_MODEL), lambda i: (0, 0),
                                    pipeline_mode=pl.Buffered(1)), row,
                  per_batch, vec, vec, per_batch, per_batch],
        out_specs=[row, row],
        out_shape=[jax.ShapeDtypeStruct((t, D_MODEL), F32),
                   jax.ShapeDtypeStruct((t, D_MODEL), BF16)],
        compiler_params=_params(("arbitrary",), 48),
        name="mix2",
    )(merged, w, x2, gate_m, g_post, g_pre, scale_f, shift_f)


FFN_UP_CHUNK = 512


def _ffn_up_kernel(h_ref, wa_ref, wv_ref, cwa_ref, cwv_ref, cba_ref, cbv_ref, o_ref,
                   wa_bf, wv_bf, halo_a, halo_v, *, tiles_per_batch):
    i = pl.program_id(1)
    tm = h_ref.shape[0]

    @pl.when(i == 0)
    def _():
        wa_bf[...] = wa_ref[...].astype(BF16)
        wv_bf[...] = wv_ref[...].astype(BF16)

    @pl.when(i % tiles_per_batch == 0)
    def _():
        halo_a[...] = jnp.zeros(halo_a.shape, F32)
        halo_v[...] = jnp.zeros(halo_v.shape, F32)

    h = h_ref[...]

    def conv(w_ref, cw_ref, cb_ref, halo, cols):
        u = jnp.dot(h, w_ref[:, cols], preferred_element_type=F32)
        cw = cw_ref[:, cols]
        cb = cb_ref[:, cols]
        w0, w1, w2 = cw[0:1], cw[1:2], cw[2:3]
        body = (u * w2 + pltpu.roll(u, 1, 0) * w1 + pltpu.roll(u, 2, 0) * w0) + cb
        u8 = u[0:SUBLANES]
        prev8 = halo[:, cols]
        r = lax.broadcasted_iota(jnp.int32, u8.shape, 0)
        s1 = jnp.where(r < 1, pltpu.roll(prev8, 1, 0), pltpu.roll(u8, 1, 0))
        s2 = jnp.where(r < 2, pltpu.roll(prev8, 2, 0), pltpu.roll(u8, 2, 0))
        top = (u8 * w2 + s1 * w1 + s2 * w0) + cb
        halo[:, cols] = u[tm - SUBLANES:tm]
        return body, top

    for c0 in range(0, o_ref.shape[1], FFN_UP_CHUNK):
        cols = slice(c0, c0 + FFN_UP_CHUNK)
        a, a_top = conv(wa_bf, cwa_ref, cba_ref, halo_a, cols)
        val, val_top = conv(wv_bf, cwv_ref, cbv_ref, halo_v, cols)
        o_ref[:, cols] = (jax.nn.gelu(a, approximate=True) * val).astype(o_ref.dtype)
        o_ref[0:SUBLANES, cols] = (jax.nn.gelu(a_top, approximate=True)
                                   * val_top).astype(o_ref.dtype)


def _ffn_up(h2, w_up, conv_w, conv_b, seq):
    t = h2.shape[0]
    tm, tn = 1024, 512
    nj = D_FF // tn
    return pl.pallas_call(
        functools.partial(_ffn_up_kernel, tiles_per_batch=seq // tm),
        grid=(nj, t // tm),
        in_specs=[pl.BlockSpec((tm, D_MODEL), lambda j, i: (i, 0)),
                  pl.BlockSpec((D_MODEL, tn), lambda j, i: (0, j)),
                  pl.BlockSpec((D_MODEL, tn), lambda j, i: (0, nj + j)),
                  pl.BlockSpec((CONV_WIDTH, tn), lambda j, i: (0, j)),
                  pl.BlockSpec((CONV_WIDTH, tn), lambda j, i: (0, nj + j)),
                  pl.BlockSpec((1, tn), lambda j, i: (0, j)),
                  pl.BlockSpec((1, tn), lambda j, i: (0, nj + j))],
        out_specs=pl.BlockSpec((tm, tn), lambda j, i: (i, j)),
        out_shape=jax.ShapeDtypeStruct((t, D_FF), BF16),
        scratch_shapes=[pltpu.VMEM((D_MODEL, tn), BF16), pltpu.VMEM((D_MODEL, tn), BF16),
                        pltpu.VMEM((SUBLANES, tn), F32), pltpu.VMEM((SUBLANES, tn), F32)],
        compiler_params=_params(("arbitrary", "arbitrary"), 56),
        name="ffn_up",
    )(h2, w_up, w_up, conv_w, conv_w, conv_b, conv_b)


def _ffn_down_kernel(g_ref, w_ref, x1_ref, gf_ref, gpost_ref, o_ref):
    f = jnp.dot(g_ref[...], w_ref[...], preferred_element_type=F32)
    o_ref[...] = x1_ref[...] + gf_ref[0] * _rms(f, gpost_ref[...])


def _ffn_down(g, w, x1, gate_f, g_post, seq):
    t = g.shape[0]
    tm = 256
    tpb = seq // tm
    return pl.pallas_call(
        _ffn_down_kernel,
        grid=(t // tm,),
        in_specs=[pl.BlockSpec((tm, D_FF), lambda i: (i, 0)),
                  pl.BlockSpec((D_FF, D_MODEL), lambda i: (0, 0), pipeline_mode=pl.Buffered(1)),
                  pl.BlockSpec((tm, D_MODEL), lambda i: (i, 0)),
                  pl.BlockSpec((1, 1, D_MODEL), lambda i: (i // tpb, 0, 0)),
                  pl.BlockSpec((1, D_MODEL), lambda i: (0, 0))],
        out_specs=pl.BlockSpec((tm, D_MODEL), lambda i: (i, 0)),
        out_shape=jax.ShapeDtypeStruct((t, D_MODEL), F32),
        compiler_params=_params(("arbitrary",), 56),
        name="ffn_down",
    )(g, w, x1, gate_f, g_post)


def _swap_halves(w):
    half = w.shape[-1] // 2
    return jnp.concatenate([w[..., half:], w[..., :half]], axis=-1)


_SRC_KROPE = MLA_Q_RANK + MLA_KV_RANK
_SRC_GQ = _SRC_KROPE + MLA_ROPE
_SRC_GK = _SRC_GQ + GLA_HEADS * GLA_DK
_SRC_GV = _SRC_GK + GLA_HEADS * GLA_DK
_SRC_GLR = _SRC_GV + GLA_HEADS * GLA_DV
_SRC_OG = _SRC_GLR + GLA_GATE_RANK
_SRC_GATE_A = _SRC_OG + GLA_HEADS * GLA_DV
_SRC_GATE_B = _SRC_GATE_A + D_MODEL
D_IN = _SRC_GATE_B + D_MODEL
_W_IN_MOVES = (
    (Z_QLAT, 0, _SRC_KROPE),
    (Z_KROPE, _SRC_KROPE, MLA_ROPE),
    (Z_KROPE + MLA_ROPE, _SRC_KROPE + MLA_ROPE // 2, MLA_ROPE // 2),
    (Z_KROPE + 3 * MLA_ROPE // 2, _SRC_KROPE, MLA_ROPE // 2),
    (Z_GLR, _SRC_GLR, GLA_GATE_RANK),
    (Z_GLR + GLA_GATE_RANK, _SRC_GLR, GLA_GATE_RANK),
    (Z_GQ, _SRC_GQ, GLA_HEADS * GLA_DK),
    (Z_GV, _SRC_GV, GLA_HEADS * GLA_DV),
    (Z_OG, _SRC_OG, GLA_HEADS * GLA_DV),
    (Z_GATE_A, _SRC_GATE_A, D_MODEL),
    (Z_GATE_B, _SRC_GATE_B, D_MODEL),
    (Z_GK, _SRC_GK, GLA_HEADS * GLA_DK),
)


def _prep_w_in_kernel(w_ref, o_ref):
    pad0 = Z_GLR + 2 * GLA_GATE_RANK
    o_ref[pad0:Z_GQ, :] = jnp.zeros((Z_GQ - pad0, o_ref.shape[1]), o_ref.dtype)
    for dst, src, width in _W_IN_MOVES:
        o_ref[dst:dst + width, :] = w_ref[src:src + width, :].astype(o_ref.dtype)


def _prep_w_in(w_in_t):
    tk = 256
    return pl.pallas_call(
        _prep_w_in_kernel,
        grid=(D_MODEL // tk,),
        in_specs=[pl.BlockSpec((D_IN, tk), lambda i: (0, i))],
        out_specs=pl.BlockSpec((Z_WIDTH, tk), lambda i: (0, i)),
        out_shape=jax.ShapeDtypeStruct((Z_WIDTH, D_MODEL), BF16),
        compiler_params=_params(("arbitrary",), 48),
        name="prep_w_in",
    )(w_in_t)


def _prep_w_q(w_q_up):
    w = w_q_up.reshape(MLA_Q_RANK, MLA_HEADS, MLA_NOPE + MLA_ROPE)
    pe = w[:, :, MLA_NOPE:]
    w = jnp.concatenate([w, _swap_halves(pe)], axis=-1)
    return jnp.transpose(w, (1, 0, 2)).astype(BF16)


def kernel(x, c, positions, w_ada, b_ada, g_pre_mix, w_in, g_q_lat, w_q_up, g_kv_lat, w_kv_up,
           w_gla_gate_up, b_gla_gate, g_gla_out, w_branch_a, w_branch_b, w_mix_out, g_post_mix,
           g_pre_ffn, w_ffn_up, conv_w, conv_b, w_ffn_down, g_post_ffn):
    batch, seq, _ = x.shape
    depth = w_ada.shape[0]
    t = batch * seq
    row = lambda v: v.reshape(1, -1)

    inv = 1.0 / (ROPE_THETA ** (jnp.arange(0, MLA_ROPE, 2, dtype=F32) / MLA_ROPE))
    zeros64 = jnp.zeros((MLA_ROPE,), F32)
    ones32 = jnp.ones((MLA_ROPE // 2,), F32)
    inv128 = row(jnp.concatenate([inv, inv, zeros64]))
    sgn128 = row(jnp.concatenate([-ones32, ones32, zeros64]))
    cm128 = row(jnp.concatenate([ones32, ones32, zeros64]))
    pos = positions.reshape(t, 1)

    c_pad = jnp.zeros((SUBLANES, D_MODEL), F32).at[:batch].set(c)
    x2 = x.reshape(t, D_MODEL)
    for l in range(depth):
        mod = _ada(c_pad, w_ada[l], row(b_ada[l]))[:batch]
        shift_m, scale_m, gate_m, shift_f, scale_f, gate_f = (
            m.reshape(batch, 1, D_MODEL) for m in jnp.split(mod, 6, axis=-1))

        z = _mm_in(x2, row(g_pre_mix[l]), scale_m, shift_m, _prep_w_in(w_in[l].T), seq)

        q, k, v = _mla_proj(z, pos, inv128, sgn128, cm128, row(g_q_lat[l]), row(g_kv_lat[l]),
                            _prep_w_q(w_q_up[l]), w_kv_up[l].astype(BF16), batch, seq)
        o_a = _attn(q, k, v, batch, seq)

        wg_hi, wg_lo = _split_bf16(w_gla_gate_up[l])
        wg = jnp.zeros((LANES, GLA_HEADS * GLA_DK), BF16)
        wg = wg.at[:GLA_GATE_RANK].set(wg_hi).at[GLA_GATE_RANK:2 * GLA_GATE_RANK].set(wg_lo)
        o_b = _gla(z, wg, row(b_gla_gate[l]), row(g_gla_out[l]), batch, seq)

        merged = _mix1(o_a, o_b, w_branch_a[l], w_branch_b[l], z)
        x1, h2 = _mix2(merged, w_mix_out[l].astype(BF16), x2, gate_m, row(g_post_mix[l]),
                       row(g_pre_ffn[l]), scale_f, shift_f, seq)

        g = _ffn_up(h2, w_ffn_up[l], conv_w[l], row(conv_b[l]), seq)
        x2 = _ffn_down(g, w_ffn_down[l].astype(BF16), x1, gate_f, row(g_post_ffn[l]), seq)
    return x2.reshape(batch, seq, D_MODEL)
```

```python
import functools

import jax
import jax.numpy as jnp
import numpy as np
from jax import lax
from jax.experimental import pallas as pl
from jax.experimental.pallas import tpu as pltpu

F32 = jnp.float32
BF16 = jnp.bfloat16

D_MODEL = 2048
MLA_HEADS = 8
MLA_Q_RANK = 512
MLA_KV_RANK = 256
MLA_NOPE = 128
MLA_ROPE = 64
MLA_V = 128
ROPE_THETA = 10000.0
GLA_HEADS = 4
GLA_DK = 256
GLA_DV = 512
GLA_GATE_RANK = 16
GLA_TAU = 16.0
D_FF = 5632
CONV_WIDTH = 3
EPS = 1e-6

LANES = 128
SUBLANES = 8
MLA_QK_PAD = 256

Z_QLAT = 0
Z_KVLAT = 512
Z_KROPE = 768
Z_GLR = 896
Z_GQ = 1024
Z_GV = 2048
Z_OG = 4096
Z_GATE_A = 6144
Z_GATE_B = 8192
Z_GK = 10240
Z_WIDTH = 11264

GLA_CHUNK = 128
GLA_LEVELS = 7

MIB = 1024 * 1024


def _params(semantics, vmem_mib):
    return pltpu.CompilerParams(dimension_semantics=semantics,
                                vmem_limit_bytes=vmem_mib * MIB)


def _rms(x, g):
    return x * lax.rsqrt(jnp.mean(x * x, axis=-1, keepdims=True) + EPS) * g


def _ada_kernel(c_ref, w_ref, b_ref, o_ref):
    c = c_ref[...]
    ca = c * jax.nn.sigmoid(c)
    o_ref[...] = jnp.dot(ca.astype(BF16), w_ref[...].astype(BF16),
                         preferred_element_type=F32) + b_ref[...]


def _ada(c_pad, w, b):
    n = w.shape[1]
    tn = 1024
    return pl.pallas_call(
        _ada_kernel,
        grid=(n // tn,),
        in_specs=[pl.BlockSpec((SUBLANES, D_MODEL), lambda j: (0, 0)),
                  pl.BlockSpec((D_MODEL, tn), lambda j: (0, j)),
                  pl.BlockSpec((1, tn), lambda j: (0, j))],
        out_specs=pl.BlockSpec((SUBLANES, tn), lambda j: (0, j)),
        out_shape=jax.ShapeDtypeStruct((SUBLANES, n), F32),
        compiler_params=_params(("arbitrary",), 32),
        name="ada",
    )(c_pad, w, b)


MM_IN_NORM_ROWS = 256


def _mm_in_kernel(x_ref, g_ref, sc_ref, sh_ref, w_ref, o_ref, h_sc):
    @pl.when(pl.program_id(1) == 0)
    def _():
        for r0 in range(0, x_ref.shape[0], MM_IN_NORM_ROWS):
            rows = slice(r0, r0 + MM_IN_NORM_ROWS)
            y = _rms(x_ref[rows], g_ref[...])
            h_sc[rows] = (y * (1.0 + sc_ref[0]) + sh_ref[0]).astype(h_sc.dtype)

    o_ref[...] = lax.dot_general(h_sc[...], w_ref[...], (((1,), (1,)), ((), ())),
                                 preferred_element_type=F32).astype(o_ref.dtype)


def _mm_in(x2, g, scale, shift, w_t, seq):
    t = x2.shape[0]
    n = w_t.shape[0]
    tm, tn = 1024, 1024
    tpb = seq // tm
    per_batch = pl.BlockSpec((1, 1, D_MODEL), lambda i, j: (i // tpb, 0, 0))
    return pl.pallas_call(
        _mm_in_kernel,
        grid=(t // tm, n // tn),
        in_specs=[pl.BlockSpec((tm, D_MODEL), lambda i, j: (i, 0)),
                  pl.BlockSpec((1, D_MODEL), lambda i, j: (0, 0)),
                  per_batch, per_batch,
                  pl.BlockSpec((tn, D_MODEL), lambda i, j: (j, 0))],
        out_specs=pl.BlockSpec((tm, tn), lambda i, j: (i, j)),
        out_shape=jax.ShapeDtypeStruct((t, n), BF16),
        scratch_shapes=[pltpu.VMEM((tm, D_MODEL), BF16)],
        compiler_params=_params(("arbitrary", "arbitrary"), 48),
        name="mm_in",
    )(x2, g, scale, shift, w_t)


def _mla_proj_kernel(ql_ref, kvl_ref, kr_ref, pos_ref, inv_ref, sgn_ref, cm_ref,
                     gq_ref, gkv_ref, wq_ref, wkv_ref, q_ref, k_ref, v_ref):
    ang = pos_ref[...].astype(F32) * inv_ref[...]
    cosm = jnp.cos(ang) * cm_ref[...]
    sinm = jnp.sin(ang) * sgn_ref[...]

    def rope(r):
        return r * cosm + pltpu.roll(r, MLA_ROPE, 1) * sinm

    scale = (MLA_NOPE + MLA_ROPE) ** -0.5 * np.log2(np.e)
    qn =_rms(ql_ref[...].astype(F32), gq_ref[...]).astype(BF16)
    for h in range(MLA_HEADS):
        r = jnp.dot(qn, wq_ref[h], preferred_element_type=F32) * scale
        q_ref[0, h, :, 0:MLA_NOPE] = r[:, 0:MLA_NOPE].astype(BF16)
        q_ref[0, h, :, MLA_NOPE:MLA_QK_PAD] = rope(r[:, MLA_NOPE:MLA_QK_PAD]).astype(BF16)

    kvn = _rms(kvl_ref[...].astype(F32), gkv_ref[...]).astype(BF16)
    kv = jnp.dot(kvn, wkv_ref[...], preferred_element_type=F32).astype(BF16)
    kpe = rope(kr_ref[...].astype(F32)).astype(BF16)
    hw = MLA_NOPE + MLA_V
    for h in range(MLA_HEADS):
        k_ref[0, h, :, 0:MLA_NOPE] = kv[:, h * hw:h * hw + MLA_NOPE]
        k_ref[0, h, :, MLA_NOPE:MLA_QK_PAD] = kpe
        v_ref[0, h] = kv[:, h * hw + MLA_NOPE:(h + 1) * hw]


def _mla_proj(z, pos, inv, sgn, cm, gq, gkv, wq, wkv, batch, seq):
    tm = 512
    tpb = seq // tm
    qk_shape = jax.ShapeDtypeStruct((batch, MLA_HEADS, seq, MLA_QK_PAD), BF16)
    v_shape = jax.ShapeDtypeStruct((batch, MLA_HEADS, seq, MLA_V), BF16)
    const2 = lambda i: (0, 0)
    qk_spec = pl.BlockSpec((1, MLA_HEADS, tm, MLA_QK_PAD), lambda i: (i // tpb, 0, i % tpb, 0))
    return pl.pallas_call(
        _mla_proj_kernel,
        grid=(batch * tpb,),
        in_specs=[pl.BlockSpec((tm, MLA_Q_RANK), lambda i: (i, Z_QLAT // MLA_Q_RANK)),
                  pl.BlockSpec((tm, MLA_KV_RANK), lambda i: (i, Z_KVLAT // MLA_KV_RANK)),
                  pl.BlockSpec((tm, LANES), lambda i: (i, Z_KROPE // LANES)),
                  pl.BlockSpec((tm, 1), lambda i: (i, 0)),
                  pl.BlockSpec((1, LANES), const2),
                  pl.BlockSpec((1, LANES), const2),
                  pl.BlockSpec((1, LANES), const2),
                  pl.BlockSpec((1, MLA_Q_RANK), const2),
                  pl.BlockSpec((1, MLA_KV_RANK), const2),
                  pl.BlockSpec((MLA_HEADS, MLA_Q_RANK, MLA_QK_PAD), lambda i: (0, 0, 0)),
                  pl.BlockSpec((MLA_KV_RANK, MLA_HEADS * (MLA_NOPE + MLA_V)), const2)],
        out_specs=[qk_spec, qk_spec,
                   pl.BlockSpec((1, MLA_HEADS, tm, MLA_V), lambda i: (i // tpb, 0, i % tpb, 0))],
        out_shape=[qk_shape, qk_shape, v_shape],
        compiler_params=_params(("arbitrary",), 48),
        name="mla_proj",
    )(z, z, z, pos, inv, sgn, cm, gq, gkv, wq, wkv)


ATTN_BLOCK = 1024
ATTN_LOOP_CHAINS = 1
ATTN_DIAG_CHAINS = 2


def _attn_kernel(q_ref, k_ref, v_ref, o_ref, m_sc, l_sc, acc_sc):
    blk = ATTN_BLOCK
    nt = (((1,), (1,)), ((), ()))
    qi = pl.program_id(2)
    m_sc[...] = jnp.full(m_sc.shape, -jnp.inf, F32)
    l_sc[...] = jnp.zeros(l_sc.shape, F32)
    acc_sc[...] = jnp.zeros(acc_sc.shape, F32)

    def update(rows, s, v):
        m_prev = m_sc[rows]
        m_new = jnp.maximum(m_prev, jnp.max(s, axis=-1, keepdims=True))
        alpha = jnp.exp2(m_prev - m_new)
        ps = [jnp.exp2(s[:, c * LANES:(c + 1) * LANES] - m_new)
              for c in range(s.shape[1] // LANES)]
        l_sc[rows] = alpha * l_sc[rows] + functools.reduce(lambda a, b: a + b, ps)
        p = jnp.concatenate(ps, axis=1).astype(BF16)
        acc_sc[rows] = alpha * acc_sc[rows] + jnp.dot(p, v, preferred_element_type=F32)
        m_sc[rows] = m_new

    def body(kb, carry):
        ks = pl.multiple_of(kb * blk, blk)
        k = k_ref[0, 0, pl.ds(ks, blk), :]
        v = v_ref[0, 0, pl.ds(ks, blk), :]
        sub = blk // ATTN_LOOP_CHAINS
        rows = [slice(r * sub, (r + 1) * sub) for r in range(ATTN_LOOP_CHAINS)]
        scores = [lax.dot_general(q_ref[0, 0, r], k, nt, preferred_element_type=F32)
                  for r in rows]
        for r, s in zip(rows, scores):
            update(r, s, v)
        return carry

    lax.fori_loop(0, qi, body, 0)

    ks = pl.multiple_of(qi * blk, blk)
    sub = blk // ATTN_DIAG_CHAINS
    for r in range(ATTN_DIAG_CHAINS):
        n = (r + 1) * sub
        k = k_ref[0, 0, pl.ds(ks, n), :]
        v = v_ref[0, 0, pl.ds(ks, n), :]
        q = q_ref[0, 0, r * sub:(r + 1) * sub, :]
        s = lax.dot_general(q, k, nt, preferred_element_type=F32)
        row = lax.broadcasted_iota(jnp.int32, s.shape, 0) + r * sub
        col = lax.broadcasted_iota(jnp.int32, s.shape, 1)
        update(slice(r * sub, n), jnp.where(col <= row, s, -jnp.inf), v)
    l = jnp.sum(l_sc[...], axis=-1, keepdims=True)
    o_ref[...] = (acc_sc[...] / l).astype(o_ref.dtype)


def _attn(q, k, v, batch, seq):
    blk = ATTN_BLOCK
    nq = seq // blk
    return pl.pallas_call(
        _attn_kernel,
        grid=(batch, MLA_HEADS, nq),
        in_specs=[pl.BlockSpec((1, 1, blk, MLA_QK_PAD), lambda b, h, i: (b, h, i, 0)),
                  pl.BlockSpec((1, 1, seq, MLA_QK_PAD), lambda b, h, i: (b, h, 0, 0)),
                  pl.BlockSpec((1, 1, seq, MLA_V), lambda b, h, i: (b, h, 0, 0))],
        out_specs=pl.BlockSpec((blk, MLA_V), lambda b, h, i: (b * nq + i, h)),
        out_shape=jax.ShapeDtypeStruct((batch * seq, MLA_HEADS * MLA_V), BF16),
        scratch_shapes=[pltpu.VMEM((blk, LANES), F32), pltpu.VMEM((blk, LANES), F32),
                        pltpu.VMEM((blk, MLA_V), F32)],
        compiler_params=_params(("arbitrary", "arbitrary", "arbitrary"), 32),
        name="attn",
    )(q, k, v)


def _gla_constants():
    c = GLA_CHUNK
    idx = np.arange(c)
    w = np.zeros((GLA_LEVELS + 2, c, c), np.float32)
    masks = np.zeros((GLA_LEVELS + 1, c, c), np.float32)
    for l in range(GLA_LEVELS):
        half = 1 << l
        blk = idx // (2 * half)
        mid = blk * 2 * half + half - 1
        upper = (idx % (2 * half)) >= half
        t = idx[None, :]
        up_rows = (t > mid[:, None]) & (t <= idx[:, None])
        lo_rows = (t > idx[:, None]) & (t <= mid[:, None])
        w[l] = np.where(upper[:, None], up_rows, lo_rows)
        masks[l] = (blk[:, None] == blk[None, :]) & upper[:, None] & (~upper[None, :])
    w[GLA_LEVELS] = idx[None, :] <= idx[:, None]
    w[GLA_LEVELS + 1] = idx[None, :] > idx[:, None]
    masks[GLA_LEVELS] = np.eye(c)
    w = w.reshape(-1, c)
    return np.concatenate([w, w], axis=1), masks


def _split_bf16(x):
    hi = x.astype(BF16)
    return hi, (x - hi.astype(F32)).astype(BF16)


def _gla_kernel(q_ref, k_ref, v_ref, lr_ref, og_ref, wg_ref, bg_ref, go_ref, ws_ref, mask_ref,
                o_ref, st_ref):
    c = GLA_CHUNK
    dk, dv = GLA_DK, GLA_DV
    nt = (((1,), (1,)), ((), ()))

    @pl.when(pl.program_id(1) == 0)
    def _():
        st_ref[...] = jnp.zeros(st_ref.shape, F32)

    lr = lr_ref[...]

    def qk(h):
        q = q_ref[:, h * dk:(h + 1) * dk] * (dk ** -0.5)
        return q, k_ref[:, h * dk:(h + 1) * dk]

    def decays(h):
        x = (jnp.dot(lr, wg_ref[:, h * dk:(h + 1) * dk], preferred_element_type=F32)
             + bg_ref[:, h * dk:(h + 1) * dk])
        log2_a = (-(jnp.maximum(-x, 0.0) + jnp.log1p(jnp.exp(-jnp.abs(x))))
                  * (np.log2(np.e) / GLA_TAU))
        expo = jnp.dot(ws_ref[...], jnp.concatenate(_split_bf16(log2_a), axis=0),
                       preferred_element_type=F32)
        p = jnp.exp2(expo)
        return p, p.astype(BF16)

    def intra(h, pb):
        q, k = qk(h)
        attn = mask_ref[GLA_LEVELS] * lax.dot_general(q, k, nt, preferred_element_type=F32)
        for l in range(GLA_LEVELS):
            pl_ = pb[l * c:(l + 1) * c]
            s = lax.dot_general(q * pl_, k * pl_, nt, preferred_element_type=F32)
            attn = attn + mask_ref[l] * s
        return attn.astype(BF16)

    def finish(h, p, pb, attn):
        q, k = qk(h)
        v = v_ref[:, h * dv:(h + 1) * dv]
        eb = pb[GLA_LEVELS * c:(GLA_LEVELS + 1) * c]
        ebl = pb[(GLA_LEVELS + 1) * c:(GLA_LEVELS + 2) * c]
        st = st_ref[h]
        o = (lax.dot_general(q * eb, st.astype(BF16), nt, preferred_element_type=F32)
             + jnp.dot(attn, v, preferred_element_type=F32))
        v_t = v.astype(F32).T.astype(BF16)
        decay = p[(GLA_LEVELS + 1) * c - 1:(GLA_LEVELS + 1) * c]
        st_ref[h] = st * decay + jnp.dot(v_t, k * ebl, preferred_element_type=F32)
        y = _rms(o, go_ref[...])
        og = og_ref[:, h * dv:(h + 1) * dv].astype(F32)
        o_ref[:, h * dv:(h + 1) * dv] = (y * (og * jax.nn.sigmoid(og))).astype(o_ref.dtype)

    dec, att = {}, {}
    for h in range(GLA_HEADS):
        dec[h] = decays(h)
    for h in range(GLA_HEADS):
        att[h] = intra(h, dec[h][1])
    for h in range(GLA_HEADS):
        finish(h, *dec[h], att[h])


def _gla(z, wg, bg, go, batch, seq):
    c = GLA_CHUNK
    nc = seq // c
    ws, masks = _gla_constants()
    hk, hv = GLA_HEADS * GLA_DK, GLA_HEADS * GLA_DV
    const2 = lambda b, i: (0, 0)
    return pl.pallas_call(
        _gla_kernel,
        grid=(batch, nc),
        in_specs=[pl.BlockSpec((c, hk), lambda b, i: (b * nc + i, Z_GQ // hk)),
                  pl.BlockSpec((c, hk), lambda b, i: (b * nc + i, Z_GK // hk)),
                  pl.BlockSpec((c, hv), lambda b, i: (b * nc + i, Z_GV // hv)),
                  pl.BlockSpec((c, LANES), lambda b, i: (b * nc + i, Z_GLR // LANES)),
                  pl.BlockSpec((c, hv), lambda b, i: (b * nc + i, Z_OG // hv)),
                  pl.BlockSpec((LANES, hk), const2),
                  pl.BlockSpec((1, hk), const2),
                  pl.BlockSpec((1, GLA_DV), const2),
                  pl.BlockSpec(ws.shape, const2),
                  pl.BlockSpec(masks.shape, lambda b, i: (0, 0, 0))],
        out_specs=pl.BlockSpec((c, hv), lambda b, i: (b * nc + i, 0)),
        out_shape=jax.ShapeDtypeStruct((batch * seq, hv), BF16),
        scratch_shapes=[pltpu.VMEM((GLA_HEADS, GLA_DV, GLA_DK), F32)],
        compiler_params=_params(("arbitrary", "arbitrary"), 48),
        name="gla",
    )(z, z, z, z, z, wg, bg, go, jnp.asarray(ws, BF16), jnp.asarray(masks))


def _mix1_kernel(oa_ref, ob_ref, wa_ref, wb_ref, ga_ref, gb_ref, o_ref, wa_bf, wb_bf):
    @pl.when(pl.program_id(1) == 0)
    def _():
        wa_bf[...] = wa_ref[...].astype(BF16)
        wb_bf[...] = wb_ref[...].astype(BF16)

    ya = jnp.dot(oa_ref[...], wa_bf[...], preferred_element_type=F32)
    yb = jnp.dot(ob_ref[...], wb_bf[...], preferred_element_type=F32)
    o_ref[...] = (jax.nn.sigmoid(ga_ref[...].astype(F32)) * ya
                  + jax.nn.sigmoid(gb_ref[...].astype(F32)) * yb).astype(o_ref.dtype)


def _mix1(oa, ob, wa, wb, z):
    t = oa.shape[0]
    tm, tn = 1024, 512
    return pl.pallas_call(
        _mix1_kernel,
        grid=(D_MODEL // tn, t // tm),
        in_specs=[pl.BlockSpec((tm, oa.shape[1]), lambda j, i: (i, 0)),
                  pl.BlockSpec((tm, ob.shape[1]), lambda j, i: (i, 0)),
                  pl.BlockSpec((wa.shape[0], tn), lambda j, i: (0, j)),
                  pl.BlockSpec((wb.shape[0], tn), lambda j, i: (0, j)),
                  pl.BlockSpec((tm, tn), lambda j, i: (i, Z_GATE_A // tn + j)),
                  pl.BlockSpec((tm, tn), lambda j, i: (i, Z_GATE_B // tn + j))],
        out_specs=pl.BlockSpec((tm, tn), lambda j, i: (i, j)),
        out_shape=jax.ShapeDtypeStruct((t, D_MODEL), BF16),
        scratch_shapes=[pltpu.VMEM((wa.shape[0], tn), BF16), pltpu.VMEM((wb.shape[0], tn), BF16)],
        compiler_params=_params(("arbitrary", "arbitrary"), 48),
        name="mix1",
    )(oa, ob, wa, wb, z, z)


def _mix2_kernel(m_ref, w_ref, x_ref, gm_ref, gpost_ref, gpre_ref, sc_ref, sh_ref,
                 x1_ref, h2_ref):
    for r0 in range(0, m_ref.shape[0], MIX2_SUBTILE):
        rows = slice(r0, r0 + MIX2_SUBTILE)
        y = jnp.dot(m_ref[rows], w_ref[...], preferred_element_type=F32)
        x1 = x_ref[rows] + gm_ref[0] * _rms(y, gpost_ref[...])
        x1_ref[rows] = x1
        h2_ref[rows] = (_rms(x1, gpre_ref[...]) * (1.0 + sc_ref[0])
                        + sh_ref[0]).astype(h2_ref.dtype)


MIX2_SUBTILE = 256


def _mix2(merged, w, x2, gate_m, g_post, g_pre, scale_f, shift_f, seq):
    t = merged.shape[0]
    tm = 512
    tpb = seq // tm
    row = pl.BlockSpec((tm, D_MODEL), lambda i: (i, 0))
    vec = pl.BlockSpec((1, D_MODEL), lambda i: (0, 0))
    per_batch = pl.BlockSpec((1, 1, D_MODEL), lambda i: (i // tpb, 0, 0))
    return pl.pallas_call(
        _mix2_kernel,
        grid=(t // tm,),
        in_specs=[row, pl.BlockSpec((D_MODEL, D_MODEL), lambda i: (0, 0),
                                    pipeline_mode=pl.Buffered(1)), row,
                  per_batch, vec, vec, per_batch, per_batch],
        out_specs=[row, row],
        out_shape=[jax.ShapeDtypeStruct((t, D_MODEL), F32),
                   jax.ShapeDtypeStruct((t, D_MODEL), BF16)],
        compiler_params=_params(("arbitrary",), 48),
        name="mix2",
    )(merged, w, x2, gate_m, g_post, g_pre, scale_f, shift_f)


FFN_UP_CHUNK = 512


def _ffn_up_kernel(h_ref, wa_ref, wv_ref, cwa_ref, cwv_ref, cba_ref, cbv_ref, o_ref,
                   wa_bf, wv_bf, ua_sc, uv_sc, *, tiles_per_batch):
    i = pl.program_id(1)
    tm = h_ref.shape[0]

    @pl.when(i == 0)
    def _():
        wa_bf[...] = wa_ref[...].astype(BF16)
        wv_bf[...] = wv_ref[...].astype(BF16)

    @pl.when(i % tiles_per_batch == 0)
    def _():
        ua_sc[0:SUBLANES] = jnp.zeros((SUBLANES, ua_sc.shape[1]), F32)
        uv_sc[0:SUBLANES] = jnp.zeros((SUBLANES, uv_sc.shape[1]), F32)

    h = h_ref[...]

    def conv(w_ref, cw_ref, cb_ref, u_sc):
        u_sc[SUBLANES:SUBLANES + tm] = jnp.dot(h, w_ref[...], preferred_element_type=F32)
        cw = cw_ref[...]
        acc = cb_ref[...] + u_sc[SUBLANES:SUBLANES + tm] * cw[CONV_WIDTH - 1:CONV_WIDTH]
        for tap in range(1, CONV_WIDTH):
            acc = acc + (u_sc[SUBLANES - tap:SUBLANES - tap + tm]
                         * cw[CONV_WIDTH - 1 - tap:CONV_WIDTH - tap])
        return acc

    a = conv(wa_bf, cwa_ref, cba_ref, ua_sc)
    val = conv(wv_bf, cwv_ref, cbv_ref, uv_sc)
    o_ref[...] = (jax.nn.gelu(a, approximate=True) * val).astype(o_ref.dtype)
    ua_sc[0:SUBLANES] = ua_sc[tm:tm + SUBLANES]
    uv_sc[0:SUBLANES] = uv_sc[tm:tm + SUBLANES]


def _ffn_up(h2, w_up, conv_w, conv_b, seq):
    t = h2.shape[0]
    tm, tn = 1024, 512
    nj = D_FF // tn
    return pl.pallas_call(
        functools.partial(_ffn_up_kernel, tiles_per_batch=seq // tm),
        grid=(nj, t // tm),
        in_specs=[pl.BlockSpec((tm, D_MODEL), lambda j, i: (i, 0)),
                  pl.BlockSpec((D_MODEL, tn), lambda j, i: (0, j)),
                  pl.BlockSpec((D_MODEL, tn), lambda j, i: (0, nj + j)),
                  pl.BlockSpec((CONV_WIDTH, tn), lambda j, i: (0, j)),
                  pl.BlockSpec((CONV_WIDTH, tn), lambda j, i: (0, nj + j)),
                  pl.BlockSpec((1, tn), lambda j, i: (0, j)),
                  pl.BlockSpec((1, tn), lambda j, i: (0, nj + j))],
        out_specs=pl.BlockSpec((tm, tn), lambda j, i: (i, j)),
        out_shape=jax.ShapeDtypeStruct((t, D_FF), BF16),
        scratch_shapes=[pltpu.VMEM((D_MODEL, tn), BF16), pltpu.VMEM((D_MODEL, tn), BF16),
                        pltpu.VMEM((tm + SUBLANES, tn), F32),
                        pltpu.VMEM((tm + SUBLANES, tn), F32)],
        compiler_params=_params(("arbitrary", "arbitrary"), 56),
        name="ffn_up",
    )(h2, w_up, w_up, conv_w, conv_w, conv_b, conv_b)


def _ffn_down_kernel(g_ref, w_ref, x1_ref, gf_ref, gpost_ref, o_ref):
    f = jnp.dot(g_ref[...], w_ref[...], preferred_element_type=F32)
    o_ref[...] = x1_ref[...] + gf_ref[0] * _rms(f, gpost_ref[...])


def _ffn_down(g, w, x1, gate_f, g_post, seq):
    t = g.shape[0]
    tm = 256
    tpb = seq // tm
    return pl.pallas_call(
        _ffn_down_kernel,
        grid=(t // tm,),
        in_specs=[pl.BlockSpec((tm, D_FF), lambda i: (i, 0)),
                  pl.BlockSpec((D_FF, D_MODEL), lambda i: (0, 0), pipeline_mode=pl.Buffered(1)),
                  pl.BlockSpec((tm, D_MODEL), lambda i: (i, 0)),
                  pl.BlockSpec((1, 1, D_MODEL), lambda i: (i // tpb, 0, 0)),
                  pl.BlockSpec((1, D_MODEL), lambda i: (0, 0))],
        out_specs=pl.BlockSpec((tm, D_MODEL), lambda i: (i, 0)),
        out_shape=jax.ShapeDtypeStruct((t, D_MODEL), F32),
        compiler_params=_params(("arbitrary",), 56),
        name="ffn_down",
    )(g, w, x1, gate_f, g_post)


def _swap_halves(w):
    half = w.shape[-1] // 2
    return jnp.concatenate([w[..., half:], w[..., :half]], axis=-1)


_SRC_KROPE = MLA_Q_RANK + MLA_KV_RANK
_SRC_GQ = _SRC_KROPE + MLA_ROPE
_SRC_GK = _SRC_GQ + GLA_HEADS * GLA_DK
_SRC_GV = _SRC_GK + GLA_HEADS * GLA_DK
_SRC_GLR = _SRC_GV + GLA_HEADS * GLA_DV
_SRC_OG = _SRC_GLR + GLA_GATE_RANK
_SRC_GATE_A = _SRC_OG + GLA_HEADS * GLA_DV
_SRC_GATE_B = _SRC_GATE_A + D_MODEL
D_IN = _SRC_GATE_B + D_MODEL
_W_IN_MOVES = (
    (Z_QLAT, 0, _SRC_KROPE),
    (Z_KROPE, _SRC_KROPE, MLA_ROPE),
    (Z_KROPE + MLA_ROPE, _SRC_KROPE + MLA_ROPE // 2, MLA_ROPE // 2),
    (Z_KROPE + 3 * MLA_ROPE // 2, _SRC_KROPE, MLA_ROPE // 2),
    (Z_GLR, _SRC_GLR, GLA_GATE_RANK),
    (Z_GLR + GLA_GATE_RANK, _SRC_GLR, GLA_GATE_RANK),
    (Z_GQ, _SRC_GQ, GLA_HEADS * GLA_DK),
    (Z_GV, _SRC_GV, GLA_HEADS * GLA_DV),
    (Z_OG, _SRC_OG, GLA_HEADS * GLA_DV),
    (Z_GATE_A, _SRC_GATE_A, D_MODEL),
    (Z_GATE_B, _SRC_GATE_B, D_MODEL),
    (Z_GK, _SRC_GK, GLA_HEADS * GLA_DK),
)


def _prep_w_in_kernel(w_ref, o_ref):
    pad0 = Z_GLR + 2 * GLA_GATE_RANK
    o_ref[pad0:Z_GQ, :] = jnp.zeros((Z_GQ - pad0, o_ref.shape[1]), o_ref.dtype)
    for dst, src, width in _W_IN_MOVES:
        o_ref[dst:dst + width, :] = w_ref[src:src + width, :].astype(o_ref.dtype)


def _prep_w_in(w_in_t):
    tk = 256
    return pl.pallas_call(
        _prep_w_in_kernel,
        grid=(D_MODEL // tk,),
        in_specs=[pl.BlockSpec((D_IN, tk), lambda i: (0, i))],
        out_specs=pl.BlockSpec((Z_WIDTH, tk), lambda i: (0, i)),
        out_shape=jax.ShapeDtypeStruct((Z_WIDTH, D_MODEL), BF16),
        compiler_params=_params(("arbitrary",), 48),
        name="prep_w_in",
    )(w_in_t)


def _prep_w_q(w_q_up):
    w = w_q_up.reshape(MLA_Q_RANK, MLA_HEADS, MLA_NOPE + MLA_ROPE)
    pe = w[:, :, MLA_NOPE:]
    w = jnp.concatenate([w, _swap_halves(pe)], axis=-1)
    return jnp.transpose(w, (1, 0, 2)).astype(BF16)


def kernel(x, c, positions, w_ada, b_ada, g_pre_mix, w_in, g_q_lat, w_q_up, g_kv_lat, w_kv_up,
           w_gla_gate_up, b_gla_gate, g_gla_out, w_branch_a, w_branch_b, w_mix_out, g_post_mix,
           g_pre_ffn, w_ffn_up, conv_w, conv_b, w_ffn_down, g_post_ffn):
    batch, seq, _ = x.shape
    depth = w_ada.shape[0]
    t = batch * seq
    row = lambda v: v.reshape(1, -1)

    inv = 1.0 / (ROPE_THETA ** (jnp.arange(0, MLA_ROPE, 2, dtype=F32) / MLA_ROPE))
    zeros64 = jnp.zeros((MLA_ROPE,), F32)
    ones32 = jnp.ones((MLA_ROPE // 2,), F32)
    inv128 = row(jnp.concatenate([inv, inv, zeros64]))
    sgn128 = row(jnp.concatenate([-ones32, ones32, zeros64]))
    cm128 = row(jnp.concatenate([ones32, ones32, zeros64]))
    pos = positions.reshape(t, 1)

    c_pad = jnp.zeros((SUBLANES, D_MODEL), F32).at[:batch].set(c)
    x2 = x.reshape(t, D_MODEL)
    for l in range(depth):
        mod = _ada(c_pad, w_ada[l], row(b_ada[l]))[:batch]
        shift_m, scale_m, gate_m, shift_f, scale_f, gate_f = (
            m.reshape(batch, 1, D_MODEL) for m in jnp.split(mod, 6, axis=-1))

        z = _mm_in(x2, row(g_pre_mix[l]), scale_m, shift_m, _prep_w_in(w_in[l].T), seq)

        q, k, v = _mla_proj(z, pos, inv128, sgn128, cm128, row(g_q_lat[l]), row(g_kv_lat[l]),
                            _prep_w_q(w_q_up[l]), w_kv_up[l].astype(BF16), batch, seq)
        o_a = _attn(q, k, v, batch, seq)

        wg_hi, wg_lo = _split_bf16(w_gla_gate_up[l])
        wg = jnp.zeros((LANES, GLA_HEADS * GLA_DK), BF16)
        wg = wg.at[:GLA_GATE_RANK].set(wg_hi).at[GLA_GATE_RANK:2 * GLA_GATE_RANK].set(wg_lo)
        o_b = _gla(z, wg, row(b_gla_gate[l]), row(g_gla_out[l]), batch, seq)

        merged = _mix1(o_a, o_b, w_branch_a[l], w_branch_b[l], z)
        x1, h2 = _mix2(merged, w_mix_out[l].astype(BF16), x2, gate_m, row(g_post_mix[l]),
                       row(g_pre_ffn[l]), scale_f, shift_f, seq)

        g = _ffn_up(h2, w_ffn_up[l], conv_w[l], row(conv_b[l]), seq)
        x2 = _ffn_down(g, w_ffn_down[l].astype(BF16), x1, gate_f, row(g_post_ffn[l]), seq)
    return x2.reshape(batch, seq, D_MODEL)
```

```python
import functools

import jax
import jax.numpy as jnp
import numpy as np
from jax import lax
from jax.experimental import pallas as pl
from jax.experimental.pallas import tpu as pltpu

F32 = jnp.float32
BF16 = jnp.bfloat16

D_MODEL = 2048
MLA_HEADS = 8
MLA_Q_RANK = 512
MLA_KV_RANK = 256
MLA_NOPE = 128
MLA_ROPE = 64
MLA_V = 128
ROPE_THETA = 10000.0
GLA_HEADS = 4
GLA_DK = 256
GLA_DV = 512
GLA_GATE_RANK = 16
GLA_TAU = 16.0
D_FF = 5632
CONV_WIDTH = 3
EPS = 1e-6

LANES = 128
SUBLANES = 8
MLA_QK_PAD = 256

Z_QLAT = 0
Z_KVLAT = 512
Z_KROPE = 768
Z_GLR = 896
Z_GQ = 1024
Z_GV = 2048
Z_OG = 4096
Z_GATE_A = 6144
Z_GATE_B = 8192
Z_GK = 10240
Z_WIDTH = 11264

_SRC_KROPE = MLA_Q_RANK + MLA_KV_RANK
_SRC_GQ = _SRC_KROPE + MLA_ROPE
_SRC_GK = _SRC_GQ + GLA_HEADS * GLA_DK
_SRC_GV = _SRC_GK + GLA_HEADS * GLA_DK
_SRC_GLR = _SRC_GV + GLA_HEADS * GLA_DV
_SRC_OG = _SRC_GLR + GLA_GATE_RANK
_SRC_GATE_A = _SRC_OG + GLA_HEADS * GLA_DV
_SRC_GATE_B = _SRC_GATE_A + D_MODEL

GLA_CHUNK = 128
GLA_LEVELS = 7

MIB = 1024 * 1024


def _params(semantics, vmem_mib):
    return pltpu.CompilerParams(dimension_semantics=semantics,
                                vmem_limit_bytes=vmem_mib * MIB)


def _rms(x, g):
    return x * lax.rsqrt(jnp.mean(x * x, axis=-1, keepdims=True) + EPS) * g


def _ada_kernel(c_ref, w_ref, b_ref, o_ref):
    c = c_ref[...]
    ca = c * jax.nn.sigmoid(c)
    o_ref[...] = jnp.dot(ca.astype(BF16), w_ref[...].astype(BF16),
                         preferred_element_type=F32) + b_ref[...]


def _ada(c_pad, w, b):
    n = w.shape[1]
    tn = 1024
    return pl.pallas_call(
        _ada_kernel,
        grid=(n // tn,),
        in_specs=[pl.BlockSpec((SUBLANES, D_MODEL), lambda j: (0, 0)),
                  pl.BlockSpec((D_MODEL, tn), lambda j: (0, j)),
                  pl.BlockSpec((1, tn), lambda j: (0, j))],
        out_specs=pl.BlockSpec((SUBLANES, tn), lambda j: (0, j)),
        out_shape=jax.ShapeDtypeStruct((SUBLANES, n), F32),
        compiler_params=_params(("arbitrary",), 32),
        name="ada",
    )(c_pad, w, b)


def _norm_mod_kernel(x_ref, g_ref, sc_ref, sh_ref, o_ref):
    y = _rms(x_ref[...], g_ref[...])
    o_ref[...] = (y * (1.0 + sc_ref[0]) + sh_ref[0]).astype(o_ref.dtype)


def _norm_mod(x2, g, scale, shift, seq):
    t = x2.shape[0]
    tm = 512
    tpb = seq // tm
    return pl.pallas_call(
        _norm_mod_kernel,
        grid=(t // tm,),
        in_specs=[pl.BlockSpec((tm, D_MODEL), lambda i: (i, 0)),
                  pl.BlockSpec((1, D_MODEL), lambda i: (0, 0)),
                  pl.BlockSpec((1, 1, D_MODEL), lambda i: (i // tpb, 0, 0)),
                  pl.BlockSpec((1, 1, D_MODEL), lambda i: (i // tpb, 0, 0))],
        out_specs=pl.BlockSpec((tm, D_MODEL), lambda i: (i, 0)),
        out_shape=jax.ShapeDtypeStruct((t, D_MODEL), BF16),
        compiler_params=_params(("arbitrary",), 32),
        name="norm_mod",
    )(x2, g, scale, shift)


MM_IN_TN = 1024
MM_IN_SRC_ALIGN = 16
_MM_IN_SRC_ROWS = (0,
                   _SRC_GQ,
                   _SRC_GV, _SRC_GV + MM_IN_TN,
                   _SRC_OG, _SRC_OG + MM_IN_TN,
                   _SRC_GATE_A, _SRC_GATE_A + MM_IN_TN,
                   _SRC_GATE_B, _SRC_GATE_B + MM_IN_TN,
                   _SRC_GK)


def _mm_in_kernel(src_ref, h_ref, w_ref, wsp_ref, o_ref, w_bf):
    del src_ref
    j, i = pl.program_id(0), pl.program_id(1)

    @pl.when(i == 0)
    def _():
        w_bf[...] = w_ref[...].astype(BF16)

    @pl.when((i == 0) & (j == 0))
    def _():
        w_bf[Z_KROPE:Z_GQ] = wsp_ref[...].astype(BF16)

    o_ref[...] = lax.dot_general(h_ref[...], w_bf[...], (((1,), (1,)), ((), ())),
                                 preferred_element_type=F32).astype(o_ref.dtype)


def _mm_in(h, w_t, w_special):
    t = h.shape[0]
    tm, tn = 1024, MM_IN_TN
    grid_spec = pltpu.PrefetchScalarGridSpec(
        num_scalar_prefetch=1,
        grid=(Z_WIDTH // tn, t // tm),
        in_specs=[pl.BlockSpec((tm, D_MODEL), lambda j, i, src: (i, 0)),
                  pl.BlockSpec((pl.Element(tn), pl.Element(D_MODEL)),
                               lambda j, i, src: (src[j] * MM_IN_SRC_ALIGN, 0)),
                  pl.BlockSpec((Z_GQ - Z_KROPE, D_MODEL), lambda j, i, src: (0, 0))],
        out_specs=pl.BlockSpec((tm, tn), lambda j, i, src: (i, j)),
        scratch_shapes=[pltpu.VMEM((tn, D_MODEL), BF16)])
    return pl.pallas_call(
        _mm_in_kernel,
        grid_spec=grid_spec,
        out_shape=jax.ShapeDtypeStruct((t, Z_WIDTH), BF16),
        compiler_params=_params(("arbitrary", "arbitrary"), 48),
        name="mm_in",
    )(jnp.asarray([r // MM_IN_SRC_ALIGN for r in _MM_IN_SRC_ROWS], jnp.int32), h, w_t, w_special)


def _mla_proj_kernel(ql_ref, kvl_ref, kr_ref, pos_ref, inv_ref, sgn_ref, cm_ref,
                     gq_ref, gkv_ref, wq_ref, wkv_ref, q_ref, k_ref, v_ref):
    ang = pos_ref[...].astype(F32) * inv_ref[...]
    cosm = jnp.cos(ang) * cm_ref[...]
    sinm = jnp.sin(ang) * sgn_ref[...]

    def rope(r):
        return r * cosm + pltpu.roll(r, MLA_ROPE, 1) * sinm

    scale = (MLA_NOPE + MLA_ROPE) ** -0.5 * np.log2(np.e)
    qn =_rms(ql_ref[...].astype(F32), gq_ref[...]).astype(BF16)
    for h in range(MLA_HEADS):
        r = jnp.dot(qn, wq_ref[h], preferred_element_type=F32) * scale
        q_ref[0, h, :, 0:MLA_NOPE] = r[:, 0:MLA_NOPE].astype(BF16)
        q_ref[0, h, :, MLA_NOPE:MLA_QK_PAD] = rope(r[:, MLA_NOPE:MLA_QK_PAD]).astype(BF16)

    kvn = _rms(kvl_ref[...].astype(F32), gkv_ref[...]).astype(BF16)
    kv = jnp.dot(kvn, wkv_ref[...], preferred_element_type=F32).astype(BF16)
    kpe = rope(kr_ref[...].astype(F32)).astype(BF16)
    hw = MLA_NOPE + MLA_V
    for h in range(MLA_HEADS):
        k_ref[0, h, :, 0:MLA_NOPE] = kv[:, h * hw:h * hw + MLA_NOPE]
        k_ref[0, h, :, MLA_NOPE:MLA_QK_PAD] = kpe
        v_ref[0, h] = kv[:, h * hw + MLA_NOPE:(h + 1) * hw]


def _mla_proj(z, pos, inv, sgn, cm, gq, gkv, wq, wkv, batch, seq):
    tm = 512
    tpb = seq // tm
    qk_shape = jax.ShapeDtypeStruct((batch, MLA_HEADS, seq, MLA_QK_PAD), BF16)
    v_shape = jax.ShapeDtypeStruct((batch, MLA_HEADS, seq, MLA_V), BF16)
    const2 = lambda i: (0, 0)
    qk_spec = pl.BlockSpec((1, MLA_HEADS, tm, MLA_QK_PAD), lambda i: (i // tpb, 0, i % tpb, 0))
    return pl.pallas_call(
        _mla_proj_kernel,
        grid=(batch * tpb,),
        in_specs=[pl.BlockSpec((tm, MLA_Q_RANK), lambda i: (i, Z_QLAT // MLA_Q_RANK)),
                  pl.BlockSpec((tm, MLA_KV_RANK), lambda i: (i, Z_KVLAT // MLA_KV_RANK)),
                  pl.BlockSpec((tm, LANES), lambda i: (i, Z_KROPE // LANES)),
                  pl.BlockSpec((tm, 1), lambda i: (i, 0)),
                  pl.BlockSpec((1, LANES), const2),
                  pl.BlockSpec((1, LANES), const2),
                  pl.BlockSpec((1, LANES), const2),
                  pl.BlockSpec((1, MLA_Q_RANK), const2),
                  pl.BlockSpec((1, MLA_KV_RANK), const2),
                  pl.BlockSpec((MLA_HEADS, MLA_Q_RANK, MLA_QK_PAD), lambda i: (0, 0, 0)),
                  pl.BlockSpec((MLA_KV_RANK, MLA_HEADS * (MLA_NOPE + MLA_V)), const2)],
        out_specs=[qk_spec, qk_spec,
                   pl.BlockSpec((1, MLA_HEADS, tm, MLA_V), lambda i: (i // tpb, 0, i % tpb, 0))],
        out_shape=[qk_shape, qk_shape, v_shape],
        compiler_params=_params(("arbitrary",), 48),
        name="mla_proj",
    )(z, z, z, pos, inv, sgn, cm, gq, gkv, wq, wkv)


ATTN_BLOCK = 1024
ATTN_LOOP_CHAINS = 1
ATTN_DIAG_CHAINS = 2


def _attn_kernel(q_ref, k_ref, v_ref, o_ref, m_sc, l_sc, acc_sc):
    blk = ATTN_BLOCK
    nt = (((1,), (1,)), ((), ()))
    qi = pl.program_id(2)
    m_sc[...] = jnp.full(m_sc.shape, -jnp.inf, F32)
    l_sc[...] = jnp.zeros(l_sc.shape, F32)
    acc_sc[...] = jnp.zeros(acc_sc.shape, F32)

    def update(rows, s, v):
        m_prev = m_sc[rows]
        m_new = jnp.maximum(m_prev, jnp.max(s, axis=-1, keepdims=True))
        alpha = jnp.exp2(m_prev - m_new)
        ps = [jnp.exp2(s[:, c * LANES:(c + 1) * LANES] - m_new)
              for c in range(s.shape[1] // LANES)]
        l_sc[rows] = alpha * l_sc[rows] + functools.reduce(lambda a, b: a + b, ps)
        p = jnp.concatenate(ps, axis=1).astype(BF16)
        acc_sc[rows] = alpha * acc_sc[rows] + jnp.dot(p, v, preferred_element_type=F32)
        m_sc[rows] = m_new

    def body(kb, carry):
        ks = pl.multiple_of(kb * blk, blk)
        k = k_ref[0, 0, pl.ds(ks, blk), :]
        v = v_ref[0, 0, pl.ds(ks, blk), :]
        sub = blk // ATTN_LOOP_CHAINS
        rows = [slice(r * sub, (r + 1) * sub) for r in range(ATTN_LOOP_CHAINS)]
        scores = [lax.dot_general(q_ref[0, 0, r], k, nt, preferred_element_type=F32)
                  for r in rows]
        for r, s in zip(rows, scores):
            update(r, s, v)
        return carry

    lax.fori_loop(0, qi, body, 0)

    ks = pl.multiple_of(qi * blk, blk)
    sub = blk // ATTN_DIAG_CHAINS
    for r in range(ATTN_DIAG_CHAINS):
        n = (r + 1) * sub
        k = k_ref[0, 0, pl.ds(ks, n), :]
        v = v_ref[0, 0, pl.ds(ks, n), :]
        q = q_ref[0, 0, r * sub:(r + 1) * sub, :]
        s = lax.dot_general(q, k, nt, preferred_element_type=F32)
        row = lax.broadcasted_iota(jnp.int32, s.shape, 0) + r * sub
        col = lax.broadcasted_iota(jnp.int32, s.shape, 1)
        update(slice(r * sub, n), jnp.where(col <= row, s, -jnp.inf), v)
    l = jnp.sum(l_sc[...], axis=-1, keepdims=True)
    o_ref[...] = (acc_sc[...] / l).astype(o_ref.dtype)


def _attn(q, k, v, batch, seq):
    blk = ATTN_BLOCK
    nq = seq // blk
    return pl.pallas_call(
        _attn_kernel,
        grid=(batch, MLA_HEADS, nq),
        in_specs=[pl.BlockSpec((1, 1, blk, MLA_QK_PAD), lambda b, h, i: (b, h, i, 0)),
                  pl.BlockSpec((1, 1, seq, MLA_QK_PAD), lambda b, h, i: (b, h, 0, 0)),
                  pl.BlockSpec((1, 1, seq, MLA_V), lambda b, h, i: (b, h, 0, 0))],
        out_specs=pl.BlockSpec((blk, MLA_V), lambda b, h, i: (b * nq + i, h)),
        out_shape=jax.ShapeDtypeStruct((batch * seq, MLA_HEADS * MLA_V), BF16),
        scratch_shapes=[pltpu.VMEM((blk, LANES), F32), pltpu.VMEM((blk, LANES), F32),
                        pltpu.VMEM((blk, MLA_V), F32)],
        compiler_params=_params(("arbitrary", "arbitrary", "arbitrary"), 32),
        name="attn",
    )(q, k, v)


def _gla_constants():
    c = GLA_CHUNK
    idx = np.arange(c)
    w = np.zeros((GLA_LEVELS + 2, c, c), np.float32)
    masks = np.zeros((GLA_LEVELS + 1, c, c), np.float32)
    for l in range(GLA_LEVELS):
        half = 1 << l
        blk = idx // (2 * half)
        mid = blk * 2 * half + half - 1
        upper = (idx % (2 * half)) >= half
        t = idx[None, :]
        up_rows = (t > mid[:, None]) & (t <= idx[:, None])
        lo_rows = (t > idx[:, None]) & (t <= mid[:, None])
        w[l] = np.where(upper[:, None], up_rows, lo_rows)
        masks[l] = (blk[:, None] == blk[None, :]) & upper[:, None] & (~upper[None, :])
    w[GLA_LEVELS] = idx[None, :] <= idx[:, None]
    w[GLA_LEVELS + 1] = idx[None, :] > idx[:, None]
    masks[GLA_LEVELS] = np.eye(c)
    w = w.reshape(-1, c)
    return np.concatenate([w, w], axis=1), masks


def _split_bf16(x):
    hi = x.astype(BF16)
    return hi, (x - hi.astype(F32)).astype(BF16)


def _gla_kernel(q_ref, k_ref, v_ref, lr_ref, og_ref, wg_ref, bg_ref, go_ref, ws_ref, mask_ref,
                o_ref, st_ref):
    c = GLA_CHUNK
    dk, dv = GLA_DK, GLA_DV
    nt = (((1,), (1,)), ((), ()))

    @pl.when(pl.program_id(1) == 0)
    def _():
        st_ref[...] = jnp.zeros(st_ref.shape, F32)

    lr = lr_ref[...]

    def qk(h):
        q = q_ref[:, h * dk:(h + 1) * dk] * (dk ** -0.5)
        return q, k_ref[:, h * dk:(h + 1) * dk]

    def decays(h):
        x = (jnp.dot(lr, wg_ref[:, h * dk:(h + 1) * dk], preferred_element_type=F32)
             + bg_ref[:, h * dk:(h + 1) * dk])
        log2_a = (-(jnp.maximum(-x, 0.0) + jnp.log1p(jnp.exp(-jnp.abs(x))))
                  * (np.log2(np.e) / GLA_TAU))
        expo = jnp.dot(ws_ref[...], jnp.concatenate(_split_bf16(log2_a), axis=0),
                       preferred_element_type=F32)
        p = jnp.exp2(expo)
        return p, p.astype(BF16)

    def intra(h, pb):
        q, k = qk(h)
        attn = mask_ref[GLA_LEVELS] * lax.dot_general(q, k, nt, preferred_element_type=F32)
        for l in range(GLA_LEVELS):
            pl_ = pb[l * c:(l + 1) * c]
            s = lax.dot_general(q * pl_, k * pl_, nt, preferred_element_type=F32)
            attn = attn + mask_ref[l] * s
        return attn.astype(BF16)

    def finish(h, p, pb, attn):
        q, k = qk(h)
        v = v_ref[:, h * dv:(h + 1) * dv]
        eb = pb[GLA_LEVELS * c:(GLA_LEVELS + 1) * c]
        ebl = pb[(GLA_LEVELS + 1) * c:(GLA_LEVELS + 2) * c]
        st = st_ref[h]
        o = (lax.dot_general(q * eb, st.astype(BF16), nt, preferred_element_type=F32)
             + jnp.dot(attn, v, preferred_element_type=F32))
        v_t = v.astype(F32).T.astype(BF16)
        decay = p[(GLA_LEVELS + 1) * c - 1:(GLA_LEVELS + 1) * c]
        st_ref[h] = st * decay + jnp.dot(v_t, k * ebl, preferred_element_type=F32)
        y = _rms(o, go_ref[...])
        og = og_ref[:, h * dv:(h + 1) * dv].astype(F32)
        o_ref[:, h * dv:(h + 1) * dv] = (y * (og * jax.nn.sigmoid(og))).astype(o_ref.dtype)

    dec, att = {}, {}
    for h in range(GLA_HEADS):
        dec[h] = decays(h)
    for h in range(GLA_HEADS):
        att[h] = intra(h, dec[h][1])
    for h in range(GLA_HEADS):
        finish(h, *dec[h], att[h])


def _gla(z, wg, bg, go, batch, seq):
    c = GLA_CHUNK
    nc = seq // c
    ws, masks = _gla_constants()
    hk, hv = GLA_HEADS * GLA_DK, GLA_HEADS * GLA_DV
    const2 = lambda b, i: (0, 0)
    return pl.pallas_call(
        _gla_kernel,
        grid=(batch, nc),
        in_specs=[pl.BlockSpec((c, hk), lambda b, i: (b * nc + i, Z_GQ // hk)),
                  pl.BlockSpec((c, hk), lambda b, i: (b * nc + i, Z_GK // hk)),
                  pl.BlockSpec((c, hv), lambda b, i: (b * nc + i, Z_GV // hv)),
                  pl.BlockSpec((c, LANES), lambda b, i: (b * nc + i, Z_GLR // LANES)),
                  pl.BlockSpec((c, hv), lambda b, i: (b * nc + i, Z_OG // hv)),
                  pl.BlockSpec((LANES, hk), const2),
                  pl.BlockSpec((1, hk), const2),
                  pl.BlockSpec((1, GLA_DV), const2),
                  pl.BlockSpec(ws.shape, const2),
                  pl.BlockSpec(masks.shape, lambda b, i: (0, 0, 0))],
        out_specs=pl.BlockSpec((c, hv), lambda b, i: (b * nc + i, 0)),
        out_shape=jax.ShapeDtypeStruct((batch * seq, hv), BF16),
        scratch_shapes=[pltpu.VMEM((GLA_HEADS, GLA_DV, GLA_DK), F32)],
        compiler_params=_params(("arbitrary", "arbitrary"), 48),
        name="gla",
    )(z, z, z, z, z, wg, bg, go, jnp.asarray(ws, BF16), jnp.asarray(masks))


def _mix1_kernel(oa_ref, ob_ref, wa_ref, wb_ref, ga_ref, gb_ref, o_ref, wa_bf, wb_bf):
    @pl.when(pl.program_id(1) == 0)
    def _():
        wa_bf[...] = wa_ref[...].astype(BF16)
        wb_bf[...] = wb_ref[...].astype(BF16)

    ya = jnp.dot(oa_ref[...], wa_bf[...], preferred_element_type=F32)
    yb = jnp.dot(ob_ref[...], wb_bf[...], preferred_element_type=F32)
    o_ref[...] = (jax.nn.sigmoid(ga_ref[...].astype(F32)) * ya
                  + jax.nn.sigmoid(gb_ref[...].astype(F32)) * yb).astype(o_ref.dtype)


def _mix1(oa, ob, wa, wb, z):
    t = oa.shape[0]
    tm, tn = 1024, 512
    return pl.pallas_call(
        _mix1_kernel,
        grid=(D_MODEL // tn, t // tm),
        in_specs=[pl.BlockSpec((tm, oa.shape[1]), lambda j, i: (i, 0)),
                  pl.BlockSpec((tm, ob.shape[1]), lambda j, i: (i, 0)),
                  pl.BlockSpec((wa.shape[0], tn), lambda j, i: (0, j)),
                  pl.BlockSpec((wb.shape[0], tn), lambda j, i: (0, j)),
                  pl.BlockSpec((tm, tn), lambda j, i: (i, Z_GATE_A // tn + j)),
                  pl.BlockSpec((tm, tn), lambda j, i: (i, Z_GATE_B // tn + j))],
        out_specs=pl.BlockSpec((tm, tn), lambda j, i: (i, j)),
        out_shape=jax.ShapeDtypeStruct((t, D_MODEL), BF16),
        scratch_shapes=[pltpu.VMEM((wa.shape[0], tn), BF16), pltpu.VMEM((wb.shape[0], tn), BF16)],
        compiler_params=_params(("arbitrary", "arbitrary"), 48),
        name="mix1",
    )(oa, ob, wa, wb, z, z)


def _mix2_kernel(m_ref, w_ref, x_ref, gm_ref, gpost_ref, gpre_ref, sc_ref, sh_ref,
                 x1_ref, h2_ref):
    for r0 in range(0, m_ref.shape[0], MIX2_SUBTILE):
        rows = slice(r0, r0 + MIX2_SUBTILE)
        y = jnp.dot(m_ref[rows], w_ref[...], preferred_element_type=F32)
        x1 = x_ref[rows] + gm_ref[0] * _rms(y, gpost_ref[...])
        x1_ref[rows] = x1
        h2_ref[rows] = (_rms(x1, gpre_ref[...]) * (1.0 + sc_ref[0])
                        + sh_ref[0]).astype(h2_ref.dtype)


MIX2_SUBTILE = 256


def _mix2(merged, w, x2, gate_m, g_post, g_pre, scale_f, shift_f, seq):
    t = merged.shape[0]
    tm = 512
    tpb = seq // tm
    row = pl.BlockSpec((tm, D_MODEL), lambda i: (i, 0))
    vec = pl.BlockSpec((1, D_MODEL), lambda i: (0, 0))
    per_batch = pl.BlockSpec((1, 1, D_MODEL), lambda i: (i // tpb, 0, 0))
    return pl.pallas_call(
        _mix2_kernel,
        grid=(t // tm,),
        in_specs=[row, pl.BlockSpec((D_MODEL, D_MODEL), lambda i: (0, 0),
                                    pipeline_mode=pl.Buffered(1)), row,
                  per_batch, vec, vec, per_batch, per_batch],
        out_specs=[row, row],
        out_shape=[jax.ShapeDtypeStruct((t, D_MODEL), F32),
                   jax.ShapeDtypeStruct((t, D_MODEL), BF16)],
        compiler_params=_params(("arbitrary",), 48),
        name="mix2",
    )(merged, w, x2, gate_m, g_post, g_pre, scale_f, shift_f)


FFN_UP_CHUNK = 512


def _ffn_up_kernel(h_ref, wa_ref, wv_ref, cwa_ref, cwv_ref, cba_ref, cbv_ref, o_ref,
                   wa_bf, wv_bf, ua_sc, uv_sc, *, tiles_per_batch):
    i = pl.program_id(1)
    tm = h_ref.shape[0]

    @pl.when(i == 0)
    def _():
        wa_bf[...] = wa_ref[...].astype(BF16)
        wv_bf[...] = wv_ref[...].astype(BF16)

    @pl.when(i % tiles_per_batch == 0)
    def _():
        ua_sc[0:SUBLANES] = jnp.zeros((SUBLANES, ua_sc.shape[1]), F32)
        uv_sc[0:SUBLANES] = jnp.zeros((SUBLANES, uv_sc.shape[1]), F32)

    h = h_ref[...]

    def conv(w_ref, cw_ref, cb_ref, u_sc):
        u_sc[SUBLANES:SUBLANES + tm] = jnp.dot(h, w_ref[...], preferred_element_type=F32)
        cw = cw_ref[...]
        acc = cb_ref[...] + u_sc[SUBLANES:SUBLANES + tm] * cw[CONV_WIDTH - 1:CONV_WIDTH]
        for tap in range(1, CONV_WIDTH):
            acc = acc + (u_sc[SUBLANES - tap:SUBLANES - tap + tm]
                         * cw[CONV_WIDTH - 1 - tap:CONV_WIDTH - tap])
        return acc

    a = conv(wa_bf, cwa_ref, cba_ref, ua_sc)
    val = conv(wv_bf, cwv_ref, cbv_ref, uv_sc)
    o_ref[...] = (jax.nn.gelu(a, approximate=True) * val).astype(o_ref.dtype)
    ua_sc[0:SUBLANES] = ua_sc[tm:tm + SUBLANES]
    uv_sc[0:SUBLANES] = uv_sc[tm:tm + SUBLANES]


def _ffn_up(h2, w_up, conv_w, conv_b, seq):
    t = h2.shape[0]
    tm, tn = 1024, 512
    nj = D_FF // tn
    return pl.pallas_call(
        functools.partial(_ffn_up_kernel, tiles_per_batch=seq // tm),
        grid=(nj, t // tm),
        in_specs=[pl.BlockSpec((tm, D_MODEL), lambda j, i: (i, 0)),
                  pl.BlockSpec((D_MODEL, tn), lambda j, i: (0, j)),
                  pl.BlockSpec((D_MODEL, tn), lambda j, i: (0, nj + j)),
                  pl.BlockSpec((CONV_WIDTH, tn), lambda j, i: (0, j)),
                  pl.BlockSpec((CONV_WIDTH, tn), lambda j, i: (0, nj + j)),
                  pl.BlockSpec((1, tn), lambda j, i: (0, j)),
                  pl.BlockSpec((1, tn), lambda j, i: (0, nj + j))],
        out_specs=pl.BlockSpec((tm, tn), lambda j, i: (i, j)),
        out_shape=jax.ShapeDtypeStruct((t, D_FF), BF16),
        scratch_shapes=[pltpu.VMEM((D_MODEL, tn), BF16), pltpu.VMEM((D_MODEL, tn), BF16),
                        pltpu.VMEM((tm + SUBLANES, tn), F32),
                        pltpu.VMEM((tm + SUBLANES, tn), F32)],
        compiler_params=_params(("arbitrary", "arbitrary"), 56),
        name="ffn_up",
    )(h2, w_up, w_up, conv_w, conv_w, conv_b, conv_b)


def _ffn_down_kernel(g_ref, w_ref, x1_ref, gf_ref, gpost_ref, o_ref):
    f = jnp.dot(g_ref[...], w_ref[...], preferred_element_type=F32)
    o_ref[...] = x1_ref[...] + gf_ref[0] * _rms(f, gpost_ref[...])


def _ffn_down(g, w, x1, gate_f, g_post, seq):
    t = g.shape[0]
    tm = 256
    tpb = seq // tm
    return pl.pallas_call(
        _ffn_down_kernel,
        grid=(t // tm,),
        in_specs=[pl.BlockSpec((tm, D_FF), lambda i: (i, 0)),
                  pl.BlockSpec((D_FF, D_MODEL), lambda i: (0, 0), pipeline_mode=pl.Buffered(1)),
                  pl.BlockSpec((tm, D_MODEL), lambda i: (i, 0)),
                  pl.BlockSpec((1, 1, D_MODEL), lambda i: (i // tpb, 0, 0)),
                  pl.BlockSpec((1, D_MODEL), lambda i: (0, 0))],
        out_specs=pl.BlockSpec((tm, D_MODEL), lambda i: (i, 0)),
        out_shape=jax.ShapeDtypeStruct((t, D_MODEL), F32),
        compiler_params=_params(("arbitrary",), 56),
        name="ffn_down",
    )(g, w, x1, gate_f, g_post)


def _swap_halves(w):
    half = w.shape[-1] // 2
    return jnp.concatenate([w[..., half:], w[..., :half]], axis=-1)


def _w_in_special(w_t):
    half = MLA_ROPE // 2
    k_rope = w_t[_SRC_KROPE:_SRC_KROPE + MLA_ROPE]
    g_lr = w_t[_SRC_GLR:_SRC_GLR + GLA_GATE_RANK]
    pad = jnp.zeros((Z_GQ - Z_GLR - 2 * GLA_GATE_RANK, w_t.shape[1]), w_t.dtype)
    return jnp.concatenate([k_rope, k_rope[half:], k_rope[:half], g_lr, g_lr, pad], axis=0)


def _prep_w_q(w_q_up):
    w = w_q_up.reshape(MLA_Q_RANK, MLA_HEADS, MLA_NOPE + MLA_ROPE)
    pe = w[:, :, MLA_NOPE:]
    w = jnp.concatenate([w, _swap_halves(pe)], axis=-1)
    return jnp.transpose(w, (1, 0, 2)).astype(BF16)


def kernel(x, c, positions, w_ada, b_ada, g_pre_mix, w_in, g_q_lat, w_q_up, g_kv_lat, w_kv_up,
           w_gla_gate_up, b_gla_gate, g_gla_out, w_branch_a, w_branch_b, w_mix_out, g_post_mix,
           g_pre_ffn, w_ffn_up, conv_w, conv_b, w_ffn_down, g_post_ffn):
    batch, seq, _ = x.shape
    depth = w_ada.shape[0]
    t = batch * seq
    row = lambda v: v.reshape(1, -1)

    inv = 1.0 / (ROPE_THETA ** (jnp.arange(0, MLA_ROPE, 2, dtype=F32) / MLA_ROPE))
    zeros64 = jnp.zeros((MLA_ROPE,), F32)
    ones32 = jnp.ones((MLA_ROPE // 2,), F32)
    inv128 = row(jnp.concatenate([inv, inv, zeros64]))
    sgn128 = row(jnp.concatenate([-ones32, ones32, zeros64]))
    cm128 = row(jnp.concatenate([ones32, ones32, zeros64]))
    pos = positions.reshape(t, 1)

    c_pad = jnp.zeros((SUBLANES, D_MODEL), F32).at[:batch].set(c)
    x2 = x.reshape(t, D_MODEL)
    for l in range(depth):
        mod = _ada(c_pad, w_ada[l], row(b_ada[l]))[:batch]
        shift_m, scale_m, gate_m, shift_f, scale_f, gate_f = (
            m.reshape(batch, 1, D_MODEL) for m in jnp.split(mod, 6, axis=-1))

        h = _norm_mod(x2, row(g_pre_mix[l]), scale_m, shift_m, seq)
        w_in_t = w_in[l].T
        z = _mm_in(h, w_in_t, _w_in_special(w_in_t))

        q, k, v = _mla_proj(z, pos, inv128, sgn128, cm128, row(g_q_lat[l]), row(g_kv_lat[l]),
                            _prep_w_q(w_q_up[l]), w_kv_up[l].astype(BF16), batch, seq)
        o_a = _attn(q, k, v, batch, seq)

        wg_hi, wg_lo = _split_bf16(w_gla_gate_up[l])
        wg = jnp.zeros((LANES, GLA_HEADS * GLA_DK), BF16)
        wg = wg.at[:GLA_GATE_RANK].set(wg_hi).at[GLA_GATE_RANK:2 * GLA_GATE_RANK].set(wg_lo)
        o_b = _gla(z, wg, row(b_gla_gate[l]), row(g_gla_out[l]), batch, seq)

        merged = _mix1(o_a, o_b, w_branch_a[l], w_branch_b[l], z)
        x1, h2 = _mix2(merged, w_mix_out[l].astype(BF16), x2, gate_m, row(g_post_mix[l]),
                       row(g_pre_ffn[l]), scale_f, shift_f, seq)

        g = _ffn_up(h2, w_ffn_up[l], conv_w[l], row(conv_b[l]), seq)
        x2 = _ffn_down(g, w_ffn_down[l].astype(BF16), x1, gate_f, row(g_post_ffn[l]), seq)
    return x2.reshape(batch, seq, D_MODEL)
```

```python
import functools

import jax
import jax.numpy as jnp
import numpy as np
from jax import lax
from jax.experimental import pallas as pl
from jax.experimental.pallas import tpu as pltpu

F32 = jnp.float32
BF16 = jnp.bfloat16

D_MODEL = 2048
MLA_HEADS = 8
MLA_Q_RANK = 512
MLA_KV_RANK = 256
MLA_NOPE = 128
MLA_ROPE = 64
MLA_V = 128
ROPE_THETA = 10000.0
GLA_HEADS = 4
GLA_DK = 256
GLA_DV = 512
GLA_GATE_RANK = 16
GLA_TAU = 16.0
D_FF = 5632
CONV_WIDTH = 3
EPS = 1e-6

LANES = 128
SUBLANES = 8
MLA_QK_PAD = 256

Z_QLAT = 0
Z_KVLAT = 512
Z_KROPE = 768
Z_GLR = 896
Z_GQ = 1024
Z_GV = 2048
Z_OG = 4096
Z_GATE_A = 6144
Z_GATE_B = 8192
Z_GK = 10240
Z_WIDTH = 11264

_SRC_KROPE = MLA_Q_RANK + MLA_KV_RANK
_SRC_GQ = _SRC_KROPE + MLA_ROPE
_SRC_GK = _SRC_GQ + GLA_HEADS * GLA_DK
_SRC_GV = _SRC_GK + GLA_HEADS * GLA_DK
_SRC_GLR = _SRC_GV + GLA_HEADS * GLA_DV
_SRC_OG = _SRC_GLR + GLA_GATE_RANK
_SRC_GATE_A = _SRC_OG + GLA_HEADS * GLA_DV
_SRC_GATE_B = _SRC_GATE_A + D_MODEL

GLA_CHUNK = 128
GLA_LEVELS = 7

MIB = 1024 * 1024


def _params(semantics, vmem_mib):
    return pltpu.CompilerParams(dimension_semantics=semantics,
                                vmem_limit_bytes=vmem_mib * MIB)


def _rms(x, g):
    return x * lax.rsqrt(jnp.mean(x * x, axis=-1, keepdims=True) + EPS) * g


def _ada_kernel(c_ref, w_ref, b_ref, o_ref):
    c = c_ref[...]
    ca = c * jax.nn.sigmoid(c)
    o_ref[...] = jnp.dot(ca.astype(BF16), w_ref[...].astype(BF16),
                         preferred_element_type=F32) + b_ref[...]


def _ada(c_pad, w, b):
    n = w.shape[1]
    tn = 1024
    return pl.pallas_call(
        _ada_kernel,
        grid=(n // tn,),
        in_specs=[pl.BlockSpec((SUBLANES, D_MODEL), lambda j: (0, 0)),
                  pl.BlockSpec((D_MODEL, tn), lambda j: (0, j)),
                  pl.BlockSpec((1, tn), lambda j: (0, j))],
        out_specs=pl.BlockSpec((SUBLANES, tn), lambda j: (0, j)),
        out_shape=jax.ShapeDtypeStruct((SUBLANES, n), F32),
        compiler_params=_params(("arbitrary",), 32),
        name="ada",
    )(c_pad, w, b)


def _norm_mod_kernel(x_ref, g_ref, sc_ref, sh_ref, o_ref):
    y = _rms(x_ref[...], g_ref[...])
    o_ref[...] = (y * (1.0 + sc_ref[0]) + sh_ref[0]).astype(o_ref.dtype)


def _norm_mod(x2, g, scale, shift, seq):
    t = x2.shape[0]
    tm = 512
    tpb = seq // tm
    return pl.pallas_call(
        _norm_mod_kernel,
        grid=(t // tm,),
        in_specs=[pl.BlockSpec((tm, D_MODEL), lambda i: (i, 0)),
                  pl.BlockSpec((1, D_MODEL), lambda i: (0, 0)),
                  pl.BlockSpec((1, 1, D_MODEL), lambda i: (i // tpb, 0, 0)),
                  pl.BlockSpec((1, 1, D_MODEL), lambda i: (i // tpb, 0, 0))],
        out_specs=pl.BlockSpec((tm, D_MODEL), lambda i: (i, 0)),
        out_shape=jax.ShapeDtypeStruct((t, D_MODEL), BF16),
        compiler_params=_params(("arbitrary",), 32),
        name="norm_mod",
    )(x2, g, scale, shift)


MM_IN_TN = 1024
MM_IN_SRC_ALIGN = 16
_MM_IN_SRC_ROWS = (0,
                   _SRC_GQ,
                   _SRC_GV, _SRC_GV + MM_IN_TN,
                   _SRC_OG, _SRC_OG + MM_IN_TN,
                   _SRC_GATE_A, _SRC_GATE_A + MM_IN_TN,
                   _SRC_GATE_B, _SRC_GATE_B + MM_IN_TN,
                   _SRC_GK)


def _mm_in_kernel(src_ref, h_ref, w_ref, wsp_ref, o_ref, w_bf):
    del src_ref
    j, i = pl.program_id(0), pl.program_id(1)

    @pl.when(i == 0)
    def _():
        w_bf[...] = w_ref[...].astype(BF16)

    @pl.when((i == 0) & (j == 0))
    def _():
        w_bf[Z_KROPE:Z_GQ] = wsp_ref[...].astype(BF16)

    o_ref[...] = lax.dot_general(h_ref[...], w_bf[...], (((1,), (1,)), ((), ())),
                                 preferred_element_type=F32).astype(o_ref.dtype)


def _mm_in(h, w_t, w_special):
    t = h.shape[0]
    tm, tn = 1024, MM_IN_TN
    grid_spec = pltpu.PrefetchScalarGridSpec(
        num_scalar_prefetch=1,
        grid=(Z_WIDTH // tn, t // tm),
        in_specs=[pl.BlockSpec((tm, D_MODEL), lambda j, i, src: (i, 0)),
                  pl.BlockSpec((pl.Element(tn), pl.Element(D_MODEL)),
                               lambda j, i, src: (src[j] * MM_IN_SRC_ALIGN, 0)),
                  pl.BlockSpec((Z_GQ - Z_KROPE, D_MODEL), lambda j, i, src: (0, 0))],
        out_specs=pl.BlockSpec((tm, tn), lambda j, i, src: (i, j)),
        scratch_shapes=[pltpu.VMEM((tn, D_MODEL), BF16)])
    return pl.pallas_call(
        _mm_in_kernel,
        grid_spec=grid_spec,
        out_shape=jax.ShapeDtypeStruct((t, Z_WIDTH), BF16),
        compiler_params=_params(("arbitrary", "arbitrary"), 48),
        name="mm_in",
    )(jnp.asarray([r // MM_IN_SRC_ALIGN for r in _MM_IN_SRC_ROWS], jnp.int32), h, w_t, w_special)


def _mla_proj_kernel(ql_ref, kvl_ref, kr_ref, pos_ref, inv_ref, sgn_ref, cm_ref,
                     gq_ref, gkv_ref, wq_ref, wkv_ref, q_ref, k_ref, v_ref):
    ang = pos_ref[...].astype(F32) * inv_ref[...]
    cosm = jnp.cos(ang) * cm_ref[...]
    sinm = jnp.sin(ang) * sgn_ref[...]

    def rope(r):
        return r * cosm + pltpu.roll(r, MLA_ROPE, 1) * sinm

    scale = (MLA_NOPE + MLA_ROPE) ** -0.5 * np.log2(np.e)
    qn =_rms(ql_ref[...].astype(F32), gq_ref[...]).astype(BF16)
    for h in range(MLA_HEADS):
        r = jnp.dot(qn, wq_ref[h], preferred_element_type=F32) * scale
        q_ref[0, h, :, 0:MLA_NOPE] = r[:, 0:MLA_NOPE].astype(BF16)
        q_ref[0, h, :, MLA_NOPE:MLA_QK_PAD] = rope(r[:, MLA_NOPE:MLA_QK_PAD]).astype(BF16)

    kvn = _rms(kvl_ref[...].astype(F32), gkv_ref[...]).astype(BF16)
    kv = jnp.dot(kvn, wkv_ref[...], preferred_element_type=F32).astype(BF16)
    kpe = rope(kr_ref[...].astype(F32)).astype(BF16)
    hw = MLA_NOPE + MLA_V
    for h in range(MLA_HEADS):
        k_ref[0, h, :, 0:MLA_NOPE] = kv[:, h * hw:h * hw + MLA_NOPE]
        k_ref[0, h, :, MLA_NOPE:MLA_QK_PAD] = kpe
        v_ref[0, h] = kv[:, h * hw + MLA_NOPE:(h + 1) * hw]


def _mla_proj(z, pos, inv, sgn, cm, gq, gkv, wq, wkv, batch, seq):
    tm = 512
    tpb = seq // tm
    qk_shape = jax.ShapeDtypeStruct((batch, MLA_HEADS, seq, MLA_QK_PAD), BF16)
    v_shape = jax.ShapeDtypeStruct((batch, MLA_HEADS, seq, MLA_V), BF16)
    const2 = lambda i: (0, 0)
    qk_spec = pl.BlockSpec((1, MLA_HEADS, tm, MLA_QK_PAD), lambda i: (i // tpb, 0, i % tpb, 0))
    return pl.pallas_call(
        _mla_proj_kernel,
        grid=(batch * tpb,),
        in_specs=[pl.BlockSpec((tm, MLA_Q_RANK), lambda i: (i, Z_QLAT // MLA_Q_RANK)),
                  pl.BlockSpec((tm, MLA_KV_RANK), lambda i: (i, Z_KVLAT // MLA_KV_RANK)),
                  pl.BlockSpec((tm, LANES), lambda i: (i, Z_KROPE // LANES)),
                  pl.BlockSpec((tm, 1), lambda i: (i, 0)),
                  pl.BlockSpec((1, LANES), const2),
                  pl.BlockSpec((1, LANES), const2),
                  pl.BlockSpec((1, LANES), const2),
                  pl.BlockSpec((1, MLA_Q_RANK), const2),
                  pl.BlockSpec((1, MLA_KV_RANK), const2),
                  pl.BlockSpec((MLA_HEADS, MLA_Q_RANK, MLA_QK_PAD), lambda i: (0, 0, 0)),
                  pl.BlockSpec((MLA_KV_RANK, MLA_HEADS * (MLA_NOPE + MLA_V)), const2)],
        out_specs=[qk_spec, qk_spec,
                   pl.BlockSpec((1, MLA_HEADS, tm, MLA_V), lambda i: (i // tpb, 0, i % tpb, 0))],
        out_shape=[qk_shape, qk_shape, v_shape],
        compiler_params=_params(("arbitrary",), 48),
        name="mla_proj",
    )(z, z, z, pos, inv, sgn, cm, gq, gkv, wq, wkv)


ATTN_BLOCK = 1024
ATTN_LOOP_CHAINS = 1
ATTN_DIAG_CHAINS = 2


def _attn_kernel(q_ref, k_ref, v_ref, o_ref, m_sc, l_sc, acc_sc):
    blk = ATTN_BLOCK
    nt = (((1,), (1,)), ((), ()))
    qi = pl.program_id(2)
    m_sc[...] = jnp.full(m_sc.shape, -jnp.inf, F32)
    l_sc[...] = jnp.zeros(l_sc.shape, F32)
    acc_sc[...] = jnp.zeros(acc_sc.shape, F32)

    def update(rows, s, v):
        m_prev = m_sc[rows]
        m_new = jnp.maximum(m_prev, jnp.max(s, axis=-1, keepdims=True))
        alpha = jnp.exp2(m_prev - m_new)
        ps = [jnp.exp2(s[:, c * LANES:(c + 1) * LANES] - m_new)
              for c in range(s.shape[1] // LANES)]
        l_sc[rows] = alpha * l_sc[rows] + functools.reduce(lambda a, b: a + b, ps)
        p = jnp.concatenate(ps, axis=1).astype(BF16)
        acc_sc[rows] = alpha * acc_sc[rows] + jnp.dot(p, v, preferred_element_type=F32)
        m_sc[rows] = m_new

    def body(kb, carry):
        ks = pl.multiple_of(kb * blk, blk)
        k = k_ref[0, 0, pl.ds(ks, blk), :]
        v = v_ref[0, 0, pl.ds(ks, blk), :]
        sub = blk // ATTN_LOOP_CHAINS
        rows = [slice(r * sub, (r + 1) * sub) for r in range(ATTN_LOOP_CHAINS)]
        scores = [lax.dot_general(q_ref[0, 0, r], k, nt, preferred_element_type=F32)
                  for r in rows]
        for r, s in zip(rows, scores):
            update(r, s, v)
        return carry

    lax.fori_loop(0, qi, body, 0)

    ks = pl.multiple_of(qi * blk, blk)
    sub = blk // ATTN_DIAG_CHAINS
    for r in range(ATTN_DIAG_CHAINS):
        n = (r + 1) * sub
        k = k_ref[0, 0, pl.ds(ks, n), :]
        v = v_ref[0, 0, pl.ds(ks, n), :]
        q = q_ref[0, 0, r * sub:(r + 1) * sub, :]
        s = lax.dot_general(q, k, nt, preferred_element_type=F32)
        row = lax.broadcasted_iota(jnp.int32, s.shape, 0) + r * sub
        col = lax.broadcasted_iota(jnp.int32, s.shape, 1)
        update(slice(r * sub, n), jnp.where(col <= row, s, -jnp.inf), v)
    l = jnp.sum(l_sc[...], axis=-1, keepdims=True)
    o_ref[...] = (acc_sc[...] / l).astype(o_ref.dtype)


def _attn(q, k, v, batch, seq):
    blk = ATTN_BLOCK
    nq = seq // blk
    return pl.pallas_call(
        _attn_kernel,
        grid=(batch, MLA_HEADS, nq),
        in_specs=[pl.BlockSpec((1, 1, blk, MLA_QK_PAD), lambda b, h, i: (b, h, i, 0)),
                  pl.BlockSpec((1, 1, seq, MLA_QK_PAD), lambda b, h, i: (b, h, 0, 0)),
                  pl.BlockSpec((1, 1, seq, MLA_V), lambda b, h, i: (b, h, 0, 0))],
        out_specs=pl.BlockSpec((blk, MLA_V), lambda b, h, i: (b * nq + i, h)),
        out_shape=jax.ShapeDtypeStruct((batch * seq, MLA_HEADS * MLA_V), BF16),
        scratch_shapes=[pltpu.VMEM((blk, LANES), F32), pltpu.VMEM((blk, LANES), F32),
                        pltpu.VMEM((blk, MLA_V), F32)],
        compiler_params=_params(("arbitrary", "arbitrary", "arbitrary"), 32),
        name="attn",
    )(q, k, v)


def _gla_constants():
    c = GLA_CHUNK
    idx = np.arange(c)
    w = np.zeros((GLA_LEVELS + 2, c, c), np.float32)
    masks = np.zeros((GLA_LEVELS + 1, c, c), np.float32)
    for l in range(GLA_LEVELS):
        half = 1 << l
        blk = idx // (2 * half)
        mid = blk * 2 * half + half - 1
        upper = (idx % (2 * half)) >= half
        t = idx[None, :]
        up_rows = (t > mid[:, None]) & (t <= idx[:, None])
        lo_rows = (t > idx[:, None]) & (t <= mid[:, None])
        w[l] = np.where(upper[:, None], up_rows, lo_rows)
        masks[l] = (blk[:, None] == blk[None, :]) & upper[:, None] & (~upper[None, :])
    w[GLA_LEVELS] = idx[None, :] <= idx[:, None]
    w[GLA_LEVELS + 1] = idx[None, :] > idx[:, None]
    masks[GLA_LEVELS] = np.eye(c)
    w = w.reshape(-1, c)
    return np.concatenate([w, w], axis=1), masks


def _split_bf16(x):
    hi = x.astype(BF16)
    return hi, (x - hi.astype(F32)).astype(BF16)


def _gla_kernel(q_ref, k_ref, v_ref, lr_ref, og_ref, wg_ref, bg_ref, go_ref, ws_ref, mask_ref,
                o_ref, st_ref):
    c = GLA_CHUNK
    dk, dv = GLA_DK, GLA_DV
    nt = (((1,), (1,)), ((), ()))

    @pl.when(pl.program_id(1) == 0)
    def _():
        st_ref[...] = jnp.zeros(st_ref.shape, F32)

    lr = lr_ref[...]

    def qk(h):
        q = q_ref[:, h * dk:(h + 1) * dk] * (dk ** -0.5)
        return q, k_ref[:, h * dk:(h + 1) * dk]

    def gate(h):
        x = (jnp.dot(lr, wg_ref[:, h * dk:(h + 1) * dk], preferred_element_type=F32)
             + bg_ref[:, h * dk:(h + 1) * dk])
        log2_a = (-(jnp.maximum(-x, 0.0) + jnp.log1p(jnp.exp(-jnp.abs(x))))
                  * (np.log2(np.e) / GLA_TAU))
        return jnp.concatenate(_split_bf16(log2_a), axis=0)

    def exponents(log2_a):
        return jnp.dot(ws_ref[...], log2_a, preferred_element_type=F32)

    def level_operands(h, pb):
        q, k = qk(h)
        ops = [(q * pb[l * c:(l + 1) * c], k * pb[l * c:(l + 1) * c]) for l in range(GLA_LEVELS)]
        return ops + [(q, k)]

    def level_scores(ops):
        return [lax.dot_general(a, b, nt, preferred_element_type=F32) for a, b in ops]

    def masked_sum(scores):
        attn = mask_ref[0] * scores[0]
        for l in range(1, GLA_LEVELS + 1):
            attn = attn + mask_ref[l] * scores[l]
        return attn.astype(BF16)

    def mix(h, p, pb, attn):
        q, k = qk(h)
        v = v_ref[:, h * dv:(h + 1) * dv]
        eb = pb[GLA_LEVELS * c:(GLA_LEVELS + 1) * c]
        ebl = pb[(GLA_LEVELS + 1) * c:(GLA_LEVELS + 2) * c]
        st = st_ref[h]
        o = (lax.dot_general(q * eb, st.astype(BF16), nt, preferred_element_type=F32)
             + jnp.dot(attn, v, preferred_element_type=F32))
        v_t = v.astype(F32).T.astype(BF16)
        decay = p[(GLA_LEVELS + 1) * c - 1:(GLA_LEVELS + 1) * c]
        st_ref[h] = st * decay + jnp.dot(v_t, k * ebl, preferred_element_type=F32)
        return o

    def emit(h, o):
        y = _rms(o, go_ref[...])
        og = og_ref[:, h * dv:(h + 1) * dv].astype(F32)
        o_ref[:, h * dv:(h + 1) * dv] = (y * (og * jax.nn.sigmoid(og))).astype(o_ref.dtype)

    heads = range(GLA_HEADS)
    log2_a = [gate(h) for h in heads]
    expo = [exponents(x) for x in log2_a]
    p = [jnp.exp2(e) for e in expo]
    pb = [x.astype(BF16) for x in p]
    operands = [level_operands(h, pb[h]) for h in heads]
    scores = [level_scores(ops) for ops in operands]
    attn = [masked_sum(s) for s in scores]
    out = [mix(h, p[h], pb[h], attn[h]) for h in heads]
    for h in heads:
        emit(h, out[h])


def _gla(z, wg, bg, go, batch, seq):
    c = GLA_CHUNK
    nc = seq // c
    ws, masks = _gla_constants()
    hk, hv = GLA_HEADS * GLA_DK, GLA_HEADS * GLA_DV
    const2 = lambda b, i: (0, 0)
    return pl.pallas_call(
        _gla_kernel,
        grid=(batch, nc),
        in_specs=[pl.BlockSpec((c, hk), lambda b, i: (b * nc + i, Z_GQ // hk)),
                  pl.BlockSpec((c, hk), lambda b, i: (b * nc + i, Z_GK // hk)),
                  pl.BlockSpec((c, hv), lambda b, i: (b * nc + i, Z_GV // hv)),
                  pl.BlockSpec((c, LANES), lambda b, i: (b * nc + i, Z_GLR // LANES)),
                  pl.BlockSpec((c, hv), lambda b, i: (b * nc + i, Z_OG // hv)),
                  pl.BlockSpec((LANES, hk), const2),
                  pl.BlockSpec((1, hk), const2),
                  pl.BlockSpec((1, GLA_DV), const2),
                  pl.BlockSpec(ws.shape, const2),
                  pl.BlockSpec(masks.shape, lambda b, i: (0, 0, 0))],
        out_specs=pl.BlockSpec((c, hv), lambda b, i: (b * nc + i, 0)),
        out_shape=jax.ShapeDtypeStruct((batch * seq, hv), BF16),
        scratch_shapes=[pltpu.VMEM((GLA_HEADS, GLA_DV, GLA_DK), F32)],
        compiler_params=_params(("arbitrary", "arbitrary"), 48),
        name="gla",
    )(z, z, z, z, z, wg, bg, go, jnp.asarray(ws, BF16), jnp.asarray(masks))


def _mix1_kernel(oa_ref, ob_ref, wa_ref, wb_ref, ga_ref, gb_ref, o_ref, wa_bf, wb_bf):
    @pl.when(pl.program_id(1) == 0)
    def _():
        wa_bf[...] = wa_ref[...].astype(BF16)
        wb_bf[...] = wb_ref[...].astype(BF16)

    ya = jnp.dot(oa_ref[...], wa_bf[...], preferred_element_type=F32)
    yb = jnp.dot(ob_ref[...], wb_bf[...], preferred_element_type=F32)
    o_ref[...] = (jax.nn.sigmoid(ga_ref[...].astype(F32)) * ya
                  + jax.nn.sigmoid(gb_ref[...].astype(F32)) * yb).astype(o_ref.dtype)


def _mix1(oa, ob, wa, wb, z):
    t = oa.shape[0]
    tm, tn = 1024, 512
    return pl.pallas_call(
        _mix1_kernel,
        grid=(D_MODEL // tn, t // tm),
        in_specs=[pl.BlockSpec((tm, oa.shape[1]), lambda j, i: (i, 0)),
                  pl.BlockSpec((tm, ob.shape[1]), lambda j, i: (i, 0)),
                  pl.BlockSpec((wa.shape[0], tn), lambda j, i: (0, j)),
                  pl.BlockSpec((wb.shape[0], tn), lambda j, i: (0, j)),
                  pl.BlockSpec((tm, tn), lambda j, i: (i, Z_GATE_A // tn + j)),
                  pl.BlockSpec((tm, tn), lambda j, i: (i, Z_GATE_B // tn + j))],
        out_specs=pl.BlockSpec((tm, tn), lambda j, i: (i, j)),
        out_shape=jax.ShapeDtypeStruct((t, D_MODEL), BF16),
        scratch_shapes=[pltpu.VMEM((wa.shape[0], tn), BF16), pltpu.VMEM((wb.shape[0], tn), BF16)],
        compiler_params=_params(("arbitrary", "arbitrary"), 48),
        name="mix1",
    )(oa, ob, wa, wb, z, z)


def _mix2_kernel(m_ref, w_ref, x_ref, gm_ref, gpost_ref, gpre_ref, sc_ref, sh_ref,
                 x1_ref, h2_ref):
    for r0 in range(0, m_ref.shape[0], MIX2_SUBTILE):
        rows = slice(r0, r0 + MIX2_SUBTILE)
        y = jnp.dot(m_ref[rows], w_ref[...], preferred_element_type=F32)
        x1 = x_ref[rows] + gm_ref[0] * _rms(y, gpost_ref[...])
        x1_ref[rows] = x1
        h2_ref[rows] = (_rms(x1, gpre_ref[...]) * (1.0 + sc_ref[0])
                        + sh_ref[0]).astype(h2_ref.dtype)


MIX2_SUBTILE = 256


def _mix2(merged, w, x2, gate_m, g_post, g_pre, scale_f, shift_f, seq):
    t = merged.shape[0]
    tm = 512
    tpb = seq // tm
    row = pl.BlockSpec((tm, D_MODEL), lambda i: (i, 0))
    vec = pl.BlockSpec((1, D_MODEL), lambda i: (0, 0))
    per_batch = pl.BlockSpec((1, 1, D_MODEL), lambda i: (i // tpb, 0, 0))
    return pl.pallas_call(
        _mix2_kernel,
        grid=(t // tm,),
        in_specs=[row, pl.BlockSpec((D_MODEL, D_MODEL), lambda i: (0, 0),
                                    pipeline_mode=pl.Buffered(1)), row,
                  per_batch, vec, vec, per_batch, per_batch],
        out_specs=[row, row],
        out_shape=[jax.ShapeDtypeStruct((t, D_MODEL), F32),
                   jax.ShapeDtypeStruct((t, D_MODEL), BF16)],
        compiler_params=_params(("arbitrary",), 48),
        name="mix2",
    )(merged, w, x2, gate_m, g_post, g_pre, scale_f, shift_f)


FFN_UP_ROW_CHUNKS = 1


def _ffn_up_kernel(h_ref, wa_ref, wv_ref, cwa_ref, cwv_ref, cba_ref, cbv_ref, o_ref,
                   wa_bf, wv_bf, *u_scs, tiles_per_batch):
    i = pl.program_id(1)
    nr = FFN_UP_ROW_CHUNKS
    ua, uv = u_scs[:nr], u_scs[nr:]
    rc = h_ref.shape[0] // nr
    head, body, tail = slice(0, SUBLANES), slice(SUBLANES, SUBLANES + rc), slice(rc, rc + SUBLANES)

    @pl.when(i == 0)
    def _():
        wa_bf[...] = wa_ref[...].astype(BF16)
        wv_bf[...] = wv_ref[...].astype(BF16)

    @pl.when(i % tiles_per_batch == 0)
    def _():
        ua[0][head] = jnp.zeros((SUBLANES, ua[0].shape[1]), F32)
        uv[0][head] = jnp.zeros((SUBLANES, uv[0].shape[1]), F32)

    @pl.when(i % tiles_per_batch != 0)
    def _():
        ua[0][head] = ua[nr - 1][tail]
        uv[0][head] = uv[nr - 1][tail]

    def matmuls(r):
        h = h_ref[r * rc:(r + 1) * rc]
        for u, w_bf in ((ua, wa_bf), (uv, wv_bf)):
            u[r][body] = jnp.dot(h, w_bf[...], preferred_element_type=F32)
            if r + 1 < nr:
                u[r + 1][head] = u[r][tail]

    def conv(u_sc, cw_ref, cb_ref):
        cw = cw_ref[...]
        acc = cb_ref[...] + u_sc[body] * cw[CONV_WIDTH - 1:CONV_WIDTH]
        for tap in range(1, CONV_WIDTH):
            acc = acc + (u_sc[SUBLANES - tap:SUBLANES - tap + rc]
                         * cw[CONV_WIDTH - 1 - tap:CONV_WIDTH - tap])
        return acc

    def epilogue(r):
        a = conv(ua[r], cwa_ref, cba_ref)
        val = conv(uv[r], cwv_ref, cbv_ref)
        o_ref[r * rc:(r + 1) * rc] = (jax.nn.gelu(a, approximate=True) * val).astype(o_ref.dtype)

    matmuls(0)
    for r in range(1, nr):
        matmuls(r)
        epilogue(r - 1)
    epilogue(nr - 1)


def _ffn_up(h2, w_up, conv_w, conv_b, seq):
    t = h2.shape[0]
    tm, tn = 1024, 512
    nj = D_FF // tn
    return pl.pallas_call(
        functools.partial(_ffn_up_kernel, tiles_per_batch=seq // tm),
        grid=(nj, t // tm),
        in_specs=[pl.BlockSpec((tm, D_MODEL), lambda j, i: (i, 0)),
                  pl.BlockSpec((D_MODEL, tn), lambda j, i: (0, j)),
                  pl.BlockSpec((D_MODEL, tn), lambda j, i: (0, nj + j)),
                  pl.BlockSpec((CONV_WIDTH, tn), lambda j, i: (0, j)),
                  pl.BlockSpec((CONV_WIDTH, tn), lambda j, i: (0, nj + j)),
                  pl.BlockSpec((1, tn), lambda j, i: (0, j)),
                  pl.BlockSpec((1, tn), lambda j, i: (0, nj + j))],
        out_specs=pl.BlockSpec((tm, tn), lambda j, i: (i, j)),
        out_shape=jax.ShapeDtypeStruct((t, D_FF), BF16),
        scratch_shapes=([pltpu.VMEM((D_MODEL, tn), BF16), pltpu.VMEM((D_MODEL, tn), BF16)]
                        + [pltpu.VMEM((tm // FFN_UP_ROW_CHUNKS + SUBLANES, tn), F32)]
                        * (2 * FFN_UP_ROW_CHUNKS)),
        compiler_params=_params(("arbitrary", "arbitrary"), 56),
        name="ffn_up",
    )(h2, w_up, w_up, conv_w, conv_w, conv_b, conv_b)


def _ffn_down_kernel(g_ref, w_ref, x1_ref, gf_ref, gpost_ref, o_ref):
    f = jnp.dot(g_ref[...], w_ref[...], preferred_element_type=F32)
    o_ref[...] = x1_ref[...] + gf_ref[0] * _rms(f, gpost_ref[...])


def _ffn_down(g, w, x1, gate_f, g_post, seq):
    t = g.shape[0]
    tm = 256
    tpb = seq // tm
    return pl.pallas_call(
        _ffn_down_kernel,
        grid=(t // tm,),
        in_specs=[pl.BlockSpec((tm, D_FF), lambda i: (i, 0)),
                  pl.BlockSpec((D_FF, D_MODEL), lambda i: (0, 0), pipeline_mode=pl.Buffered(1)),
                  pl.BlockSpec((tm, D_MODEL), lambda i: (i, 0)),
                  pl.BlockSpec((1, 1, D_MODEL), lambda i: (i // tpb, 0, 0)),
                  pl.BlockSpec((1, D_MODEL), lambda i: (0, 0))],
        out_specs=pl.BlockSpec((tm, D_MODEL), lambda i: (i, 0)),
        out_shape=jax.ShapeDtypeStruct((t, D_MODEL), F32),
        compiler_params=_params(("arbitrary",), 56),
        name="ffn_down",
    )(g, w, x1, gate_f, g_post)


def _swap_halves(w):
    half = w.shape[-1] // 2
    return jnp.concatenate([w[..., half:], w[..., :half]], axis=-1)


def _w_in_special(w_t):
    half = MLA_ROPE // 2
    k_rope = w_t[_SRC_KROPE:_SRC_KROPE + MLA_ROPE]
    g_lr = w_t[_SRC_GLR:_SRC_GLR + GLA_GATE_RANK]
    pad = jnp.zeros((Z_GQ - Z_GLR - 2 * GLA_GATE_RANK, w_t.shape[1]), w_t.dtype)
    return jnp.concatenate([k_rope, k_rope[half:], k_rope[:half], g_lr, g_lr, pad], axis=0)


def _prep_w_q(w_q_up):
    w = w_q_up.reshape(MLA_Q_RANK, MLA_HEADS, MLA_NOPE + MLA_ROPE)
    pe = w[:, :, MLA_NOPE:]
    w = jnp.concatenate([w, _swap_halves(pe)], axis=-1)
    return jnp.transpose(w, (1, 0, 2)).astype(BF16)


def kernel(x, c, positions, w_ada, b_ada, g_pre_mix, w_in, g_q_lat, w_q_up, g_kv_lat, w_kv_up,
           w_gla_gate_up, b_gla_gate, g_gla_out, w_branch_a, w_branch_b, w_mix_out, g_post_mix,
           g_pre_ffn, w_ffn_up, conv_w, conv_b, w_ffn_down, g_post_ffn):
    batch, seq, _ = x.shape
    depth = w_ada.shape[0]
    t = batch * seq
    row = lambda v: v.reshape(1, -1)

    inv = 1.0 / (ROPE_THETA ** (jnp.arange(0, MLA_ROPE, 2, dtype=F32) / MLA_ROPE))
    zeros64 = jnp.zeros((MLA_ROPE,), F32)
    ones32 = jnp.ones((MLA_ROPE // 2,), F32)
    inv128 = row(jnp.concatenate([inv, inv, zeros64]))
    sgn128 = row(jnp.concatenate([-ones32, ones32, zeros64]))
    cm128 = row(jnp.concatenate([ones32, ones32, zeros64]))
    pos = positions.reshape(t, 1)

    c_pad = jnp.zeros((SUBLANES, D_MODEL), F32).at[:batch].set(c)
    x2 = x.reshape(t, D_MODEL)
    for l in range(depth):
        mod = _ada(c_pad, w_ada[l], row(b_ada[l]))[:batch]
        shift_m, scale_m, gate_m, shift_f, scale_f, gate_f = (
            m.reshape(batch, 1, D_MODEL) for m in jnp.split(mod, 6, axis=-1))

        h = _norm_mod(x2, row(g_pre_mix[l]), scale_m, shift_m, seq)
        w_in_t = w_in[l].T
        z = _mm_in(h, w_in_t, _w_in_special(w_in_t))

        q, k, v = _mla_proj(z, pos, inv128, sgn128, cm128, row(g_q_lat[l]), row(g_kv_lat[l]),
                            _prep_w_q(w_q_up[l]), w_kv_up[l].astype(BF16), batch, seq)
        o_a = _attn(q, k, v, batch, seq)

        wg_hi, wg_lo = _split_bf16(w_gla_gate_up[l])
        wg = jnp.zeros((LANES, GLA_HEADS * GLA_DK), BF16)
        wg = wg.at[:GLA_GATE_RANK].set(wg_hi).at[GLA_GATE_RANK:2 * GLA_GATE_RANK].set(wg_lo)
        o_b = _gla(z, wg, row(b_gla_gate[l]), row(g_gla_out[l]), batch, seq)

        merged = _mix1(o_a, o_b, w_branch_a[l], w_branch_b[l], z)
        x1, h2 = _mix2(merged, w_mix_out[l].astype(BF16), x2, gate_m, row(g_post_mix[l]),
                       row(g_pre_ffn[l]), scale_f, shift_f, seq)

        g = _ffn_up(h2, w_ffn_up[l], conv_w[l], row(conv_b[l]), seq)
        x2 = _ffn_down(g, w_ffn_down[l].astype(BF16), x1, gate_f, row(g_post_ffn[l]), seq)
    return x2.reshape(batch, seq, D_MODEL)
```

```python
import functools

import jax
import jax.numpy as jnp
import numpy as np
from jax import lax
from jax.experimental import pallas as pl
from jax.experimental.pallas import tpu as pltpu

F32 = jnp.float32
BF16 = jnp.bfloat16

D_MODEL = 2048
MLA_HEADS = 8
MLA_Q_RANK = 512
MLA_KV_RANK = 256
MLA_NOPE = 128
MLA_ROPE = 64
MLA_V = 128
ROPE_THETA = 10000.0
GLA_HEADS = 4
GLA_DK = 256
GLA_DV = 512
GLA_GATE_RANK = 16
GLA_TAU = 16.0
D_FF = 5632
CONV_WIDTH = 3
EPS = 1e-6

LANES = 128
SUBLANES = 8
MLA_QK_PAD = 256

Z_QLAT = 0
Z_KVLAT = 512
Z_KROPE = 768
Z_GLR = 896
Z_GQ = 1024
Z_GV = 2048
Z_OG = 4096
Z_GATE_A = 6144
Z_GATE_B = 8192
Z_GK = 10240
Z_WIDTH = 11264

_SRC_KROPE = MLA_Q_RANK + MLA_KV_RANK
_SRC_GQ = _SRC_KROPE + MLA_ROPE
_SRC_GK = _SRC_GQ + GLA_HEADS * GLA_DK
_SRC_GV = _SRC_GK + GLA_HEADS * GLA_DK
_SRC_GLR = _SRC_GV + GLA_HEADS * GLA_DV
_SRC_OG = _SRC_GLR + GLA_GATE_RANK
_SRC_GATE_A = _SRC_OG + GLA_HEADS * GLA_DV
_SRC_GATE_B = _SRC_GATE_A + D_MODEL

GLA_CHUNK = 128
GLA_LEVELS = 7

MIB = 1024 * 1024


def _params(semantics, vmem_mib):
    return pltpu.CompilerParams(dimension_semantics=semantics,
                                vmem_limit_bytes=vmem_mib * MIB)


def _rms(x, g):
    return x * lax.rsqrt(jnp.mean(x * x, axis=-1, keepdims=True) + EPS) * g


def _ada_kernel(c_ref, w_ref, b_ref, o_ref):
    c = c_ref[...]
    ca = c * jax.nn.sigmoid(c)
    o_ref[...] = jnp.dot(ca.astype(BF16), w_ref[...].astype(BF16),
                         preferred_element_type=F32) + b_ref[...]


def _ada(c_pad, w, b):
    n = w.shape[1]
    tn = 1024
    return pl.pallas_call(
        _ada_kernel,
        grid=(n // tn,),
        in_specs=[pl.BlockSpec((SUBLANES, D_MODEL), lambda j: (0, 0)),
                  pl.BlockSpec((D_MODEL, tn), lambda j: (0, j)),
                  pl.BlockSpec((1, tn), lambda j: (0, j))],
        out_specs=pl.BlockSpec((SUBLANES, tn), lambda j: (0, j)),
        out_shape=jax.ShapeDtypeStruct((SUBLANES, n), F32),
        compiler_params=_params(("arbitrary",), 32),
        name="ada",
    )(c_pad, w, b)


def _norm_mod_kernel(x_ref, g_ref, sc_ref, sh_ref, o_ref):
    y = _rms(x_ref[...], g_ref[...])
    o_ref[...] = (y * (1.0 + sc_ref[0]) + sh_ref[0]).astype(o_ref.dtype)


def _norm_mod(x2, g, scale, shift, seq):
    t = x2.shape[0]
    tm = 512
    tpb = seq // tm
    return pl.pallas_call(
        _norm_mod_kernel,
        grid=(t // tm,),
        in_specs=[pl.BlockSpec((tm, D_MODEL), lambda i: (i, 0)),
                  pl.BlockSpec((1, D_MODEL), lambda i: (0, 0)),
                  pl.BlockSpec((1, 1, D_MODEL), lambda i: (i // tpb, 0, 0)),
                  pl.BlockSpec((1, 1, D_MODEL), lambda i: (i // tpb, 0, 0))],
        out_specs=pl.BlockSpec((tm, D_MODEL), lambda i: (i, 0)),
        out_shape=jax.ShapeDtypeStruct((t, D_MODEL), BF16),
        compiler_params=_params(("arbitrary",), 32),
        name="norm_mod",
    )(x2, g, scale, shift)


MM_IN_TN = 1024
MM_IN_SRC_ALIGN = 16
_MM_IN_SRC_ROWS = (0,
                   _SRC_GQ,
                   _SRC_GV, _SRC_GV + MM_IN_TN,
                   _SRC_OG, _SRC_OG + MM_IN_TN,
                   _SRC_GATE_A, _SRC_GATE_A + MM_IN_TN,
                   _SRC_GATE_B, _SRC_GATE_B + MM_IN_TN,
                   _SRC_GK)


def _mm_in_kernel(src_ref, h_ref, w_ref, wsp_ref, o_ref, w_bf):
    del src_ref
    j, i = pl.program_id(0), pl.program_id(1)

    @pl.when(i == 0)
    def _():
        w_bf[...] = w_ref[...].astype(BF16)

    @pl.when((i == 0) & (j == 0))
    def _():
        w_bf[Z_KROPE:Z_GQ] = wsp_ref[...].astype(BF16)

    o_ref[...] = lax.dot_general(h_ref[...], w_bf[...], (((1,), (1,)), ((), ())),
                                 preferred_element_type=F32).astype(o_ref.dtype)


def _mm_in(h, w_t, w_special):
    t = h.shape[0]
    tm, tn = 1024, MM_IN_TN
    grid_spec = pltpu.PrefetchScalarGridSpec(
        num_scalar_prefetch=1,
        grid=(Z_WIDTH // tn, t // tm),
        in_specs=[pl.BlockSpec((tm, D_MODEL), lambda j, i, src: (i, 0)),
                  pl.BlockSpec((pl.Element(tn), pl.Element(D_MODEL)),
                               lambda j, i, src: (src[j] * MM_IN_SRC_ALIGN, 0)),
                  pl.BlockSpec((Z_GQ - Z_KROPE, D_MODEL), lambda j, i, src: (0, 0))],
        out_specs=pl.BlockSpec((tm, tn), lambda j, i, src: (i, j)),
        scratch_shapes=[pltpu.VMEM((tn, D_MODEL), BF16)])
    return pl.pallas_call(
        _mm_in_kernel,
        grid_spec=grid_spec,
        out_shape=jax.ShapeDtypeStruct((t, Z_WIDTH), BF16),
        compiler_params=_params(("arbitrary", "arbitrary"), 48),
        name="mm_in",
    )(jnp.asarray([r // MM_IN_SRC_ALIGN for r in _MM_IN_SRC_ROWS], jnp.int32), h, w_t, w_special)


ROPE_HALF = MLA_ROPE // 2
ROPE_PACK = LANES // ROPE_HALF


def _rope_kernel(pos_ref, inv_ref, cos_ref, sin_ref):
    ang = pos_ref[...].astype(F32) * inv_ref[...]
    cos_ref[...] = jnp.cos(ang)
    sin_ref[...] = jnp.sin(ang)


def _rope_tables(positions, inv):
    t = positions.size
    rows = t // ROPE_PACK
    pos_dense = jnp.repeat(positions.reshape(t), ROPE_HALF).reshape(rows, LANES)
    inv_dense = jnp.tile(inv, ROPE_PACK).reshape(1, LANES)
    tr = 512
    spec = pl.BlockSpec((tr, LANES), lambda i: (i, 0))
    cos, sin = pl.pallas_call(
        _rope_kernel,
        grid=(rows // tr,),
        in_specs=[spec, pl.BlockSpec((1, LANES), lambda i: (0, 0))],
        out_specs=[spec, spec],
        out_shape=[jax.ShapeDtypeStruct((rows, LANES), F32)] * 2,
        compiler_params=_params(("arbitrary",), 32),
        name="rope_tables",
    )(pos_dense, inv_dense)
    return cos.reshape(t, ROPE_HALF), sin.reshape(t, ROPE_HALF)


def _mla_proj_kernel(ql_ref, kvl_ref, kr_ref, cos_ref, sin_ref,
                     gq_ref, gkv_ref, wq_ref, wkv_ref, q_ref, k_ref, v_ref):
    c, s = cos_ref[...], sin_ref[...]
    zeros = jnp.zeros((c.shape[0], LANES - MLA_ROPE), F32)
    cosm = jnp.concatenate([c, c, zeros], axis=1)
    sinm = jnp.concatenate([-s, s, zeros], axis=1)

    def rope(r):
        return r * cosm + pltpu.roll(r, MLA_ROPE, 1) * sinm

    scale = (MLA_NOPE + MLA_ROPE) ** -0.5 * np.log2(np.e)
    qn =_rms(ql_ref[...].astype(F32), gq_ref[...]).astype(BF16)
    for h in range(MLA_HEADS):
        r = jnp.dot(qn, wq_ref[h], preferred_element_type=F32) * scale
        q_ref[0, h, :, 0:MLA_NOPE] = r[:, 0:MLA_NOPE].astype(BF16)
        q_ref[0, h, :, MLA_NOPE:MLA_QK_PAD] = rope(r[:, MLA_NOPE:MLA_QK_PAD]).astype(BF16)

    kvn = _rms(kvl_ref[...].astype(F32), gkv_ref[...]).astype(BF16)
    kv = jnp.dot(kvn, wkv_ref[...], preferred_element_type=F32).astype(BF16)
    kpe = rope(kr_ref[...].astype(F32)).astype(BF16)
    hw = MLA_NOPE + MLA_V
    for h in range(MLA_HEADS):
        k_ref[0, h, :, 0:MLA_NOPE] = kv[:, h * hw:h * hw + MLA_NOPE]
        k_ref[0, h, :, MLA_NOPE:MLA_QK_PAD] = kpe
        v_ref[0, h] = kv[:, h * hw + MLA_NOPE:(h + 1) * hw]


def _mla_proj(z, cos, sin, gq, gkv, wq, wkv, batch, seq):
    tm = 512
    tpb = seq // tm
    qk_shape = jax.ShapeDtypeStruct((batch, MLA_HEADS, seq, MLA_QK_PAD), BF16)
    v_shape = jax.ShapeDtypeStruct((batch, MLA_HEADS, seq, MLA_V), BF16)
    const2 = lambda i: (0, 0)
    qk_spec = pl.BlockSpec((1, MLA_HEADS, tm, MLA_QK_PAD), lambda i: (i // tpb, 0, i % tpb, 0))
    return pl.pallas_call(
        _mla_proj_kernel,
        grid=(batch * tpb,),
        in_specs=[pl.BlockSpec((tm, MLA_Q_RANK), lambda i: (i, Z_QLAT // MLA_Q_RANK)),
                  pl.BlockSpec((tm, MLA_KV_RANK), lambda i: (i, Z_KVLAT // MLA_KV_RANK)),
                  pl.BlockSpec((tm, LANES), lambda i: (i, Z_KROPE // LANES)),
                  pl.BlockSpec((tm, ROPE_HALF), lambda i: (i, 0)),
                  pl.BlockSpec((tm, ROPE_HALF), lambda i: (i, 0)),
                  pl.BlockSpec((1, MLA_Q_RANK), const2),
                  pl.BlockSpec((1, MLA_KV_RANK), const2),
                  pl.BlockSpec((MLA_HEADS, MLA_Q_RANK, MLA_QK_PAD), lambda i: (0, 0, 0)),
                  pl.BlockSpec((MLA_KV_RANK, MLA_HEADS * (MLA_NOPE + MLA_V)), const2)],
        out_specs=[qk_spec, qk_spec,
                   pl.BlockSpec((1, MLA_HEADS, tm, MLA_V), lambda i: (i // tpb, 0, i % tpb, 0))],
        out_shape=[qk_shape, qk_shape, v_shape],
        compiler_params=_params(("arbitrary",), 48),
        name="mla_proj",
    )(z, z, z, cos, sin, gq, gkv, wq, wkv)


ATTN_BLOCK = 1024
ATTN_LOOP_CHAINS = 1
ATTN_DIAG_CHAINS = 2


def _attn_kernel(q_ref, k_ref, v_ref, o_ref, m_sc, l_sc, acc_sc):
    blk = ATTN_BLOCK
    nt = (((1,), (1,)), ((), ()))
    qi = pl.program_id(2)
    m_sc[...] = jnp.full(m_sc.shape, -jnp.inf, F32)
    l_sc[...] = jnp.zeros(l_sc.shape, F32)
    acc_sc[...] = jnp.zeros(acc_sc.shape, F32)

    def update(rows, s, v):
        m_prev = m_sc[rows]
        m_new = jnp.maximum(m_prev, jnp.max(s, axis=-1, keepdims=True))
        alpha = jnp.exp2(m_prev - m_new)
        ps = [jnp.exp2(s[:, c * LANES:(c + 1) * LANES] - m_new)
              for c in range(s.shape[1] // LANES)]
        l_sc[rows] = alpha * l_sc[rows] + functools.reduce(lambda a, b: a + b, ps)
        p = jnp.concatenate(ps, axis=1).astype(BF16)
        acc_sc[rows] = alpha * acc_sc[rows] + jnp.dot(p, v, preferred_element_type=F32)
        m_sc[rows] = m_new

    def body(kb, carry):
        ks = pl.multiple_of(kb * blk, blk)
        k = k_ref[0, 0, pl.ds(ks, blk), :]
        v = v_ref[0, 0, pl.ds(ks, blk), :]
        sub = blk // ATTN_LOOP_CHAINS
        rows = [slice(r * sub, (r + 1) * sub) for r in range(ATTN_LOOP_CHAINS)]
        scores = [lax.dot_general(q_ref[0, 0, r], k, nt, preferred_element_type=F32)
                  for r in rows]
        for r, s in zip(rows, scores):
            update(r, s, v)
        return carry

    lax.fori_loop(0, qi, body, 0)

    ks = pl.multiple_of(qi * blk, blk)
    sub = blk // ATTN_DIAG_CHAINS
    for r in range(ATTN_DIAG_CHAINS):
        n = (r + 1) * sub
        k = k_ref[0, 0, pl.ds(ks, n), :]
        v = v_ref[0, 0, pl.ds(ks, n), :]
        q = q_ref[0, 0, r * sub:(r + 1) * sub, :]
        s = lax.dot_general(q, k, nt, preferred_element_type=F32)
        row = lax.broadcasted_iota(jnp.int32, s.shape, 0) + r * sub
        col = lax.broadcasted_iota(jnp.int32, s.shape, 1)
        update(slice(r * sub, n), jnp.where(col <= row, s, -jnp.inf), v)
    l = jnp.sum(l_sc[...], axis=-1, keepdims=True)
    o_ref[...] = (acc_sc[...] / l).astype(o_ref.dtype)


def _attn(q, k, v, batch, seq):
    blk = ATTN_BLOCK
    nq = seq // blk
    return pl.pallas_call(
        _attn_kernel,
        grid=(batch, MLA_HEADS, nq),
        in_specs=[pl.BlockSpec((1, 1, blk, MLA_QK_PAD), lambda b, h, i: (b, h, i, 0)),
                  pl.BlockSpec((1, 1, seq, MLA_QK_PAD), lambda b, h, i: (b, h, 0, 0)),
                  pl.BlockSpec((1, 1, seq, MLA_V), lambda b, h, i: (b, h, 0, 0))],
        out_specs=pl.BlockSpec((blk, MLA_V), lambda b, h, i: (b * nq + i, h)),
        out_shape=jax.ShapeDtypeStruct((batch * seq, MLA_HEADS * MLA_V), BF16),
        scratch_shapes=[pltpu.VMEM((blk, LANES), F32), pltpu.VMEM((blk, LANES), F32),
                        pltpu.VMEM((blk, MLA_V), F32)],
        compiler_params=_params(("arbitrary", "arbitrary", "arbitrary"), 32),
        name="attn",
    )(q, k, v)


def _gla_constants():
    c = GLA_CHUNK
    idx = np.arange(c)
    w = np.zeros((GLA_LEVELS + 2, c, c), np.float32)
    masks = np.zeros((GLA_LEVELS + 1, c, c), np.float32)
    for l in range(GLA_LEVELS):
        half = 1 << l
        blk = idx // (2 * half)
        mid = blk * 2 * half + half - 1
        upper = (idx % (2 * half)) >= half
        t = idx[None, :]
        up_rows = (t > mid[:, None]) & (t <= idx[:, None])
        lo_rows = (t > idx[:, None]) & (t <= mid[:, None])
        w[l] = np.where(upper[:, None], up_rows, lo_rows)
        masks[l] = (blk[:, None] == blk[None, :]) & upper[:, None] & (~upper[None, :])
    w[GLA_LEVELS] = idx[None, :] <= idx[:, None]
    w[GLA_LEVELS + 1] = idx[None, :] > idx[:, None]
    masks[GLA_LEVELS] = np.eye(c)
    w = w.reshape(-1, c)
    return np.concatenate([w, w], axis=1), masks


def _split_bf16(x):
    hi = x.astype(BF16)
    return hi, (x - hi.astype(F32)).astype(BF16)


RIDER_ROWS = 128


def _gla_kernel(q_ref, k_ref, v_ref, lr_ref, og_ref, wg_ref, bg_ref, go_ref, ws_ref, mask_ref,
                wd_ref, wm_ref, o_ref, wd_o, wm_o, st_ref, *, down_blocks, mix_blocks):
    c = GLA_CHUNK
    dk, dv = GLA_DK, GLA_DV
    nt = (((1,), (1,)), ((), ()))
    step = pl.program_id(0) * pl.num_programs(1) + pl.program_id(1)

    @pl.when(step < down_blocks)
    def _():
        wd_o[...] = wd_ref[...].astype(BF16)

    @pl.when((step >= down_blocks) & (step < down_blocks + mix_blocks))
    def _():
        wm_o[...] = wm_ref[...].astype(BF16)

    @pl.when(pl.program_id(1) == 0)
    def _():
        st_ref[...] = jnp.zeros(st_ref.shape, F32)

    lr = lr_ref[...]

    def qk(h):
        q = q_ref[:, h * dk:(h + 1) * dk] * (dk ** -0.5)
        return q, k_ref[:, h * dk:(h + 1) * dk]

    def gate(h):
        x = (jnp.dot(lr, wg_ref[:, h * dk:(h + 1) * dk], preferred_element_type=F32)
             + bg_ref[:, h * dk:(h + 1) * dk])
        log2_a = (-(jnp.maximum(-x, 0.0) + jnp.log1p(jnp.exp(-jnp.abs(x))))
                  * (np.log2(np.e) / GLA_TAU))
        return jnp.concatenate(_split_bf16(log2_a), axis=0)

    def exponents(log2_a):
        return jnp.dot(ws_ref[...], log2_a, preferred_element_type=F32)

    def level_operands(h, pb):
        q, k = qk(h)
        ops = [(q * pb[l * c:(l + 1) * c], k * pb[l * c:(l + 1) * c]) for l in range(GLA_LEVELS)]
        return ops + [(q, k)]

    def level_scores(ops):
        return [lax.dot_general(a, b, nt, preferred_element_type=F32) for a, b in ops]

    def masked_sum(scores):
        attn = mask_ref[0] * scores[0]
        for l in range(1, GLA_LEVELS + 1):
            attn = attn + mask_ref[l] * scores[l]
        return attn.astype(BF16)

    def mix(h, p, pb, attn):
        q, k = qk(h)
        v = v_ref[:, h * dv:(h + 1) * dv]
        eb = pb[GLA_LEVELS * c:(GLA_LEVELS + 1) * c]
        ebl = pb[(GLA_LEVELS + 1) * c:(GLA_LEVELS + 2) * c]
        st = st_ref[h]
        o = (lax.dot_general(q * eb, st.astype(BF16), nt, preferred_element_type=F32)
             + jnp.dot(attn, v, preferred_element_type=F32))
        v_t = v.astype(F32).T.astype(BF16)
        decay = p[(GLA_LEVELS + 1) * c - 1:(GLA_LEVELS + 1) * c]
        st_ref[h] = st * decay + jnp.dot(v_t, k * ebl, preferred_element_type=F32)
        return o

    def emit(h, o):
        y = _rms(o, go_ref[...])
        og = og_ref[:, h * dv:(h + 1) * dv].astype(F32)
        o_ref[:, h * dv:(h + 1) * dv] = (y * (og * jax.nn.sigmoid(og))).astype(o_ref.dtype)

    heads = range(GLA_HEADS)
    log2_a = [gate(h) for h in heads]
    expo = [exponents(x) for x in log2_a]
    p = [jnp.exp2(e) for e in expo]
    pb = [x.astype(BF16) for x in p]
    operands = [level_operands(h, pb[h]) for h in heads]
    scores = [level_scores(ops) for ops in operands]
    attn = [masked_sum(s) for s in scores]
    out = [mix(h, p[h], pb[h], attn[h]) for h in heads]
    for h in heads:
        emit(h, out[h])


def _gla(z, wg, bg, go, w_down, w_mix, batch, seq):
    c = GLA_CHUNK
    nc = seq // c
    ws, masks = _gla_constants()
    hk, hv = GLA_HEADS * GLA_DK, GLA_HEADS * GLA_DV
    const2 = lambda b, i: (0, 0)
    down_blocks = w_down.shape[0] // RIDER_ROWS
    mix_blocks = w_mix.shape[0] // RIDER_ROWS
    assert down_blocks + mix_blocks <= batch * nc
    down_spec = pl.BlockSpec((RIDER_ROWS, D_MODEL),
                             lambda b, i: (jnp.minimum(b * nc + i, down_blocks - 1), 0))
    mix_spec = pl.BlockSpec((RIDER_ROWS, D_MODEL),
                            lambda b, i: (jnp.clip(b * nc + i - down_blocks, 0, mix_blocks - 1), 0))
    return pl.pallas_call(
        functools.partial(_gla_kernel, down_blocks=down_blocks, mix_blocks=mix_blocks),
        grid=(batch, nc),
        in_specs=[pl.BlockSpec((c, hk), lambda b, i: (b * nc + i, Z_GQ // hk)),
                  pl.BlockSpec((c, hk), lambda b, i: (b * nc + i, Z_GK // hk)),
                  pl.BlockSpec((c, hv), lambda b, i: (b * nc + i, Z_GV // hv)),
                  pl.BlockSpec((c, LANES), lambda b, i: (b * nc + i, Z_GLR // LANES)),
                  pl.BlockSpec((c, hv), lambda b, i: (b * nc + i, Z_OG // hv)),
                  pl.BlockSpec((LANES, hk), const2),
                  pl.BlockSpec((1, hk), const2),
                  pl.BlockSpec((1, GLA_DV), const2),
                  pl.BlockSpec(ws.shape, const2),
                  pl.BlockSpec(masks.shape, lambda b, i: (0, 0, 0)),
                  down_spec, mix_spec],
        out_specs=[pl.BlockSpec((c, hv), lambda b, i: (b * nc + i, 0)), down_spec, mix_spec],
        out_shape=[jax.ShapeDtypeStruct((batch * seq, hv), BF16),
                   jax.ShapeDtypeStruct(w_down.shape, BF16),
                   jax.ShapeDtypeStruct(w_mix.shape, BF16)],
        scratch_shapes=[pltpu.VMEM((GLA_HEADS, GLA_DV, GLA_DK), F32)],
        compiler_params=_params(("arbitrary", "arbitrary"), 48),
        name="gla",
    )(z, z, z, z, z, wg, bg, go, jnp.asarray(ws, BF16), jnp.asarray(masks), w_down, w_mix)


def _mix1_kernel(oa_ref, ob_ref, wa_ref, wb_ref, ga_ref, gb_ref, o_ref, wa_bf, wb_bf):
    @pl.when(pl.program_id(1) == 0)
    def _():
        wa_bf[...] = wa_ref[...].astype(BF16)
        wb_bf[...] = wb_ref[...].astype(BF16)

    ya = jnp.dot(oa_ref[...], wa_bf[...], preferred_element_type=F32)
    yb = jnp.dot(ob_ref[...], wb_bf[...], preferred_element_type=F32)
    o_ref[...] = (jax.nn.sigmoid(ga_ref[...].astype(F32)) * ya
                  + jax.nn.sigmoid(gb_ref[...].astype(F32)) * yb).astype(o_ref.dtype)


def _mix1(oa, ob, wa, wb, z):
    t = oa.shape[0]
    tm, tn = 1024, 512
    return pl.pallas_call(
        _mix1_kernel,
        grid=(D_MODEL // tn, t // tm),
        in_specs=[pl.BlockSpec((tm, oa.shape[1]), lambda j, i: (i, 0)),
                  pl.BlockSpec((tm, ob.shape[1]), lambda j, i: (i, 0)),
                  pl.BlockSpec((wa.shape[0], tn), lambda j, i: (0, j)),
                  pl.BlockSpec((wb.shape[0], tn), lambda j, i: (0, j)),
                  pl.BlockSpec((tm, tn), lambda j, i: (i, Z_GATE_A // tn + j)),
                  pl.BlockSpec((tm, tn), lambda j, i: (i, Z_GATE_B // tn + j))],
        out_specs=pl.BlockSpec((tm, tn), lambda j, i: (i, j)),
        out_shape=jax.ShapeDtypeStruct((t, D_MODEL), BF16),
        scratch_shapes=[pltpu.VMEM((wa.shape[0], tn), BF16), pltpu.VMEM((wb.shape[0], tn), BF16)],
        compiler_params=_params(("arbitrary", "arbitrary"), 48),
        name="mix1",
    )(oa, ob, wa, wb, z, z)


def _mix2_kernel(m_ref, w_ref, x_ref, gm_ref, gpost_ref, gpre_ref, sc_ref, sh_ref,
                 x1_ref, h2_ref):
    for r0 in range(0, m_ref.shape[0], MIX2_SUBTILE):
        rows = slice(r0, r0 + MIX2_SUBTILE)
        y = jnp.dot(m_ref[rows], w_ref[...], preferred_element_type=F32)
        x1 = x_ref[rows] + gm_ref[0] * _rms(y, gpost_ref[...])
        x1_ref[rows] = x1
        h2_ref[rows] = (_rms(x1, gpre_ref[...]) * (1.0 + sc_ref[0])
                        + sh_ref[0]).astype(h2_ref.dtype)


MIX2_SUBTILE = 256


def _mix2(merged, w, x2, gate_m, g_post, g_pre, scale_f, shift_f, seq):
    t = merged.shape[0]
    tm = 512
    tpb = seq // tm
    row = pl.BlockSpec((tm, D_MODEL), lambda i: (i, 0))
    vec = pl.BlockSpec((1, D_MODEL), lambda i: (0, 0))
    per_batch = pl.BlockSpec((1, 1, D_MODEL), lambda i: (i // tpb, 0, 0))
    return pl.pallas_call(
        _mix2_kernel,
        grid=(t // tm,),
        in_specs=[row, pl.BlockSpec((D_MODEL, D_MODEL), lambda i: (0, 0),
                                    pipeline_mode=pl.Buffered(1)), row,
                  per_batch, vec, vec, per_batch, per_batch],
        out_specs=[row, row],
        out_shape=[jax.ShapeDtypeStruct((t, D_MODEL), F32),
                   jax.ShapeDtypeStruct((t, D_MODEL), BF16)],
        compiler_params=_params(("arbitrary",), 48),
        name="mix2",
    )(merged, w, x2, gate_m, g_post, g_pre, scale_f, shift_f)


FFN_UP_ROW_CHUNKS = 1


def _ffn_up_kernel(h_ref, wa_ref, wv_ref, cwa_ref, cwv_ref, cba_ref, cbv_ref, o_ref,
                   wa_bf, wv_bf, *u_scs, tiles_per_batch):
    i = pl.program_id(1)
    nr = FFN_UP_ROW_CHUNKS
    ua, uv = u_scs[:nr], u_scs[nr:]
    rc = h_ref.shape[0] // nr
    head, body, tail = slice(0, SUBLANES), slice(SUBLANES, SUBLANES + rc), slice(rc, rc + SUBLANES)

    @pl.when(i == 0)
    def _():
        wa_bf[...] = wa_ref[...].astype(BF16)
        wv_bf[...] = wv_ref[...].astype(BF16)

    @pl.when(i % tiles_per_batch == 0)
    def _():
        ua[0][head] = jnp.zeros((SUBLANES, ua[0].shape[1]), F32)
        uv[0][head] = jnp.zeros((SUBLANES, uv[0].shape[1]), F32)

    @pl.when(i % tiles_per_batch != 0)
    def _():
        ua[0][head] = ua[nr - 1][tail]
        uv[0][head] = uv[nr - 1][tail]

    def matmuls(r):
        h = h_ref[r * rc:(r + 1) * rc]
        for u, w_bf in ((ua, wa_bf), (uv, wv_bf)):
            u[r][body] = jnp.dot(h, w_bf[...], preferred_element_type=F32)
            if r + 1 < nr:
                u[r + 1][head] = u[r][tail]

    def conv(u_sc, cw_ref, cb_ref):
        cw = cw_ref[...]
        acc = cb_ref[...] + u_sc[body] * cw[CONV_WIDTH - 1:CONV_WIDTH]
        for tap in range(1, CONV_WIDTH):
            acc = acc + (u_sc[SUBLANES - tap:SUBLANES - tap + rc]
                         * cw[CONV_WIDTH - 1 - tap:CONV_WIDTH - tap])
        return acc

    def epilogue(r):
        a = conv(ua[r], cwa_ref, cba_ref)
        val = conv(uv[r], cwv_ref, cbv_ref)
        o_ref[r * rc:(r + 1) * rc] = (jax.nn.gelu(a, approximate=True) * val).astype(o_ref.dtype)

    matmuls(0)
    for r in range(1, nr):
        matmuls(r)
        epilogue(r - 1)
    epilogue(nr - 1)


def _ffn_up(h2, w_up, conv_w, conv_b, seq):
    t = h2.shape[0]
    tm, tn = 1024, 512
    nj = D_FF // tn
    return pl.pallas_call(
        functools.partial(_ffn_up_kernel, tiles_per_batch=seq // tm),
        grid=(nj, t // tm),
        in_specs=[pl.BlockSpec((tm, D_MODEL), lambda j, i: (i, 0)),
                  pl.BlockSpec((D_MODEL, tn), lambda j, i: (0, j)),
                  pl.BlockSpec((D_MODEL, tn), lambda j, i: (0, nj + j)),
                  pl.BlockSpec((CONV_WIDTH, tn), lambda j, i: (0, j)),
                  pl.BlockSpec((CONV_WIDTH, tn), lambda j, i: (0, nj + j)),
                  pl.BlockSpec((1, tn), lambda j, i: (0, j)),
                  pl.BlockSpec((1, tn), lambda j, i: (0, nj + j))],
        out_specs=pl.BlockSpec((tm, tn), lambda j, i: (i, j)),
        out_shape=jax.ShapeDtypeStruct((t, D_FF), BF16),
        scratch_shapes=([pltpu.VMEM((D_MODEL, tn), BF16), pltpu.VMEM((D_MODEL, tn), BF16)]
                        + [pltpu.VMEM((tm // FFN_UP_ROW_CHUNKS + SUBLANES, tn), F32)]
                        * (2 * FFN_UP_ROW_CHUNKS)),
        compiler_params=_params(("arbitrary", "arbitrary"), 56),
        name="ffn_up",
    )(h2, w_up, w_up, conv_w, conv_w, conv_b, conv_b)


def _ffn_down_kernel(g_ref, w_ref, x1_ref, gf_ref, gpost_ref, o_ref):
    f = jnp.dot(g_ref[...], w_ref[...], preferred_element_type=F32)
    o_ref[...] = x1_ref[...] + gf_ref[0] * _rms(f, gpost_ref[...])


def _ffn_down(g, w, x1, gate_f, g_post, seq):
    t = g.shape[0]
    tm = 256
    tpb = seq // tm
    return pl.pallas_call(
        _ffn_down_kernel,
        grid=(t // tm,),
        in_specs=[pl.BlockSpec((tm, D_FF), lambda i: (i, 0)),
                  pl.BlockSpec((D_FF, D_MODEL), lambda i: (0, 0), pipeline_mode=pl.Buffered(1)),
                  pl.BlockSpec((tm, D_MODEL), lambda i: (i, 0)),
                  pl.BlockSpec((1, 1, D_MODEL), lambda i: (i // tpb, 0, 0)),
                  pl.BlockSpec((1, D_MODEL), lambda i: (0, 0))],
        out_specs=pl.BlockSpec((tm, D_MODEL), lambda i: (i, 0)),
        out_shape=jax.ShapeDtypeStruct((t, D_MODEL), F32),
        compiler_params=_params(("arbitrary",), 56),
        name="ffn_down",
    )(g, w, x1, gate_f, g_post)


def _swap_halves(w):
    half = w.shape[-1] // 2
    return jnp.concatenate([w[..., half:], w[..., :half]], axis=-1)


def _w_in_special(w_t):
    half = MLA_ROPE // 2
    k_rope = w_t[_SRC_KROPE:_SRC_KROPE + MLA_ROPE]
    g_lr = w_t[_SRC_GLR:_SRC_GLR + GLA_GATE_RANK]
    pad = jnp.zeros((Z_GQ - Z_GLR - 2 * GLA_GATE_RANK, w_t.shape[1]), w_t.dtype)
    return jnp.concatenate([k_rope, k_rope[half:], k_rope[:half], g_lr, g_lr, pad], axis=0)


def _prep_w_q(w_q_up):
    w = w_q_up.reshape(MLA_Q_RANK, MLA_HEADS, MLA_NOPE + MLA_ROPE)
    pe = w[:, :, MLA_NOPE:]
    w = jnp.concatenate([w, _swap_halves(pe)], axis=-1)
    return jnp.transpose(w, (1, 0, 2)).astype(BF16)


def kernel(x, c, positions, w_ada, b_ada, g_pre_mix, w_in, g_q_lat, w_q_up, g_kv_lat, w_kv_up,
           w_gla_gate_up, b_gla_gate, g_gla_out, w_branch_a, w_branch_b, w_mix_out, g_post_mix,
           g_pre_ffn, w_ffn_up, conv_w, conv_b, w_ffn_down, g_post_ffn):
    batch, seq, _ = x.shape
    depth = w_ada.shape[0]
    t = batch * seq
    row = lambda v: v.reshape(1, -1)

    inv = 1.0 / (ROPE_THETA ** (jnp.arange(0, MLA_ROPE, 2, dtype=F32) / MLA_ROPE))
    cos, sin = _rope_tables(positions, inv)

    c_pad = jnp.zeros((SUBLANES, D_MODEL), F32).at[:batch].set(c)
    x2 = x.reshape(t, D_MODEL)
    for l in range(depth):
        mod = _ada(c_pad, w_ada[l], row(b_ada[l]))[:batch]
        shift_m, scale_m, gate_m, shift_f, scale_f, gate_f = (
            m.reshape(batch, 1, D_MODEL) for m in jnp.split(mod, 6, axis=-1))

        h = _norm_mod(x2, row(g_pre_mix[l]), scale_m, shift_m, seq)
        w_in_t = w_in[l].T
        z = _mm_in(h, w_in_t, _w_in_special(w_in_t))

        q, k, v = _mla_proj(z, cos, sin, row(g_q_lat[l]), row(g_kv_lat[l]),
                            _prep_w_q(w_q_up[l]), w_kv_up[l].astype(BF16), batch, seq)
        o_a = _attn(q, k, v, batch, seq)

        wg_hi, wg_lo = _split_bf16(w_gla_gate_up[l])
        wg = jnp.zeros((LANES, GLA_HEADS * GLA_DK), BF16)
        wg = wg.at[:GLA_GATE_RANK].set(wg_hi).at[GLA_GATE_RANK:2 * GLA_GATE_RANK].set(wg_lo)
        o_b, w_down_bf, w_mix_bf = _gla(z, wg, row(b_gla_gate[l]), row(g_gla_out[l]),
                                        w_ffn_down[l], w_mix_out[l], batch, seq)

        merged = _mix1(o_a, o_b, w_branch_a[l], w_branch_b[l], z)
        x1, h2 = _mix2(merged, w_mix_bf, x2, gate_m, row(g_post_mix[l]),
                       row(g_pre_ffn[l]), scale_f, shift_f, seq)

        g = _ffn_up(h2, w_ffn_up[l], conv_w[l], row(conv_b[l]), seq)
        x2 = _ffn_down(g, w_down_bf, x1, gate_f, row(g_post_ffn[l]), seq)
    return x2.reshape(batch, seq, D_MODEL)
```

```python
import functools

import jax
import jax.numpy as jnp
import numpy as np
from jax import lax
from jax.experimental import pallas as pl
from jax.experimental.pallas import tpu as pltpu

F32 = jnp.float32
BF16 = jnp.bfloat16

D_MODEL = 2048
MLA_HEADS = 8
MLA_Q_RANK = 512
MLA_KV_RANK = 256
MLA_NOPE = 128
MLA_ROPE = 64
MLA_V = 128
ROPE_THETA = 10000.0
GLA_HEADS = 4
GLA_DK = 256
GLA_DV = 512
GLA_GATE_RANK = 16
GLA_TAU = 16.0
D_FF = 5632
CONV_WIDTH = 3
EPS = 1e-6

LANES = 128
SUBLANES = 8
MLA_QK_PAD = 256

Z_QLAT = 0
Z_KVLAT = 512
Z_KROPE = 768
Z_GLR = 896
Z_GQ = 1024
Z_GV = 2048
Z_OG = 4096
Z_GATE_A = 6144
Z_GATE_B = 8192
Z_GK = 10240
Z_WIDTH = 11264

_SRC_KROPE = MLA_Q_RANK + MLA_KV_RANK
_SRC_GQ = _SRC_KROPE + MLA_ROPE
_SRC_GK = _SRC_GQ + GLA_HEADS * GLA_DK
_SRC_GV = _SRC_GK + GLA_HEADS * GLA_DK
_SRC_GLR = _SRC_GV + GLA_HEADS * GLA_DV
_SRC_OG = _SRC_GLR + GLA_GATE_RANK
_SRC_GATE_A = _SRC_OG + GLA_HEADS * GLA_DV
_SRC_GATE_B = _SRC_GATE_A + D_MODEL

GLA_CHUNK = 128
GLA_LEVELS = 7

MIB = 1024 * 1024


def _params(semantics, vmem_mib):
    return pltpu.CompilerParams(dimension_semantics=semantics,
                                vmem_limit_bytes=vmem_mib * MIB)


def _rms(x, g):
    return x * lax.rsqrt(jnp.mean(x * x, axis=-1, keepdims=True) + EPS) * g


def _ada_kernel(c_ref, w_ref, b_ref, o_ref):
    c = c_ref[...]
    ca = c * jax.nn.sigmoid(c)
    o_ref[...] = jnp.dot(ca.astype(BF16), w_ref[...].astype(BF16),
                         preferred_element_type=F32) + b_ref[...]


def _ada(c_pad, w, b):
    n = w.shape[1]
    tn = 1024
    return pl.pallas_call(
        _ada_kernel,
        grid=(n // tn,),
        in_specs=[pl.BlockSpec((SUBLANES, D_MODEL), lambda j: (0, 0)),
                  pl.BlockSpec((D_MODEL, tn), lambda j: (0, j)),
                  pl.BlockSpec((1, tn), lambda j: (0, j))],
        out_specs=pl.BlockSpec((SUBLANES, tn), lambda j: (0, j)),
        out_shape=jax.ShapeDtypeStruct((SUBLANES, n), F32),
        compiler_params=_params(("arbitrary",), 32),
        name="ada",
    )(c_pad, w, b)


def _norm_mod_kernel(x_ref, g_ref, sc_ref, sh_ref, o_ref):
    y = _rms(x_ref[...], g_ref[...] * (1.0 + sc_ref[0]))
    o_ref[...] = (y + sh_ref[0]).astype(o_ref.dtype)


def _norm_mod(x2, g, scale, shift, seq):
    t = x2.shape[0]
    tm = 512
    tpb = seq // tm
    return pl.pallas_call(
        _norm_mod_kernel,
        grid=(t // tm,),
        in_specs=[pl.BlockSpec((tm, D_MODEL), lambda i: (i, 0)),
                  pl.BlockSpec((1, D_MODEL), lambda i: (0, 0)),
                  pl.BlockSpec((1, 1, D_MODEL), lambda i: (i // tpb, 0, 0)),
                  pl.BlockSpec((1, 1, D_MODEL), lambda i: (i // tpb, 0, 0))],
        out_specs=pl.BlockSpec((tm, D_MODEL), lambda i: (i, 0)),
        out_shape=jax.ShapeDtypeStruct((t, D_MODEL), BF16),
        compiler_params=_params(("arbitrary",), 32),
        name="norm_mod",
    )(x2, g, scale, shift)


MM_IN_TN = 1024
MM_IN_SRC_ALIGN = 16
_MM_IN_SRC_ROWS = (0,
                   _SRC_GQ,
                   _SRC_GV, _SRC_GV + MM_IN_TN,
                   _SRC_OG, _SRC_OG + MM_IN_TN,
                   _SRC_GATE_A, _SRC_GATE_A + MM_IN_TN,
                   _SRC_GATE_B, _SRC_GATE_B + MM_IN_TN,
                   _SRC_GK)


def _mm_in_kernel(src_ref, h_ref, w_ref, wsp_ref, o_ref, w_bf):
    del src_ref
    j, i = pl.program_id(0), pl.program_id(1)

    @pl.when(i == 0)
    def _():
        w_bf[...] = w_ref[...].astype(BF16)

    @pl.when((i == 0) & (j == 0))
    def _():
        w_bf[Z_KROPE:Z_GQ] = wsp_ref[...].astype(BF16)

    o_ref[...] = lax.dot_general(h_ref[...], w_bf[...], (((1,), (1,)), ((), ())),
                                 preferred_element_type=F32).astype(o_ref.dtype)


def _mm_in(h, w_t, w_special):
    t = h.shape[0]
    tm, tn = 1024, MM_IN_TN
    grid_spec = pltpu.PrefetchScalarGridSpec(
        num_scalar_prefetch=1,
        grid=(Z_WIDTH // tn, t // tm),
        in_specs=[pl.BlockSpec((tm, D_MODEL), lambda j, i, src: (i, 0)),
                  pl.BlockSpec((pl.Element(tn), pl.Element(D_MODEL)),
                               lambda j, i, src: (src[j] * MM_IN_SRC_ALIGN, 0)),
                  pl.BlockSpec((Z_GQ - Z_KROPE, D_MODEL), lambda j, i, src: (0, 0))],
        out_specs=pl.BlockSpec((tm, tn), lambda j, i, src: (i, j)),
        scratch_shapes=[pltpu.VMEM((tn, D_MODEL), BF16)])
    return pl.pallas_call(
        _mm_in_kernel,
        grid_spec=grid_spec,
        out_shape=jax.ShapeDtypeStruct((t, Z_WIDTH), BF16),
        compiler_params=_params(("arbitrary", "arbitrary"), 48),
        name="mm_in",
    )(jnp.asarray([r // MM_IN_SRC_ALIGN for r in _MM_IN_SRC_ROWS], jnp.int32), h, w_t, w_special)


ROPE_HALF = MLA_ROPE // 2
ROPE_PACK = LANES // ROPE_HALF


def _rope_kernel(pos_ref, inv_ref, cos_ref, sin_ref):
    ang = pos_ref[...].astype(F32) * inv_ref[...]
    cos_ref[...] = jnp.cos(ang)
    sin_ref[...] = jnp.sin(ang)


def _rope_tables(positions, inv):
    t = positions.size
    rows = t // ROPE_PACK
    pos_dense = jnp.repeat(positions.reshape(t), ROPE_HALF).reshape(rows, LANES)
    inv_dense = jnp.tile(inv, ROPE_PACK).reshape(1, LANES)
    tr = 512
    spec = pl.BlockSpec((tr, LANES), lambda i: (i, 0))
    cos, sin = pl.pallas_call(
        _rope_kernel,
        grid=(rows // tr,),
        in_specs=[spec, pl.BlockSpec((1, LANES), lambda i: (0, 0))],
        out_specs=[spec, spec],
        out_shape=[jax.ShapeDtypeStruct((rows, LANES), F32)] * 2,
        compiler_params=_params(("arbitrary",), 32),
        name="rope_tables",
    )(pos_dense, inv_dense)
    return cos.reshape(t, ROPE_HALF), sin.reshape(t, ROPE_HALF)


def _mla_proj_kernel(ql_ref, kvl_ref, kr_ref, cos_ref, sin_ref,
                     gq_ref, gkv_ref, wq_ref, wkv_ref, q_ref, k_ref, v_ref):
    c, s = cos_ref[...], sin_ref[...]
    zeros = jnp.zeros((c.shape[0], LANES - MLA_ROPE), F32)
    cosm = jnp.concatenate([c, c, zeros], axis=1)
    sinm = jnp.concatenate([-s, s, zeros], axis=1)

    def rope(r):
        return r * cosm + pltpu.roll(r, MLA_ROPE, 1) * sinm

    scale = (MLA_NOPE + MLA_ROPE) ** -0.5 * np.log2(np.e)
    qn =_rms(ql_ref[...].astype(F32), gq_ref[...]).astype(BF16)
    for h in range(MLA_HEADS):
        r = jnp.dot(qn, wq_ref[h], preferred_element_type=F32) * scale
        q_ref[0, h, :, 0:MLA_NOPE] = r[:, 0:MLA_NOPE].astype(BF16)
        q_ref[0, h, :, MLA_NOPE:MLA_QK_PAD] = rope(r[:, MLA_NOPE:MLA_QK_PAD]).astype(BF16)

    kvn = _rms(kvl_ref[...].astype(F32), gkv_ref[...]).astype(BF16)
    kv = jnp.dot(kvn, wkv_ref[...], preferred_element_type=F32).astype(BF16)
    kpe = rope(kr_ref[...].astype(F32)).astype(BF16)
    hw = MLA_NOPE + MLA_V
    for h in range(MLA_HEADS):
        k_ref[0, h, :, 0:MLA_NOPE] = kv[:, h * hw:h * hw + MLA_NOPE]
        k_ref[0, h, :, MLA_NOPE:MLA_QK_PAD] = kpe
        v_ref[0, h] = kv[:, h * hw + MLA_NOPE:(h + 1) * hw]


def _mla_proj(z, cos, sin, gq, gkv, wq, wkv, batch, seq):
    tm = 512
    tpb = seq // tm
    qk_shape = jax.ShapeDtypeStruct((batch, MLA_HEADS, seq, MLA_QK_PAD), BF16)
    v_shape = jax.ShapeDtypeStruct((batch, MLA_HEADS, seq, MLA_V), BF16)
    const2 = lambda i: (0, 0)
    qk_spec = pl.BlockSpec((1, MLA_HEADS, tm, MLA_QK_PAD), lambda i: (i // tpb, 0, i % tpb, 0))
    return pl.pallas_call(
        _mla_proj_kernel,
        grid=(batch * tpb,),
        in_specs=[pl.BlockSpec((tm, MLA_Q_RANK), lambda i: (i, Z_QLAT // MLA_Q_RANK)),
                  pl.BlockSpec((tm, MLA_KV_RANK), lambda i: (i, Z_KVLAT // MLA_KV_RANK)),
                  pl.BlockSpec((tm, LANES), lambda i: (i, Z_KROPE // LANES)),
                  pl.BlockSpec((tm, ROPE_HALF), lambda i: (i, 0)),
                  pl.BlockSpec((tm, ROPE_HALF), lambda i: (i, 0)),
                  pl.BlockSpec((1, MLA_Q_RANK), const2),
                  pl.BlockSpec((1, MLA_KV_RANK), const2),
                  pl.BlockSpec((MLA_HEADS, MLA_Q_RANK, MLA_QK_PAD), lambda i: (0, 0, 0)),
                  pl.BlockSpec((MLA_KV_RANK, MLA_HEADS * (MLA_NOPE + MLA_V)), const2)],
        out_specs=[qk_spec, qk_spec,
                   pl.BlockSpec((1, MLA_HEADS, tm, MLA_V), lambda i: (i // tpb, 0, i % tpb, 0))],
        out_shape=[qk_shape, qk_shape, v_shape],
        compiler_params=_params(("arbitrary",), 48),
        name="mla_proj",
    )(z, z, z, cos, sin, gq, gkv, wq, wkv)


ATTN_KV_BLOCK = 1024
ATTN_Q_BLOCKS = 2
ATTN_DIAG_CHAINS = 2


def _attn_kernel(q_ref, k_ref, v_ref, o_ref, m_sc, l_sc, acc_sc):
    blk = ATTN_KV_BLOCK
    nt = (((1,), (1,)), ((), ()))
    qi = pl.program_id(2)
    m_sc[...] = jnp.full(m_sc.shape, -jnp.inf, F32)
    l_sc[...] = jnp.zeros(l_sc.shape, F32)
    acc_sc[...] = jnp.zeros(acc_sc.shape, F32)

    def update(rows, s, v):
        m_prev = m_sc[rows]
        m_new = jnp.maximum(m_prev, jnp.max(s, axis=-1, keepdims=True))
        alpha = jnp.exp2(m_prev - m_new)
        ps = [jnp.exp2(s[:, c * LANES:(c + 1) * LANES] - m_new)
              for c in range(s.shape[1] // LANES)]
        l_sc[rows] = alpha * l_sc[rows] + functools.reduce(lambda a, b: a + b, ps)
        p = jnp.concatenate(ps, axis=1).astype(BF16)
        acc_sc[rows] = alpha * acc_sc[rows] + jnp.dot(p, v, preferred_element_type=F32)
        m_sc[rows] = m_new

    def full_block(rows, ks):
        k = k_ref[0, 0, pl.ds(ks, blk), :]
        v = v_ref[0, 0, pl.ds(ks, blk), :]
        update(rows, lax.dot_general(q_ref[0, 0, rows], k, nt, preferred_element_type=F32), v)

    def diag_block(row0, ks):
        sub = blk // ATTN_DIAG_CHAINS
        for r in range(ATTN_DIAG_CHAINS):
            n = (r + 1) * sub
            rows = slice(row0 + r * sub, row0 + n)
            k = k_ref[0, 0, pl.ds(ks, n), :]
            v = v_ref[0, 0, pl.ds(ks, n), :]
            s = lax.dot_general(q_ref[0, 0, rows], k, nt, preferred_element_type=F32)
            row = lax.broadcasted_iota(jnp.int32, s.shape, 0) + r * sub
            col = lax.broadcasted_iota(jnp.int32, s.shape, 1)
            update(rows, jnp.where(col <= row, s, -jnp.inf), v)

    def body(kb, carry):
        full_block(slice(0, ATTN_Q_BLOCKS * blk), pl.multiple_of(kb * blk, blk))
        return carry

    lax.fori_loop(0, qi * ATTN_Q_BLOCKS, body, 0)

    base = qi * ATTN_Q_BLOCKS * blk
    for a in range(ATTN_Q_BLOCKS):
        for kb in range(a):
            full_block(slice(a * blk, (a + 1) * blk), pl.multiple_of(base + kb * blk, blk))
        diag_block(a * blk, pl.multiple_of(base + a * blk, blk))
    l = jnp.sum(l_sc[...], axis=-1, keepdims=True)
    o_ref[...] = (acc_sc[...] / l).astype(o_ref.dtype)


def _attn(q, k, v, batch, seq):
    blk = ATTN_Q_BLOCKS * ATTN_KV_BLOCK
    nq = seq // blk
    return pl.pallas_call(
        _attn_kernel,
        grid=(batch, MLA_HEADS, nq),
        in_specs=[pl.BlockSpec((1, 1, blk, MLA_QK_PAD), lambda b, h, i: (b, h, i, 0)),
                  pl.BlockSpec((1, 1, seq, MLA_QK_PAD), lambda b, h, i: (b, h, 0, 0)),
                  pl.BlockSpec((1, 1, seq, MLA_V), lambda b, h, i: (b, h, 0, 0))],
        out_specs=pl.BlockSpec((blk, MLA_V), lambda b, h, i: (b * nq + i, h)),
        out_shape=jax.ShapeDtypeStruct((batch * seq, MLA_HEADS * MLA_V), BF16),
        scratch_shapes=[pltpu.VMEM((blk, LANES), F32), pltpu.VMEM((blk, LANES), F32),
                        pltpu.VMEM((blk, MLA_V), F32)],
        compiler_params=_params(("arbitrary", "arbitrary", "arbitrary"), 48),
        name="attn",
    )(q, k, v)


def _gla_constants():
    c = GLA_CHUNK
    idx = np.arange(c)
    w = np.zeros((GLA_LEVELS + 2, c, c), np.float32)
    masks = np.zeros((GLA_LEVELS + 1, c, c), np.float32)
    for l in range(GLA_LEVELS):
        half = 1 << l
        blk = idx // (2 * half)
        mid = blk * 2 * half + half - 1
        upper = (idx % (2 * half)) >= half
        t = idx[None, :]
        up_rows = (t > mid[:, None]) & (t <= idx[:, None])
        lo_rows = (t > idx[:, None]) & (t <= mid[:, None])
        w[l] = np.where(upper[:, None], up_rows, lo_rows)
        masks[l] = (blk[:, None] == blk[None, :]) & upper[:, None] & (~upper[None, :])
    w[GLA_LEVELS] = idx[None, :] <= idx[:, None]
    w[GLA_LEVELS + 1] = idx[None, :] > idx[:, None]
    masks[GLA_LEVELS] = np.eye(c)
    w = w.reshape(-1, c)
    return np.concatenate([w, w], axis=1), masks


def _split_bf16(x):
    hi = x.astype(BF16)
    return hi, (x - hi.astype(F32)).astype(BF16)


RIDER_ROWS = 128


def _gla_kernel(q_ref, k_ref, v_ref, lr_ref, og_ref, wg_ref, bg_ref, go_ref, ws_ref, mask_ref,
                wd_ref, wm_ref, o_ref, wd_o, wm_o, st_ref, *, down_blocks, mix_blocks):
    c = GLA_CHUNK
    dk, dv = GLA_DK, GLA_DV
    nt = (((1,), (1,)), ((), ()))
    step = pl.program_id(0) * pl.num_programs(1) + pl.program_id(1)

    @pl.when(step < down_blocks)
    def _():
        wd_o[...] = wd_ref[...].astype(BF16)

    @pl.when((step >= down_blocks) & (step < down_blocks + mix_blocks))
    def _():
        wm_o[...] = wm_ref[...].astype(BF16)

    @pl.when(pl.program_id(1) == 0)
    def _():
        st_ref[...] = jnp.zeros(st_ref.shape, F32)

    lr = lr_ref[...]

    def qk(h):
        q = q_ref[:, h * dk:(h + 1) * dk] * (dk ** -0.5)
        return q, k_ref[:, h * dk:(h + 1) * dk]

    def gate(h):
        x = (jnp.dot(lr, wg_ref[:, h * dk:(h + 1) * dk], preferred_element_type=F32)
             + bg_ref[:, h * dk:(h + 1) * dk])
        log2_a = (-(jnp.maximum(-x, 0.0) + jnp.log1p(jnp.exp(-jnp.abs(x))))
                  * (np.log2(np.e) / GLA_TAU))
        return jnp.concatenate(_split_bf16(log2_a), axis=0)

    def exponents(log2_a):
        return jnp.dot(ws_ref[...], log2_a, preferred_element_type=F32)

    def level_operands(h, pb):
        q, k = qk(h)
        ops = [(q * pb[l * c:(l + 1) * c], k * pb[l * c:(l + 1) * c]) for l in range(GLA_LEVELS)]
        return ops + [(q, k)]

    def level_scores(ops):
        return [lax.dot_general(a, b, nt, preferred_element_type=F32) for a, b in ops]

    def masked_sum(scores):
        attn = mask_ref[0] * scores[0]
        for l in range(1, GLA_LEVELS + 1):
            attn = attn + mask_ref[l] * scores[l]
        return attn.astype(BF16)

    def mix(h, p, pb, attn):
        q, k = qk(h)
        v = v_ref[:, h * dv:(h + 1) * dv]
        eb = pb[GLA_LEVELS * c:(GLA_LEVELS + 1) * c]
        ebl = pb[(GLA_LEVELS + 1) * c:(GLA_LEVELS + 2) * c]
        st = st_ref[h]
        o = (lax.dot_general(q * eb, st.astype(BF16), nt, preferred_element_type=F32)
             + jnp.dot(attn, v, preferred_element_type=F32))
        v_t = v.astype(F32).T.astype(BF16)
        decay = p[(GLA_LEVELS + 1) * c - 1:(GLA_LEVELS + 1) * c]
        st_ref[h] = st * decay + jnp.dot(v_t, k * ebl, preferred_element_type=F32)
        return o

    def emit(h, o):
        y = _rms(o, go_ref[...])
        og = og_ref[:, h * dv:(h + 1) * dv].astype(F32)
        o_ref[:, h * dv:(h + 1) * dv] = (y * (og * jax.nn.sigmoid(og))).astype(o_ref.dtype)

    heads = range(GLA_HEADS)
    log2_a = [gate(h) for h in heads]
    expo = [exponents(x) for x in log2_a]
    p = [jnp.exp2(e) for e in expo]
    pb = [x.astype(BF16) for x in p]
    operands = [level_operands(h, pb[h]) for h in heads]
    scores = [level_scores(ops) for ops in operands]
    attn = [masked_sum(s) for s in scores]
    out = [mix(h, p[h], pb[h], attn[h]) for h in heads]
    for h in heads:
        emit(h, out[h])


def _gla(z, wg, bg, go, w_down, w_mix, batch, seq):
    c = GLA_CHUNK
    nc = seq // c
    ws, masks = _gla_constants()
    hk, hv = GLA_HEADS * GLA_DK, GLA_HEADS * GLA_DV
    const2 = lambda b, i: (0, 0)
    down_blocks = w_down.shape[0] // RIDER_ROWS
    mix_blocks = w_mix.shape[0] // RIDER_ROWS
    assert down_blocks + mix_blocks <= batch * nc
    down_spec = pl.BlockSpec((RIDER_ROWS, D_MODEL),
                             lambda b, i: (jnp.minimum(b * nc + i, down_blocks - 1), 0))
    mix_spec = pl.BlockSpec((RIDER_ROWS, D_MODEL),
                            lambda b, i: (jnp.clip(b * nc + i - down_blocks, 0, mix_blocks - 1), 0))
    return pl.pallas_call(
        functools.partial(_gla_kernel, down_blocks=down_blocks, mix_blocks=mix_blocks),
        grid=(batch, nc),
        in_specs=[pl.BlockSpec((c, hk), lambda b, i: (b * nc + i, Z_GQ // hk)),
                  pl.BlockSpec((c, hk), lambda b, i: (b * nc + i, Z_GK // hk)),
                  pl.BlockSpec((c, hv), lambda b, i: (b * nc + i, Z_GV // hv)),
                  pl.BlockSpec((c, LANES), lambda b, i: (b * nc + i, Z_GLR // LANES)),
                  pl.BlockSpec((c, hv), lambda b, i: (b * nc + i, Z_OG // hv)),
                  pl.BlockSpec((LANES, hk), const2),
                  pl.BlockSpec((1, hk), const2),
                  pl.BlockSpec((1, GLA_DV), const2),
                  pl.BlockSpec(ws.shape, const2),
                  pl.BlockSpec(masks.shape, lambda b, i: (0, 0, 0)),
                  down_spec, mix_spec],
        out_specs=[pl.BlockSpec((c, hv), lambda b, i: (b * nc + i, 0)), down_spec, mix_spec],
        out_shape=[jax.ShapeDtypeStruct((batch * seq, hv), BF16),
                   jax.ShapeDtypeStruct(w_down.shape, BF16),
                   jax.ShapeDtypeStruct(w_mix.shape, BF16)],
        scratch_shapes=[pltpu.VMEM((GLA_HEADS, GLA_DV, GLA_DK), F32)],
        compiler_params=_params(("arbitrary", "arbitrary"), 48),
        name="gla",
    )(z, z, z, z, z, wg, bg, go, jnp.asarray(ws, BF16), jnp.asarray(masks), w_down, w_mix)


def _mix1_kernel(oa_ref, ob_ref, wa_ref, wb_ref, ga_ref, gb_ref, o_ref, wa_bf, wb_bf):
    @pl.when(pl.program_id(1) == 0)
    def _():
        wa_bf[...] = wa_ref[...].astype(BF16)
        wb_bf[...] = wb_ref[...].astype(BF16)

    ya = jnp.dot(oa_ref[...], wa_bf[...], preferred_element_type=F32)
    yb = jnp.dot(ob_ref[...], wb_bf[...], preferred_element_type=F32)
    o_ref[...] = (jax.nn.sigmoid(ga_ref[...].astype(F32)) * ya
                  + jax.nn.sigmoid(gb_ref[...].astype(F32)) * yb).astype(o_ref.dtype)


def _mix1(oa, ob, wa, wb, z):
    t = oa.shape[0]
    tm, tn = 1024, 512
    return pl.pallas_call(
        _mix1_kernel,
        grid=(D_MODEL // tn, t // tm),
        in_specs=[pl.BlockSpec((tm, oa.shape[1]), lambda j, i: (i, 0)),
                  pl.BlockSpec((tm, ob.shape[1]), lambda j, i: (i, 0)),
                  pl.BlockSpec((wa.shape[0], tn), lambda j, i: (0, j)),
                  pl.BlockSpec((wb.shape[0], tn), lambda j, i: (0, j)),
                  pl.BlockSpec((tm, tn), lambda j, i: (i, Z_GATE_A // tn + j)),
                  pl.BlockSpec((tm, tn), lambda j, i: (i, Z_GATE_B // tn + j))],
        out_specs=pl.BlockSpec((tm, tn), lambda j, i: (i, j)),
        out_shape=jax.ShapeDtypeStruct((t, D_MODEL), BF16),
        scratch_shapes=[pltpu.VMEM((wa.shape[0], tn), BF16), pltpu.VMEM((wb.shape[0], tn), BF16)],
        compiler_params=_params(("arbitrary", "arbitrary"), 48),
        name="mix1",
    )(oa, ob, wa, wb, z, z)


def _mix2_kernel(m_ref, w_ref, x_ref, gm_ref, gpost_ref, gpre_ref, sc_ref, sh_ref,
                 x1_ref, h2_ref):
    post = gm_ref[0] * gpost_ref[...]
    pre = gpre_ref[...] * (1.0 + sc_ref[0])
    for r0 in range(0, m_ref.shape[0], MIX2_SUBTILE):
        rows = slice(r0, r0 + MIX2_SUBTILE)
        y = jnp.dot(m_ref[rows], w_ref[...], preferred_element_type=F32)
        x1 = x_ref[rows] + _rms(y, post)
        x1_ref[rows] = x1
        h2_ref[rows] = (_rms(x1, pre) + sh_ref[0]).astype(h2_ref.dtype)


MIX2_SUBTILE = 256


def _mix2(merged, w, x2, gate_m, g_post, g_pre, scale_f, shift_f, seq):
    t = merged.shape[0]
    tm = 512
    tpb = seq // tm
    row = pl.BlockSpec((tm, D_MODEL), lambda i: (i, 0))
    vec = pl.BlockSpec((1, D_MODEL), lambda i: (0, 0))
    per_batch = pl.BlockSpec((1, 1, D_MODEL), lambda i: (i // tpb, 0, 0))
    return pl.pallas_call(
        _mix2_kernel,
        grid=(t // tm,),
        in_specs=[row, pl.BlockSpec((D_MODEL, D_MODEL), lambda i: (0, 0),
                                    pipeline_mode=pl.Buffered(1)), row,
                  per_batch, vec, vec, per_batch, per_batch],
        out_specs=[row, row],
        out_shape=[jax.ShapeDtypeStruct((t, D_MODEL), F32),
                   jax.ShapeDtypeStruct((t, D_MODEL), BF16)],
        compiler_params=_params(("arbitrary",), 48),
        name="mix2",
    )(merged, w, x2, gate_m, g_post, g_pre, scale_f, shift_f)


def _ffn_up_kernel(h_ref, wa_ref, wv_ref, cwa_ref, cwv_ref, cba_ref, cbv_ref, o_ref,
                   w_bf, u_sc, *, tiles_per_batch):
    i = pl.program_id(1)
    tm, tn = o_ref.shape
    head, body, tail = slice(0, SUBLANES), slice(SUBLANES, SUBLANES + tm), slice(tm, tm + SUBLANES)

    @pl.when(i == 0)
    def _():
        w_bf[:, 0:tn] = wa_ref[...].astype(BF16)
        w_bf[:, tn:2 * tn] = wv_ref[...].astype(BF16)

    @pl.when(i % tiles_per_batch == 0)
    def _():
        u_sc[head] = jnp.zeros((SUBLANES, u_sc.shape[1]), F32)

    @pl.when(i % tiles_per_batch != 0)
    def _():
        u_sc[head] = u_sc[tail]

    u_sc[body] = jnp.dot(h_ref[...], w_bf[...], preferred_element_type=F32)

    def conv(cols, cw_ref, cb_ref):
        cw = cw_ref[...]
        acc = cb_ref[...] + u_sc[body, cols] * cw[CONV_WIDTH - 1:CONV_WIDTH]
        for tap in range(1, CONV_WIDTH):
            acc = acc + (u_sc[SUBLANES - tap:SUBLANES - tap + tm, cols]
                         * cw[CONV_WIDTH - 1 - tap:CONV_WIDTH - tap])
        return acc

    a = conv(slice(0, tn), cwa_ref, cba_ref)
    val = conv(slice(tn, 2 * tn), cwv_ref, cbv_ref)
    o_ref[...] = (jax.nn.gelu(a, approximate=True) * val).astype(o_ref.dtype)


def _ffn_up(h2, w_up, conv_w, conv_b, seq):
    t = h2.shape[0]
    tm, tn = 1024, 512
    nj = D_FF // tn
    return pl.pallas_call(
        functools.partial(_ffn_up_kernel, tiles_per_batch=seq // tm),
        grid=(nj, t // tm),
        in_specs=[pl.BlockSpec((tm, D_MODEL), lambda j, i: (i, 0)),
                  pl.BlockSpec((D_MODEL, tn), lambda j, i: (0, j)),
                  pl.BlockSpec((D_MODEL, tn), lambda j, i: (0, nj + j)),
                  pl.BlockSpec((CONV_WIDTH, tn), lambda j, i: (0, j)),
                  pl.BlockSpec((CONV_WIDTH, tn), lambda j, i: (0, nj + j)),
                  pl.BlockSpec((1, tn), lambda j, i: (0, j)),
                  pl.BlockSpec((1, tn), lambda j, i: (0, nj + j))],
        out_specs=pl.BlockSpec((tm, tn), lambda j, i: (i, j)),
        out_shape=jax.ShapeDtypeStruct((t, D_FF), BF16),
        scratch_shapes=[pltpu.VMEM((D_MODEL, 2 * tn), BF16),
                        pltpu.VMEM((tm + SUBLANES, 2 * tn), F32)],
        compiler_params=_params(("arbitrary", "arbitrary"), 56),
        name="ffn_up",
    )(h2, w_up, w_up, conv_w, conv_w, conv_b, conv_b)


def _ffn_down_kernel(g_ref, w_ref, x1_ref, gf_ref, gpost_ref, o_ref):
    f = jnp.dot(g_ref[...], w_ref[...], preferred_element_type=F32)
    o_ref[...] = x1_ref[...] + _rms(f, gf_ref[0] * gpost_ref[...])


def _ffn_down(g, w, x1, gate_f, g_post, seq):
    t = g.shape[0]
    tm = 256
    tpb = seq // tm
    return pl.pallas_call(
        _ffn_down_kernel,
        grid=(t // tm,),
        in_specs=[pl.BlockSpec((tm, D_FF), lambda i: (i, 0)),
                  pl.BlockSpec((D_FF, D_MODEL), lambda i: (0, 0), pipeline_mode=pl.Buffered(1)),
                  pl.BlockSpec((tm, D_MODEL), lambda i: (i, 0)),
                  pl.BlockSpec((1, 1, D_MODEL), lambda i: (i // tpb, 0, 0)),
                  pl.BlockSpec((1, D_MODEL), lambda i: (0, 0))],
        out_specs=pl.BlockSpec((tm, D_MODEL), lambda i: (i, 0)),
        out_shape=jax.ShapeDtypeStruct((t, D_MODEL), F32),
        compiler_params=_params(("arbitrary",), 56),
        name="ffn_down",
    )(g, w, x1, gate_f, g_post)


def _swap_halves(w):
    half = w.shape[-1] // 2
    return jnp.concatenate([w[..., half:], w[..., :half]], axis=-1)


def _w_in_special(w_t):
    half = MLA_ROPE // 2
    k_rope = w_t[_SRC_KROPE:_SRC_KROPE + MLA_ROPE]
    g_lr = w_t[_SRC_GLR:_SRC_GLR + GLA_GATE_RANK]
    pad = jnp.zeros((Z_GQ - Z_GLR - 2 * GLA_GATE_RANK, w_t.shape[1]), w_t.dtype)
    return jnp.concatenate([k_rope, k_rope[half:], k_rope[:half], g_lr, g_lr, pad], axis=0)


def _prep_w_q(w_q_up):
    w = w_q_up.reshape(MLA_Q_RANK, MLA_HEADS, MLA_NOPE + MLA_ROPE)
    pe = w[:, :, MLA_NOPE:]
    w = jnp.concatenate([w, _swap_halves(pe)], axis=-1)
    return jnp.transpose(w, (1, 0, 2)).astype(BF16)


def kernel(x, c, positions, w_ada, b_ada, g_pre_mix, w_in, g_q_lat, w_q_up, g_kv_lat, w_kv_up,
           w_gla_gate_up, b_gla_gate, g_gla_out, w_branch_a, w_branch_b, w_mix_out, g_post_mix,
           g_pre_ffn, w_ffn_up, conv_w, conv_b, w_ffn_down, g_post_ffn):
    batch, seq, _ = x.shape
    depth = w_ada.shape[0]
    t = batch * seq
    row = lambda v: v.reshape(1, -1)

    inv = 1.0 / (ROPE_THETA ** (jnp.arange(0, MLA_ROPE, 2, dtype=F32) / MLA_ROPE))
    cos, sin = _rope_tables(positions, inv)

    c_pad = jnp.zeros((SUBLANES, D_MODEL), F32).at[:batch].set(c)
    x2 = x.reshape(t, D_MODEL)
    for l in range(depth):
        mod = _ada(c_pad, w_ada[l], row(b_ada[l]))[:batch]
        shift_m, scale_m, gate_m, shift_f, scale_f, gate_f = (
            m.reshape(batch, 1, D_MODEL) for m in jnp.split(mod, 6, axis=-1))

        h = _norm_mod(x2, row(g_pre_mix[l]), scale_m, shift_m, seq)
        w_in_t = w_in[l].T
        z = _mm_in(h, w_in_t, _w_in_special(w_in_t))

        q, k, v = _mla_proj(z, cos, sin, row(g_q_lat[l]), row(g_kv_lat[l]),
                            _prep_w_q(w_q_up[l]), w_kv_up[l].astype(BF16), batch, seq)
        o_a = _attn(q, k, v, batch, seq)

        wg_hi, wg_lo = _split_bf16(w_gla_gate_up[l])
        wg = jnp.zeros((LANES, GLA_HEADS * GLA_DK), BF16)
        wg = wg.at[:GLA_GATE_RANK].set(wg_hi).at[GLA_GATE_RANK:2 * GLA_GATE_RANK].set(wg_lo)
        o_b, w_down_bf, w_mix_bf = _gla(z, wg, row(b_gla_gate[l]), row(g_gla_out[l]),
                                        w_ffn_down[l], w_mix_out[l], batch, seq)

        merged = _mix1(o_a, o_b, w_branch_a[l], w_branch_b[l], z)
        x1, h2 = _mix2(merged, w_mix_bf, x2, gate_m, row(g_post_mix[l]),
                       row(g_pre_ffn[l]), scale_f, shift_f, seq)

        g = _ffn_up(h2, w_ffn_up[l], conv_w[l], row(conv_b[l]), seq)
        x2 = _ffn_down(g, w_down_bf, x1, gate_f, row(g_post_ffn[l]), seq)
    return x2.reshape(batch, seq, D_MODEL)
```

```python
import functools

import jax
import jax.numpy as jnp
import numpy as np
from jax import lax
from jax.experimental import pallas as pl
from jax.experimental.pallas import tpu as pltpu

F32 = jnp.float32
BF16 = jnp.bfloat16

D_MODEL = 2048
MLA_HEADS = 8
MLA_Q_RANK = 512
MLA_KV_RANK = 256
MLA_NOPE = 128
MLA_ROPE = 64
MLA_V = 128
ROPE_THETA = 10000.0
GLA_HEADS = 4
GLA_DK = 256
GLA_DV = 512
GLA_GATE_RANK = 16
GLA_TAU = 16.0
D_FF = 5632
CONV_WIDTH = 3
EPS = 1e-6

LANES = 128
SUBLANES = 8
MLA_QK_PAD = 256

Z_QLAT = 0
Z_KVLAT = 512
Z_KROPE = 768
Z_GLR = 896
Z_GQ = 1024
Z_GV = 2048
Z_OG = 4096
Z_GATE_A = 6144
Z_GATE_B = 8192
Z_GK = 10240
Z_WIDTH = 11264

_SRC_KROPE = MLA_Q_RANK + MLA_KV_RANK
_SRC_GQ = _SRC_KROPE + MLA_ROPE
_SRC_GK = _SRC_GQ + GLA_HEADS * GLA_DK
_SRC_GV = _SRC_GK + GLA_HEADS * GLA_DK
_SRC_GLR = _SRC_GV + GLA_HEADS * GLA_DV
_SRC_OG = _SRC_GLR + GLA_GATE_RANK
_SRC_GATE_A = _SRC_OG + GLA_HEADS * GLA_DV
_SRC_GATE_B = _SRC_GATE_A + D_MODEL

GLA_CHUNK = 128
GLA_LEVELS = 7

MIB = 1024 * 1024


def _params(semantics, vmem_mib):
    return pltpu.CompilerParams(dimension_semantics=semantics,
                                vmem_limit_bytes=vmem_mib * MIB)


def _rms(x, g):
    return x * lax.rsqrt(jnp.mean(x * x, axis=-1, keepdims=True) + EPS) * g


def _ada_kernel(c_ref, w_ref, b_ref, o_ref):
    c = c_ref[...]
    ca = c * jax.nn.sigmoid(c)
    o_ref[...] = jnp.dot(ca.astype(BF16), w_ref[...].astype(BF16),
                         preferred_element_type=F32) + b_ref[...]


def _ada(c_pad, w, b):
    n = w.shape[1]
    tn = 1024
    return pl.pallas_call(
        _ada_kernel,
        grid=(n // tn,),
        in_specs=[pl.BlockSpec((SUBLANES, D_MODEL), lambda j: (0, 0)),
                  pl.BlockSpec((D_MODEL, tn), lambda j: (0, j)),
                  pl.BlockSpec((1, tn), lambda j: (0, j))],
        out_specs=pl.BlockSpec((SUBLANES, tn), lambda j: (0, j)),
        out_shape=jax.ShapeDtypeStruct((SUBLANES, n), F32),
        compiler_params=_params(("arbitrary",), 32),
        name="ada",
    )(c_pad, w, b)


def _norm_mod_kernel(x_ref, g_ref, sc_ref, sh_ref, o_ref):
    y = _rms(x_ref[...], g_ref[...] * (1.0 + sc_ref[0]))
    o_ref[...] = (y + sh_ref[0]).astype(o_ref.dtype)


def _norm_mod(x2, g, scale, shift, seq):
    t = x2.shape[0]
    tm = 512
    tpb = seq // tm
    return pl.pallas_call(
        _norm_mod_kernel,
        grid=(t // tm,),
        in_specs=[pl.BlockSpec((tm, D_MODEL), lambda i: (i, 0)),
                  pl.BlockSpec((1, D_MODEL), lambda i: (0, 0)),
                  pl.BlockSpec((1, 1, D_MODEL), lambda i: (i // tpb, 0, 0)),
                  pl.BlockSpec((1, 1, D_MODEL), lambda i: (i // tpb, 0, 0))],
        out_specs=pl.BlockSpec((tm, D_MODEL), lambda i: (i, 0)),
        out_shape=jax.ShapeDtypeStruct((t, D_MODEL), BF16),
        compiler_params=_params(("arbitrary",), 32),
        name="norm_mod",
    )(x2, g, scale, shift)


MM_IN_TN = 1024
MM_IN_SRC_ALIGN = 16
_MM_IN_SRC_ROWS = (0,
                   _SRC_GQ,
                   _SRC_GV, _SRC_GV + MM_IN_TN,
                   _SRC_OG, _SRC_OG + MM_IN_TN,
                   _SRC_GATE_A, _SRC_GATE_A + MM_IN_TN,
                   _SRC_GATE_B, _SRC_GATE_B + MM_IN_TN,
                   _SRC_GK)


def _mm_in_kernel(src_ref, h_ref, w_ref, wsp_ref, o_ref, w_bf):
    del src_ref
    j, i = pl.program_id(0), pl.program_id(1)

    @pl.when(i == 0)
    def _():
        w_bf[...] = w_ref[...].astype(BF16)

    @pl.when((i == 0) & (j == 0))
    def _():
        w_bf[Z_KROPE:Z_GQ] = wsp_ref[...].astype(BF16)

    o_ref[...] = lax.dot_general(h_ref[...], w_bf[...], (((1,), (1,)), ((), ())),
                                 preferred_element_type=F32).astype(o_ref.dtype)


def _mm_in(h, w_t, w_special):
    t = h.shape[0]
    tm, tn = 1024, MM_IN_TN
    grid_spec = pltpu.PrefetchScalarGridSpec(
        num_scalar_prefetch=1,
        grid=(Z_WIDTH // tn, t // tm),
        in_specs=[pl.BlockSpec((tm, D_MODEL), lambda j, i, src: (i, 0)),
                  pl.BlockSpec((pl.Element(tn), pl.Element(D_MODEL)),
                               lambda j, i, src: (src[j] * MM_IN_SRC_ALIGN, 0)),
                  pl.BlockSpec((Z_GQ - Z_KROPE, D_MODEL), lambda j, i, src: (0, 0))],
        out_specs=pl.BlockSpec((tm, tn), lambda j, i, src: (i, j)),
        scratch_shapes=[pltpu.VMEM((tn, D_MODEL), BF16)])
    return pl.pallas_call(
        _mm_in_kernel,
        grid_spec=grid_spec,
        out_shape=jax.ShapeDtypeStruct((t, Z_WIDTH), BF16),
        compiler_params=_params(("arbitrary", "arbitrary"), 48),
        name="mm_in",
    )(jnp.asarray([r // MM_IN_SRC_ALIGN for r in _MM_IN_SRC_ROWS], jnp.int32), h, w_t, w_special)


ROPE_HALF = MLA_ROPE // 2
ROPE_PACK = LANES // ROPE_HALF


def _rope_kernel(pos_ref, inv_ref, cos_ref, sin_ref):
    ang = pos_ref[...].astype(F32) * inv_ref[...]
    cos_ref[...] = jnp.cos(ang)
    sin_ref[...] = jnp.sin(ang)


def _rope_tables(positions, inv):
    t = positions.size
    rows = t // ROPE_PACK
    pos_dense = jnp.repeat(positions.reshape(t), ROPE_HALF).reshape(rows, LANES)
    inv_dense = jnp.tile(inv, ROPE_PACK).reshape(1, LANES)
    tr = 512
    spec = pl.BlockSpec((tr, LANES), lambda i: (i, 0))
    cos, sin = pl.pallas_call(
        _rope_kernel,
        grid=(rows // tr,),
        in_specs=[spec, pl.BlockSpec((1, LANES), lambda i: (0, 0))],
        out_specs=[spec, spec],
        out_shape=[jax.ShapeDtypeStruct((rows, LANES), F32)] * 2,
        compiler_params=_params(("arbitrary",), 32),
        name="rope_tables",
    )(pos_dense, inv_dense)
    return cos.reshape(t, ROPE_HALF), sin.reshape(t, ROPE_HALF)


def _mla_proj_kernel(ql_ref, kvl_ref, kr_ref, cos_ref, sin_ref,
                     gq_ref, gkv_ref, wq_ref, wkv_ref, q_ref, k_ref, v_ref):
    c, s = cos_ref[...], sin_ref[...]
    zeros = jnp.zeros((c.shape[0], LANES - MLA_ROPE), F32)
    cosm = jnp.concatenate([c, c, zeros], axis=1)
    sinm = jnp.concatenate([-s, s, zeros], axis=1)

    def rope(r):
        return r * cosm + pltpu.roll(r, MLA_ROPE, 1) * sinm

    scale = (MLA_NOPE + MLA_ROPE) ** -0.5 * np.log2(np.e)
    qn =_rms(ql_ref[...].astype(F32), gq_ref[...]).astype(BF16)
    for h in range(MLA_HEADS):
        r = jnp.dot(qn, wq_ref[h], preferred_element_type=F32) * scale
        q_ref[0, h, :, 0:MLA_NOPE] = r[:, 0:MLA_NOPE].astype(BF16)
        q_ref[0, h, :, MLA_NOPE:MLA_QK_PAD] = rope(r[:, MLA_NOPE:MLA_QK_PAD]).astype(BF16)

    kvn = _rms(kvl_ref[...].astype(F32), gkv_ref[...]).astype(BF16)
    kv = jnp.dot(kvn, wkv_ref[...], preferred_element_type=F32).astype(BF16)
    kpe = rope(kr_ref[...].astype(F32)).astype(BF16)
    hw = MLA_NOPE + MLA_V
    for h in range(MLA_HEADS):
        k_ref[0, h, :, 0:MLA_NOPE] = kv[:, h * hw:h * hw + MLA_NOPE]
        k_ref[0, h, :, MLA_NOPE:MLA_QK_PAD] = kpe
        v_ref[0, h] = kv[:, h * hw + MLA_NOPE:(h + 1) * hw]


def _mla_proj(z, cos, sin, gq, gkv, wq, wkv, batch, seq):
    tm = 512
    tpb = seq // tm
    qk_shape = jax.ShapeDtypeStruct((batch, MLA_HEADS, seq, MLA_QK_PAD), BF16)
    v_shape = jax.ShapeDtypeStruct((batch, MLA_HEADS, seq, MLA_V), BF16)
    const2 = lambda i: (0, 0)
    qk_spec = pl.BlockSpec((1, MLA_HEADS, tm, MLA_QK_PAD), lambda i: (i // tpb, 0, i % tpb, 0))
    return pl.pallas_call(
        _mla_proj_kernel,
        grid=(batch * tpb,),
        in_specs=[pl.BlockSpec((tm, MLA_Q_RANK), lambda i: (i, Z_QLAT // MLA_Q_RANK)),
                  pl.BlockSpec((tm, MLA_KV_RANK), lambda i: (i, Z_KVLAT // MLA_KV_RANK)),
                  pl.BlockSpec((tm, LANES), lambda i: (i, Z_KROPE // LANES)),
                  pl.BlockSpec((tm, ROPE_HALF), lambda i: (i, 0)),
                  pl.BlockSpec((tm, ROPE_HALF), lambda i: (i, 0)),
                  pl.BlockSpec((1, MLA_Q_RANK), const2),
                  pl.BlockSpec((1, MLA_KV_RANK), const2),
                  pl.BlockSpec((MLA_HEADS, MLA_Q_RANK, MLA_QK_PAD), lambda i: (0, 0, 0)),
                  pl.BlockSpec((MLA_KV_RANK, MLA_HEADS * (MLA_NOPE + MLA_V)), const2)],
        out_specs=[qk_spec, qk_spec,
                   pl.BlockSpec((1, MLA_HEADS, tm, MLA_V), lambda i: (i // tpb, 0, i % tpb, 0))],
        out_shape=[qk_shape, qk_shape, v_shape],
        compiler_params=_params(("arbitrary",), 48),
        name="mla_proj",
    )(z, z, z, cos, sin, gq, gkv, wq, wkv)


ATTN_KV_BLOCK = 1024
ATTN_Q_BLOCKS = 2
ATTN_DIAG_CHAINS = 2


def _attn_kernel(q_ref, k_ref, v_ref, o_ref, m_sc, l_sc, acc_sc):
    blk = ATTN_KV_BLOCK
    nt = (((1,), (1,)), ((), ()))
    qi = pl.program_id(2)
    m_sc[...] = jnp.full(m_sc.shape, -jnp.inf, F32)
    l_sc[...] = jnp.zeros(l_sc.shape, F32)
    acc_sc[...] = jnp.zeros(acc_sc.shape, F32)

    def update(rows, s, v):
        m_prev = m_sc[rows]
        m_new = jnp.maximum(m_prev, jnp.max(s, axis=-1, keepdims=True))
        alpha = jnp.exp2(m_prev - m_new)
        ps = [jnp.exp2(s[:, c * LANES:(c + 1) * LANES] - m_new)
              for c in range(s.shape[1] // LANES)]
        l_sc[rows] = alpha * l_sc[rows] + functools.reduce(lambda a, b: a + b, ps)
        p = jnp.concatenate(ps, axis=1).astype(BF16)
        acc_sc[rows] = alpha * acc_sc[rows] + jnp.dot(p, v, preferred_element_type=F32)
        m_sc[rows] = m_new

    def full_block(rows, ks):
        k = k_ref[0, 0, pl.ds(ks, blk), :]
        v = v_ref[0, 0, pl.ds(ks, blk), :]
        update(rows, lax.dot_general(q_ref[0, 0, rows], k, nt, preferred_element_type=F32), v)

    def diag_block(row0, ks):
        sub = blk // ATTN_DIAG_CHAINS
        for r in range(ATTN_DIAG_CHAINS):
            n = (r + 1) * sub
            rows = slice(row0 + r * sub, row0 + n)
            k = k_ref[0, 0, pl.ds(ks, n), :]
            v = v_ref[0, 0, pl.ds(ks, n), :]
            s = lax.dot_general(q_ref[0, 0, rows], k, nt, preferred_element_type=F32)
            row = lax.broadcasted_iota(jnp.int32, s.shape, 0) + r * sub
            col = lax.broadcasted_iota(jnp.int32, s.shape, 1)
            update(rows, jnp.where(col <= row, s, -jnp.inf), v)

    def body(kb, carry):
        full_block(slice(0, ATTN_Q_BLOCKS * blk), pl.multiple_of(kb * blk, blk))
        return carry

    lax.fori_loop(0, qi * ATTN_Q_BLOCKS, body, 0)

    base = qi * ATTN_Q_BLOCKS * blk
    for a in range(ATTN_Q_BLOCKS):
        for kb in range(a):
            full_block(slice(a * blk, (a + 1) * blk), pl.multiple_of(base + kb * blk, blk))
        diag_block(a * blk, pl.multiple_of(base + a * blk, blk))
    l = jnp.sum(l_sc[...], axis=-1, keepdims=True)
    o_ref[...] = (acc_sc[...] / l).astype(o_ref.dtype)


def _attn(q, k, v, batch, seq):
    blk = ATTN_Q_BLOCKS * ATTN_KV_BLOCK
    nq = seq // blk
    return pl.pallas_call(
        _attn_kernel,
        grid=(batch, MLA_HEADS, nq),
        in_specs=[pl.BlockSpec((1, 1, blk, MLA_QK_PAD), lambda b, h, i: (b, h, i, 0)),
                  pl.BlockSpec((1, 1, seq, MLA_QK_PAD), lambda b, h, i: (b, h, 0, 0)),
                  pl.BlockSpec((1, 1, seq, MLA_V), lambda b, h, i: (b, h, 0, 0))],
        out_specs=pl.BlockSpec((blk, MLA_V), lambda b, h, i: (b * nq + i, h)),
        out_shape=jax.ShapeDtypeStruct((batch * seq, MLA_HEADS * MLA_V), BF16),
        scratch_shapes=[pltpu.VMEM((blk, LANES), F32), pltpu.VMEM((blk, LANES), F32),
                        pltpu.VMEM((blk, MLA_V), F32)],
        compiler_params=_params(("arbitrary", "arbitrary", "arbitrary"), 48),
        name="attn",
    )(q, k, v)


def _gla_constants():
    c = GLA_CHUNK
    idx = np.arange(c)
    w = np.zeros((GLA_LEVELS + 2, c, c), np.float32)
    masks = np.zeros((GLA_LEVELS + 1, c, c), np.float32)
    for l in range(GLA_LEVELS):
        half = 1 << l
        blk = idx // (2 * half)
        mid = blk * 2 * half + half - 1
        upper = (idx % (2 * half)) >= half
        t = idx[None, :]
        up_rows = (t > mid[:, None]) & (t <= idx[:, None])
        lo_rows = (t > idx[:, None]) & (t <= mid[:, None])
        w[l] = np.where(upper[:, None], up_rows, lo_rows)
        masks[l] = (blk[:, None] == blk[None, :]) & upper[:, None] & (~upper[None, :])
    w[GLA_LEVELS] = idx[None, :] <= idx[:, None]
    w[GLA_LEVELS + 1] = idx[None, :] > idx[:, None]
    masks[GLA_LEVELS] = np.eye(c)
    w = w.reshape(-1, c)
    return np.concatenate([w, w], axis=1), masks


def _split_bf16(x):
    hi = x.astype(BF16)
    return hi, (x - hi.astype(F32)).astype(BF16)


GLA_CHUNKS_PER_STEP = 2
GLA_STAGGER = 2
RIDER_ROWS = 256


def _gla_kernel(q_ref, k_ref, v_ref, lr_ref, og_ref, wg_ref, bg_ref, go_ref, ws_ref, mask_ref,
                wd_ref, wm_ref, o_ref, wd_o, wm_o, st_ref, *, down_blocks, mix_blocks):
    c = GLA_CHUNK
    dk, dv = GLA_DK, GLA_DV
    nt = (((1,), (1,)), ((), ()))
    step = pl.program_id(0) * pl.num_programs(1) + pl.program_id(1)

    @pl.when(step < down_blocks)
    def _():
        wd_o[...] = wd_ref[...].astype(BF16)

    @pl.when((step >= down_blocks) & (step < down_blocks + mix_blocks))
    def _():
        wm_o[...] = wm_ref[...].astype(BF16)

    @pl.when(pl.program_id(1) == 0)
    def _():
        st_ref[...] = jnp.zeros(st_ref.shape, F32)

    def rows(ci):
        return slice(ci * c, (ci + 1) * c)

    def qk(ci, h):
        q = q_ref[rows(ci), h * dk:(h + 1) * dk] * (dk ** -0.5)
        return q, k_ref[rows(ci), h * dk:(h + 1) * dk]

    def gate(ci, h):
        lr = lr_ref[rows(ci)]
        x = (jnp.dot(lr, wg_ref[:, h * dk:(h + 1) * dk], preferred_element_type=F32)
             + bg_ref[:, h * dk:(h + 1) * dk])
        log2_a = (-(jnp.maximum(-x, 0.0) + jnp.log1p(jnp.exp(-jnp.abs(x))))
                  * (np.log2(np.e) / GLA_TAU))
        return jnp.concatenate(_split_bf16(log2_a), axis=0)

    def exponents(log2_a):
        return jnp.dot(ws_ref[...], log2_a, preferred_element_type=F32)

    def level_operands(ci, h, pb):
        q, k = qk(ci, h)
        ops = [(q * pb[l * c:(l + 1) * c], k * pb[l * c:(l + 1) * c]) for l in range(GLA_LEVELS)]
        return ops + [(q, k)]

    def level_scores(ops):
        return [lax.dot_general(a, b, nt, preferred_element_type=F32) for a, b in ops]

    def masked_sum(scores):
        attn = mask_ref[0] * scores[0]
        for l in range(1, GLA_LEVELS + 1):
            attn = attn + mask_ref[l] * scores[l]
        return attn.astype(BF16)

    def mix(ci, h, p, pb, attn):
        q, k = qk(ci, h)
        v = v_ref[rows(ci), h * dv:(h + 1) * dv]
        eb = pb[GLA_LEVELS * c:(GLA_LEVELS + 1) * c]
        ebl = pb[(GLA_LEVELS + 1) * c:(GLA_LEVELS + 2) * c]
        st = st_ref[h]
        o = (lax.dot_general(q * eb, st.astype(BF16), nt, preferred_element_type=F32)
             + jnp.dot(attn, v, preferred_element_type=F32))
        v_t = v.astype(F32).T.astype(BF16)
        decay = p[(GLA_LEVELS + 1) * c - 1:(GLA_LEVELS + 1) * c]
        st_ref[h] = st * decay + jnp.dot(v_t, k * ebl, preferred_element_type=F32)
        return o

    def emit(ci, h, o):
        y = _rms(o, go_ref[...])
        og = og_ref[rows(ci), h * dv:(h + 1) * dv].astype(F32)
        o_ref[rows(ci), h * dv:(h + 1) * dv] = (y * (og * jax.nn.sigmoid(og))).astype(o_ref.dtype)

    heads = range(GLA_HEADS)

    def chunk_stages(ci):
        log2_a = [gate(ci, h) for h in heads]
        yield
        expo = [exponents(x) for x in log2_a]
        p = [jnp.exp2(e) for e in expo]
        pb = [x.astype(BF16) for x in p]
        yield
        operands = [level_operands(ci, h, pb[h]) for h in heads]
        scores = [level_scores(ops) for ops in operands]
        yield
        attn = [masked_sum(s) for s in scores]
        out = [mix(ci, h, p[h], pb[h], attn[h]) for h in heads]
        yield
        for h in heads:
            emit(ci, h, out[h])
        yield

    live = []
    pending = [chunk_stages(ci) for ci in range(GLA_CHUNKS_PER_STEP)]
    tick = 0
    while pending or live:
        if pending and tick % GLA_STAGGER == 0:
            live.append(pending.pop(0))
        for gen in list(live):
            if next(gen, "done") == "done":
                live.remove(gen)
        tick += 1


def _gla(z, wg, bg, go, w_down, w_mix, batch, seq):
    c = GLA_CHUNK * GLA_CHUNKS_PER_STEP
    nc = seq // c
    ws, masks = _gla_constants()
    hk, hv = GLA_HEADS * GLA_DK, GLA_HEADS * GLA_DV
    const2 = lambda b, i: (0, 0)
    down_blocks = w_down.shape[0] // RIDER_ROWS
    mix_blocks = w_mix.shape[0] // RIDER_ROWS
    assert down_blocks + mix_blocks <= batch * nc
    down_spec = pl.BlockSpec((RIDER_ROWS, D_MODEL),
                             lambda b, i: (jnp.minimum(b * nc + i, down_blocks - 1), 0))
    mix_spec = pl.BlockSpec((RIDER_ROWS, D_MODEL),
                            lambda b, i: (jnp.clip(b * nc + i - down_blocks, 0, mix_blocks - 1), 0))
    return pl.pallas_call(
        functools.partial(_gla_kernel, down_blocks=down_blocks, mix_blocks=mix_blocks),
        grid=(batch, nc),
        in_specs=[pl.BlockSpec((c, hk), lambda b, i: (b * nc + i, Z_GQ // hk)),
                  pl.BlockSpec((c, hk), lambda b, i: (b * nc + i, Z_GK // hk)),
                  pl.BlockSpec((c, hv), lambda b, i: (b * nc + i, Z_GV // hv)),
                  pl.BlockSpec((c, LANES), lambda b, i: (b * nc + i, Z_GLR // LANES)),
                  pl.BlockSpec((c, hv), lambda b, i: (b * nc + i, Z_OG // hv)),
                  pl.BlockSpec((LANES, hk), const2),
                  pl.BlockSpec((1, hk), const2),
                  pl.BlockSpec((1, GLA_DV), const2),
                  pl.BlockSpec(ws.shape, const2),
                  pl.BlockSpec(masks.shape, lambda b, i: (0, 0, 0)),
                  down_spec, mix_spec],
        out_specs=[pl.BlockSpec((c, hv), lambda b, i: (b * nc + i, 0)), down_spec, mix_spec],
        out_shape=[jax.ShapeDtypeStruct((batch * seq, hv), BF16),
                   jax.ShapeDtypeStruct(w_down.shape, BF16),
                   jax.ShapeDtypeStruct(w_mix.shape, BF16)],
        scratch_shapes=[pltpu.VMEM((GLA_HEADS, GLA_DV, GLA_DK), F32)],
        compiler_params=_params(("arbitrary", "arbitrary"), 48),
        name="gla",
    )(z, z, z, z, z, wg, bg, go, jnp.asarray(ws, BF16), jnp.asarray(masks), w_down, w_mix)


def _mix1_kernel(oa_ref, ob_ref, wa_ref, wb_ref, ga_ref, gb_ref, o_ref, wa_bf, wb_bf):
    @pl.when(pl.program_id(1) == 0)
    def _():
        wa_bf[...] = wa_ref[...].astype(BF16)
        wb_bf[...] = wb_ref[...].astype(BF16)

    ya = jnp.dot(oa_ref[...], wa_bf[...], preferred_element_type=F32)
    yb = jnp.dot(ob_ref[...], wb_bf[...], preferred_element_type=F32)
    o_ref[...] = (jax.nn.sigmoid(ga_ref[...].astype(F32)) * ya
                  + jax.nn.sigmoid(gb_ref[...].astype(F32)) * yb).astype(o_ref.dtype)


def _mix1(oa, ob, wa, wb, z):
    t = oa.shape[0]
    tm, tn = 512, 1024
    return pl.pallas_call(
        _mix1_kernel,
        grid=(D_MODEL // tn, t // tm),
        in_specs=[pl.BlockSpec((tm, oa.shape[1]), lambda j, i: (i, 0)),
                  pl.BlockSpec((tm, ob.shape[1]), lambda j, i: (i, 0)),
                  pl.BlockSpec((wa.shape[0], tn), lambda j, i: (0, j)),
                  pl.BlockSpec((wb.shape[0], tn), lambda j, i: (0, j)),
                  pl.BlockSpec((tm, tn), lambda j, i: (i, Z_GATE_A // tn + j)),
                  pl.BlockSpec((tm, tn), lambda j, i: (i, Z_GATE_B // tn + j))],
        out_specs=pl.BlockSpec((tm, tn), lambda j, i: (i, j)),
        out_shape=jax.ShapeDtypeStruct((t, D_MODEL), BF16),
        scratch_shapes=[pltpu.VMEM((wa.shape[0], tn), BF16), pltpu.VMEM((wb.shape[0], tn), BF16)],
        compiler_params=_params(("arbitrary", "arbitrary"), 48),
        name="mix1",
    )(oa, ob, wa, wb, z, z)


def _mix2_kernel(m_ref, w_ref, x_ref, gm_ref, gpost_ref, gpre_ref, sc_ref, sh_ref,
                 x1_ref, h2_ref):
    post = gm_ref[0] * gpost_ref[...]
    pre = gpre_ref[...] * (1.0 + sc_ref[0])
    for r0 in range(0, m_ref.shape[0], MIX2_SUBTILE):
        rows = slice(r0, r0 + MIX2_SUBTILE)
        y = jnp.dot(m_ref[rows], w_ref[...], preferred_element_type=F32)
        x1 = x_ref[rows] + _rms(y, post)
        x1_ref[rows] = x1
        h2_ref[rows] = (_rms(x1, pre) + sh_ref[0]).astype(h2_ref.dtype)


MIX2_SUBTILE = 256


def _mix2(merged, w, x2, gate_m, g_post, g_pre, scale_f, shift_f, seq):
    t = merged.shape[0]
    tm = 512
    tpb = seq // tm
    row = pl.BlockSpec((tm, D_MODEL), lambda i: (i, 0))
    vec = pl.BlockSpec((1, D_MODEL), lambda i: (0, 0))
    per_batch = pl.BlockSpec((1, 1, D_MODEL), lambda i: (i // tpb, 0, 0))
    return pl.pallas_call(
        _mix2_kernel,
        grid=(t // tm,),
        in_specs=[row, pl.BlockSpec((D_MODEL, D_MODEL), lambda i: (0, 0),
                                    pipeline_mode=pl.Buffered(1)), row,
                  per_batch, vec, vec, per_batch, per_batch],
        out_specs=[row, row],
        out_shape=[jax.ShapeDtypeStruct((t, D_MODEL), F32),
                   jax.ShapeDtypeStruct((t, D_MODEL), BF16)],
        compiler_params=_params(("arbitrary",), 48),
        name="mix2",
    )(merged, w, x2, gate_m, g_post, g_pre, scale_f, shift_f)


def _ffn_up_kernel(h_ref, wa_ref, wv_ref, cwa_ref, cwv_ref, cba_ref, cbv_ref, o_ref,
                   w_bf, u_sc, *, tiles_per_batch):
    i = pl.program_id(1)
    tm, tn = o_ref.shape
    head, body, tail = slice(0, SUBLANES), slice(SUBLANES, SUBLANES + tm), slice(tm, tm + SUBLANES)

    @pl.when(i == 0)
    def _():
        w_bf[:, 0:tn] = wa_ref[...].astype(BF16)
        w_bf[:, tn:2 * tn] = wv_ref[...].astype(BF16)

    @pl.when(i % tiles_per_batch == 0)
    def _():
        u_sc[head] = jnp.zeros((SUBLANES, u_sc.shape[1]), F32)

    @pl.when(i % tiles_per_batch != 0)
    def _():
        u_sc[head] = u_sc[tail]

    u_sc[body] = jnp.dot(h_ref[...], w_bf[...], preferred_element_type=F32)

    def conv(cols, cw_ref, cb_ref):
        cw = cw_ref[...]
        acc = cb_ref[...] + u_sc[body, cols] * cw[CONV_WIDTH - 1:CONV_WIDTH]
        for tap in range(1, CONV_WIDTH):
            acc = acc + (u_sc[SUBLANES - tap:SUBLANES - tap + tm, cols]
                         * cw[CONV_WIDTH - 1 - tap:CONV_WIDTH - tap])
        return acc

    a = conv(slice(0, tn), cwa_ref, cba_ref)
    val = conv(slice(tn, 2 * tn), cwv_ref, cbv_ref)
    o_ref[...] = (jax.nn.gelu(a, approximate=True) * val).astype(o_ref.dtype)


def _ffn_up(h2, w_up, conv_w, conv_b, seq):
    t = h2.shape[0]
    tm, tn = 1024, 512
    nj = D_FF // tn
    return pl.pallas_call(
        functools.partial(_ffn_up_kernel, tiles_per_batch=seq // tm),
        grid=(nj, t // tm),
        in_specs=[pl.BlockSpec((tm, D_MODEL), lambda j, i: (i, 0)),
                  pl.BlockSpec((D_MODEL, tn), lambda j, i: (0, j)),
                  pl.BlockSpec((D_MODEL, tn), lambda j, i: (0, nj + j)),
                  pl.BlockSpec((CONV_WIDTH, tn), lambda j, i: (0, j)),
                  pl.BlockSpec((CONV_WIDTH, tn), lambda j, i: (0, nj + j)),
                  pl.BlockSpec((1, tn), lambda j, i: (0, j)),
                  pl.BlockSpec((1, tn), lambda j, i: (0, nj + j))],
        out_specs=pl.BlockSpec((tm, tn), lambda j, i: (i, j)),
        out_shape=jax.ShapeDtypeStruct((t, D_FF), BF16),
        scratch_shapes=[pltpu.VMEM((D_MODEL, 2 * tn), BF16),
                        pltpu.VMEM((tm + SUBLANES, 2 * tn), F32)],
        compiler_params=_params(("arbitrary", "arbitrary"), 56),
        name="ffn_up",
    )(h2, w_up, w_up, conv_w, conv_w, conv_b, conv_b)


def _ffn_down_kernel(g_ref, w_ref, x1_ref, gf_ref, gpost_ref, o_ref):
    f = jnp.dot(g_ref[...], w_ref[...], preferred_element_type=F32)
    o_ref[...] = x1_ref[...] + _rms(f, gf_ref[0] * gpost_ref[...])


def _ffn_down(g, w, x1, gate_f, g_post, seq):
    t = g.shape[0]
    tm = 256
    tpb = seq // tm
    return pl.pallas_call(
        _ffn_down_kernel,
        grid=(t // tm,),
        in_specs=[pl.BlockSpec((tm, D_FF), lambda i: (i, 0)),
                  pl.BlockSpec((D_FF, D_MODEL), lambda i: (0, 0), pipeline_mode=pl.Buffered(1)),
                  pl.BlockSpec((tm, D_MODEL), lambda i: (i, 0)),
                  pl.BlockSpec((1, 1, D_MODEL), lambda i: (i // tpb, 0, 0)),
                  pl.BlockSpec((1, D_MODEL), lambda i: (0, 0))],
        out_specs=pl.BlockSpec((tm, D_MODEL), lambda i: (i, 0)),
        out_shape=jax.ShapeDtypeStruct((t, D_MODEL), F32),
        compiler_params=_params(("arbitrary",), 56),
        name="ffn_down",
    )(g, w, x1, gate_f, g_post)


def _swap_halves(w):
    half = w.shape[-1] // 2
    return jnp.concatenate([w[..., half:], w[..., :half]], axis=-1)


def _w_in_special(w_t):
    half = MLA_ROPE // 2
    k_rope = w_t[_SRC_KROPE:_SRC_KROPE + MLA_ROPE]
    g_lr = w_t[_SRC_GLR:_SRC_GLR + GLA_GATE_RANK]
    pad = jnp.zeros((Z_GQ - Z_GLR - 2 * GLA_GATE_RANK, w_t.shape[1]), w_t.dtype)
    return jnp.concatenate([k_rope, k_rope[half:], k_rope[:half], g_lr, g_lr, pad], axis=0)


def _prep_w_q(w_q_up):
    w = w_q_up.reshape(MLA_Q_RANK, MLA_HEADS, MLA_NOPE + MLA_ROPE)
    pe = w[:, :, MLA_NOPE:]
    w = jnp.concatenate([w, _swap_halves(pe)], axis=-1)
    return jnp.transpose(w, (1, 0, 2)).astype(BF16)


def kernel(x, c, positions, w_ada, b_ada, g_pre_mix, w_in, g_q_lat, w_q_up, g_kv_lat, w_kv_up,
           w_gla_gate_up, b_gla_gate, g_gla_out, w_branch_a, w_branch_b, w_mix_out, g_post_mix,
           g_pre_ffn, w_ffn_up, conv_w, conv_b, w_ffn_down, g_post_ffn):
    batch, seq, _ = x.shape
    depth = w_ada.shape[0]
    t = batch * seq
    row = lambda v: v.reshape(1, -1)

    inv = 1.0 / (ROPE_THETA ** (jnp.arange(0, MLA_ROPE, 2, dtype=F32) / MLA_ROPE))
    cos, sin = _rope_tables(positions, inv)

    c_pad = jnp.zeros((SUBLANES, D_MODEL), F32).at[:batch].set(c)
    x2 = x.reshape(t, D_MODEL)
    for l in range(depth):
        mod = _ada(c_pad, w_ada[l], row(b_ada[l]))[:batch]
        shift_m, scale_m, gate_m, shift_f, scale_f, gate_f = (
            m.reshape(batch, 1, D_MODEL) for m in jnp.split(mod, 6, axis=-1))

        h = _norm_mod(x2, row(g_pre_mix[l]), scale_m, shift_m, seq)
        w_in_t = w_in[l].T
        z = _mm_in(h, w_in_t, _w_in_special(w_in_t))

        q, k, v = _mla_proj(z, cos, sin, row(g_q_lat[l]), row(g_kv_lat[l]),
                            _prep_w_q(w_q_up[l]), w_kv_up[l].astype(BF16), batch, seq)
        o_a = _attn(q, k, v, batch, seq)

        wg_hi, wg_lo = _split_bf16(w_gla_gate_up[l])
        wg = jnp.zeros((LANES, GLA_HEADS * GLA_DK), BF16)
        wg = wg.at[:GLA_GATE_RANK].set(wg_hi).at[GLA_GATE_RANK:2 * GLA_GATE_RANK].set(wg_lo)
        o_b, w_down_bf, w_mix_bf = _gla(z, wg, row(b_gla_gate[l]), row(g_gla_out[l]),
                                        w_ffn_down[l], w_mix_out[l], batch, seq)

        merged = _mix1(o_a, o_b, w_branch_a[l], w_branch_b[l], z)
        x1, h2 = _mix2(merged, w_mix_bf, x2, gate_m, row(g_post_mix[l]),
                       row(g_pre_ffn[l]), scale_f, shift_f, seq)

        g = _ffn_up(h2, w_ffn_up[l], conv_w[l], row(conv_b[l]), seq)
        x2 = _ffn_down(g, w_down_bf, x1, gate_f, row(g_post_ffn[l]), seq)
    return x2.reshape(batch, seq, D_MODEL)
```

```python
import functools

import jax
import jax.numpy as jnp
import numpy as np
from jax import lax
from jax.experimental import pallas as pl
from jax.experimental.pallas import tpu as pltpu

F32 = jnp.float32
BF16 = jnp.bfloat16

D_MODEL = 2048
MLA_HEADS = 8
MLA_Q_RANK = 512
MLA_KV_RANK = 256
MLA_NOPE = 128
MLA_ROPE = 64
MLA_V = 128
ROPE_THETA = 10000.0
GLA_HEADS = 4
GLA_DK = 256
GLA_DV = 512
GLA_GATE_RANK = 16
GLA_TAU = 16.0
D_FF = 5632
CONV_WIDTH = 3
EPS = 1e-6

LANES = 128
SUBLANES = 8
MLA_QK_PAD = 256

Z_QLAT = 0
Z_KVLAT = 512
Z_KROPE = 768
Z_GLR = 896
Z_GQ = 1024
Z_GV = 2048
Z_OG = 4096
Z_GATE_A = 6144
Z_GATE_B = 8192
Z_GK = 10240
Z_WIDTH = 11264

_SRC_KROPE = MLA_Q_RANK + MLA_KV_RANK
_SRC_GQ = _SRC_KROPE + MLA_ROPE
_SRC_GK = _SRC_GQ + GLA_HEADS * GLA_DK
_SRC_GV = _SRC_GK + GLA_HEADS * GLA_DK
_SRC_GLR = _SRC_GV + GLA_HEADS * GLA_DV
_SRC_OG = _SRC_GLR + GLA_GATE_RANK
_SRC_GATE_A = _SRC_OG + GLA_HEADS * GLA_DV
_SRC_GATE_B = _SRC_GATE_A + D_MODEL

GLA_CHUNK = 128
GLA_LEVELS = 7

MIB = 1024 * 1024


def _params(semantics, vmem_mib):
    return pltpu.CompilerParams(dimension_semantics=semantics,
                                vmem_limit_bytes=vmem_mib * MIB)


def _rms(x, g):
    return x * lax.rsqrt(jnp.mean(x * x, axis=-1, keepdims=True) + EPS) * g


RIDER_ROWS = 256


def _rider_specs(arrays, step_of):
    specs, windows, start = [], [], 0
    for a in arrays:
        n = a.shape[0] // RIDER_ROWS
        specs.append(pl.BlockSpec(
            (RIDER_ROWS, a.shape[1]),
            lambda *idx, lo=start, n=n: (jnp.clip(step_of(*idx) - lo, 0, n - 1), 0)))
        windows.append((start, start + n))
        start += n
    return specs, windows, start


def _run_riders(step, windows, src_refs, dst_refs):
    for (lo, hi), src, dst in zip(windows, src_refs, dst_refs):
        @pl.when((step >= lo) & (step < hi))
        def _(src=src, dst=dst):
            dst[...] = src[...].astype(BF16)


def _ada_kernel(c_ref, w_ref, b_ref, o_ref):
    c = c_ref[...]
    ca = c * jax.nn.sigmoid(c)
    o_ref[...] = jnp.dot(ca.astype(BF16), w_ref[...].astype(BF16),
                         preferred_element_type=F32) + b_ref[...]


def _ada(c_pad, w, b):
    n = w.shape[1]
    tn = 1024
    return pl.pallas_call(
        _ada_kernel,
        grid=(n // tn,),
        in_specs=[pl.BlockSpec((SUBLANES, D_MODEL), lambda j: (0, 0)),
                  pl.BlockSpec((D_MODEL, tn), lambda j: (0, j)),
                  pl.BlockSpec((1, tn), lambda j: (0, j))],
        out_specs=pl.BlockSpec((SUBLANES, tn), lambda j: (0, j)),
        out_shape=jax.ShapeDtypeStruct((SUBLANES, n), F32),
        compiler_params=_params(("arbitrary",), 32),
        name="ada",
    )(c_pad, w, b)


def _norm_mod_kernel(x_ref, g_ref, sc_ref, sh_ref, o_ref):
    y = _rms(x_ref[...], g_ref[...] * (1.0 + sc_ref[0]))
    o_ref[...] = (y + sh_ref[0]).astype(o_ref.dtype)


def _norm_mod(x2, g, scale, shift, seq):
    t = x2.shape[0]
    tm = 512
    tpb = seq // tm
    return pl.pallas_call(
        _norm_mod_kernel,
        grid=(t // tm,),
        in_specs=[pl.BlockSpec((tm, D_MODEL), lambda i: (i, 0)),
                  pl.BlockSpec((1, D_MODEL), lambda i: (0, 0)),
                  pl.BlockSpec((1, 1, D_MODEL), lambda i: (i // tpb, 0, 0)),
                  pl.BlockSpec((1, 1, D_MODEL), lambda i: (i // tpb, 0, 0))],
        out_specs=pl.BlockSpec((tm, D_MODEL), lambda i: (i, 0)),
        out_shape=jax.ShapeDtypeStruct((t, D_MODEL), BF16),
        compiler_params=_params(("arbitrary",), 32),
        name="norm_mod",
    )(x2, g, scale, shift)


MM_IN_TN = 1024
MM_IN_SRC_ALIGN = 16
_MM_IN_SRC_ROWS = (0,
                   _SRC_GQ,
                   _SRC_GV, _SRC_GV + MM_IN_TN,
                   _SRC_OG, _SRC_OG + MM_IN_TN,
                   _SRC_GATE_A, _SRC_GATE_A + MM_IN_TN,
                   _SRC_GATE_B, _SRC_GATE_B + MM_IN_TN,
                   _SRC_GK)


def _mm_in_kernel(src_ref, h_ref, w_ref, wsp_ref, o_ref, w_bf):
    del src_ref
    j, i = pl.program_id(0), pl.program_id(1)

    @pl.when(i == 0)
    def _():
        w_bf[...] = w_ref[...].astype(BF16)

    @pl.when((i == 0) & (j == 0))
    def _():
        w_bf[Z_KROPE:Z_GQ] = wsp_ref[...].astype(BF16)

    o_ref[...] = lax.dot_general(h_ref[...], w_bf[...], (((1,), (1,)), ((), ())),
                                 preferred_element_type=F32).astype(o_ref.dtype)


def _mm_in(h, w_t, w_special):
    t = h.shape[0]
    tm, tn = 1024, MM_IN_TN
    grid_spec = pltpu.PrefetchScalarGridSpec(
        num_scalar_prefetch=1,
        grid=(Z_WIDTH // tn, t // tm),
        in_specs=[pl.BlockSpec((tm, D_MODEL), lambda j, i, src: (i, 0)),
                  pl.BlockSpec((pl.Element(tn), pl.Element(D_MODEL)),
                               lambda j, i, src: (src[j] * MM_IN_SRC_ALIGN, 0)),
                  pl.BlockSpec((Z_GQ - Z_KROPE, D_MODEL), lambda j, i, src: (0, 0))],
        out_specs=pl.BlockSpec((tm, tn), lambda j, i, src: (i, j)),
        scratch_shapes=[pltpu.VMEM((tn, D_MODEL), BF16)])
    return pl.pallas_call(
        _mm_in_kernel,
        grid_spec=grid_spec,
        out_shape=jax.ShapeDtypeStruct((t, Z_WIDTH), BF16),
        compiler_params=_params(("arbitrary", "arbitrary"), 48),
        name="mm_in",
    )(jnp.asarray([r // MM_IN_SRC_ALIGN for r in _MM_IN_SRC_ROWS], jnp.int32), h, w_t, w_special)


ROPE_HALF = MLA_ROPE // 2
ROPE_PACK = LANES // ROPE_HALF


def _rope_kernel(pos_ref, inv_ref, cos_ref, sin_ref):
    ang = pos_ref[...].astype(F32) * inv_ref[...]
    cos_ref[...] = jnp.cos(ang)
    sin_ref[...] = jnp.sin(ang)


def _rope_tables(positions, inv):
    t = positions.size
    rows = t // ROPE_PACK
    pos_dense = jnp.repeat(positions.reshape(t), ROPE_HALF).reshape(rows, LANES)
    inv_dense = jnp.tile(inv, ROPE_PACK).reshape(1, LANES)
    tr = 512
    spec = pl.BlockSpec((tr, LANES), lambda i: (i, 0))
    cos, sin = pl.pallas_call(
        _rope_kernel,
        grid=(rows // tr,),
        in_specs=[spec, pl.BlockSpec((1, LANES), lambda i: (0, 0))],
        out_specs=[spec, spec],
        out_shape=[jax.ShapeDtypeStruct((rows, LANES), F32)] * 2,
        compiler_params=_params(("arbitrary",), 32),
        name="rope_tables",
    )(pos_dense, inv_dense)
    return cos.reshape(t, ROPE_HALF), sin.reshape(t, ROPE_HALF)


def _mla_proj_kernel(ql_ref, kvl_ref, kr_ref, cos_ref, sin_ref,
                     gq_ref, gkv_ref, wq_ref, wkv_ref, q_ref, k_ref, v_ref):
    c, s = cos_ref[...], sin_ref[...]
    zeros = jnp.zeros((c.shape[0], LANES - MLA_ROPE), F32)
    cosm = jnp.concatenate([c, c, zeros], axis=1)
    sinm = jnp.concatenate([-s, s, zeros], axis=1)

    def rope(r):
        return r * cosm + pltpu.roll(r, MLA_ROPE, 1) * sinm

    scale = (MLA_NOPE + MLA_ROPE) ** -0.5 * np.log2(np.e)
    qn =_rms(ql_ref[...].astype(F32), gq_ref[...]).astype(BF16)
    for h in range(MLA_HEADS):
        r = jnp.dot(qn, wq_ref[h], preferred_element_type=F32) * scale
        q_ref[0, h, :, 0:MLA_NOPE] = r[:, 0:MLA_NOPE].astype(BF16)
        q_ref[0, h, :, MLA_NOPE:MLA_QK_PAD] = rope(r[:, MLA_NOPE:MLA_QK_PAD]).astype(BF16)

    kvn = _rms(kvl_ref[...].astype(F32), gkv_ref[...]).astype(BF16)
    kv = jnp.dot(kvn, wkv_ref[...], preferred_element_type=F32).astype(BF16)
    kpe = rope(kr_ref[...].astype(F32)).astype(BF16)
    hw = MLA_NOPE + MLA_V
    for h in range(MLA_HEADS):
        k_ref[0, h, :, 0:MLA_NOPE] = kv[:, h * hw:h * hw + MLA_NOPE]
        k_ref[0, h, :, MLA_NOPE:MLA_QK_PAD] = kpe
        v_ref[0, h] = kv[:, h * hw + MLA_NOPE:(h + 1) * hw]


def _mla_proj(z, cos, sin, gq, gkv, wq, wkv, batch, seq):
    tm = 512
    tpb = seq // tm
    qk_shape = jax.ShapeDtypeStruct((batch, MLA_HEADS, seq, MLA_QK_PAD), BF16)
    v_shape = jax.ShapeDtypeStruct((batch, MLA_HEADS, seq, MLA_V), BF16)
    const2 = lambda i: (0, 0)
    qk_spec = pl.BlockSpec((1, MLA_HEADS, tm, MLA_QK_PAD), lambda i: (i // tpb, 0, i % tpb, 0))
    return pl.pallas_call(
        _mla_proj_kernel,
        grid=(batch * tpb,),
        in_specs=[pl.BlockSpec((tm, MLA_Q_RANK), lambda i: (i, Z_QLAT // MLA_Q_RANK)),
                  pl.BlockSpec((tm, MLA_KV_RANK), lambda i: (i, Z_KVLAT // MLA_KV_RANK)),
                  pl.BlockSpec((tm, LANES), lambda i: (i, Z_KROPE // LANES)),
                  pl.BlockSpec((tm, ROPE_HALF), lambda i: (i, 0)),
                  pl.BlockSpec((tm, ROPE_HALF), lambda i: (i, 0)),
                  pl.BlockSpec((1, MLA_Q_RANK), const2),
                  pl.BlockSpec((1, MLA_KV_RANK), const2),
                  pl.BlockSpec((MLA_HEADS, MLA_Q_RANK, MLA_QK_PAD), lambda i: (0, 0, 0)),
                  pl.BlockSpec((MLA_KV_RANK, MLA_HEADS * (MLA_NOPE + MLA_V)), const2)],
        out_specs=[qk_spec, qk_spec,
                   pl.BlockSpec((1, MLA_HEADS, tm, MLA_V), lambda i: (i // tpb, 0, i % tpb, 0))],
        out_shape=[qk_shape, qk_shape, v_shape],
        compiler_params=_params(("arbitrary",), 48),
        name="mla_proj",
    )(z, z, z, cos, sin, gq, gkv, wq, wkv)


ATTN_KV_BLOCK = 1024
ATTN_Q_BLOCKS = 2
ATTN_DIAG_CHAINS = 2


def _attn_kernel(q_ref, k_ref, v_ref, *refs, rider_windows):
    nr = len(rider_windows)
    rider_src, o_ref, rider_dst = refs[:nr], refs[nr], refs[nr + 1:2 * nr + 1]
    m_sc, l_sc, acc_sc = refs[2 * nr + 1:]
    blk = ATTN_KV_BLOCK
    nt = (((1,), (1,)), ((), ()))
    qi = pl.program_id(2)
    step = ((pl.program_id(0) * pl.num_programs(1) + pl.program_id(1)) * pl.num_programs(2) + qi)
    _run_riders(step, rider_windows, rider_src, rider_dst)
    m_sc[...] = jnp.full(m_sc.shape, -jnp.inf, F32)
    l_sc[...] = jnp.zeros(l_sc.shape, F32)
    acc_sc[...] = jnp.zeros(acc_sc.shape, F32)

    def update(rows, s, v):
        m_prev = m_sc[rows]
        m_new = jnp.maximum(m_prev, jnp.max(s, axis=-1, keepdims=True))
        alpha = jnp.exp2(m_prev - m_new)
        ps = [jnp.exp2(s[:, c * LANES:(c + 1) * LANES] - m_new)
              for c in range(s.shape[1] // LANES)]
        l_sc[rows] = alpha * l_sc[rows] + functools.reduce(lambda a, b: a + b, ps)
        p = jnp.concatenate(ps, axis=1).astype(BF16)
        acc_sc[rows] = alpha * acc_sc[rows] + jnp.dot(p, v, preferred_element_type=F32)
        m_sc[rows] = m_new

    def full_block(rows, ks):
        k = k_ref[0, 0, pl.ds(ks, blk), :]
        v = v_ref[0, 0, pl.ds(ks, blk), :]
        update(rows, lax.dot_general(q_ref[0, 0, rows], k, nt, preferred_element_type=F32), v)

    def diag_block(row0, ks):
        sub = blk // ATTN_DIAG_CHAINS
        for r in range(ATTN_DIAG_CHAINS):
            n = (r + 1) * sub
            rows = slice(row0 + r * sub, row0 + n)
            k = k_ref[0, 0, pl.ds(ks, n), :]
            v = v_ref[0, 0, pl.ds(ks, n), :]
            s = lax.dot_general(q_ref[0, 0, rows], k, nt, preferred_element_type=F32)
            row = lax.broadcasted_iota(jnp.int32, s.shape, 0) + r * sub
            col = lax.broadcasted_iota(jnp.int32, s.shape, 1)
            update(rows, jnp.where(col <= row, s, -jnp.inf), v)

    def body(kb, carry):
        full_block(slice(0, ATTN_Q_BLOCKS * blk), pl.multiple_of(kb * blk, blk))
        return carry

    lax.fori_loop(0, qi * ATTN_Q_BLOCKS, body, 0)

    base = qi * ATTN_Q_BLOCKS * blk
    for a in range(ATTN_Q_BLOCKS):
        for kb in range(a):
            full_block(slice(a * blk, (a + 1) * blk), pl.multiple_of(base + kb * blk, blk))
        diag_block(a * blk, pl.multiple_of(base + a * blk, blk))
    l = jnp.sum(l_sc[...], axis=-1, keepdims=True)
    o_ref[...] = (acc_sc[...] / l).astype(o_ref.dtype)


def _attn(q, k, v, riders, batch, seq):
    blk = ATTN_Q_BLOCKS * ATTN_KV_BLOCK
    nq = seq // blk
    rider_specs, windows, steps = _rider_specs(
        riders, lambda b, h, i: (b * MLA_HEADS + h) * nq + i)
    assert steps <= batch * MLA_HEADS * nq
    return pl.pallas_call(
        functools.partial(_attn_kernel, rider_windows=windows),
        grid=(batch, MLA_HEADS, nq),
        in_specs=[pl.BlockSpec((1, 1, blk, MLA_QK_PAD), lambda b, h, i: (b, h, i, 0)),
                  pl.BlockSpec((1, 1, seq, MLA_QK_PAD), lambda b, h, i: (b, h, 0, 0)),
                  pl.BlockSpec((1, 1, seq, MLA_V), lambda b, h, i: (b, h, 0, 0))] + rider_specs,
        out_specs=[pl.BlockSpec((blk, MLA_V), lambda b, h, i: (b * nq + i, h))] + rider_specs,
        out_shape=([jax.ShapeDtypeStruct((batch * seq, MLA_HEADS * MLA_V), BF16)]
                   + [jax.ShapeDtypeStruct(w.shape, BF16) for w in riders]),
        scratch_shapes=[pltpu.VMEM((blk, LANES), F32), pltpu.VMEM((blk, LANES), F32),
                        pltpu.VMEM((blk, MLA_V), F32)],
        compiler_params=_params(("arbitrary", "arbitrary", "arbitrary"), 48),
        name="attn",
    )(q, k, v, *riders)


def _gla_constants():
    c = GLA_CHUNK
    idx = np.arange(c)
    w = np.zeros((GLA_LEVELS + 2, c, c), np.float32)
    masks = np.zeros((GLA_LEVELS + 1, c, c), np.float32)
    for l in range(GLA_LEVELS):
        half = 1 << l
        blk = idx // (2 * half)
        mid = blk * 2 * half + half - 1
        upper = (idx % (2 * half)) >= half
        t = idx[None, :]
        up_rows = (t > mid[:, None]) & (t <= idx[:, None])
        lo_rows = (t > idx[:, None]) & (t <= mid[:, None])
        w[l] = np.where(upper[:, None], up_rows, lo_rows)
        masks[l] = (blk[:, None] == blk[None, :]) & upper[:, None] & (~upper[None, :])
    w[GLA_LEVELS] = idx[None, :] <= idx[:, None]
    w[GLA_LEVELS + 1] = idx[None, :] > idx[:, None]
    masks[GLA_LEVELS] = np.eye(c)
    w = w.reshape(-1, c)
    return np.concatenate([w, w], axis=1), masks


def _split_bf16(x):
    hi = x.astype(BF16)
    return hi, (x - hi.astype(F32)).astype(BF16)


GLA_CHUNKS_PER_STEP = 2
GLA_STAGGER = 2


def _gla_kernel(q_ref, k_ref, v_ref, lr_ref, og_ref, wg_ref, bg_ref, go_ref, ws_ref, mask_ref,
                *refs, rider_windows):
    nr = len(rider_windows)
    rider_src, o_ref, rider_dst, st_ref = refs[:nr], refs[nr], refs[nr + 1:2 * nr + 1], refs[-1]
    c = GLA_CHUNK
    dk, dv = GLA_DK, GLA_DV
    nt = (((1,), (1,)), ((), ()))
    _run_riders(pl.program_id(0) * pl.num_programs(1) + pl.program_id(1), rider_windows,
                rider_src, rider_dst)

    @pl.when(pl.program_id(1) == 0)
    def _():
        st_ref[...] = jnp.zeros(st_ref.shape, F32)

    def rows(ci):
        return slice(ci * c, (ci + 1) * c)

    def qk(ci, h):
        q = q_ref[rows(ci), h * dk:(h + 1) * dk] * (dk ** -0.5)
        return q, k_ref[rows(ci), h * dk:(h + 1) * dk]

    def gate(ci, h):
        lr = lr_ref[rows(ci)]
        x = (jnp.dot(lr, wg_ref[:, h * dk:(h + 1) * dk], preferred_element_type=F32)
             + bg_ref[:, h * dk:(h + 1) * dk])
        log2_a = (-(jnp.maximum(-x, 0.0) + jnp.log1p(jnp.exp(-jnp.abs(x))))
                  * (np.log2(np.e) / GLA_TAU))
        return jnp.concatenate(_split_bf16(log2_a), axis=0)

    def exponents(log2_a):
        return jnp.dot(ws_ref[...], log2_a, preferred_element_type=F32)

    def level_operands(ci, h, pb):
        q, k = qk(ci, h)
        ops = [(q * pb[l * c:(l + 1) * c], k * pb[l * c:(l + 1) * c]) for l in range(GLA_LEVELS)]
        return ops + [(q, k)]

    def level_scores(ops):
        return [lax.dot_general(a, b, nt, preferred_element_type=F32) for a, b in ops]

    def masked_sum(scores):
        attn = mask_ref[0] * scores[0]
        for l in range(1, GLA_LEVELS + 1):
            attn = attn + mask_ref[l] * scores[l]
        return attn.astype(BF16)

    def mix(ci, h, p, pb, attn):
        q, k = qk(ci, h)
        v = v_ref[rows(ci), h * dv:(h + 1) * dv]
        eb = pb[GLA_LEVELS * c:(GLA_LEVELS + 1) * c]
        ebl = pb[(GLA_LEVELS + 1) * c:(GLA_LEVELS + 2) * c]
        st = st_ref[h]
        o = (lax.dot_general(q * eb, st.astype(BF16), nt, preferred_element_type=F32)
             + jnp.dot(attn, v, preferred_element_type=F32))
        v_t = v.astype(F32).T.astype(BF16)
        decay = p[(GLA_LEVELS + 1) * c - 1:(GLA_LEVELS + 1) * c]
        st_ref[h] = st * decay + jnp.dot(v_t, k * ebl, preferred_element_type=F32)
        return o

    def emit(ci, h, o):
        y = _rms(o, go_ref[...])
        og = og_ref[rows(ci), h * dv:(h + 1) * dv].astype(F32)
        o_ref[rows(ci), h * dv:(h + 1) * dv] = (y * (og * jax.nn.sigmoid(og))).astype(o_ref.dtype)

    heads = range(GLA_HEADS)

    def chunk_stages(ci):
        log2_a = [gate(ci, h) for h in heads]
        yield
        expo = [exponents(x) for x in log2_a]
        p = [jnp.exp2(e) for e in expo]
        pb = [x.astype(BF16) for x in p]
        yield
        operands = [level_operands(ci, h, pb[h]) for h in heads]
        scores = [level_scores(ops) for ops in operands]
        yield
        attn = [masked_sum(s) for s in scores]
        out = [mix(ci, h, p[h], pb[h], attn[h]) for h in heads]
        yield
        for h in heads:
            emit(ci, h, out[h])
        yield

    live = []
    pending = [chunk_stages(ci) for ci in range(GLA_CHUNKS_PER_STEP)]
    tick = 0
    while pending or live:
        if pending and tick % GLA_STAGGER == 0:
            live.append(pending.pop(0))
        for gen in list(live):
            if next(gen, "done") == "done":
                live.remove(gen)
        tick += 1


def _gla(z, wg, bg, go, riders, batch, seq):
    c = GLA_CHUNK * GLA_CHUNKS_PER_STEP
    nc = seq // c
    ws, masks = _gla_constants()
    hk, hv = GLA_HEADS * GLA_DK, GLA_HEADS * GLA_DV
    const2 = lambda b, i: (0, 0)
    rider_specs, windows, steps = _rider_specs(riders, lambda b, i: b * nc + i)
    assert steps <= batch * nc
    return pl.pallas_call(
        functools.partial(_gla_kernel, rider_windows=windows),
        grid=(batch, nc),
        in_specs=[pl.BlockSpec((c, hk), lambda b, i: (b * nc + i, Z_GQ // hk)),
                  pl.BlockSpec((c, hk), lambda b, i: (b * nc + i, Z_GK // hk)),
                  pl.BlockSpec((c, hv), lambda b, i: (b * nc + i, Z_GV // hv)),
                  pl.BlockSpec((c, LANES), lambda b, i: (b * nc + i, Z_GLR // LANES)),
                  pl.BlockSpec((c, hv), lambda b, i: (b * nc + i, Z_OG // hv)),
                  pl.BlockSpec((LANES, hk), const2),
                  pl.BlockSpec((1, hk), const2),
                  pl.BlockSpec((1, GLA_DV), const2),
                  pl.BlockSpec(ws.shape, const2),
                  pl.BlockSpec(masks.shape, lambda b, i: (0, 0, 0))] + rider_specs,
        out_specs=[pl.BlockSpec((c, hv), lambda b, i: (b * nc + i, 0))] + rider_specs,
        out_shape=([jax.ShapeDtypeStruct((batch * seq, hv), BF16)]
                   + [jax.ShapeDtypeStruct(w.shape, BF16) for w in riders]),
        scratch_shapes=[pltpu.VMEM((GLA_HEADS, GLA_DV, GLA_DK), F32)],
        compiler_params=_params(("arbitrary", "arbitrary"), 48),
        name="gla",
    )(z, z, z, z, z, wg, bg, go, jnp.asarray(ws, BF16), jnp.asarray(masks), *riders)


def _mix1_kernel(oa_ref, ob_ref, wa_ref, wb_ref, ga_ref, gb_ref, o_ref, wa_bf, wb_bf):
    @pl.when(pl.program_id(1) == 0)
    def _():
        wa_bf[...] = wa_ref[...].astype(BF16)
        wb_bf[...] = wb_ref[...].astype(BF16)

    for r0 in range(0, o_ref.shape[0], MIX1_SUBTILE):
        rows = slice(r0, r0 + MIX1_SUBTILE)
        ya = jnp.dot(oa_ref[rows], wa_bf[...], preferred_element_type=F32)
        yb = jnp.dot(ob_ref[rows], wb_bf[...], preferred_element_type=F32)
        o_ref[rows] = (jax.nn.sigmoid(ga_ref[rows].astype(F32)) * ya
                       + jax.nn.sigmoid(gb_ref[rows].astype(F32)) * yb).astype(o_ref.dtype)


MIX1_SUBTILE = 128


def _mix1(oa, ob, wa, wb, z):
    t = oa.shape[0]
    tm, tn = 512, 1024
    return pl.pallas_call(
        _mix1_kernel,
        grid=(D_MODEL // tn, t // tm),
        in_specs=[pl.BlockSpec((tm, oa.shape[1]), lambda j, i: (i, 0)),
                  pl.BlockSpec((tm, ob.shape[1]), lambda j, i: (i, 0)),
                  pl.BlockSpec((wa.shape[0], tn), lambda j, i: (0, j)),
                  pl.BlockSpec((wb.shape[0], tn), lambda j, i: (0, j)),
                  pl.BlockSpec((tm, tn), lambda j, i: (i, Z_GATE_A // tn + j)),
                  pl.BlockSpec((tm, tn), lambda j, i: (i, Z_GATE_B // tn + j))],
        out_specs=pl.BlockSpec((tm, tn), lambda j, i: (i, j)),
        out_shape=jax.ShapeDtypeStruct((t, D_MODEL), BF16),
        scratch_shapes=[pltpu.VMEM((wa.shape[0], tn), BF16), pltpu.VMEM((wb.shape[0], tn), BF16)],
        compiler_params=_params(("arbitrary", "arbitrary"), 48),
        name="mix1",
    )(oa, ob, wa, wb, z, z)


def _mix2_kernel(m_ref, w_ref, x_ref, gm_ref, gpost_ref, gpre_ref, sc_ref, sh_ref,
                 x1_ref, h2_ref):
    post = gm_ref[0] * gpost_ref[...]
    pre = gpre_ref[...] * (1.0 + sc_ref[0])
    for r0 in range(0, m_ref.shape[0], MIX2_SUBTILE):
        rows = slice(r0, r0 + MIX2_SUBTILE)
        y = jnp.dot(m_ref[rows], w_ref[...], preferred_element_type=F32)
        x1 = x_ref[rows] + _rms(y, post)
        x1_ref[rows] = x1
        h2_ref[rows] = (_rms(x1, pre) + sh_ref[0]).astype(h2_ref.dtype)


MIX2_SUBTILE = 128


def _mix2(merged, w, x2, gate_m, g_post, g_pre, scale_f, shift_f, seq):
    t = merged.shape[0]
    tm = 512
    tpb = seq // tm
    row = pl.BlockSpec((tm, D_MODEL), lambda i: (i, 0))
    vec = pl.BlockSpec((1, D_MODEL), lambda i: (0, 0))
    per_batch = pl.BlockSpec((1, 1, D_MODEL), lambda i: (i // tpb, 0, 0))
    return pl.pallas_call(
        _mix2_kernel,
        grid=(t // tm,),
        in_specs=[row, pl.BlockSpec((D_MODEL, D_MODEL), lambda i: (0, 0),
                                    pipeline_mode=pl.Buffered(1)), row,
                  per_batch, vec, vec, per_batch, per_batch],
        out_specs=[row, row],
        out_shape=[jax.ShapeDtypeStruct((t, D_MODEL), F32),
                   jax.ShapeDtypeStruct((t, D_MODEL), BF16)],
        compiler_params=_params(("arbitrary",), 48),
        name="mix2",
    )(merged, w, x2, gate_m, g_post, g_pre, scale_f, shift_f)


MIX_SUBTILE = 128


def _mix_kernel(oa_ref, ob_ref, ga_ref, gb_ref, wa_ref, wb_ref, wm_ref, x_ref, gm_ref, gpost_ref,
                gpre_ref, sc_ref, sh_ref, x1_ref, h2_ref):
    post = gm_ref[0] * gpost_ref[...]
    pre = gpre_ref[...] * (1.0 + sc_ref[0])

    def sub_tile(rows):
        ya = jnp.dot(oa_ref[rows], wa_ref[...], preferred_element_type=F32)
        yb = jnp.dot(ob_ref[rows], wb_ref[...], preferred_element_type=F32)
        yield
        merged = (jax.nn.sigmoid(ga_ref[rows].astype(F32)) * ya
                  + jax.nn.sigmoid(gb_ref[rows].astype(F32)) * yb).astype(BF16)
        yield
        y = jnp.dot(merged, wm_ref[...], preferred_element_type=F32)
        yield
        x1 = x_ref[rows] + _rms(y, post)
        x1_ref[rows] = x1
        h2_ref[rows] = (_rms(x1, pre) + sh_ref[0]).astype(h2_ref.dtype)
        yield

    pending = [sub_tile(slice(r0, r0 + MIX_SUBTILE))
               for r0 in range(0, x_ref.shape[0], MIX_SUBTILE)]
    live = []
    while pending or live:
        if pending:
            live.append(pending.pop(0))
        for gen in list(live):
            if next(gen, "done") == "done":
                live.remove(gen)


def _mix(oa, ob, z, wa, wb, wm, x2, gate_m, g_post, g_pre, scale_f, shift_f, seq):
    t = oa.shape[0]
    tm = 256
    tpb = seq // tm
    row = pl.BlockSpec((tm, D_MODEL), lambda i: (i, 0))
    vec = pl.BlockSpec((1, D_MODEL), lambda i: (0, 0))
    per_batch = pl.BlockSpec((1, 1, D_MODEL), lambda i: (i // tpb, 0, 0))
    resident = lambda w: pl.BlockSpec(w.shape, lambda i: (0, 0), pipeline_mode=pl.Buffered(1))
    return pl.pallas_call(
        _mix_kernel,
        grid=(t // tm,),
        in_specs=[pl.BlockSpec((tm, oa.shape[1]), lambda i: (i, 0)),
                  pl.BlockSpec((tm, ob.shape[1]), lambda i: (i, 0)),
                  pl.BlockSpec((tm, D_MODEL), lambda i: (i, Z_GATE_A // D_MODEL)),
                  pl.BlockSpec((tm, D_MODEL), lambda i: (i, Z_GATE_B // D_MODEL)),
                  resident(wa), resident(wb), resident(wm),
                  row, per_batch, vec, vec, per_batch, per_batch],
        out_specs=[row, row],
        out_shape=[jax.ShapeDtypeStruct((t, D_MODEL), F32),
                   jax.ShapeDtypeStruct((t, D_MODEL), BF16)],
        compiler_params=_params(("arbitrary",), 56),
        name="mix",
    )(oa, ob, z, z, wa, wb, wm, x2, gate_m, g_post, g_pre, scale_f, shift_f)


def _ffn_up_kernel(h_ref, wa_ref, wv_ref, cwa_ref, cwv_ref, cba_ref, cbv_ref, o_ref,
                   w_bf, u_sc, *, tiles_per_batch):
    i = pl.program_id(1)
    tm, tn = o_ref.shape
    head, body, tail = slice(0, SUBLANES), slice(SUBLANES, SUBLANES + tm), slice(tm, tm + SUBLANES)

    @pl.when(i == 0)
    def _():
        w_bf[:, 0:tn] = wa_ref[...].astype(BF16)
        w_bf[:, tn:2 * tn] = wv_ref[...].astype(BF16)

    @pl.when(i % tiles_per_batch == 0)
    def _():
        u_sc[head] = jnp.zeros((SUBLANES, u_sc.shape[1]), F32)

    @pl.when(i % tiles_per_batch != 0)
    def _():
        u_sc[head] = u_sc[tail]

    u_sc[body] = jnp.dot(h_ref[...], w_bf[...], preferred_element_type=F32)

    def conv(cols, cw_ref, cb_ref):
        cw = cw_ref[...]
        acc = cb_ref[...] + u_sc[body, cols] * cw[CONV_WIDTH - 1:CONV_WIDTH]
        for tap in range(1, CONV_WIDTH):
            acc = acc + (u_sc[SUBLANES - tap:SUBLANES - tap + tm, cols]
                         * cw[CONV_WIDTH - 1 - tap:CONV_WIDTH - tap])
        return acc

    a = conv(slice(0, tn), cwa_ref, cba_ref)
    val = conv(slice(tn, 2 * tn), cwv_ref, cbv_ref)
    o_ref[...] = (jax.nn.gelu(a, approximate=True) * val).astype(o_ref.dtype)


def _ffn_up(h2, w_up, conv_w, conv_b, seq):
    t = h2.shape[0]
    tm, tn = 1024, 512
    nj = D_FF // tn
    return pl.pallas_call(
        functools.partial(_ffn_up_kernel, tiles_per_batch=seq // tm),
        grid=(nj, t // tm),
        in_specs=[pl.BlockSpec((tm, D_MODEL), lambda j, i: (i, 0)),
                  pl.BlockSpec((D_MODEL, tn), lambda j, i: (0, j)),
                  pl.BlockSpec((D_MODEL, tn), lambda j, i: (0, nj + j)),
                  pl.BlockSpec((CONV_WIDTH, tn), lambda j, i: (0, j)),
                  pl.BlockSpec((CONV_WIDTH, tn), lambda j, i: (0, nj + j)),
                  pl.BlockSpec((1, tn), lambda j, i: (0, j)),
                  pl.BlockSpec((1, tn), lambda j, i: (0, nj + j))],
        out_specs=pl.BlockSpec((tm, tn), lambda j, i: (i, j)),
        out_shape=jax.ShapeDtypeStruct((t, D_FF), BF16),
        scratch_shapes=[pltpu.VMEM((D_MODEL, 2 * tn), BF16),
                        pltpu.VMEM((tm + SUBLANES, 2 * tn), F32)],
        compiler_params=_params(("arbitrary", "arbitrary"), 56),
        name="ffn_up",
    )(h2, w_up, w_up, conv_w, conv_w, conv_b, conv_b)


def _ffn_down_kernel(g_ref, w_ref, x1_ref, gf_ref, gpost_ref, o_ref):
    f = jnp.dot(g_ref[...], w_ref[...], preferred_element_type=F32)
    o_ref[...] = x1_ref[...] + _rms(f, gf_ref[0] * gpost_ref[...])


def _ffn_down(g, w, x1, gate_f, g_post, seq):
    t = g.shape[0]
    tm = 256
    tpb = seq // tm
    return pl.pallas_call(
        _ffn_down_kernel,
        grid=(t // tm,),
        in_specs=[pl.BlockSpec((tm, D_FF), lambda i: (i, 0)),
                  pl.BlockSpec((D_FF, D_MODEL), lambda i: (0, 0), pipeline_mode=pl.Buffered(1)),
                  pl.BlockSpec((tm, D_MODEL), lambda i: (i, 0)),
                  pl.BlockSpec((1, 1, D_MODEL), lambda i: (i // tpb, 0, 0)),
                  pl.BlockSpec((1, D_MODEL), lambda i: (0, 0))],
        out_specs=pl.BlockSpec((tm, D_MODEL), lambda i: (i, 0)),
        out_shape=jax.ShapeDtypeStruct((t, D_MODEL), F32),
        compiler_params=_params(("arbitrary",), 56),
        name="ffn_down",
    )(g, w, x1, gate_f, g_post)


def _swap_halves(w):
    half = w.shape[-1] // 2
    return jnp.concatenate([w[..., half:], w[..., :half]], axis=-1)


def _w_in_special(w_t):
    half = MLA_ROPE // 2
    k_rope = w_t[_SRC_KROPE:_SRC_KROPE + MLA_ROPE]
    g_lr = w_t[_SRC_GLR:_SRC_GLR + GLA_GATE_RANK]
    pad = jnp.zeros((Z_GQ - Z_GLR - 2 * GLA_GATE_RANK, w_t.shape[1]), w_t.dtype)
    return jnp.concatenate([k_rope, k_rope[half:], k_rope[:half], g_lr, g_lr, pad], axis=0)


def _prep_w_q(w_q_up):
    w = w_q_up.reshape(MLA_Q_RANK, MLA_HEADS, MLA_NOPE + MLA_ROPE)
    pe = w[:, :, MLA_NOPE:]
    w = jnp.concatenate([w, _swap_halves(pe)], axis=-1)
    return jnp.transpose(w, (1, 0, 2)).astype(BF16)


def kernel(x, c, positions, w_ada, b_ada, g_pre_mix, w_in, g_q_lat, w_q_up, g_kv_lat, w_kv_up,
           w_gla_gate_up, b_gla_gate, g_gla_out, w_branch_a, w_branch_b, w_mix_out, g_post_mix,
           g_pre_ffn, w_ffn_up, conv_w, conv_b, w_ffn_down, g_post_ffn):
    batch, seq, _ = x.shape
    depth = w_ada.shape[0]
    t = batch * seq
    row = lambda v: v.reshape(1, -1)

    inv = 1.0 / (ROPE_THETA ** (jnp.arange(0, MLA_ROPE, 2, dtype=F32) / MLA_ROPE))
    cos, sin = _rope_tables(positions, inv)

    c_pad = jnp.zeros((SUBLANES, D_MODEL), F32).at[:batch].set(c)
    x2 = x.reshape(t, D_MODEL)
    for l in range(depth):
        mod = _ada(c_pad, w_ada[l], row(b_ada[l]))[:batch]
        shift_m, scale_m, gate_m, shift_f, scale_f, gate_f = (
            m.reshape(batch, 1, D_MODEL) for m in jnp.split(mod, 6, axis=-1))

        h = _norm_mod(x2, row(g_pre_mix[l]), scale_m, shift_m, seq)
        w_in_t = w_in[l].T
        z = _mm_in(h, w_in_t, _w_in_special(w_in_t))

        q, k, v = _mla_proj(z, cos, sin, row(g_q_lat[l]), row(g_kv_lat[l]),
                            _prep_w_q(w_q_up[l]), w_kv_up[l].astype(BF16), batch, seq)
        o_a, w_a_bf, w_b_bf = _attn(q, k, v, [w_branch_a[l], w_branch_b[l]], batch, seq)

        wg_hi, wg_lo = _split_bf16(w_gla_gate_up[l])
        wg = jnp.zeros((LANES, GLA_HEADS * GLA_DK), BF16)
        wg = wg.at[:GLA_GATE_RANK].set(wg_hi).at[GLA_GATE_RANK:2 * GLA_GATE_RANK].set(wg_lo)
        o_b, w_down_bf, w_mix_bf = _gla(z, wg, row(b_gla_gate[l]), row(g_gla_out[l]),
                                        [w_ffn_down[l], w_mix_out[l]], batch, seq)

        x1, h2 = _mix(o_a, o_b, z, w_a_bf, w_b_bf, w_mix_bf, x2, gate_m, row(g_post_mix[l]),
                      row(g_pre_ffn[l]), scale_f, shift_f, seq)

        g = _ffn_up(h2, w_ffn_up[l], conv_w[l], row(conv_b[l]), seq)
        x2 = _ffn_down(g, w_down_bf, x1, gate_f, row(g_post_ffn[l]), seq)
    return x2.reshape(batch, seq, D_MODEL)
```

```python
import functools

import jax
import jax.numpy as jnp
import numpy as np
from jax import lax
from jax.experimental import pallas as pl
from jax.experimental.pallas import tpu as pltpu

F32 = jnp.float32
BF16 = jnp.bfloat16

D_MODEL = 2048
MLA_HEADS = 8
MLA_Q_RANK = 512
MLA_KV_RANK = 256
MLA_NOPE = 128
MLA_ROPE = 64
MLA_V = 128
ROPE_THETA = 10000.0
GLA_HEADS = 4
GLA_DK = 256
GLA_DV = 512
GLA_GATE_RANK = 16
GLA_TAU = 16.0
D_FF = 5632
CONV_WIDTH = 3
EPS = 1e-6

LANES = 128
SUBLANES = 8
MLA_QK_PAD = 256

Z_QLAT = 0
Z_KVLAT = 512
Z_KROPE = 768
Z_GLR = 896
Z_GQ = 1024
Z_GV = 2048
Z_OG = 4096
Z_GATE_A = 6144
Z_GATE_B = 8192
Z_GK = 10240
Z_WIDTH = 11264

_SRC_KROPE = MLA_Q_RANK + MLA_KV_RANK
_SRC_GQ = _SRC_KROPE + MLA_ROPE
_SRC_GK = _SRC_GQ + GLA_HEADS * GLA_DK
_SRC_GV = _SRC_GK + GLA_HEADS * GLA_DK
_SRC_GLR = _SRC_GV + GLA_HEADS * GLA_DV
_SRC_OG = _SRC_GLR + GLA_GATE_RANK
_SRC_GATE_A = _SRC_OG + GLA_HEADS * GLA_DV
_SRC_GATE_B = _SRC_GATE_A + D_MODEL

GLA_CHUNK = 128
GLA_LEVELS = 7

MIB = 1024 * 1024


def _params(semantics, vmem_mib):
    return pltpu.CompilerParams(dimension_semantics=semantics,
                                vmem_limit_bytes=vmem_mib * MIB)


def _rms(x, g):
    return x * lax.rsqrt(jnp.mean(x * x, axis=-1, keepdims=True) + EPS) * g


RIDER_ROWS = 256


def _rider_specs(arrays, step_of):
    specs, windows, start = [], [], 0
    for a in arrays:
        n = a.shape[0] // RIDER_ROWS
        specs.append(pl.BlockSpec(
            (RIDER_ROWS, a.shape[1]),
            lambda *idx, lo=start, n=n: (jnp.clip(step_of(*idx) - lo, 0, n - 1), 0)))
        windows.append((start, start + n))
        start += n
    return specs, windows, start


def _run_riders(step, windows, src_refs, dst_refs):
    for (lo, hi), src, dst in zip(windows, src_refs, dst_refs):
        @pl.when((step >= lo) & (step < hi))
        def _(src=src, dst=dst):
            dst[...] = src[...].astype(BF16)


def _ada_kernel(c_ref, w_ref, b_ref, o_ref):
    c = c_ref[...]
    ca = c * jax.nn.sigmoid(c)
    o_ref[...] = jnp.dot(ca.astype(BF16), w_ref[...].astype(BF16),
                         preferred_element_type=F32) + b_ref[...]


def _ada(c_pad, w, b):
    n = w.shape[1]
    tn = 1024
    return pl.pallas_call(
        _ada_kernel,
        grid=(n // tn,),
        in_specs=[pl.BlockSpec((SUBLANES, D_MODEL), lambda j: (0, 0)),
                  pl.BlockSpec((D_MODEL, tn), lambda j: (0, j)),
                  pl.BlockSpec((1, tn), lambda j: (0, j))],
        out_specs=pl.BlockSpec((SUBLANES, tn), lambda j: (0, j)),
        out_shape=jax.ShapeDtypeStruct((SUBLANES, n), F32),
        compiler_params=_params(("arbitrary",), 32),
        name="ada",
    )(c_pad, w, b)


def _norm_mod_kernel(x_ref, g_ref, sc_ref, sh_ref, o_ref):
    y = _rms(x_ref[...], g_ref[...] * (1.0 + sc_ref[0]))
    o_ref[...] = (y + sh_ref[0]).astype(o_ref.dtype)


def _norm_mod(x2, g, scale, shift, seq):
    t = x2.shape[0]
    tm = 512
    tpb = seq // tm
    return pl.pallas_call(
        _norm_mod_kernel,
        grid=(t // tm,),
        in_specs=[pl.BlockSpec((tm, D_MODEL), lambda i: (i, 0)),
                  pl.BlockSpec((1, D_MODEL), lambda i: (0, 0)),
                  pl.BlockSpec((1, 1, D_MODEL), lambda i: (i // tpb, 0, 0)),
                  pl.BlockSpec((1, 1, D_MODEL), lambda i: (i // tpb, 0, 0))],
        out_specs=pl.BlockSpec((tm, D_MODEL), lambda i: (i, 0)),
        out_shape=jax.ShapeDtypeStruct((t, D_MODEL), BF16),
        compiler_params=_params(("arbitrary",), 32),
        name="norm_mod",
    )(x2, g, scale, shift)


MM_IN_TN = 1024
MM_IN_SRC_ALIGN = 16
_MM_IN_SRC_ROWS = (0,
                   _SRC_GQ,
                   _SRC_GV, _SRC_GV + MM_IN_TN,
                   _SRC_OG, _SRC_OG + MM_IN_TN,
                   _SRC_GATE_A, _SRC_GATE_A + MM_IN_TN,
                   _SRC_GATE_B, _SRC_GATE_B + MM_IN_TN,
                   _SRC_GK)


def _mm_in_kernel(src_ref, h_ref, w_ref, wsp_ref, o_ref, w_bf):
    del src_ref
    j, i = pl.program_id(0), pl.program_id(1)

    @pl.when(i == 0)
    def _():
        w_bf[...] = w_ref[...].astype(BF16)

    @pl.when((i == 0) & (j == 0))
    def _():
        w_bf[Z_KROPE:Z_GQ] = wsp_ref[...].astype(BF16)

    for r0 in range(0, o_ref.shape[0], MM_IN_SUBTILE):
        rows = slice(r0, r0 + MM_IN_SUBTILE)
        o_ref[rows] = lax.dot_general(h_ref[rows], w_bf[...], (((1,), (1,)), ((), ())),
                                      preferred_element_type=F32).astype(o_ref.dtype)


MM_IN_SUBTILE = 1024


def _mm_in(h, w_t, w_special):
    t = h.shape[0]
    tm, tn = 2048, MM_IN_TN
    grid_spec = pltpu.PrefetchScalarGridSpec(
        num_scalar_prefetch=1,
        grid=(Z_WIDTH // tn, t // tm),
        in_specs=[pl.BlockSpec((tm, D_MODEL), lambda j, i, src: (i, 0)),
                  pl.BlockSpec((pl.Element(tn), pl.Element(D_MODEL)),
                               lambda j, i, src: (src[j] * MM_IN_SRC_ALIGN, 0)),
                  pl.BlockSpec((Z_GQ - Z_KROPE, D_MODEL), lambda j, i, src: (0, 0))],
        out_specs=pl.BlockSpec((tm, tn), lambda j, i, src: (i, j)),
        scratch_shapes=[pltpu.VMEM((tn, D_MODEL), BF16)])
    return pl.pallas_call(
        _mm_in_kernel,
        grid_spec=grid_spec,
        out_shape=jax.ShapeDtypeStruct((t, Z_WIDTH), BF16),
        compiler_params=_params(("arbitrary", "arbitrary"), 56),
        name="mm_in",
    )(jnp.asarray([r // MM_IN_SRC_ALIGN for r in _MM_IN_SRC_ROWS], jnp.int32), h, w_t, w_special)


ROPE_HALF = MLA_ROPE // 2
ROPE_PACK = LANES // ROPE_HALF


def _rope_kernel(pos_ref, inv_ref, cos_ref, sin_ref):
    ang = pos_ref[...].astype(F32) * inv_ref[...]
    cos_ref[...] = jnp.cos(ang)
    sin_ref[...] = jnp.sin(ang)


def _rope_tables(positions, inv):
    t = positions.size
    rows = t // ROPE_PACK
    pos_dense = jnp.repeat(positions.reshape(t), ROPE_HALF).reshape(rows, LANES)
    inv_dense = jnp.tile(inv, ROPE_PACK).reshape(1, LANES)
    tr = 512
    spec = pl.BlockSpec((tr, LANES), lambda i: (i, 0))
    cos, sin = pl.pallas_call(
        _rope_kernel,
        grid=(rows // tr,),
        in_specs=[spec, pl.BlockSpec((1, LANES), lambda i: (0, 0))],
        out_specs=[spec, spec],
        out_shape=[jax.ShapeDtypeStruct((rows, LANES), F32)] * 2,
        compiler_params=_params(("arbitrary",), 32),
        name="rope_tables",
    )(pos_dense, inv_dense)
    return cos.reshape(t, ROPE_HALF), sin.reshape(t, ROPE_HALF)


def _mla_proj_kernel(ql_ref, kvl_ref, kr_ref, cos_ref, sin_ref,
                     gq_ref, gkv_ref, wq_ref, wkv_ref, q_ref, k_ref, v_ref):
    c, s = cos_ref[...], sin_ref[...]
    zeros = jnp.zeros((c.shape[0], LANES - MLA_ROPE), F32)
    cosm = jnp.concatenate([c, c, zeros], axis=1)
    sinm = jnp.concatenate([-s, s, zeros], axis=1)

    def rope(r):
        return r * cosm + pltpu.roll(r, MLA_ROPE, 1) * sinm

    scale = (MLA_NOPE + MLA_ROPE) ** -0.5 * np.log2(np.e)
    qn =_rms(ql_ref[...].astype(F32), gq_ref[...]).astype(BF16)
    for h in range(MLA_HEADS):
        r = jnp.dot(qn, wq_ref[h], preferred_element_type=F32) * scale
        q_ref[0, h, :, 0:MLA_NOPE] = r[:, 0:MLA_NOPE].astype(BF16)
        q_ref[0, h, :, MLA_NOPE:MLA_QK_PAD] = rope(r[:, MLA_NOPE:MLA_QK_PAD]).astype(BF16)

    kvn = _rms(kvl_ref[...].astype(F32), gkv_ref[...]).astype(BF16)
    kv = jnp.dot(kvn, wkv_ref[...], preferred_element_type=F32).astype(BF16)
    kpe = rope(kr_ref[...].astype(F32)).astype(BF16)
    hw = MLA_NOPE + MLA_V
    for h in range(MLA_HEADS):
        k_ref[0, h, :, 0:MLA_NOPE] = kv[:, h * hw:h * hw + MLA_NOPE]
        k_ref[0, h, :, MLA_NOPE:MLA_QK_PAD] = kpe
        v_ref[0, h] = kv[:, h * hw + MLA_NOPE:(h + 1) * hw]


def _mla_proj(z, cos, sin, gq, gkv, wq, wkv, batch, seq):
    tm = 512
    tpb = seq // tm
    qk_shape = jax.ShapeDtypeStruct((batch, MLA_HEADS, seq, MLA_QK_PAD), BF16)
    v_shape = jax.ShapeDtypeStruct((batch, MLA_HEADS, seq, MLA_V), BF16)
    const2 = lambda i: (0, 0)
    qk_spec = pl.BlockSpec((1, MLA_HEADS, tm, MLA_QK_PAD), lambda i: (i // tpb, 0, i % tpb, 0))
    return pl.pallas_call(
        _mla_proj_kernel,
        grid=(batch * tpb,),
        in_specs=[pl.BlockSpec((tm, MLA_Q_RANK), lambda i: (i, Z_QLAT // MLA_Q_RANK)),
                  pl.BlockSpec((tm, MLA_KV_RANK), lambda i: (i, Z_KVLAT // MLA_KV_RANK)),
                  pl.BlockSpec((tm, LANES), lambda i: (i, Z_KROPE // LANES)),
                  pl.BlockSpec((tm, ROPE_HALF), lambda i: (i, 0)),
                  pl.BlockSpec((tm, ROPE_HALF), lambda i: (i, 0)),
                  pl.BlockSpec((1, MLA_Q_RANK), const2),
                  pl.BlockSpec((1, MLA_KV_RANK), const2),
                  pl.BlockSpec((MLA_HEADS, MLA_Q_RANK, MLA_QK_PAD), lambda i: (0, 0, 0)),
                  pl.BlockSpec((MLA_KV_RANK, MLA_HEADS * (MLA_NOPE + MLA_V)), const2)],
        out_specs=[qk_spec, qk_spec,
                   pl.BlockSpec((1, MLA_HEADS, tm, MLA_V), lambda i: (i // tpb, 0, i % tpb, 0))],
        out_shape=[qk_shape, qk_shape, v_shape],
        compiler_params=_params(("arbitrary",), 48),
        name="mla_proj",
    )(z, z, z, cos, sin, gq, gkv, wq, wkv)


ATTN_KV_BLOCK = 1024
ATTN_Q_BLOCKS = 2
ATTN_DIAG_CHAINS = 2


def _attn_kernel(q_ref, k_ref, v_ref, *refs, rider_windows):
    nr = len(rider_windows)
    rider_src, o_ref, rider_dst = refs[:nr], refs[nr], refs[nr + 1:2 * nr + 1]
    m_sc, l_sc, acc_sc = refs[2 * nr + 1:]
    blk = ATTN_KV_BLOCK
    nt = (((1,), (1,)), ((), ()))
    qi = pl.program_id(2)
    step = ((pl.program_id(0) * pl.num_programs(1) + pl.program_id(1)) * pl.num_programs(2) + qi)
    _run_riders(step, rider_windows, rider_src, rider_dst)
    m_sc[...] = jnp.full(m_sc.shape, -jnp.inf, F32)
    l_sc[...] = jnp.zeros(l_sc.shape, F32)
    acc_sc[...] = jnp.zeros(acc_sc.shape, F32)

    def update(rows, s, v):
        m_prev = m_sc[rows]
        m_new = jnp.maximum(m_prev, jnp.max(s, axis=-1, keepdims=True))
        alpha = jnp.exp2(m_prev - m_new)
        ps = [jnp.exp2(s[:, c * LANES:(c + 1) * LANES] - m_new)
              for c in range(s.shape[1] // LANES)]
        l_sc[rows] = alpha * l_sc[rows] + functools.reduce(lambda a, b: a + b, ps)
        p = jnp.concatenate(ps, axis=1).astype(BF16)
        acc_sc[rows] = alpha * acc_sc[rows] + jnp.dot(p, v, preferred_element_type=F32)
        m_sc[rows] = m_new

    def full_block(rows, ks):
        k = k_ref[0, 0, pl.ds(ks, blk), :]
        v = v_ref[0, 0, pl.ds(ks, blk), :]
        update(rows, lax.dot_general(q_ref[0, 0, rows], k, nt, preferred_element_type=F32), v)

    def diag_block(row0, ks):
        sub = blk // ATTN_DIAG_CHAINS
        for r in range(ATTN_DIAG_CHAINS):
            n = (r + 1) * sub
            rows = slice(row0 + r * sub, row0 + n)
            k = k_ref[0, 0, pl.ds(ks, n), :]
            v = v_ref[0, 0, pl.ds(ks, n), :]
            s = lax.dot_general(q_ref[0, 0, rows], k, nt, preferred_element_type=F32)
            row = lax.broadcasted_iota(jnp.int32, s.shape, 0) + r * sub
            col = lax.broadcasted_iota(jnp.int32, s.shape, 1)
            update(rows, jnp.where(col <= row, s, -jnp.inf), v)

    def body(kb, carry):
        full_block(slice(0, ATTN_Q_BLOCKS * blk), pl.multiple_of(kb * blk, blk))
        return carry

    lax.fori_loop(0, qi * ATTN_Q_BLOCKS, body, 0)

    base = qi * ATTN_Q_BLOCKS * blk
    for a in range(ATTN_Q_BLOCKS):
        for kb in range(a):
            full_block(slice(a * blk, (a + 1) * blk), pl.multiple_of(base + kb * blk, blk))
        diag_block(a * blk, pl.multiple_of(base + a * blk, blk))
    l = jnp.sum(l_sc[...], axis=-1, keepdims=True)
    o_ref[...] = (acc_sc[...] / l).astype(o_ref.dtype)


def _attn(q, k, v, riders, batch, seq):
    blk = ATTN_Q_BLOCKS * ATTN_KV_BLOCK
    nq = seq // blk
    rider_specs, windows, steps = _rider_specs(
        riders, lambda b, h, i: (b * MLA_HEADS + h) * nq + i)
    assert steps <= batch * MLA_HEADS * nq
    return pl.pallas_call(
        functools.partial(_attn_kernel, rider_windows=windows),
        grid=(batch, MLA_HEADS, nq),
        in_specs=[pl.BlockSpec((1, 1, blk, MLA_QK_PAD), lambda b, h, i: (b, h, i, 0)),
                  pl.BlockSpec((1, 1, seq, MLA_QK_PAD), lambda b, h, i: (b, h, 0, 0)),
                  pl.BlockSpec((1, 1, seq, MLA_V), lambda b, h, i: (b, h, 0, 0))] + rider_specs,
        out_specs=[pl.BlockSpec((blk, MLA_V), lambda b, h, i: (b * nq + i, h))] + rider_specs,
        out_shape=([jax.ShapeDtypeStruct((batch * seq, MLA_HEADS * MLA_V), BF16)]
                   + [jax.ShapeDtypeStruct(w.shape, BF16) for w in riders]),
        scratch_shapes=[pltpu.VMEM((blk, LANES), F32), pltpu.VMEM((blk, LANES), F32),
                        pltpu.VMEM((blk, MLA_V), F32)],
        compiler_params=_params(("arbitrary", "arbitrary", "arbitrary"), 48),
        name="attn",
    )(q, k, v, *riders)


def _gla_constants():
    c = GLA_CHUNK
    idx = np.arange(c)
    w = np.zeros((GLA_LEVELS + 2, c, c), np.float32)
    masks = np.zeros((GLA_LEVELS + 1, c, c), np.float32)
    for l in range(GLA_LEVELS):
        half = 1 << l
        blk = idx // (2 * half)
        mid = blk * 2 * half + half - 1
        upper = (idx % (2 * half)) >= half
        t = idx[None, :]
        up_rows = (t > mid[:, None]) & (t <= idx[:, None])
        lo_rows = (t > idx[:, None]) & (t <= mid[:, None])
        w[l] = np.where(upper[:, None], up_rows, lo_rows)
        masks[l] = (blk[:, None] == blk[None, :]) & upper[:, None] & (~upper[None, :])
    w[GLA_LEVELS] = idx[None, :] <= idx[:, None]
    w[GLA_LEVELS + 1] = idx[None, :] > idx[:, None]
    masks[GLA_LEVELS] = np.eye(c)
    w = w.reshape(-1, c)
    return np.concatenate([w, w], axis=1), masks


def _split_bf16(x):
    hi = x.astype(BF16)
    return hi, (x - hi.astype(F32)).astype(BF16)


GLA_CHUNKS_PER_STEP = 2
GLA_STAGGER = 2


def _gla_kernel(q_ref, k_ref, v_ref, lr_ref, og_ref, wg_ref, bg_ref, go_ref, ws_ref, mask_ref,
                *refs, rider_windows):
    nr = len(rider_windows)
    rider_src, o_ref, rider_dst, st_ref = refs[:nr], refs[nr], refs[nr + 1:2 * nr + 1], refs[-1]
    c = GLA_CHUNK
    dk, dv = GLA_DK, GLA_DV
    nt = (((1,), (1,)), ((), ()))
    _run_riders(pl.program_id(0) * pl.num_programs(1) + pl.program_id(1), rider_windows,
                rider_src, rider_dst)

    @pl.when(pl.program_id(1) == 0)
    def _():
        st_ref[...] = jnp.zeros(st_ref.shape, F32)

    def rows(ci):
        return slice(ci * c, (ci + 1) * c)

    def qk(ci, h):
        q = q_ref[rows(ci), h * dk:(h + 1) * dk] * (dk ** -0.5)
        return q, k_ref[rows(ci), h * dk:(h + 1) * dk]

    def gate(ci, h):
        lr = lr_ref[rows(ci)]
        x = (jnp.dot(lr, wg_ref[:, h * dk:(h + 1) * dk], preferred_element_type=F32)
             + bg_ref[:, h * dk:(h + 1) * dk])
        log2_a = (-(jnp.maximum(-x, 0.0) + jnp.log1p(jnp.exp(-jnp.abs(x))))
                  * (np.log2(np.e) / GLA_TAU))
        return jnp.concatenate(_split_bf16(log2_a), axis=0)

    def exponents(log2_a):
        return jnp.dot(ws_ref[...], log2_a, preferred_element_type=F32)

    def level_operands(ci, h, pb):
        q, k = qk(ci, h)
        ops = [(q * pb[l * c:(l + 1) * c], k * pb[l * c:(l + 1) * c]) for l in range(GLA_LEVELS)]
        return ops + [(q, k)]

    def level_scores(ops):
        return [lax.dot_general(a, b, nt, preferred_element_type=F32) for a, b in ops]

    def masked_sum(scores):
        attn = mask_ref[0] * scores[0]
        for l in range(1, GLA_LEVELS + 1):
            attn = attn + mask_ref[l] * scores[l]
        return attn.astype(BF16)

    def mix(ci, h, p, pb, attn):
        q, k = qk(ci, h)
        v = v_ref[rows(ci), h * dv:(h + 1) * dv]
        eb = pb[GLA_LEVELS * c:(GLA_LEVELS + 1) * c]
        ebl = pb[(GLA_LEVELS + 1) * c:(GLA_LEVELS + 2) * c]
        st = st_ref[h]
        o = (lax.dot_general(q * eb, st.astype(BF16), nt, preferred_element_type=F32)
             + jnp.dot(attn, v, preferred_element_type=F32))
        v_t = v.astype(F32).T.astype(BF16)
        decay = p[(GLA_LEVELS + 1) * c - 1:(GLA_LEVELS + 1) * c]
        st_ref[h] = st * decay + jnp.dot(v_t, k * ebl, preferred_element_type=F32)
        return o

    def emit(ci, h, o):
        y = _rms(o, go_ref[...])
        og = og_ref[rows(ci), h * dv:(h + 1) * dv].astype(F32)
        o_ref[rows(ci), h * dv:(h + 1) * dv] = (y * (og * jax.nn.sigmoid(og))).astype(o_ref.dtype)

    heads = range(GLA_HEADS)

    def chunk_stages(ci):
        log2_a = [gate(ci, h) for h in heads]
        yield
        expo = [exponents(x) for x in log2_a]
        p = [jnp.exp2(e) for e in expo]
        pb = [x.astype(BF16) for x in p]
        yield
        operands = [level_operands(ci, h, pb[h]) for h in heads]
        scores = [level_scores(ops) for ops in operands]
        yield
        attn = [masked_sum(s) for s in scores]
        out = [mix(ci, h, p[h], pb[h], attn[h]) for h in heads]
        yield
        for h in heads:
            emit(ci, h, out[h])
        yield

    live = []
    pending = [chunk_stages(ci) for ci in range(GLA_CHUNKS_PER_STEP)]
    tick = 0
    while pending or live:
        if pending and tick % GLA_STAGGER == 0:
            live.append(pending.pop(0))
        for gen in list(live):
            if next(gen, "done") == "done":
                live.remove(gen)
        tick += 1


def _gla(z, wg, bg, go, riders, batch, seq):
    c = GLA_CHUNK * GLA_CHUNKS_PER_STEP
    nc = seq // c
    ws, masks = _gla_constants()
    hk, hv = GLA_HEADS * GLA_DK, GLA_HEADS * GLA_DV
    const2 = lambda b, i: (0, 0)
    rider_specs, windows, steps = _rider_specs(riders, lambda b, i: b * nc + i)
    assert steps <= batch * nc
    return pl.pallas_call(
        functools.partial(_gla_kernel, rider_windows=windows),
        grid=(batch, nc),
        in_specs=[pl.BlockSpec((c, hk), lambda b, i: (b * nc + i, Z_GQ // hk)),
                  pl.BlockSpec((c, hk), lambda b, i: (b * nc + i, Z_GK // hk)),
                  pl.BlockSpec((c, hv), lambda b, i: (b * nc + i, Z_GV // hv)),
                  pl.BlockSpec((c, LANES), lambda b, i: (b * nc + i, Z_GLR // LANES)),
                  pl.BlockSpec((c, hv), lambda b, i: (b * nc + i, Z_OG // hv)),
                  pl.BlockSpec((LANES, hk), const2),
                  pl.BlockSpec((1, hk), const2),
                  pl.BlockSpec((1, GLA_DV), const2),
                  pl.BlockSpec(ws.shape, const2),
                  pl.BlockSpec(masks.shape, lambda b, i: (0, 0, 0))] + rider_specs,
        out_specs=[pl.BlockSpec((c, hv), lambda b, i: (b * nc + i, 0))] + rider_specs,
        out_shape=([jax.ShapeDtypeStruct((batch * seq, hv), BF16)]
                   + [jax.ShapeDtypeStruct(w.shape, BF16) for w in riders]),
        scratch_shapes=[pltpu.VMEM((GLA_HEADS, GLA_DV, GLA_DK), F32)],
        compiler_params=_params(("arbitrary", "arbitrary"), 48),
        name="gla",
    )(z, z, z, z, z, wg, bg, go, jnp.asarray(ws, BF16), jnp.asarray(masks), *riders)


MIX_SUBTILE = 128


def _mix_kernel(oa_ref, ob_ref, ga_ref, gb_ref, wa_ref, wb_ref, wm_ref, x_ref, gm_ref, gpost_ref,
                gpre_ref, sc_ref, sh_ref, x1_ref, h2_ref):
    post = gm_ref[0] * gpost_ref[...]
    pre = gpre_ref[...] * (1.0 + sc_ref[0])

    def sub_tile(rows):
        ya = jnp.dot(oa_ref[rows], wa_ref[...], preferred_element_type=F32)
        yb = jnp.dot(ob_ref[rows], wb_ref[...], preferred_element_type=F32)
        yield
        merged = (jax.nn.sigmoid(ga_ref[rows].astype(F32)) * ya
                  + jax.nn.sigmoid(gb_ref[rows].astype(F32)) * yb).astype(BF16)
        yield
        y = jnp.dot(merged, wm_ref[...], preferred_element_type=F32)
        yield
        x1 = x_ref[rows] + _rms(y, post)
        x1_ref[rows] = x1
        h2_ref[rows] = (_rms(x1, pre) + sh_ref[0]).astype(h2_ref.dtype)
        yield

    pending = [sub_tile(slice(r0, r0 + MIX_SUBTILE))
               for r0 in range(0, x_ref.shape[0], MIX_SUBTILE)]
    live = []
    while pending or live:
        if pending:
            live.append(pending.pop(0))
        for gen in list(live):
            if next(gen, "done") == "done":
                live.remove(gen)


def _mix(oa, ob, z, wa, wb, wm, x2, gate_m, g_post, g_pre, scale_f, shift_f, seq):
    t = oa.shape[0]
    tm = 256
    tpb = seq // tm
    row = pl.BlockSpec((tm, D_MODEL), lambda i: (i, 0))
    vec = pl.BlockSpec((1, D_MODEL), lambda i: (0, 0))
    per_batch = pl.BlockSpec((1, 1, D_MODEL), lambda i: (i // tpb, 0, 0))
    resident = lambda w: pl.BlockSpec(w.shape, lambda i: (0, 0), pipeline_mode=pl.Buffered(1))
    return pl.pallas_call(
        _mix_kernel,
        grid=(t // tm,),
        in_specs=[pl.BlockSpec((tm, oa.shape[1]), lambda i: (i, 0)),
                  pl.BlockSpec((tm, ob.shape[1]), lambda i: (i, 0)),
                  pl.BlockSpec((tm, D_MODEL), lambda i: (i, Z_GATE_A // D_MODEL)),
                  pl.BlockSpec((tm, D_MODEL), lambda i: (i, Z_GATE_B // D_MODEL)),
                  resident(wa), resident(wb), resident(wm),
                  row, per_batch, vec, vec, per_batch, per_batch],
        out_specs=[row, row],
        out_shape=[jax.ShapeDtypeStruct((t, D_MODEL), F32),
                   jax.ShapeDtypeStruct((t, D_MODEL), BF16)],
        compiler_params=_params(("arbitrary",), 56),
        name="mix",
    )(oa, ob, z, z, wa, wb, wm, x2, gate_m, g_post, g_pre, scale_f, shift_f)


def _ffn_up_kernel(h_ref, wa_ref, wv_ref, cwa_ref, cwv_ref, cba_ref, cbv_ref, o_ref,
                   w_bf, u_sc, *, tiles_per_batch):
    i = pl.program_id(1)
    tm, tn = o_ref.shape
    head, body, tail = slice(0, SUBLANES), slice(SUBLANES, SUBLANES + tm), slice(tm, tm + SUBLANES)

    @pl.when(i == 0)
    def _():
        w_bf[:, 0:tn] = wa_ref[...].astype(BF16)
        w_bf[:, tn:2 * tn] = wv_ref[...].astype(BF16)

    @pl.when(i % tiles_per_batch == 0)
    def _():
        u_sc[head] = jnp.zeros((SUBLANES, u_sc.shape[1]), F32)

    @pl.when(i % tiles_per_batch != 0)
    def _():
        u_sc[head] = u_sc[tail]

    u_sc[body] = jnp.dot(h_ref[...], w_bf[...], preferred_element_type=F32)

    def conv(cols, cw_ref, cb_ref):
        cw = cw_ref[...]
        acc = cb_ref[...] + u_sc[body, cols] * cw[CONV_WIDTH - 1:CONV_WIDTH]
        for tap in range(1, CONV_WIDTH):
            acc = acc + (u_sc[SUBLANES - tap:SUBLANES - tap + tm, cols]
                         * cw[CONV_WIDTH - 1 - tap:CONV_WIDTH - tap])
        return acc

    a = conv(slice(0, tn), cwa_ref, cba_ref)
    val = conv(slice(tn, 2 * tn), cwv_ref, cbv_ref)
    o_ref[...] = (jax.nn.gelu(a, approximate=True) * val).astype(o_ref.dtype)


def _ffn_up(h2, w_up, conv_w, conv_b, seq):
    t = h2.shape[0]
    tm, tn = 1024, 512
    nj = D_FF // tn
    return pl.pallas_call(
        functools.partial(_ffn_up_kernel, tiles_per_batch=seq // tm),
        grid=(nj, t // tm),
        in_specs=[pl.BlockSpec((tm, D_MODEL), lambda j, i: (i, 0)),
                  pl.BlockSpec((D_MODEL, tn), lambda j, i: (0, j)),
                  pl.BlockSpec((D_MODEL, tn), lambda j, i: (0, nj + j)),
                  pl.BlockSpec((CONV_WIDTH, tn), lambda j, i: (0, j)),
                  pl.BlockSpec((CONV_WIDTH, tn), lambda j, i: (0, nj + j)),
                  pl.BlockSpec((1, tn), lambda j, i: (0, j)),
                  pl.BlockSpec((1, tn), lambda j, i: (0, nj + j))],
        out_specs=pl.BlockSpec((tm, tn), lambda j, i: (i, j)),
        out_shape=jax.ShapeDtypeStruct((t, D_FF), BF16),
        scratch_shapes=[pltpu.VMEM((D_MODEL, 2 * tn), BF16),
                        pltpu.VMEM((tm + SUBLANES, 2 * tn), F32)],
        compiler_params=_params(("arbitrary", "arbitrary"), 56),
        name="ffn_up",
    )(h2, w_up, w_up, conv_w, conv_w, conv_b, conv_b)


def _ffn_down_kernel(g_ref, w_ref, x1_ref, gf_ref, gpost_ref, o_ref):
    f = jnp.dot(g_ref[...], w_ref[...], preferred_element_type=F32)
    o_ref[...] = x1_ref[...] + _rms(f, gf_ref[0] * gpost_ref[...])


def _ffn_down(g, w, x1, gate_f, g_post, seq):
    t = g.shape[0]
    tm = 256
    tpb = seq // tm
    return pl.pallas_call(
        _ffn_down_kernel,
        grid=(t // tm,),
        in_specs=[pl.BlockSpec((tm, D_FF), lambda i: (i, 0)),
                  pl.BlockSpec((D_FF, D_MODEL), lambda i: (0, 0), pipeline_mode=pl.Buffered(1)),
                  pl.BlockSpec((tm, D_MODEL), lambda i: (i, 0)),
                  pl.BlockSpec((1, 1, D_MODEL), lambda i: (i // tpb, 0, 0)),
                  pl.BlockSpec((1, D_MODEL), lambda i: (0, 0))],
        out_specs=pl.BlockSpec((tm, D_MODEL), lambda i: (i, 0)),
        out_shape=jax.ShapeDtypeStruct((t, D_MODEL), F32),
        compiler_params=_params(("arbitrary",), 56),
        name="ffn_down",
    )(g, w, x1, gate_f, g_post)


def _swap_halves(w):
    half = w.shape[-1] // 2
    return jnp.concatenate([w[..., half:], w[..., :half]], axis=-1)


def _w_in_special(w_t):
    half = MLA_ROPE // 2
    k_rope = w_t[_SRC_KROPE:_SRC_KROPE + MLA_ROPE]
    g_lr = w_t[_SRC_GLR:_SRC_GLR + GLA_GATE_RANK]
    pad = jnp.zeros((Z_GQ - Z_GLR - 2 * GLA_GATE_RANK, w_t.shape[1]), w_t.dtype)
    return jnp.concatenate([k_rope, k_rope[half:], k_rope[:half], g_lr, g_lr, pad], axis=0)


def _prep_w_q(w_q_up):
    w = w_q_up.reshape(MLA_Q_RANK, MLA_HEADS, MLA_NOPE + MLA_ROPE)
    pe = w[:, :, MLA_NOPE:]
    w = jnp.concatenate([w, _swap_halves(pe)], axis=-1)
    return jnp.transpose(w, (1, 0, 2)).astype(BF16)


def kernel(x, c, positions, w_ada, b_ada, g_pre_mix, w_in, g_q_lat, w_q_up, g_kv_lat, w_kv_up,
           w_gla_gate_up, b_gla_gate, g_gla_out, w_branch_a, w_branch_b, w_mix_out, g_post_mix,
           g_pre_ffn, w_ffn_up, conv_w, conv_b, w_ffn_down, g_post_ffn):
    batch, seq, _ = x.shape
    depth = w_ada.shape[0]
    t = batch * seq
    row = lambda v: v.reshape(1, -1)

    inv = 1.0 / (ROPE_THETA ** (jnp.arange(0, MLA_ROPE, 2, dtype=F32) / MLA_ROPE))
    cos, sin = _rope_tables(positions, inv)

    c_pad = jnp.zeros((SUBLANES, D_MODEL), F32).at[:batch].set(c)
    x2 = x.reshape(t, D_MODEL)
    for l in range(depth):
        mod = _ada(c_pad, w_ada[l], row(b_ada[l]))[:batch]
        shift_m, scale_m, gate_m, shift_f, scale_f, gate_f = (
            m.reshape(batch, 1, D_MODEL) for m in jnp.split(mod, 6, axis=-1))

        h = _norm_mod(x2, row(g_pre_mix[l]), scale_m, shift_m, seq)
        w_in_t = w_in[l].T
        z = _mm_in(h, w_in_t, _w_in_special(w_in_t))

        q, k, v = _mla_proj(z, cos, sin, row(g_q_lat[l]), row(g_kv_lat[l]),
                            _prep_w_q(w_q_up[l]), w_kv_up[l].astype(BF16), batch, seq)
        o_a, w_a_bf, w_b_bf = _attn(q, k, v, [w_branch_a[l], w_branch_b[l]], batch, seq)

        wg_hi, wg_lo = _split_bf16(w_gla_gate_up[l])
        wg = jnp.zeros((LANES, GLA_HEADS * GLA_DK), BF16)
        wg = wg.at[:GLA_GATE_RANK].set(wg_hi).at[GLA_GATE_RANK:2 * GLA_GATE_RANK].set(wg_lo)
        o_b, w_down_bf, w_mix_bf = _gla(z, wg, row(b_gla_gate[l]), row(g_gla_out[l]),
                                        [w_ffn_down[l], w_mix_out[l]], batch, seq)

        x1, h2 = _mix(o_a, o_b, z, w_a_bf, w_b_bf, w_mix_bf, x2, gate_m, row(g_post_mix[l]),
                      row(g_pre_ffn[l]), scale_f, shift_f, seq)

        g = _ffn_up(h2, w_ffn_up[l], conv_w[l], row(conv_b[l]), seq)
        x2 = _ffn_down(g, w_down_bf, x1, gate_f, row(g_post_ffn[l]), seq)
    return x2.reshape(batch, seq, D_MODEL)
```

```python
import functools

import jax
import jax.numpy as jnp
import numpy as np
from jax import lax
from jax.experimental import pallas as pl
from jax.experimental.pallas import tpu as pltpu

F32 = jnp.float32
BF16 = jnp.bfloat16

D_MODEL = 2048
MLA_HEADS = 8
MLA_Q_RANK = 512
MLA_KV_RANK = 256
MLA_NOPE = 128
MLA_ROPE = 64
MLA_V = 128
ROPE_THETA = 10000.0
GLA_HEADS = 4
GLA_DK = 256
GLA_DV = 512
GLA_GATE_RANK = 16
GLA_TAU = 16.0
D_FF = 5632
CONV_WIDTH = 3
EPS = 1e-6

LANES = 128
SUBLANES = 8
MLA_QK_PAD = 256

Z_QLAT = 0
Z_KVLAT = 512
Z_KROPE = 768
Z_GLR = 896
Z_GQ = 1024
Z_GV = 2048
Z_OG = 4096
Z_GATE_A = 6144
Z_GATE_B = 8192
Z_GK = 10240
Z_WIDTH = 11264

_SRC_KROPE = MLA_Q_RANK + MLA_KV_RANK
_SRC_GQ = _SRC_KROPE + MLA_ROPE
_SRC_GK = _SRC_GQ + GLA_HEADS * GLA_DK
_SRC_GV = _SRC_GK + GLA_HEADS * GLA_DK
_SRC_GLR = _SRC_GV + GLA_HEADS * GLA_DV
_SRC_OG = _SRC_GLR + GLA_GATE_RANK
_SRC_GATE_A = _SRC_OG + GLA_HEADS * GLA_DV
_SRC_GATE_B = _SRC_GATE_A + D_MODEL

GLA_CHUNK = 128
GLA_LEVELS = 7

MIB = 1024 * 1024


def _params(semantics, vmem_mib):
    return pltpu.CompilerParams(dimension_semantics=semantics,
                                vmem_limit_bytes=vmem_mib * MIB)


def _rms(x, g):
    return x * lax.rsqrt(jnp.mean(x * x, axis=-1, keepdims=True) + EPS) * g


RIDER_ROWS = 256


def _rider_specs(arrays, step_of):
    specs, windows, start = [], [], 0
    for a in arrays:
        n = a.shape[0] // RIDER_ROWS
        specs.append(pl.BlockSpec(
            (RIDER_ROWS, a.shape[1]),
            lambda *idx, lo=start, n=n: (jnp.clip(step_of(*idx) - lo, 0, n - 1), 0)))
        windows.append((start, start + n))
        start += n
    return specs, windows, start


def _run_riders(step, windows, src_refs, dst_refs):
    for (lo, hi), src, dst in zip(windows, src_refs, dst_refs):
        @pl.when((step >= lo) & (step < hi))
        def _(src=src, dst=dst):
            dst[...] = src[...].astype(BF16)


def _ada_kernel(c_ref, w_ref, b_ref, o_ref):
    c = c_ref[...]
    ca = c * jax.nn.sigmoid(c)
    o_ref[...] = jnp.dot(ca.astype(BF16), w_ref[...].astype(BF16),
                         preferred_element_type=F32) + b_ref[...]


def _ada(c_pad, w, b):
    n = w.shape[1]
    tn = 1024
    return pl.pallas_call(
        _ada_kernel,
        grid=(n // tn,),
        in_specs=[pl.BlockSpec((SUBLANES, D_MODEL), lambda j: (0, 0)),
                  pl.BlockSpec((D_MODEL, tn), lambda j: (0, j)),
                  pl.BlockSpec((1, tn), lambda j: (0, j))],
        out_specs=pl.BlockSpec((SUBLANES, tn), lambda j: (0, j)),
        out_shape=jax.ShapeDtypeStruct((SUBLANES, n), F32),
        compiler_params=_params(("arbitrary",), 32),
        name="ada",
    )(c_pad, w, b)


def _norm_mod_kernel(x_ref, g_ref, sc_ref, sh_ref, o_ref):
    y = _rms(x_ref[...], g_ref[...] * (1.0 + sc_ref[0]))
    o_ref[...] = (y + sh_ref[0]).astype(o_ref.dtype)


def _norm_mod(x2, g, scale, shift, seq):
    t = x2.shape[0]
    tm = 512
    tpb = seq // tm
    return pl.pallas_call(
        _norm_mod_kernel,
        grid=(t // tm,),
        in_specs=[pl.BlockSpec((tm, D_MODEL), lambda i: (i, 0)),
                  pl.BlockSpec((1, D_MODEL), lambda i: (0, 0)),
                  pl.BlockSpec((1, 1, D_MODEL), lambda i: (i // tpb, 0, 0)),
                  pl.BlockSpec((1, 1, D_MODEL), lambda i: (i // tpb, 0, 0))],
        out_specs=pl.BlockSpec((tm, D_MODEL), lambda i: (i, 0)),
        out_shape=jax.ShapeDtypeStruct((t, D_MODEL), BF16),
        compiler_params=_params(("arbitrary",), 32),
        name="norm_mod",
    )(x2, g, scale, shift)


MM_IN_TN = 1024
MM_IN_SRC_ALIGN = 16
_MM_IN_SRC_ROWS = (0,
                   _SRC_GQ,
                   _SRC_GV, _SRC_GV + MM_IN_TN,
                   _SRC_OG, _SRC_OG + MM_IN_TN,
                   _SRC_GATE_A, _SRC_GATE_A + MM_IN_TN,
                   _SRC_GATE_B, _SRC_GATE_B + MM_IN_TN,
                   _SRC_GK)


def _mm_in_kernel(src_ref, h_ref, w_ref, wsp_ref, o_ref, w_bf):
    del src_ref
    j, i = pl.program_id(0), pl.program_id(1)

    @pl.when(i == 0)
    def _():
        w_bf[...] = w_ref[...].astype(BF16)

    @pl.when((i == 0) & (j == 0))
    def _():
        w_bf[Z_KROPE:Z_GQ] = wsp_ref[...].astype(BF16)

    for r0 in range(0, o_ref.shape[0], MM_IN_SUBTILE):
        rows = slice(r0, r0 + MM_IN_SUBTILE)
        o_ref[rows] = lax.dot_general(h_ref[rows], w_bf[...], (((1,), (1,)), ((), ())),
                                      preferred_element_type=F32).astype(o_ref.dtype)


MM_IN_SUBTILE = 1024


def _mm_in(h, w_t, w_special):
    t = h.shape[0]
    tm, tn = 2048, MM_IN_TN
    grid_spec = pltpu.PrefetchScalarGridSpec(
        num_scalar_prefetch=1,
        grid=(Z_WIDTH // tn, t // tm),
        in_specs=[pl.BlockSpec((tm, D_MODEL), lambda j, i, src: (i, 0)),
                  pl.BlockSpec((pl.Element(tn), pl.Element(D_MODEL)),
                               lambda j, i, src: (src[j] * MM_IN_SRC_ALIGN, 0)),
                  pl.BlockSpec((Z_GQ - Z_KROPE, D_MODEL), lambda j, i, src: (0, 0))],
        out_specs=pl.BlockSpec((tm, tn), lambda j, i, src: (i, j)),
        scratch_shapes=[pltpu.VMEM((tn, D_MODEL), BF16)])
    return pl.pallas_call(
        _mm_in_kernel,
        grid_spec=grid_spec,
        out_shape=jax.ShapeDtypeStruct((t, Z_WIDTH), BF16),
        compiler_params=_params(("arbitrary", "arbitrary"), 56),
        name="mm_in",
    )(jnp.asarray([r // MM_IN_SRC_ALIGN for r in _MM_IN_SRC_ROWS], jnp.int32), h, w_t, w_special)


ROPE_HALF = MLA_ROPE // 2
ROPE_PACK = LANES // ROPE_HALF


def _rope_kernel(pos_ref, inv_ref, cos_ref, sin_ref):
    ang = pos_ref[...].astype(F32) * inv_ref[...]
    cos_ref[...] = jnp.cos(ang)
    sin_ref[...] = jnp.sin(ang)


def _rope_tables(positions, inv):
    t = positions.size
    rows = t // ROPE_PACK
    pos_dense = jnp.repeat(positions.reshape(t), ROPE_HALF).reshape(rows, LANES)
    inv_dense = jnp.tile(inv, ROPE_PACK).reshape(1, LANES)
    tr = 512
    spec = pl.BlockSpec((tr, LANES), lambda i: (i, 0))
    cos, sin = pl.pallas_call(
        _rope_kernel,
        grid=(rows // tr,),
        in_specs=[spec, pl.BlockSpec((1, LANES), lambda i: (0, 0))],
        out_specs=[spec, spec],
        out_shape=[jax.ShapeDtypeStruct((rows, LANES), F32)] * 2,
        compiler_params=_params(("arbitrary",), 32),
        name="rope_tables",
    )(pos_dense, inv_dense)
    return cos.reshape(t, ROPE_HALF), sin.reshape(t, ROPE_HALF)


def _mla_proj_kernel(ql_ref, kvl_ref, kr_ref, cos_ref, sin_ref,
                     gq_ref, gkv_ref, wq_ref, wkv_ref, q_ref, k_ref, v_ref):
    c, s = cos_ref[...], sin_ref[...]
    zeros = jnp.zeros((c.shape[0], LANES - MLA_ROPE), F32)
    cosm = jnp.concatenate([c, c, zeros], axis=1)
    sinm = jnp.concatenate([-s, s, zeros], axis=1)

    def rope(r):
        return r * cosm + pltpu.roll(r, MLA_ROPE, 1) * sinm

    scale = (MLA_NOPE + MLA_ROPE) ** -0.5 * np.log2(np.e)
    qn =_rms(ql_ref[...].astype(F32), gq_ref[...]).astype(BF16)
    for h in range(MLA_HEADS):
        r = jnp.dot(qn, wq_ref[h], preferred_element_type=F32) * scale
        q_ref[0, h, :, 0:MLA_NOPE] = r[:, 0:MLA_NOPE].astype(BF16)
        q_ref[0, h, :, MLA_NOPE:MLA_QK_PAD] = rope(r[:, MLA_NOPE:MLA_QK_PAD]).astype(BF16)

    kvn = _rms(kvl_ref[...].astype(F32), gkv_ref[...]).astype(BF16)
    kv = jnp.dot(kvn, wkv_ref[...], preferred_element_type=F32).astype(BF16)
    kpe = rope(kr_ref[...].astype(F32)).astype(BF16)
    hw = MLA_NOPE + MLA_V
    for h in range(MLA_HEADS):
        k_ref[0, h, :, 0:MLA_NOPE] = kv[:, h * hw:h * hw + MLA_NOPE]
        k_ref[0, h, :, MLA_NOPE:MLA_QK_PAD] = kpe
        v_ref[0, h] = kv[:, h * hw + MLA_NOPE:(h + 1) * hw]


def _mla_proj(z, cos, sin, gq, gkv, wq, wkv, batch, seq):
    tm = 512
    tpb = seq // tm
    qk_shape = jax.ShapeDtypeStruct((batch, MLA_HEADS, seq, MLA_QK_PAD), BF16)
    v_shape = jax.ShapeDtypeStruct((batch, MLA_HEADS, seq, MLA_V), BF16)
    const2 = lambda i: (0, 0)
    qk_spec = pl.BlockSpec((1, MLA_HEADS, tm, MLA_QK_PAD), lambda i: (i // tpb, 0, i % tpb, 0))
    return pl.pallas_call(
        _mla_proj_kernel,
        grid=(batch * tpb,),
        in_specs=[pl.BlockSpec((tm, MLA_Q_RANK), lambda i: (i, Z_QLAT // MLA_Q_RANK)),
                  pl.BlockSpec((tm, MLA_KV_RANK), lambda i: (i, Z_KVLAT // MLA_KV_RANK)),
                  pl.BlockSpec((tm, LANES), lambda i: (i, Z_KROPE // LANES)),
                  pl.BlockSpec((tm, ROPE_HALF), lambda i: (i, 0)),
                  pl.BlockSpec((tm, ROPE_HALF), lambda i: (i, 0)),
                  pl.BlockSpec((1, MLA_Q_RANK), const2),
                  pl.BlockSpec((1, MLA_KV_RANK), const2),
                  pl.BlockSpec((MLA_HEADS, MLA_Q_RANK, MLA_QK_PAD), lambda i: (0, 0, 0)),
                  pl.BlockSpec((MLA_KV_RANK, MLA_HEADS * (MLA_NOPE + MLA_V)), const2)],
        out_specs=[qk_spec, qk_spec,
                   pl.BlockSpec((1, MLA_HEADS, tm, MLA_V), lambda i: (i // tpb, 0, i % tpb, 0))],
        out_shape=[qk_shape, qk_shape, v_shape],
        compiler_params=_params(("arbitrary",), 48),
        name="mla_proj",
    )(z, z, z, cos, sin, gq, gkv, wq, wkv)


ATTN_KV_BLOCK = 1024
ATTN_Q_BLOCKS = 2
ATTN_DIAG_CHAINS = 2


def _attn_kernel(q_ref, k_ref, v_ref, *refs, rider_windows):
    nr = len(rider_windows)
    rider_src, o_ref, rider_dst = refs[:nr], refs[nr], refs[nr + 1:2 * nr + 1]
    m_sc, l_sc, acc_sc = refs[2 * nr + 1:]
    blk = ATTN_KV_BLOCK
    nt = (((1,), (1,)), ((), ()))
    qi = pl.program_id(2)
    step = ((pl.program_id(0) * pl.num_programs(1) + pl.program_id(1)) * pl.num_programs(2) + qi)
    _run_riders(step, rider_windows, rider_src, rider_dst)
    m_sc[...] = jnp.full(m_sc.shape, -jnp.inf, F32)
    l_sc[...] = jnp.zeros(l_sc.shape, F32)
    acc_sc[...] = jnp.zeros(acc_sc.shape, F32)

    def update(rows, s, v):
        m_prev = m_sc[rows]
        m_new = jnp.maximum(m_prev, jnp.max(s, axis=-1, keepdims=True))
        alpha = jnp.exp2(m_prev - m_new)
        ps = [jnp.exp2(s[:, c * LANES:(c + 1) * LANES] - m_new)
              for c in range(s.shape[1] // LANES)]
        l_sc[rows] = alpha * l_sc[rows] + functools.reduce(lambda a, b: a + b, ps)
        p = jnp.concatenate(ps, axis=1).astype(BF16)
        acc_sc[rows] = alpha * acc_sc[rows] + jnp.dot(p, v, preferred_element_type=F32)
        m_sc[rows] = m_new

    def full_block(rows, ks):
        k = k_ref[0, 0, pl.ds(ks, blk), :]
        v = v_ref[0, 0, pl.ds(ks, blk), :]
        update(rows, lax.dot_general(q_ref[0, 0, rows], k, nt, preferred_element_type=F32), v)

    def diag_block(row0, ks):
        sub = blk // ATTN_DIAG_CHAINS
        for r in range(ATTN_DIAG_CHAINS):
            n = (r + 1) * sub
            rows = slice(row0 + r * sub, row0 + n)
            k = k_ref[0, 0, pl.ds(ks, n), :]
            v = v_ref[0, 0, pl.ds(ks, n), :]
            s = lax.dot_general(q_ref[0, 0, rows], k, nt, preferred_element_type=F32)
            row = lax.broadcasted_iota(jnp.int32, s.shape, 0) + r * sub
            col = lax.broadcasted_iota(jnp.int32, s.shape, 1)
            update(rows, jnp.where(col <= row, s, -jnp.inf), v)

    def body(kb, carry):
        full_block(slice(0, ATTN_Q_BLOCKS * blk), pl.multiple_of(kb * blk, blk))
        return carry

    lax.fori_loop(0, qi * ATTN_Q_BLOCKS, body, 0)

    base = qi * ATTN_Q_BLOCKS * blk
    for a in range(ATTN_Q_BLOCKS):
        for kb in range(a):
            full_block(slice(a * blk, (a + 1) * blk), pl.multiple_of(base + kb * blk, blk))
        diag_block(a * blk, pl.multiple_of(base + a * blk, blk))
    l = jnp.sum(l_sc[...], axis=-1, keepdims=True)
    o_ref[...] = (acc_sc[...] / l).astype(o_ref.dtype)


def _attn(q, k, v, riders, batch, seq):
    blk = ATTN_Q_BLOCKS * ATTN_KV_BLOCK
    nq = seq // blk
    rider_specs, windows, steps = _rider_specs(
        riders, lambda b, h, i: (b * MLA_HEADS + h) * nq + i)
    assert steps <= batch * MLA_HEADS * nq
    return pl.pallas_call(
        functools.partial(_attn_kernel, rider_windows=windows),
        grid=(batch, MLA_HEADS, nq),
        in_specs=[pl.BlockSpec((1, 1, blk, MLA_QK_PAD), lambda b, h, i: (b, h, i, 0)),
                  pl.BlockSpec((1, 1, seq, MLA_QK_PAD), lambda b, h, i: (b, h, 0, 0)),
                  pl.BlockSpec((1, 1, seq, MLA_V), lambda b, h, i: (b, h, 0, 0))] + rider_specs,
        out_specs=[pl.BlockSpec((blk, MLA_V), lambda b, h, i: (b * nq + i, h))] + rider_specs,
        out_shape=([jax.ShapeDtypeStruct((batch * seq, MLA_HEADS * MLA_V), BF16)]
                   + [jax.ShapeDtypeStruct(w.shape, BF16) for w in riders]),
        scratch_shapes=[pltpu.VMEM((blk, LANES), F32), pltpu.VMEM((blk, LANES), F32),
                        pltpu.VMEM((blk, MLA_V), F32)],
        compiler_params=_params(("arbitrary", "arbitrary", "arbitrary"), 48),
        name="attn",
    )(q, k, v, *riders)


def _gla_constants():
    c = GLA_CHUNK
    idx = np.arange(c)
    w = np.zeros((GLA_LEVELS + 2, c, c), np.float32)
    masks = np.zeros((GLA_LEVELS + 1, c, c), np.float32)
    for l in range(GLA_LEVELS):
        half = 1 << l
        blk = idx // (2 * half)
        mid = blk * 2 * half + half - 1
        upper = (idx % (2 * half)) >= half
        t = idx[None, :]
        up_rows = (t > mid[:, None]) & (t <= idx[:, None])
        lo_rows = (t > idx[:, None]) & (t <= mid[:, None])
        w[l] = np.where(upper[:, None], up_rows, lo_rows)
        masks[l] = (blk[:, None] == blk[None, :]) & upper[:, None] & (~upper[None, :])
    w[GLA_LEVELS] = idx[None, :] <= idx[:, None]
    w[GLA_LEVELS + 1] = idx[None, :] > idx[:, None]
    masks[GLA_LEVELS] = np.eye(c)
    w = w.reshape(-1, c)
    return np.concatenate([w, w], axis=1), masks


def _split_bf16(x):
    hi = x.astype(BF16)
    return hi, (x - hi.astype(F32)).astype(BF16)


GLA_CHUNKS_PER_STEP = 2
GLA_STAGGER = 2


def _gla_kernel(q_ref, k_ref, v_ref, lr_ref, og_ref, wg_ref, bg_ref, go_ref, ws_ref, mask_ref,
                *refs, rider_windows):
    nr = len(rider_windows)
    rider_src, o_ref, rider_dst, st_ref = refs[:nr], refs[nr], refs[nr + 1:2 * nr + 1], refs[-1]
    c = GLA_CHUNK
    dk, dv = GLA_DK, GLA_DV
    nt = (((1,), (1,)), ((), ()))
    _run_riders(pl.program_id(0) * pl.num_programs(1) + pl.program_id(1), rider_windows,
                rider_src, rider_dst)

    @pl.when(pl.program_id(1) == 0)
    def _():
        st_ref[...] = jnp.zeros(st_ref.shape, F32)

    def rows(ci):
        return slice(ci * c, (ci + 1) * c)

    def qk(ci, h):
        q = q_ref[rows(ci), h * dk:(h + 1) * dk] * (dk ** -0.5)
        return q, k_ref[rows(ci), h * dk:(h + 1) * dk]

    def gate(ci, h):
        lr = lr_ref[rows(ci)]
        x = (jnp.dot(lr, wg_ref[:, h * dk:(h + 1) * dk], preferred_element_type=F32)
             + bg_ref[:, h * dk:(h + 1) * dk])
        log2_a = (-(jnp.maximum(-x, 0.0) + jnp.log1p(jnp.exp(-jnp.abs(x))))
                  * (np.log2(np.e) / GLA_TAU))
        return jnp.concatenate(_split_bf16(log2_a), axis=0)

    def exponents(log2_a):
        return jnp.dot(ws_ref[...], log2_a, preferred_element_type=F32)

    def level_operands(ci, h, pb):
        q, k = qk(ci, h)
        ops = [(q * pb[l * c:(l + 1) * c], k * pb[l * c:(l + 1) * c]) for l in range(GLA_LEVELS)]
        return ops + [(q, k)]

    def level_scores(ops):
        return [lax.dot_general(a, b, nt, preferred_element_type=F32) for a, b in ops]

    def masked_sum(scores):
        attn = mask_ref[0] * scores[0]
        for l in range(1, GLA_LEVELS + 1):
            attn = attn + mask_ref[l] * scores[l]
        return attn.astype(BF16)

    def mix(ci, h, p, pb, attn):
        q, k = qk(ci, h)
        v = v_ref[rows(ci), h * dv:(h + 1) * dv]
        eb = pb[GLA_LEVELS * c:(GLA_LEVELS + 1) * c]
        ebl = pb[(GLA_LEVELS + 1) * c:(GLA_LEVELS + 2) * c]
        st = st_ref[h]
        o = (lax.dot_general(q * eb, st.astype(BF16), nt, preferred_element_type=F32)
             + jnp.dot(attn, v, preferred_element_type=F32))
        v_t = v.astype(F32).T.astype(BF16)
        decay = p[(GLA_LEVELS + 1) * c - 1:(GLA_LEVELS + 1) * c]
        st_ref[h] = st * decay + jnp.dot(v_t, k * ebl, preferred_element_type=F32)
        return o

    def emit(ci, h, o):
        y = _rms(o, go_ref[...])
        og = og_ref[rows(ci), h * dv:(h + 1) * dv].astype(F32)
        o_ref[rows(ci), h * dv:(h + 1) * dv] = (y * (og * jax.nn.sigmoid(og))).astype(o_ref.dtype)

    heads = range(GLA_HEADS)

    def chunk_stages(ci):
        log2_a = [gate(ci, h) for h in heads]
        yield
        expo = [exponents(x) for x in log2_a]
        p = [jnp.exp2(e) for e in expo]
        pb = [x.astype(BF16) for x in p]
        yield
        operands = [level_operands(ci, h, pb[h]) for h in heads]
        scores = [level_scores(ops) for ops in operands]
        yield
        attn = [masked_sum(s) for s in scores]
        out = [mix(ci, h, p[h], pb[h], attn[h]) for h in heads]
        yield
        for h in heads:
            emit(ci, h, out[h])
        yield

    live = []
    pending = [chunk_stages(ci) for ci in range(GLA_CHUNKS_PER_STEP)]
    tick = 0
    while pending or live:
        if pending and tick % GLA_STAGGER == 0:
            live.append(pending.pop(0))
        for gen in list(live):
            if next(gen, "done") == "done":
                live.remove(gen)
        tick += 1


def _gla(z, wg, bg, go, riders, batch, seq):
    c = GLA_CHUNK * GLA_CHUNKS_PER_STEP
    nc = seq // c
    ws, masks = _gla_constants()
    hk, hv = GLA_HEADS * GLA_DK, GLA_HEADS * GLA_DV
    const2 = lambda b, i: (0, 0)
    rider_specs, windows, steps = _rider_specs(riders, lambda b, i: b * nc + i)
    assert steps <= batch * nc
    return pl.pallas_call(
        functools.partial(_gla_kernel, rider_windows=windows),
        grid=(batch, nc),
        in_specs=[pl.BlockSpec((c, hk), lambda b, i: (b * nc + i, Z_GQ // hk)),
                  pl.BlockSpec((c, hk), lambda b, i: (b * nc + i, Z_GK // hk)),
                  pl.BlockSpec((c, hv), lambda b, i: (b * nc + i, Z_GV // hv)),
                  pl.BlockSpec((c, LANES), lambda b, i: (b * nc + i, Z_GLR // LANES)),
                  pl.BlockSpec((c, hv), lambda b, i: (b * nc + i, Z_OG // hv)),
                  pl.BlockSpec((LANES, hk), const2),
                  pl.BlockSpec((1, hk), const2),
                  pl.BlockSpec((1, GLA_DV), const2),
                  pl.BlockSpec(ws.shape, const2),
                  pl.BlockSpec(masks.shape, lambda b, i: (0, 0, 0))] + rider_specs,
        out_specs=[pl.BlockSpec((c, hv), lambda b, i: (b * nc + i, 0))] + rider_specs,
        out_shape=([jax.ShapeDtypeStruct((batch * seq, hv), BF16)]
                   + [jax.ShapeDtypeStruct(w.shape, BF16) for w in riders]),
        scratch_shapes=[pltpu.VMEM((GLA_HEADS, GLA_DV, GLA_DK), F32)],
        compiler_params=_params(("arbitrary", "arbitrary"), 48),
        name="gla",
    )(z, z, z, z, z, wg, bg, go, jnp.asarray(ws, BF16), jnp.asarray(masks), *riders)


MIX_SUBTILE = 128


def _mix_kernel(oa_ref, ob_ref, ga_ref, gb_ref, wa_ref, wb_ref, wm_ref, x_ref, gm_ref, gpost_ref,
                gpre_ref, sc_ref, sh_ref, x1_ref, h2_ref):
    post = gm_ref[0] * gpost_ref[...]
    pre = gpre_ref[...] * (1.0 + sc_ref[0])

    def sub_tile(rows):
        ya = jnp.dot(oa_ref[rows], wa_ref[...], preferred_element_type=F32)
        yb = jnp.dot(ob_ref[rows], wb_ref[...], preferred_element_type=F32)
        yield
        merged = (jax.nn.sigmoid(ga_ref[rows].astype(F32)) * ya
                  + jax.nn.sigmoid(gb_ref[rows].astype(F32)) * yb).astype(BF16)
        yield
        y = jnp.dot(merged, wm_ref[...], preferred_element_type=F32)
        yield
        x1 = x_ref[rows] + _rms(y, post)
        x1_ref[rows] = x1
        h2_ref[rows] = (_rms(x1, pre) + sh_ref[0]).astype(h2_ref.dtype)
        yield

    pending = [sub_tile(slice(r0, r0 + MIX_SUBTILE))
               for r0 in range(0, x_ref.shape[0], MIX_SUBTILE)]
    live = []
    while pending or live:
        if pending:
            live.append(pending.pop(0))
        for gen in list(live):
            if next(gen, "done") == "done":
                live.remove(gen)


def _mix(oa, ob, z, wa, wb, wm, x2, gate_m, g_post, g_pre, scale_f, shift_f, seq):
    t = oa.shape[0]
    tm = 256
    tpb = seq // tm
    row = pl.BlockSpec((tm, D_MODEL), lambda i: (i, 0))
    vec = pl.BlockSpec((1, D_MODEL), lambda i: (0, 0))
    per_batch = pl.BlockSpec((1, 1, D_MODEL), lambda i: (i // tpb, 0, 0))
    resident = lambda w: pl.BlockSpec(w.shape, lambda i: (0, 0), pipeline_mode=pl.Buffered(1))
    return pl.pallas_call(
        _mix_kernel,
        grid=(t // tm,),
        in_specs=[pl.BlockSpec((tm, oa.shape[1]), lambda i: (i, 0)),
                  pl.BlockSpec((tm, ob.shape[1]), lambda i: (i, 0)),
                  pl.BlockSpec((tm, D_MODEL), lambda i: (i, Z_GATE_A // D_MODEL)),
                  pl.BlockSpec((tm, D_MODEL), lambda i: (i, Z_GATE_B // D_MODEL)),
                  resident(wa), resident(wb), resident(wm),
                  row, per_batch, vec, vec, per_batch, per_batch],
        out_specs=[row, row],
        out_shape=[jax.ShapeDtypeStruct((t, D_MODEL), F32),
                   jax.ShapeDtypeStruct((t, D_MODEL), BF16)],
        compiler_params=_params(("arbitrary",), 56),
        name="mix",
    )(oa, ob, z, z, wa, wb, wm, x2, gate_m, g_post, g_pre, scale_f, shift_f)


def _ffn_up_kernel(h_ref, wa_ref, wv_ref, cwa_ref, cwv_ref, cba_ref, cbv_ref, o_ref,
                   w_bf, u_sc, *, tiles_per_batch):
    i = pl.program_id(1)
    tm, tn = o_ref.shape
    head, body, tail = slice(0, SUBLANES), slice(SUBLANES, SUBLANES + tm), slice(tm, tm + SUBLANES)

    @pl.when(i == 0)
    def _():
        w_bf[:, 0:tn] = wa_ref[...].astype(BF16)
        w_bf[:, tn:2 * tn] = wv_ref[...].astype(BF16)

    @pl.when(i % tiles_per_batch == 0)
    def _():
        u_sc[head] = jnp.zeros((SUBLANES, u_sc.shape[1]), F32)

    @pl.when(i % tiles_per_batch != 0)
    def _():
        u_sc[head] = u_sc[tail]

    u_sc[body] = jnp.dot(h_ref[...], w_bf[...], preferred_element_type=F32)

    def conv(cols, cw_ref, cb_ref):
        cw = cw_ref[...]
        acc = cb_ref[...] + u_sc[body, cols] * cw[CONV_WIDTH - 1:CONV_WIDTH]
        for tap in range(1, CONV_WIDTH):
            acc = acc + (u_sc[SUBLANES - tap:SUBLANES - tap + tm, cols]
                         * cw[CONV_WIDTH - 1 - tap:CONV_WIDTH - tap])
        return acc

    a = conv(slice(0, tn), cwa_ref, cba_ref)
    val = conv(slice(tn, 2 * tn), cwv_ref, cbv_ref)
    o_ref[...] = (jax.nn.gelu(a, approximate=True) * val).astype(o_ref.dtype)


def _ffn_up(h2, w_up, conv_w, conv_b, seq):
    t = h2.shape[0]
    tm, tn = 1024, 512
    nj = D_FF // tn
    return pl.pallas_call(
        functools.partial(_ffn_up_kernel, tiles_per_batch=seq // tm),
        grid=(nj, t // tm),
        in_specs=[pl.BlockSpec((tm, D_MODEL), lambda j, i: (i, 0)),
                  pl.BlockSpec((D_MODEL, tn), lambda j, i: (0, j)),
                  pl.BlockSpec((D_MODEL, tn), lambda j, i: (0, nj + j)),
                  pl.BlockSpec((CONV_WIDTH, tn), lambda j, i: (0, j)),
                  pl.BlockSpec((CONV_WIDTH, tn), lambda j, i: (0, nj + j)),
                  pl.BlockSpec((1, tn), lambda j, i: (0, j)),
                  pl.BlockSpec((1, tn), lambda j, i: (0, nj + j))],
        out_specs=pl.BlockSpec((tm, tn), lambda j, i: (i, j)),
        out_shape=jax.ShapeDtypeStruct((t, D_FF), BF16),
        scratch_shapes=[pltpu.VMEM((D_MODEL, 2 * tn), BF16),
                        pltpu.VMEM((tm + SUBLANES, 2 * tn), F32)],
        compiler_params=_params(("arbitrary", "arbitrary"), 56),
        name="ffn_up",
    )(h2, w_up, w_up, conv_w, conv_w, conv_b, conv_b)


def _ffn_down_kernel(g_ref, w_ref, x1_ref, gf_ref, gpost_ref, o_ref):
    post = gf_ref[0] * gpost_ref[...]
    for r0 in range(0, g_ref.shape[0], FFN_DOWN_SUBTILE):
        rows = slice(r0, r0 + FFN_DOWN_SUBTILE)
        f = jnp.dot(g_ref[rows], w_ref[...], preferred_element_type=F32)
        o_ref[rows] = x1_ref[rows] + _rms(f, post)


FFN_DOWN_SUBTILE = 256


def _ffn_down(g, w, x1, gate_f, g_post, seq):
    t = g.shape[0]
    tm = 512
    tpb = seq // tm
    return pl.pallas_call(
        _ffn_down_kernel,
        grid=(t // tm,),
        in_specs=[pl.BlockSpec((tm, D_FF), lambda i: (i, 0)),
                  pl.BlockSpec((D_FF, D_MODEL), lambda i: (0, 0), pipeline_mode=pl.Buffered(1)),
                  pl.BlockSpec((tm, D_MODEL), lambda i: (i, 0)),
                  pl.BlockSpec((1, 1, D_MODEL), lambda i: (i // tpb, 0, 0)),
                  pl.BlockSpec((1, D_MODEL), lambda i: (0, 0))],
        out_specs=pl.BlockSpec((tm, D_MODEL), lambda i: (i, 0)),
        out_shape=jax.ShapeDtypeStruct((t, D_MODEL), F32),
        compiler_params=_params(("arbitrary",), 56),
        name="ffn_down",
    )(g, w, x1, gate_f, g_post)


def _swap_halves(w):
    half = w.shape[-1] // 2
    return jnp.concatenate([w[..., half:], w[..., :half]], axis=-1)


def _w_in_special(w_t):
    half = MLA_ROPE // 2
    k_rope = w_t[_SRC_KROPE:_SRC_KROPE + MLA_ROPE]
    g_lr = w_t[_SRC_GLR:_SRC_GLR + GLA_GATE_RANK]
    pad = jnp.zeros((Z_GQ - Z_GLR - 2 * GLA_GATE_RANK, w_t.shape[1]), w_t.dtype)
    return jnp.concatenate([k_rope, k_rope[half:], k_rope[:half], g_lr, g_lr, pad], axis=0)


def _prep_w_q(w_q_up):
    w = w_q_up.reshape(MLA_Q_RANK, MLA_HEADS, MLA_NOPE + MLA_ROPE)
    pe = w[:, :, MLA_NOPE:]
    w = jnp.concatenate([w, _swap_halves(pe)], axis=-1)
    return jnp.transpose(w, (1, 0, 2)).astype(BF16)


def kernel(x, c, positions, w_ada, b_ada, g_pre_mix, w_in, g_q_lat, w_q_up, g_kv_lat, w_kv_up,
           w_gla_gate_up, b_gla_gate, g_gla_out, w_branch_a, w_branch_b, w_mix_out, g_post_mix,
           g_pre_ffn, w_ffn_up, conv_w, conv_b, w_ffn_down, g_post_ffn):
    batch, seq, _ = x.shape
    depth = w_ada.shape[0]
    t = batch * seq
    row = lambda v: v.reshape(1, -1)

    inv = 1.0 / (ROPE_THETA ** (jnp.arange(0, MLA_ROPE, 2, dtype=F32) / MLA_ROPE))
    cos, sin = _rope_tables(positions, inv)

    c_pad = jnp.zeros((SUBLANES, D_MODEL), F32).at[:batch].set(c)
    x2 = x.reshape(t, D_MODEL)
    for l in range(depth):
        mod = _ada(c_pad, w_ada[l], row(b_ada[l]))[:batch]
        shift_m, scale_m, gate_m, shift_f, scale_f, gate_f = (
            m.reshape(batch, 1, D_MODEL) for m in jnp.split(mod, 6, axis=-1))

        h = _norm_mod(x2, row(g_pre_mix[l]), scale_m, shift_m, seq)
        w_in_t = w_in[l].T
        z = _mm_in(h, w_in_t, _w_in_special(w_in_t))

        q, k, v = _mla_proj(z, cos, sin, row(g_q_lat[l]), row(g_kv_lat[l]),
                            _prep_w_q(w_q_up[l]), w_kv_up[l].astype(BF16), batch, seq)
        o_a, w_a_bf, w_b_bf = _attn(q, k, v, [w_branch_a[l], w_branch_b[l]], batch, seq)

        wg_hi, wg_lo = _split_bf16(w_gla_gate_up[l])
        wg = jnp.zeros((LANES, GLA_HEADS * GLA_DK), BF16)
        wg = wg.at[:GLA_GATE_RANK].set(wg_hi).at[GLA_GATE_RANK:2 * GLA_GATE_RANK].set(wg_lo)
        o_b, w_down_bf, w_mix_bf = _gla(z, wg, row(b_gla_gate[l]), row(g_gla_out[l]),
                                        [w_ffn_down[l], w_mix_out[l]], batch, seq)

        x1, h2 = _mix(o_a, o_b, z, w_a_bf, w_b_bf, w_mix_bf, x2, gate_m, row(g_post_mix[l]),
                      row(g_pre_ffn[l]), scale_f, shift_f, seq)

        g = _ffn_up(h2, w_ffn_up[l], conv_w[l], row(conv_b[l]), seq)
        x2 = _ffn_down(g, w_down_bf, x1, gate_f, row(g_post_ffn[l]), seq)
    return x2.reshape(batch, seq, D_MODEL)
```

```python
import functools

import jax
import jax.numpy as jnp
import numpy as np
from jax import lax
from jax.experimental import pallas as pl
from jax.experimental.pallas import tpu as pltpu

F32 = jnp.float32
BF16 = jnp.bfloat16

D_MODEL = 2048
MLA_HEADS = 8
MLA_Q_RANK = 512
MLA_KV_RANK = 256
MLA_NOPE = 128
MLA_ROPE = 64
MLA_V = 128
ROPE_THETA = 10000.0
GLA_HEADS = 4
GLA_DK = 256
GLA_DV = 512
GLA_GATE_RANK = 16
GLA_TAU = 16.0
D_FF = 5632
CONV_WIDTH = 3
EPS = 1e-6

LANES = 128
SUBLANES = 8
MLA_QK_PAD = 256

Z_QLAT = 0
Z_KVLAT = 512
Z_KROPE = 768
Z_GLR = 896
Z_GQ = 1024
Z_GV = 2048
Z_OG = 4096
Z_GATE_A = 6144
Z_GATE_B = 8192
Z_GK = 10240
Z_WIDTH = 11264

_SRC_KROPE = MLA_Q_RANK + MLA_KV_RANK
_SRC_GQ = _SRC_KROPE + MLA_ROPE
_SRC_GK = _SRC_GQ + GLA_HEADS * GLA_DK
_SRC_GV = _SRC_GK + GLA_HEADS * GLA_DK
_SRC_GLR = _SRC_GV + GLA_HEADS * GLA_DV
_SRC_OG = _SRC_GLR + GLA_GATE_RANK
_SRC_GATE_A = _SRC_OG + GLA_HEADS * GLA_DV
_SRC_GATE_B = _SRC_GATE_A + D_MODEL

GLA_CHUNK = 128
GLA_LEVELS = 7

MIB = 1024 * 1024


def _params(semantics, vmem_mib):
    return pltpu.CompilerParams(dimension_semantics=semantics,
                                vmem_limit_bytes=vmem_mib * MIB)


def _rms(x, g):
    return x * lax.rsqrt(jnp.mean(x * x, axis=-1, keepdims=True) + EPS) * g


RIDER_ROWS = 256


def _rider_specs(arrays, step_of):
    specs, windows, start = [], [], 0
    for a in arrays:
        n = a.shape[0] // RIDER_ROWS
        specs.append(pl.BlockSpec(
            (RIDER_ROWS, a.shape[1]),
            lambda *idx, lo=start, n=n: (jnp.clip(step_of(*idx) - lo, 0, n - 1), 0)))
        windows.append((start, start + n))
        start += n
    return specs, windows, start


def _run_riders(step, windows, src_refs, dst_refs):
    for (lo, hi), src, dst in zip(windows, src_refs, dst_refs):
        @pl.when((step >= lo) & (step < hi))
        def _(src=src, dst=dst):
            dst[...] = src[...].astype(BF16)


def _ada_kernel(c_ref, w_ref, b_ref, o_ref):
    c = c_ref[...]
    ca = c * jax.nn.sigmoid(c)
    o_ref[...] = jnp.dot(ca.astype(BF16), w_ref[...].astype(BF16),
                         preferred_element_type=F32) + b_ref[...]


def _ada(c_pad, w, b):
    n = w.shape[1]
    tn = 1024
    return pl.pallas_call(
        _ada_kernel,
        grid=(n // tn,),
        in_specs=[pl.BlockSpec((SUBLANES, D_MODEL), lambda j: (0, 0)),
                  pl.BlockSpec((D_MODEL, tn), lambda j: (0, j)),
                  pl.BlockSpec((1, tn), lambda j: (0, j))],
        out_specs=pl.BlockSpec((SUBLANES, tn), lambda j: (0, j)),
        out_shape=jax.ShapeDtypeStruct((SUBLANES, n), F32),
        compiler_params=_params(("arbitrary",), 32),
        name="ada",
    )(c_pad, w, b)


def _norm_mod_kernel(x_ref, g_ref, sc_ref, sh_ref, o_ref):
    y = _rms(x_ref[...], g_ref[...] * (1.0 + sc_ref[0]))
    o_ref[...] = (y + sh_ref[0]).astype(o_ref.dtype)


def _norm_mod(x2, g, scale, shift, seq):
    t = x2.shape[0]
    tm = 512
    tpb = seq // tm
    return pl.pallas_call(
        _norm_mod_kernel,
        grid=(t // tm,),
        in_specs=[pl.BlockSpec((tm, D_MODEL), lambda i: (i, 0)),
                  pl.BlockSpec((1, D_MODEL), lambda i: (0, 0)),
                  pl.BlockSpec((1, 1, D_MODEL), lambda i: (i // tpb, 0, 0)),
                  pl.BlockSpec((1, 1, D_MODEL), lambda i: (i // tpb, 0, 0))],
        out_specs=pl.BlockSpec((tm, D_MODEL), lambda i: (i, 0)),
        out_shape=jax.ShapeDtypeStruct((t, D_MODEL), BF16),
        compiler_params=_params(("arbitrary",), 32),
        name="norm_mod",
    )(x2, g, scale, shift)


MM_IN_TN = 1024
MM_IN_SRC_ALIGN = 16
_MM_IN_SRC_ROWS = (0,
                   _SRC_GQ,
                   _SRC_GV, _SRC_GV + MM_IN_TN,
                   _SRC_OG, _SRC_OG + MM_IN_TN,
                   _SRC_GATE_A, _SRC_GATE_A + MM_IN_TN,
                   _SRC_GATE_B, _SRC_GATE_B + MM_IN_TN,
                   _SRC_GK)


def _mm_in_kernel(src_ref, h_ref, w_ref, wsp_ref, o_ref, w_bf):
    del src_ref
    j, i = pl.program_id(0), pl.program_id(1)

    @pl.when(i == 0)
    def _():
        w_bf[...] = w_ref[...].astype(BF16)

    @pl.when((i == 0) & (j == 0))
    def _():
        w_bf[Z_KROPE:Z_GQ] = wsp_ref[...].astype(BF16)

    for r0 in range(0, o_ref.shape[0], MM_IN_SUBTILE):
        rows = slice(r0, r0 + MM_IN_SUBTILE)
        o_ref[rows] = lax.dot_general(h_ref[rows], w_bf[...], (((1,), (1,)), ((), ())),
                                      preferred_element_type=F32).astype(o_ref.dtype)


MM_IN_SUBTILE = 1024


def _mm_in(h, w_t, w_special):
    t = h.shape[0]
    tm, tn = 2048, MM_IN_TN
    grid_spec = pltpu.PrefetchScalarGridSpec(
        num_scalar_prefetch=1,
        grid=(Z_WIDTH // tn, t // tm),
        in_specs=[pl.BlockSpec((tm, D_MODEL), lambda j, i, src: (i, 0)),
                  pl.BlockSpec((pl.Element(tn), pl.Element(D_MODEL)),
                               lambda j, i, src: (src[j] * MM_IN_SRC_ALIGN, 0)),
                  pl.BlockSpec((Z_GQ - Z_KROPE, D_MODEL), lambda j, i, src: (0, 0))],
        out_specs=pl.BlockSpec((tm, tn), lambda j, i, src: (i, j)),
        scratch_shapes=[pltpu.VMEM((tn, D_MODEL), BF16)])
    return pl.pallas_call(
        _mm_in_kernel,
        grid_spec=grid_spec,
        out_shape=jax.ShapeDtypeStruct((t, Z_WIDTH), BF16),
        compiler_params=_params(("arbitrary", "arbitrary"), 56),
        name="mm_in",
    )(jnp.asarray([r // MM_IN_SRC_ALIGN for r in _MM_IN_SRC_ROWS], jnp.int32), h, w_t, w_special)


ROPE_HALF = MLA_ROPE // 2
ROPE_PACK = LANES // ROPE_HALF


def _rope_kernel(pos_ref, inv_ref, cos_ref, sin_ref):
    ang = pos_ref[...].astype(F32) * inv_ref[...]
    cos_ref[...] = jnp.cos(ang)
    sin_ref[...] = jnp.sin(ang)


def _rope_tables(positions, inv):
    t = positions.size
    rows = t // ROPE_PACK
    pos_dense = jnp.repeat(positions.reshape(t), ROPE_HALF).reshape(rows, LANES)
    inv_dense = jnp.tile(inv, ROPE_PACK).reshape(1, LANES)
    tr = 512
    spec = pl.BlockSpec((tr, LANES), lambda i: (i, 0))
    cos, sin = pl.pallas_call(
        _rope_kernel,
        grid=(rows // tr,),
        in_specs=[spec, pl.BlockSpec((1, LANES), lambda i: (0, 0))],
        out_specs=[spec, spec],
        out_shape=[jax.ShapeDtypeStruct((rows, LANES), F32)] * 2,
        compiler_params=_params(("arbitrary",), 32),
        name="rope_tables",
    )(pos_dense, inv_dense)
    return cos.reshape(t, ROPE_HALF), sin.reshape(t, ROPE_HALF)


def _mla_proj_kernel(ql_ref, kvl_ref, kr_ref, cos_ref, sin_ref,
                     gq_ref, gkv_ref, wq_ref, wkv_ref, q_ref, k_ref, v_ref):
    c, s = cos_ref[...], sin_ref[...]
    zeros = jnp.zeros((c.shape[0], LANES - MLA_ROPE), F32)
    cosm = jnp.concatenate([c, c, zeros], axis=1)
    sinm = jnp.concatenate([-s, s, zeros], axis=1)

    def rope(r):
        return r * cosm + pltpu.roll(r, MLA_ROPE, 1) * sinm

    scale = (MLA_NOPE + MLA_ROPE) ** -0.5 * np.log2(np.e)
    qn =_rms(ql_ref[...].astype(F32), gq_ref[...]).astype(BF16)
    for h in range(MLA_HEADS):
        r = jnp.dot(qn, wq_ref[h], preferred_element_type=F32) * scale
        q_ref[0, h, :, 0:MLA_NOPE] = r[:, 0:MLA_NOPE].astype(BF16)
        q_ref[0, h, :, MLA_NOPE:MLA_QK_PAD] = rope(r[:, MLA_NOPE:MLA_QK_PAD]).astype(BF16)

    kvn = _rms(kvl_ref[...].astype(F32), gkv_ref[...]).astype(BF16)
    kv = jnp.dot(kvn, wkv_ref[...], preferred_element_type=F32).astype(BF16)
    kpe = rope(kr_ref[...].astype(F32)).astype(BF16)
    hw = MLA_NOPE + MLA_V
    for h in range(MLA_HEADS):
        k_ref[0, h, :, 0:MLA_NOPE] = kv[:, h * hw:h * hw + MLA_NOPE]
        k_ref[0, h, :, MLA_NOPE:MLA_QK_PAD] = kpe
        v_ref[0, h] = kv[:, h * hw + MLA_NOPE:(h + 1) * hw]


def _mla_proj(z, cos, sin, gq, gkv, wq, wkv, batch, seq):
    tm = 1024
    tpb = seq // tm
    qk_shape = jax.ShapeDtypeStruct((batch, MLA_HEADS, seq, MLA_QK_PAD), BF16)
    v_shape = jax.ShapeDtypeStruct((batch, MLA_HEADS, seq, MLA_V), BF16)
    const2 = lambda i: (0, 0)
    qk_spec = pl.BlockSpec((1, MLA_HEADS, tm, MLA_QK_PAD), lambda i: (i // tpb, 0, i % tpb, 0))
    return pl.pallas_call(
        _mla_proj_kernel,
        grid=(batch * tpb,),
        in_specs=[pl.BlockSpec((tm, MLA_Q_RANK), lambda i: (i, Z_QLAT // MLA_Q_RANK)),
                  pl.BlockSpec((tm, MLA_KV_RANK), lambda i: (i, Z_KVLAT // MLA_KV_RANK)),
                  pl.BlockSpec((tm, LANES), lambda i: (i, Z_KROPE // LANES)),
                  pl.BlockSpec((tm, ROPE_HALF), lambda i: (i, 0)),
                  pl.BlockSpec((tm, ROPE_HALF), lambda i: (i, 0)),
                  pl.BlockSpec((1, MLA_Q_RANK), const2),
                  pl.BlockSpec((1, MLA_KV_RANK), const2),
                  pl.BlockSpec((MLA_HEADS, MLA_Q_RANK, MLA_QK_PAD), lambda i: (0, 0, 0)),
                  pl.BlockSpec((MLA_KV_RANK, MLA_HEADS * (MLA_NOPE + MLA_V)), const2)],
        out_specs=[qk_spec, qk_spec,
                   pl.BlockSpec((1, MLA_HEADS, tm, MLA_V), lambda i: (i // tpb, 0, i % tpb, 0))],
        out_shape=[qk_shape, qk_shape, v_shape],
        compiler_params=_params(("arbitrary",), 48),
        name="mla_proj",
    )(z, z, z, cos, sin, gq, gkv, wq, wkv)


ATTN_KV_BLOCK = 1024
ATTN_Q_BLOCKS = 2
ATTN_DIAG_CHAINS = 2


def _attn_kernel(q_ref, k_ref, v_ref, *refs, rider_windows):
    nr = len(rider_windows)
    rider_src, o_ref, rider_dst = refs[:nr], refs[nr], refs[nr + 1:2 * nr + 1]
    m_sc, l_sc, acc_sc = refs[2 * nr + 1:]
    blk = ATTN_KV_BLOCK
    nt = (((1,), (1,)), ((), ()))
    qi = pl.program_id(2)
    step = ((pl.program_id(0) * pl.num_programs(1) + pl.program_id(1)) * pl.num_programs(2) + qi)
    _run_riders(step, rider_windows, rider_src, rider_dst)
    m_sc[...] = jnp.full(m_sc.shape, -jnp.inf, F32)
    l_sc[...] = jnp.zeros(l_sc.shape, F32)
    acc_sc[...] = jnp.zeros(acc_sc.shape, F32)

    def update(rows, s, v):
        m_prev = m_sc[rows]
        m_new = jnp.maximum(m_prev, jnp.max(s, axis=-1, keepdims=True))
        alpha = jnp.exp2(m_prev - m_new)
        ps = [jnp.exp2(s[:, c * LANES:(c + 1) * LANES] - m_new)
              for c in range(s.shape[1] // LANES)]
        l_sc[rows] = alpha * l_sc[rows] + functools.reduce(lambda a, b: a + b, ps)
        p = jnp.concatenate(ps, axis=1).astype(BF16)
        acc_sc[rows] = alpha * acc_sc[rows] + jnp.dot(p, v, preferred_element_type=F32)
        m_sc[rows] = m_new

    def full_block(rows, ks):
        k = k_ref[0, 0, pl.ds(ks, blk), :]
        v = v_ref[0, 0, pl.ds(ks, blk), :]
        update(rows, lax.dot_general(q_ref[0, 0, rows], k, nt, preferred_element_type=F32), v)

    def diag_block(row0, ks):
        sub = blk // ATTN_DIAG_CHAINS
        for r in range(ATTN_DIAG_CHAINS):
            n = (r + 1) * sub
            rows = slice(row0 + r * sub, row0 + n)
            k = k_ref[0, 0, pl.ds(ks, n), :]
            v = v_ref[0, 0, pl.ds(ks, n), :]
            s = lax.dot_general(q_ref[0, 0, rows], k, nt, preferred_element_type=F32)
            row = lax.broadcasted_iota(jnp.int32, s.shape, 0) + r * sub
            col = lax.broadcasted_iota(jnp.int32, s.shape, 1)
            update(rows, jnp.where(col <= row, s, -jnp.inf), v)

    def body(kb, carry):
        full_block(slice(0, ATTN_Q_BLOCKS * blk), pl.multiple_of(kb * blk, blk))
        return carry

    lax.fori_loop(0, qi * ATTN_Q_BLOCKS, body, 0)

    base = qi * ATTN_Q_BLOCKS * blk
    for a in range(ATTN_Q_BLOCKS):
        for kb in range(a):
            full_block(slice(a * blk, (a + 1) * blk), pl.multiple_of(base + kb * blk, blk))
        diag_block(a * blk, pl.multiple_of(base + a * blk, blk))
    l = jnp.sum(l_sc[...], axis=-1, keepdims=True)
    o_ref[...] = (acc_sc[...] / l).astype(o_ref.dtype)


def _attn(q, k, v, riders, batch, seq):
    blk = ATTN_Q_BLOCKS * ATTN_KV_BLOCK
    nq = seq // blk
    rider_specs, windows, steps = _rider_specs(
        riders, lambda b, h, i: (b * MLA_HEADS + h) * nq + i)
    assert steps <= batch * MLA_HEADS * nq
    return pl.pallas_call(
        functools.partial(_attn_kernel, rider_windows=windows),
        grid=(batch, MLA_HEADS, nq),
        in_specs=[pl.BlockSpec((1, 1, blk, MLA_QK_PAD), lambda b, h, i: (b, h, i, 0)),
                  pl.BlockSpec((1, 1, seq, MLA_QK_PAD), lambda b, h, i: (b, h, 0, 0)),
                  pl.BlockSpec((1, 1, seq, MLA_V), lambda b, h, i: (b, h, 0, 0))] + rider_specs,
        out_specs=[pl.BlockSpec((blk, MLA_V), lambda b, h, i: (b * nq + i, h))] + rider_specs,
        out_shape=([jax.ShapeDtypeStruct((batch * seq, MLA_HEADS * MLA_V), BF16)]
                   + [jax.ShapeDtypeStruct(w.shape, BF16) for w in riders]),
        scratch_shapes=[pltpu.VMEM((blk, LANES), F32), pltpu.VMEM((blk, LANES), F32),
                        pltpu.VMEM((blk, MLA_V), F32)],
        compiler_params=_params(("arbitrary", "arbitrary", "arbitrary"), 48),
        name="attn",
    )(q, k, v, *riders)


def _gla_constants():
    c = GLA_CHUNK
    idx = np.arange(c)
    w = np.zeros((GLA_LEVELS + 2, c, c), np.float32)
    masks = np.zeros((GLA_LEVELS + 1, c, c), np.float32)
    for l in range(GLA_LEVELS):
        half = 1 << l
        blk = idx // (2 * half)
        mid = blk * 2 * half + half - 1
        upper = (idx % (2 * half)) >= half
        t = idx[None, :]
        up_rows = (t > mid[:, None]) & (t <= idx[:, None])
        lo_rows = (t > idx[:, None]) & (t <= mid[:, None])
        w[l] = np.where(upper[:, None], up_rows, lo_rows)
        masks[l] = (blk[:, None] == blk[None, :]) & upper[:, None] & (~upper[None, :])
    w[GLA_LEVELS] = idx[None, :] <= idx[:, None]
    w[GLA_LEVELS + 1] = idx[None, :] > idx[:, None]
    masks[GLA_LEVELS] = np.eye(c)
    w = w.reshape(-1, c)
    return np.concatenate([w, w], axis=1), masks


def _split_bf16(x):
    hi = x.astype(BF16)
    return hi, (x - hi.astype(F32)).astype(BF16)


GLA_CHUNKS_PER_STEP = 2
GLA_STAGGER = 2


def _gla_kernel(q_ref, k_ref, v_ref, lr_ref, og_ref, wg_ref, bg_ref, go_ref, ws_ref, mask_ref,
                *refs, rider_windows):
    nr = len(rider_windows)
    rider_src, o_ref, rider_dst, st_ref = refs[:nr], refs[nr], refs[nr + 1:2 * nr + 1], refs[-1]
    c = GLA_CHUNK
    dk, dv = GLA_DK, GLA_DV
    nt = (((1,), (1,)), ((), ()))
    _run_riders(pl.program_id(0) * pl.num_programs(1) + pl.program_id(1), rider_windows,
                rider_src, rider_dst)

    @pl.when(pl.program_id(1) == 0)
    def _():
        st_ref[...] = jnp.zeros(st_ref.shape, F32)

    def rows(ci):
        return slice(ci * c, (ci + 1) * c)

    def qk(ci, h):
        q = q_ref[rows(ci), h * dk:(h + 1) * dk] * (dk ** -0.5)
        return q, k_ref[rows(ci), h * dk:(h + 1) * dk]

    def gate(ci, h):
        lr = lr_ref[rows(ci)]
        x = (jnp.dot(lr, wg_ref[:, h * dk:(h + 1) * dk], preferred_element_type=F32)
             + bg_ref[:, h * dk:(h + 1) * dk])
        log2_a = (-(jnp.maximum(-x, 0.0) + jnp.log1p(jnp.exp(-jnp.abs(x))))
                  * (np.log2(np.e) / GLA_TAU))
        return jnp.concatenate(_split_bf16(log2_a), axis=0)

    def exponents(log2_a):
        return jnp.dot(ws_ref[...], log2_a, preferred_element_type=F32)

    def level_operands(ci, h, pb):
        q, k = qk(ci, h)
        ops = [(q * pb[l * c:(l + 1) * c], k * pb[l * c:(l + 1) * c]) for l in range(GLA_LEVELS)]
        return ops + [(q, k)]

    def level_scores(ops):
        return [lax.dot_general(a, b, nt, preferred_element_type=F32) for a, b in ops]

    def masked_sum(scores):
        attn = mask_ref[0] * scores[0]
        for l in range(1, GLA_LEVELS + 1):
            attn = attn + mask_ref[l] * scores[l]
        return attn.astype(BF16)

    def mix(ci, h, p, pb, attn):
        q, k = qk(ci, h)
        v = v_ref[rows(ci), h * dv:(h + 1) * dv]
        eb = pb[GLA_LEVELS * c:(GLA_LEVELS + 1) * c]
        ebl = pb[(GLA_LEVELS + 1) * c:(GLA_LEVELS + 2) * c]
        st = st_ref[h]
        o = (lax.dot_general(q * eb, st.astype(BF16), nt, preferred_element_type=F32)
             + jnp.dot(attn, v, preferred_element_type=F32))
        v_t = v.astype(F32).T.astype(BF16)
        decay = p[(GLA_LEVELS + 1) * c - 1:(GLA_LEVELS + 1) * c]
        st_ref[h] = st * decay + jnp.dot(v_t, k * ebl, preferred_element_type=F32)
        return o

    def emit(ci, h, o):
        y = _rms(o, go_ref[...])
        og = og_ref[rows(ci), h * dv:(h + 1) * dv].astype(F32)
        o_ref[rows(ci), h * dv:(h + 1) * dv] = (y * (og * jax.nn.sigmoid(og))).astype(o_ref.dtype)

    heads = range(GLA_HEADS)

    def chunk_stages(ci):
        log2_a = [gate(ci, h) for h in heads]
        yield
        expo = [exponents(x) for x in log2_a]
        p = [jnp.exp2(e) for e in expo]
        pb = [x.astype(BF16) for x in p]
        yield
        operands = [level_operands(ci, h, pb[h]) for h in heads]
        scores = [level_scores(ops) for ops in operands]
        yield
        attn = [masked_sum(s) for s in scores]
        out = [mix(ci, h, p[h], pb[h], attn[h]) for h in heads]
        yield
        for h in heads:
            emit(ci, h, out[h])
        yield

    live = []
    pending = [chunk_stages(ci) for ci in range(GLA_CHUNKS_PER_STEP)]
    tick = 0
    while pending or live:
        if pending and tick % GLA_STAGGER == 0:
            live.append(pending.pop(0))
        for gen in list(live):
            if next(gen, "done") == "done":
                live.remove(gen)
        tick += 1


def _gla(z, wg, bg, go, riders, batch, seq):
    c = GLA_CHUNK * GLA_CHUNKS_PER_STEP
    nc = seq // c
    ws, masks = _gla_constants()
    hk, hv = GLA_HEADS * GLA_DK, GLA_HEADS * GLA_DV
    const2 = lambda b, i: (0, 0)
    rider_specs, windows, steps = _rider_specs(riders, lambda b, i: b * nc + i)
    assert steps <= batch * nc
    return pl.pallas_call(
        functools.partial(_gla_kernel, rider_windows=windows),
        grid=(batch, nc),
        in_specs=[pl.BlockSpec((c, hk), lambda b, i: (b * nc + i, Z_GQ // hk)),
                  pl.BlockSpec((c, hk), lambda b, i: (b * nc + i, Z_GK // hk)),
                  pl.BlockSpec((c, hv), lambda b, i: (b * nc + i, Z_GV // hv)),
                  pl.BlockSpec((c, LANES), lambda b, i: (b * nc + i, Z_GLR // LANES)),
                  pl.BlockSpec((c, hv), lambda b, i: (b * nc + i, Z_OG // hv)),
                  pl.BlockSpec((LANES, hk), const2),
                  pl.BlockSpec((1, hk), const2),
                  pl.BlockSpec((1, GLA_DV), const2),
                  pl.BlockSpec(ws.shape, const2),
                  pl.BlockSpec(masks.shape, lambda b, i: (0, 0, 0))] + rider_specs,
        out_specs=[pl.BlockSpec((c, hv), lambda b, i: (b * nc + i, 0))] + rider_specs,
        out_shape=([jax.ShapeDtypeStruct((batch * seq, hv), BF16)]
                   + [jax.ShapeDtypeStruct(w.shape, BF16) for w in riders]),
        scratch_shapes=[pltpu.VMEM((GLA_HEADS, GLA_DV, GLA_DK), F32)],
        compiler_params=_params(("arbitrary", "arbitrary"), 48),
        name="gla",
    )(z, z, z, z, z, wg, bg, go, jnp.asarray(ws, BF16), jnp.asarray(masks), *riders)


MIX_SUBTILE = 128


def _mix_kernel(oa_ref, ob_ref, ga_ref, gb_ref, wa_ref, wb_ref, wm_ref, x_ref, gm_ref, gpost_ref,
                gpre_ref, sc_ref, sh_ref, x1_ref, h2_ref):
    post = gm_ref[0] * gpost_ref[...]
    pre = gpre_ref[...] * (1.0 + sc_ref[0])

    def sub_tile(rows):
        ya = jnp.dot(oa_ref[rows], wa_ref[...], preferred_element_type=F32)
        yb = jnp.dot(ob_ref[rows], wb_ref[...], preferred_element_type=F32)
        yield
        merged = (jax.nn.sigmoid(ga_ref[rows].astype(F32)) * ya
                  + jax.nn.sigmoid(gb_ref[rows].astype(F32)) * yb).astype(BF16)
        yield
        y = jnp.dot(merged, wm_ref[...], preferred_element_type=F32)
        yield
        x1 = x_ref[rows] + _rms(y, post)
        x1_ref[rows] = x1
        h2_ref[rows] = (_rms(x1, pre) + sh_ref[0]).astype(h2_ref.dtype)
        yield

    pending = [sub_tile(slice(r0, r0 + MIX_SUBTILE))
               for r0 in range(0, x_ref.shape[0], MIX_SUBTILE)]
    live = []
    while pending or live:
        if pending:
            live.append(pending.pop(0))
        for gen in list(live):
            if next(gen, "done") == "done":
                live.remove(gen)


def _mix(oa, ob, z, wa, wb, wm, x2, gate_m, g_post, g_pre, scale_f, shift_f, seq):
    t = oa.shape[0]
    tm = 256
    tpb = seq // tm
    row = pl.BlockSpec((tm, D_MODEL), lambda i: (i, 0))
    vec = pl.BlockSpec((1, D_MODEL), lambda i: (0, 0))
    per_batch = pl.BlockSpec((1, 1, D_MODEL), lambda i: (i // tpb, 0, 0))
    resident = lambda w: pl.BlockSpec(w.shape, lambda i: (0, 0), pipeline_mode=pl.Buffered(1))
    return pl.pallas_call(
        _mix_kernel,
        grid=(t // tm,),
        in_specs=[pl.BlockSpec((tm, oa.shape[1]), lambda i: (i, 0)),
                  pl.BlockSpec((tm, ob.shape[1]), lambda i: (i, 0)),
                  pl.BlockSpec((tm, D_MODEL), lambda i: (i, Z_GATE_A // D_MODEL)),
                  pl.BlockSpec((tm, D_MODEL), lambda i: (i, Z_GATE_B // D_MODEL)),
                  resident(wa), resident(wb), resident(wm),
                  row, per_batch, vec, vec, per_batch, per_batch],
        out_specs=[row, row],
        out_shape=[jax.ShapeDtypeStruct((t, D_MODEL), F32),
                   jax.ShapeDtypeStruct((t, D_MODEL), BF16)],
        compiler_params=_params(("arbitrary",), 56),
        name="mix",
    )(oa, ob, z, z, wa, wb, wm, x2, gate_m, g_post, g_pre, scale_f, shift_f)


def _ffn_up_kernel(h_ref, wa_ref, wv_ref, cwa_ref, cwv_ref, cba_ref, cbv_ref, o_ref,
                   w_bf, u_sc, *, tiles_per_batch):
    i = pl.program_id(1)
    tm, tn = o_ref.shape
    head, body, tail = slice(0, SUBLANES), slice(SUBLANES, SUBLANES + tm), slice(tm, tm + SUBLANES)

    @pl.when(i == 0)
    def _():
        w_bf[:, 0:tn] = wa_ref[...].astype(BF16)
        w_bf[:, tn:2 * tn] = wv_ref[...].astype(BF16)

    @pl.when(i % tiles_per_batch == 0)
    def _():
        u_sc[head] = jnp.zeros((SUBLANES, u_sc.shape[1]), F32)

    @pl.when(i % tiles_per_batch != 0)
    def _():
        u_sc[head] = u_sc[tail]

    u_sc[body] = jnp.dot(h_ref[...], w_bf[...], preferred_element_type=F32)

    def conv(cols, cw_ref, cb_ref):
        cw = cw_ref[...]
        acc = cb_ref[...] + u_sc[body, cols] * cw[CONV_WIDTH - 1:CONV_WIDTH]
        for tap in range(1, CONV_WIDTH):
            acc = acc + (u_sc[SUBLANES - tap:SUBLANES - tap + tm, cols]
                         * cw[CONV_WIDTH - 1 - tap:CONV_WIDTH - tap])
        return acc

    a = conv(slice(0, tn), cwa_ref, cba_ref)
    val = conv(slice(tn, 2 * tn), cwv_ref, cbv_ref)
    o_ref[...] = (jax.nn.gelu(a, approximate=True) * val).astype(o_ref.dtype)


def _ffn_up(h2, w_up, conv_w, conv_b, seq):
    t = h2.shape[0]
    tm, tn = 1024, 512
    nj = D_FF // tn
    return pl.pallas_call(
        functools.partial(_ffn_up_kernel, tiles_per_batch=seq // tm),
        grid=(nj, t // tm),
        in_specs=[pl.BlockSpec((tm, D_MODEL), lambda j, i: (i, 0)),
                  pl.BlockSpec((D_MODEL, tn), lambda j, i: (0, j)),
                  pl.BlockSpec((D_MODEL, tn), lambda j, i: (0, nj + j)),
                  pl.BlockSpec((CONV_WIDTH, tn), lambda j, i: (0, j)),
                  pl.BlockSpec((CONV_WIDTH, tn), lambda j, i: (0, nj + j)),
                  pl.BlockSpec((1, tn), lambda j, i: (0, j)),
                  pl.BlockSpec((1, tn), lambda j, i: (0, nj + j))],
        out_specs=pl.BlockSpec((tm, tn), lambda j, i: (i, j)),
        out_shape=jax.ShapeDtypeStruct((t, D_FF), BF16),
        scratch_shapes=[pltpu.VMEM((D_MODEL, 2 * tn), BF16),
                        pltpu.VMEM((tm + SUBLANES, 2 * tn), F32)],
        compiler_params=_params(("arbitrary", "arbitrary"), 56),
        name="ffn_up",
    )(h2, w_up, w_up, conv_w, conv_w, conv_b, conv_b)


def _ffn_down_kernel(g_ref, w_ref, x1_ref, gf_ref, gpost_ref, o_ref):
    f = jnp.dot(g_ref[...], w_ref[...], preferred_element_type=F32)
    o_ref[...] = x1_ref[...] + _rms(f, gf_ref[0] * gpost_ref[...])


def _ffn_down(g, w, x1, gate_f, g_post, seq):
    t = g.shape[0]
    tm = 256
    tpb = seq // tm
    return pl.pallas_call(
        _ffn_down_kernel,
        grid=(t // tm,),
        in_specs=[pl.BlockSpec((tm, D_FF), lambda i: (i, 0)),
                  pl.BlockSpec((D_FF, D_MODEL), lambda i: (0, 0), pipeline_mode=pl.Buffered(1)),
                  pl.BlockSpec((tm, D_MODEL), lambda i: (i, 0)),
                  pl.BlockSpec((1, 1, D_MODEL), lambda i: (i // tpb, 0, 0)),
                  pl.BlockSpec((1, D_MODEL), lambda i: (0, 0))],
        out_specs=pl.BlockSpec((tm, D_MODEL), lambda i: (i, 0)),
        out_shape=jax.ShapeDtypeStruct((t, D_MODEL), F32),
        compiler_params=_params(("arbitrary",), 56),
        name="ffn_down",
    )(g, w, x1, gate_f, g_post)


def _swap_halves(w):
    half = w.shape[-1] // 2
    return jnp.concatenate([w[..., half:], w[..., :half]], axis=-1)


def _w_in_special(w_t):
    half = MLA_ROPE // 2
    k_rope = w_t[_SRC_KROPE:_SRC_KROPE + MLA_ROPE]
    g_lr = w_t[_SRC_GLR:_SRC_GLR + GLA_GATE_RANK]
    pad = jnp.zeros((Z_GQ - Z_GLR - 2 * GLA_GATE_RANK, w_t.shape[1]), w_t.dtype)
    return jnp.concatenate([k_rope, k_rope[half:], k_rope[:half], g_lr, g_lr, pad], axis=0)


def _prep_w_q(w_q_up):
    w = w_q_up.reshape(MLA_Q_RANK, MLA_HEADS, MLA_NOPE + MLA_ROPE)
    pe = w[:, :, MLA_NOPE:]
    w = jnp.concatenate([w, _swap_halves(pe)], axis=-1)
    return jnp.transpose(w, (1, 0, 2)).astype(BF16)


def kernel(x, c, positions, w_ada, b_ada, g_pre_mix, w_in, g_q_lat, w_q_up, g_kv_lat, w_kv_up,
           w_gla_gate_up, b_gla_gate, g_gla_out, w_branch_a, w_branch_b, w_mix_out, g_post_mix,
           g_pre_ffn, w_ffn_up, conv_w, conv_b, w_ffn_down, g_post_ffn):
    batch, seq, _ = x.shape
    depth = w_ada.shape[0]
    t = batch * seq
    row = lambda v: v.reshape(1, -1)

    inv = 1.0 / (ROPE_THETA ** (jnp.arange(0, MLA_ROPE, 2, dtype=F32) / MLA_ROPE))
    cos, sin = _rope_tables(positions, inv)

    c_pad = jnp.zeros((SUBLANES, D_MODEL), F32).at[:batch].set(c)
    x2 = x.reshape(t, D_MODEL)
    for l in range(depth):
        mod = _ada(c_pad, w_ada[l], row(b_ada[l]))[:batch]
        shift_m, scale_m, gate_m, shift_f, scale_f, gate_f = (
            m.reshape(batch, 1, D_MODEL) for m in jnp.split(mod, 6, axis=-1))

        h = _norm_mod(x2, row(g_pre_mix[l]), scale_m, shift_m, seq)
        w_in_t = w_in[l].T
        z = _mm_in(h, w_in_t, _w_in_special(w_in_t))

        q, k, v = _mla_proj(z, cos, sin, row(g_q_lat[l]), row(g_kv_lat[l]),
                            _prep_w_q(w_q_up[l]), w_kv_up[l].astype(BF16), batch, seq)
        o_a, w_a_bf, w_b_bf = _attn(q, k, v, [w_branch_a[l], w_branch_b[l]], batch, seq)

        wg_hi, wg_lo = _split_bf16(w_gla_gate_up[l])
        wg = jnp.zeros((LANES, GLA_HEADS * GLA_DK), BF16)
        wg = wg.at[:GLA_GATE_RANK].set(wg_hi).at[GLA_GATE_RANK:2 * GLA_GATE_RANK].set(wg_lo)
        o_b, w_down_bf, w_mix_bf = _gla(z, wg, row(b_gla_gate[l]), row(g_gla_out[l]),
                                        [w_ffn_down[l], w_mix_out[l]], batch, seq)

        x1, h2 = _mix(o_a, o_b, z, w_a_bf, w_b_bf, w_mix_bf, x2, gate_m, row(g_post_mix[l]),
                      row(g_pre_ffn[l]), scale_f, shift_f, seq)

        g = _ffn_up(h2, w_ffn_up[l], conv_w[l], row(conv_b[l]), seq)
        x2 = _ffn_down(g, w_down_bf, x1, gate_f, row(g_post_ffn[l]), seq)
    return x2.reshape(batch, seq, D_MODEL)
```

```python
import functools

import jax
import jax.numpy as jnp
import numpy as np
from jax import lax
from jax.experimental import pallas as pl
from jax.experimental.pallas import tpu as pltpu

F32 = jnp.float32
BF16 = jnp.bfloat16

D_MODEL = 2048
MLA_HEADS = 8
MLA_Q_RANK = 512
MLA_KV_RANK = 256
MLA_NOPE = 128
MLA_ROPE = 64
MLA_V = 128
ROPE_THETA = 10000.0
GLA_HEADS = 4
GLA_DK = 256
GLA_DV = 512
GLA_GATE_RANK = 16
GLA_TAU = 16.0
D_FF = 5632
CONV_WIDTH = 3
EPS = 1e-6

LANES = 128
SUBLANES = 8
MLA_QK_PAD = 256

Z_QLAT = 0
Z_KVLAT = 512
Z_KROPE = 768
Z_GLR = 896
Z_GQ = 1024
Z_GV = 2048
Z_OG = 4096
Z_GATE_A = 6144
Z_GATE_B = 8192
Z_GK = 10240
Z_WIDTH = 11264

_SRC_KROPE = MLA_Q_RANK + MLA_KV_RANK
_SRC_GQ = _SRC_KROPE + MLA_ROPE
_SRC_GK = _SRC_GQ + GLA_HEADS * GLA_DK
_SRC_GV = _SRC_GK + GLA_HEADS * GLA_DK
_SRC_GLR = _SRC_GV + GLA_HEADS * GLA_DV
_SRC_OG = _SRC_GLR + GLA_GATE_RANK
_SRC_GATE_A = _SRC_OG + GLA_HEADS * GLA_DV
_SRC_GATE_B = _SRC_GATE_A + D_MODEL

GLA_CHUNK = 128
GLA_LEVELS = 7

MIB = 1024 * 1024


def _params(semantics, vmem_mib):
    return pltpu.CompilerParams(dimension_semantics=semantics,
                                vmem_limit_bytes=vmem_mib * MIB)


def _rms(x, g):
    return x * lax.rsqrt(jnp.mean(x * x, axis=-1, keepdims=True) + EPS) * g


def _rider_specs(arrays, step_of, rows):
    specs, windows, start = [], [], 0
    for a in arrays:
        n = a.shape[0] // rows
        specs.append(pl.BlockSpec(
            (rows, a.shape[1]),
            lambda *idx, lo=start, n=n: (jnp.clip(step_of(*idx) - lo, 0, n - 1), 0)))
        windows.append((start, start + n))
        start += n
    return specs, windows, start


def _run_riders(step, windows, src_refs, dst_refs):
    for (lo, hi), src, dst in zip(windows, src_refs, dst_refs):
        @pl.when((step >= lo) & (step < hi))
        def _(src=src, dst=dst):
            dst[...] = src[...].astype(BF16)


def _ada_kernel(c_ref, w_ref, b_ref, o_ref):
    c = c_ref[...]
    ca = c * jax.nn.sigmoid(c)
    o_ref[...] = jnp.dot(ca.astype(BF16), w_ref[...].astype(BF16),
                         preferred_element_type=F32) + b_ref[...]


def _ada(c_pad, w, b):
    n = w.shape[1]
    tn = 1024
    return pl.pallas_call(
        _ada_kernel,
        grid=(n // tn,),
        in_specs=[pl.BlockSpec((SUBLANES, D_MODEL), lambda j: (0, 0)),
                  pl.BlockSpec((D_MODEL, tn), lambda j: (0, j)),
                  pl.BlockSpec((1, tn), lambda j: (0, j))],
        out_specs=pl.BlockSpec((SUBLANES, tn), lambda j: (0, j)),
        out_shape=jax.ShapeDtypeStruct((SUBLANES, n), F32),
        compiler_params=_params(("arbitrary",), 32),
        name="ada",
    )(c_pad, w, b)


def _norm_mod_kernel(x_ref, g_ref, sc_ref, sh_ref, o_ref):
    y = _rms(x_ref[...], g_ref[...] * (1.0 + sc_ref[0]))
    o_ref[...] = (y + sh_ref[0]).astype(o_ref.dtype)


def _norm_mod(x2, g, scale, shift, seq):
    t = x2.shape[0]
    tm = 512
    tpb = seq // tm
    return pl.pallas_call(
        _norm_mod_kernel,
        grid=(t // tm,),
        in_specs=[pl.BlockSpec((tm, D_MODEL), lambda i: (i, 0)),
                  pl.BlockSpec((1, D_MODEL), lambda i: (0, 0)),
                  pl.BlockSpec((1, 1, D_MODEL), lambda i: (i // tpb, 0, 0)),
                  pl.BlockSpec((1, 1, D_MODEL), lambda i: (i // tpb, 0, 0))],
        out_specs=pl.BlockSpec((tm, D_MODEL), lambda i: (i, 0)),
        out_shape=jax.ShapeDtypeStruct((t, D_MODEL), BF16),
        compiler_params=_params(("arbitrary",), 32),
        name="norm_mod",
    )(x2, g, scale, shift)


MM_IN_TN = 1024
MM_IN_SRC_ALIGN = 16
_MM_IN_SRC_ROWS = (0,
                   _SRC_GQ,
                   _SRC_GV, _SRC_GV + MM_IN_TN,
                   _SRC_OG, _SRC_OG + MM_IN_TN,
                   _SRC_GATE_A, _SRC_GATE_A + MM_IN_TN,
                   _SRC_GATE_B, _SRC_GATE_B + MM_IN_TN,
                   _SRC_GK)


def _mm_in_kernel(src_ref, h_ref, w_ref, wsp_ref, o_ref, w_bf):
    del src_ref
    j, i = pl.program_id(0), pl.program_id(1)

    @pl.when(i == 0)
    def _():
        w_bf[...] = w_ref[...].astype(BF16)

    @pl.when((i == 0) & (j == 0))
    def _():
        w_bf[Z_KROPE:Z_GQ] = wsp_ref[...].astype(BF16)

    for r0 in range(0, o_ref.shape[0], MM_IN_SUBTILE):
        rows = slice(r0, r0 + MM_IN_SUBTILE)
        o_ref[rows] = lax.dot_general(h_ref[rows], w_bf[...], (((1,), (1,)), ((), ())),
                                      preferred_element_type=F32).astype(o_ref.dtype)


MM_IN_SUBTILE = 1024


def _mm_in(h, w_t, w_special):
    t = h.shape[0]
    tm, tn = 2048, MM_IN_TN
    grid_spec = pltpu.PrefetchScalarGridSpec(
        num_scalar_prefetch=1,
        grid=(Z_WIDTH // tn, t // tm),
        in_specs=[pl.BlockSpec((tm, D_MODEL), lambda j, i, src: (i, 0)),
                  pl.BlockSpec((pl.Element(tn), pl.Element(D_MODEL)),
                               lambda j, i, src: (src[j] * MM_IN_SRC_ALIGN, 0)),
                  pl.BlockSpec((Z_GQ - Z_KROPE, D_MODEL), lambda j, i, src: (0, 0))],
        out_specs=pl.BlockSpec((tm, tn), lambda j, i, src: (i, j)),
        scratch_shapes=[pltpu.VMEM((tn, D_MODEL), BF16)])
    return pl.pallas_call(
        _mm_in_kernel,
        grid_spec=grid_spec,
        out_shape=jax.ShapeDtypeStruct((t, Z_WIDTH), BF16),
        compiler_params=_params(("arbitrary", "arbitrary"), 56),
        name="mm_in",
    )(jnp.asarray([r // MM_IN_SRC_ALIGN for r in _MM_IN_SRC_ROWS], jnp.int32), h, w_t, w_special)


ROPE_HALF = MLA_ROPE // 2
ROPE_PACK = LANES // ROPE_HALF


def _rope_kernel(pos_ref, inv_ref, cos_ref, sin_ref):
    ang = pos_ref[...].astype(F32) * inv_ref[...]
    cos_ref[...] = jnp.cos(ang)
    sin_ref[...] = jnp.sin(ang)


def _rope_tables(positions, inv):
    t = positions.size
    rows = t // ROPE_PACK
    pos_dense = jnp.repeat(positions.reshape(t), ROPE_HALF).reshape(rows, LANES)
    inv_dense = jnp.tile(inv, ROPE_PACK).reshape(1, LANES)
    tr = 512
    spec = pl.BlockSpec((tr, LANES), lambda i: (i, 0))
    cos, sin = pl.pallas_call(
        _rope_kernel,
        grid=(rows // tr,),
        in_specs=[spec, pl.BlockSpec((1, LANES), lambda i: (0, 0))],
        out_specs=[spec, spec],
        out_shape=[jax.ShapeDtypeStruct((rows, LANES), F32)] * 2,
        compiler_params=_params(("arbitrary",), 32),
        name="rope_tables",
    )(pos_dense, inv_dense)
    return cos.reshape(t, ROPE_HALF), sin.reshape(t, ROPE_HALF)


MLA_PROJ_RIDER_ROWS = 256


def _mla_proj_kernel(ql_ref, kvl_ref, kr_ref, cos_ref, sin_ref,
                     gq_ref, gkv_ref, wq_ref, wkv_ref, *refs, rider_windows):
    nr = len(rider_windows)
    rider_src, (q_ref, k_ref, v_ref), rider_dst = refs[:nr], refs[nr:nr + 3], refs[nr + 3:]
    _run_riders(pl.program_id(0), rider_windows, rider_src, rider_dst)
    c, s = cos_ref[...], sin_ref[...]
    zeros = jnp.zeros((c.shape[0], LANES - MLA_ROPE), F32)
    cosm = jnp.concatenate([c, c, zeros], axis=1)
    sinm = jnp.concatenate([-s, s, zeros], axis=1)

    def rope(r):
        return r * cosm + pltpu.roll(r, MLA_ROPE, 1) * sinm

    scale = (MLA_NOPE + MLA_ROPE) ** -0.5 * np.log2(np.e)
    qn =_rms(ql_ref[...].astype(F32), gq_ref[...]).astype(BF16)
    for h in range(MLA_HEADS):
        r = jnp.dot(qn, wq_ref[h], preferred_element_type=F32) * scale
        q_ref[0, h, :, 0:MLA_NOPE] = r[:, 0:MLA_NOPE].astype(BF16)
        q_ref[0, h, :, MLA_NOPE:MLA_QK_PAD] = rope(r[:, MLA_NOPE:MLA_QK_PAD]).astype(BF16)

    kvn = _rms(kvl_ref[...].astype(F32), gkv_ref[...]).astype(BF16)
    kv = jnp.dot(kvn, wkv_ref[...], preferred_element_type=F32).astype(BF16)
    kpe = rope(kr_ref[...].astype(F32)).astype(BF16)
    hw = MLA_NOPE + MLA_V
    for h in range(MLA_HEADS):
        k_ref[0, h, :, 0:MLA_NOPE] = kv[:, h * hw:h * hw + MLA_NOPE]
        k_ref[0, h, :, MLA_NOPE:MLA_QK_PAD] = kpe
        v_ref[0, h] = kv[:, h * hw + MLA_NOPE:(h + 1) * hw]


def _mla_proj(z, cos, sin, gq, gkv, wq, wkv, riders, batch, seq):
    tm = 1024
    tpb = seq // tm
    rider_specs, windows, steps = _rider_specs(riders, lambda i: i, MLA_PROJ_RIDER_ROWS)
    assert steps <= batch * tpb
    qk_shape = jax.ShapeDtypeStruct((batch, MLA_HEADS, seq, MLA_QK_PAD), BF16)
    v_shape = jax.ShapeDtypeStruct((batch, MLA_HEADS, seq, MLA_V), BF16)
    const2 = lambda i: (0, 0)
    qk_spec = pl.BlockSpec((1, MLA_HEADS, tm, MLA_QK_PAD), lambda i: (i // tpb, 0, i % tpb, 0))
    return pl.pallas_call(
        functools.partial(_mla_proj_kernel, rider_windows=windows),
        grid=(batch * tpb,),
        in_specs=[pl.BlockSpec((tm, MLA_Q_RANK), lambda i: (i, Z_QLAT // MLA_Q_RANK)),
                  pl.BlockSpec((tm, MLA_KV_RANK), lambda i: (i, Z_KVLAT // MLA_KV_RANK)),
                  pl.BlockSpec((tm, LANES), lambda i: (i, Z_KROPE // LANES)),
                  pl.BlockSpec((tm, ROPE_HALF), lambda i: (i, 0)),
                  pl.BlockSpec((tm, ROPE_HALF), lambda i: (i, 0)),
                  pl.BlockSpec((1, MLA_Q_RANK), const2),
                  pl.BlockSpec((1, MLA_KV_RANK), const2),
                  pl.BlockSpec((MLA_HEADS, MLA_Q_RANK, MLA_QK_PAD), lambda i: (0, 0, 0)),
                  pl.BlockSpec((MLA_KV_RANK, MLA_HEADS * (MLA_NOPE + MLA_V)), const2)]
        + rider_specs,
        out_specs=[qk_spec, qk_spec,
                   pl.BlockSpec((1, MLA_HEADS, tm, MLA_V), lambda i: (i // tpb, 0, i % tpb, 0))]
        + rider_specs,
        out_shape=([qk_shape, qk_shape, v_shape]
                   + [jax.ShapeDtypeStruct(w.shape, BF16) for w in riders]),
        compiler_params=_params(("arbitrary",), 48),
        name="mla_proj",
    )(z, z, z, cos, sin, gq, gkv, wq, wkv, *riders)


ATTN_KV_BLOCK = 1024
ATTN_Q_BLOCKS = 4
ATTN_DIAG_CHAINS = 2
ATTN_RIDER_ROWS = 256


def _attn_kernel(q_ref, k_ref, v_ref, *refs, rider_windows):
    nr = len(rider_windows)
    rider_src, o_ref, rider_dst = refs[:nr], refs[nr], refs[nr + 1:2 * nr + 1]
    m_sc, l_sc, acc_sc = refs[2 * nr + 1:]
    blk = ATTN_KV_BLOCK
    nt = (((1,), (1,)), ((), ()))
    qi = pl.program_id(2)
    step = ((pl.program_id(0) * pl.num_programs(1) + pl.program_id(1)) * pl.num_programs(2) + qi)
    _run_riders(step, rider_windows, rider_src, rider_dst)
    m_sc[...] = jnp.full(m_sc.shape, -jnp.inf, F32)
    l_sc[...] = jnp.zeros(l_sc.shape, F32)
    acc_sc[...] = jnp.zeros(acc_sc.shape, F32)

    def update(rows, s, v):
        m_prev = m_sc[rows]
        m_new = jnp.maximum(m_prev, jnp.max(s, axis=-1, keepdims=True))
        alpha = jnp.exp2(m_prev - m_new)
        ps = [jnp.exp2(s[:, c * LANES:(c + 1) * LANES] - m_new)
              for c in range(s.shape[1] // LANES)]
        l_sc[rows] = alpha * l_sc[rows] + functools.reduce(lambda a, b: a + b, ps)
        p = jnp.concatenate(ps, axis=1).astype(BF16)
        acc_sc[rows] = alpha * acc_sc[rows] + jnp.dot(p, v, preferred_element_type=F32)
        m_sc[rows] = m_new

    def full_block(rows, ks):
        k = k_ref[0, 0, pl.ds(ks, blk), :]
        v = v_ref[0, 0, pl.ds(ks, blk), :]
        update(rows, lax.dot_general(q_ref[0, 0, rows], k, nt, preferred_element_type=F32), v)

    def diag_block(row0, ks):
        sub = blk // ATTN_DIAG_CHAINS
        for r in range(ATTN_DIAG_CHAINS):
            n = (r + 1) * sub
            rows = slice(row0 + r * sub, row0 + n)
            k = k_ref[0, 0, pl.ds(ks, n), :]
            v = v_ref[0, 0, pl.ds(ks, n), :]
            s = lax.dot_general(q_ref[0, 0, rows], k, nt, preferred_element_type=F32)
            row = lax.broadcasted_iota(jnp.int32, s.shape, 0) + r * sub
            col = lax.broadcasted_iota(jnp.int32, s.shape, 1)
            update(rows, jnp.where(col <= row, s, -jnp.inf), v)

    def body(kb, carry):
        full_block(slice(0, ATTN_Q_BLOCKS * blk), pl.multiple_of(kb * blk, blk))
        return carry

    lax.fori_loop(0, qi * ATTN_Q_BLOCKS, body, 0)

    base = qi * ATTN_Q_BLOCKS * blk
    for a in range(ATTN_Q_BLOCKS):
        for kb in range(a):
            full_block(slice(a * blk, (a + 1) * blk), pl.multiple_of(base + kb * blk, blk))
        diag_block(a * blk, pl.multiple_of(base + a * blk, blk))
    l = jnp.sum(l_sc[...], axis=-1, keepdims=True)
    o_ref[...] = (acc_sc[...] / l).astype(o_ref.dtype)


def _attn(q, k, v, riders, batch, seq):
    blk = ATTN_Q_BLOCKS * ATTN_KV_BLOCK
    nq = seq // blk
    rider_specs, windows, steps = _rider_specs(
        riders, lambda b, h, i: (b * MLA_HEADS + h) * nq + i, ATTN_RIDER_ROWS)
    assert steps <= batch * MLA_HEADS * nq
    return pl.pallas_call(
        functools.partial(_attn_kernel, rider_windows=windows),
        grid=(batch, MLA_HEADS, nq),
        in_specs=[pl.BlockSpec((1, 1, blk, MLA_QK_PAD), lambda b, h, i: (b, h, i, 0)),
                  pl.BlockSpec((1, 1, seq, MLA_QK_PAD), lambda b, h, i: (b, h, 0, 0)),
                  pl.BlockSpec((1, 1, seq, MLA_V), lambda b, h, i: (b, h, 0, 0))] + rider_specs,
        out_specs=[pl.BlockSpec((blk, MLA_V), lambda b, h, i: (b * nq + i, h))] + rider_specs,
        out_shape=([jax.ShapeDtypeStruct((batch * seq, MLA_HEADS * MLA_V), BF16)]
                   + [jax.ShapeDtypeStruct(w.shape, BF16) for w in riders]),
        scratch_shapes=[pltpu.VMEM((blk, LANES), F32), pltpu.VMEM((blk, LANES), F32),
                        pltpu.VMEM((blk, MLA_V), F32)],
        compiler_params=_params(("arbitrary", "arbitrary", "arbitrary"), 58),
        name="attn",
    )(q, k, v, *riders)


def _gla_constants():
    c = GLA_CHUNK
    idx = np.arange(c)
    w = np.zeros((GLA_LEVELS + 2, c, c), np.float32)
    masks = np.zeros((GLA_LEVELS + 1, c, c), np.float32)
    for l in range(GLA_LEVELS):
        half = 1 << l
        blk = idx // (2 * half)
        mid = blk * 2 * half + half - 1
        upper = (idx % (2 * half)) >= half
        t = idx[None, :]
        up_rows = (t > mid[:, None]) & (t <= idx[:, None])
        lo_rows = (t > idx[:, None]) & (t <= mid[:, None])
        w[l] = np.where(upper[:, None], up_rows, lo_rows)
        masks[l] = (blk[:, None] == blk[None, :]) & upper[:, None] & (~upper[None, :])
    w[GLA_LEVELS] = idx[None, :] <= idx[:, None]
    w[GLA_LEVELS + 1] = idx[None, :] > idx[:, None]
    masks[GLA_LEVELS] = np.eye(c)
    w = w.reshape(-1, c)
    return np.concatenate([w, w], axis=1), masks


def _split_bf16(x):
    hi = x.astype(BF16)
    return hi, (x - hi.astype(F32)).astype(BF16)


GLA_CHUNKS_PER_STEP = 4
GLA_RIDER_ROWS = 512
GLA_STAGGER = 2


def _gla_kernel(q_ref, k_ref, v_ref, lr_ref, og_ref, wg_ref, bg_ref, go_ref, ws_ref, mask_ref,
                *refs, rider_windows):
    nr = len(rider_windows)
    rider_src, o_ref, rider_dst, st_ref = refs[:nr], refs[nr], refs[nr + 1:2 * nr + 1], refs[-1]
    c = GLA_CHUNK
    dk, dv = GLA_DK, GLA_DV
    nt = (((1,), (1,)), ((), ()))
    _run_riders(pl.program_id(0) * pl.num_programs(1) + pl.program_id(1), rider_windows,
                rider_src, rider_dst)

    @pl.when(pl.program_id(1) == 0)
    def _():
        st_ref[...] = jnp.zeros(st_ref.shape, F32)

    def rows(ci):
        return slice(ci * c, (ci + 1) * c)

    def qk(ci, h):
        q = q_ref[rows(ci), h * dk:(h + 1) * dk] * (dk ** -0.5)
        return q, k_ref[rows(ci), h * dk:(h + 1) * dk]

    def gate(ci, h):
        lr = lr_ref[rows(ci)]
        x = (jnp.dot(lr, wg_ref[:, h * dk:(h + 1) * dk], preferred_element_type=F32)
             + bg_ref[:, h * dk:(h + 1) * dk])
        log2_a = (-(jnp.maximum(-x, 0.0) + jnp.log1p(jnp.exp(-jnp.abs(x))))
                  * (np.log2(np.e) / GLA_TAU))
        return jnp.concatenate(_split_bf16(log2_a), axis=0)

    def exponents(log2_a):
        return jnp.dot(ws_ref[...], log2_a, preferred_element_type=F32)

    def level_operands(ci, h, pb):
        q, k = qk(ci, h)
        ops = [(q * pb[l * c:(l + 1) * c], k * pb[l * c:(l + 1) * c]) for l in range(GLA_LEVELS)]
        return ops + [(q, k)]

    def level_scores(ops):
        return [lax.dot_general(a, b, nt, preferred_element_type=F32) for a, b in ops]

    def masked_sum(scores):
        attn = mask_ref[0] * scores[0]
        for l in range(1, GLA_LEVELS + 1):
            attn = attn + mask_ref[l] * scores[l]
        return attn.astype(BF16)

    def mix(ci, h, p, pb, attn):
        q, k = qk(ci, h)
        v = v_ref[rows(ci), h * dv:(h + 1) * dv]
        eb = pb[GLA_LEVELS * c:(GLA_LEVELS + 1) * c]
        ebl = pb[(GLA_LEVELS + 1) * c:(GLA_LEVELS + 2) * c]
        st = st_ref[h]
        o = (lax.dot_general(q * eb, st.astype(BF16), nt, preferred_element_type=F32)
             + jnp.dot(attn, v, preferred_element_type=F32))
        v_t = v.astype(F32).T.astype(BF16)
        decay = p[(GLA_LEVELS + 1) * c - 1:(GLA_LEVELS + 1) * c]
        st_ref[h] = st * decay + jnp.dot(v_t, k * ebl, preferred_element_type=F32)
        return o

    def emit(ci, h, o):
        y = _rms(o, go_ref[...])
        og = og_ref[rows(ci), h * dv:(h + 1) * dv].astype(F32)
        o_ref[rows(ci), h * dv:(h + 1) * dv] = (y * (og * jax.nn.sigmoid(og))).astype(o_ref.dtype)

    heads = range(GLA_HEADS)

    def chunk_stages(ci):
        log2_a = [gate(ci, h) for h in heads]
        yield
        expo = [exponents(x) for x in log2_a]
        p = [jnp.exp2(e) for e in expo]
        pb = [x.astype(BF16) for x in p]
        yield
        operands = [level_operands(ci, h, pb[h]) for h in heads]
        scores = [level_scores(ops) for ops in operands]
        yield
        attn = [masked_sum(s) for s in scores]
        out = [mix(ci, h, p[h], pb[h], attn[h]) for h in heads]
        yield
        for h in heads:
            emit(ci, h, out[h])
        yield

    live = []
    pending = [chunk_stages(ci) for ci in range(GLA_CHUNKS_PER_STEP)]
    tick = 0
    while pending or live:
        if pending and tick % GLA_STAGGER == 0:
            live.append(pending.pop(0))
        for gen in list(live):
            if next(gen, "done") == "done":
                live.remove(gen)
        tick += 1


def _gla(z, wg, bg, go, riders, batch, seq):
    c = GLA_CHUNK * GLA_CHUNKS_PER_STEP
    nc = seq // c
    ws, masks = _gla_constants()
    hk, hv = GLA_HEADS * GLA_DK, GLA_HEADS * GLA_DV
    const2 = lambda b, i: (0, 0)
    rider_specs, windows, steps = _rider_specs(riders, lambda b, i: b * nc + i, GLA_RIDER_ROWS)
    assert steps <= batch * nc
    return pl.pallas_call(
        functools.partial(_gla_kernel, rider_windows=windows),
        grid=(batch, nc),
        in_specs=[pl.BlockSpec((c, hk), lambda b, i: (b * nc + i, Z_GQ // hk)),
                  pl.BlockSpec((c, hk), lambda b, i: (b * nc + i, Z_GK // hk)),
                  pl.BlockSpec((c, hv), lambda b, i: (b * nc + i, Z_GV // hv)),
                  pl.BlockSpec((c, LANES), lambda b, i: (b * nc + i, Z_GLR // LANES)),
                  pl.BlockSpec((c, hv), lambda b, i: (b * nc + i, Z_OG // hv)),
                  pl.BlockSpec((LANES, hk), const2),
                  pl.BlockSpec((1, hk), const2),
                  pl.BlockSpec((1, GLA_DV), const2),
                  pl.BlockSpec(ws.shape, const2),
                  pl.BlockSpec(masks.shape, lambda b, i: (0, 0, 0))] + rider_specs,
        out_specs=[pl.BlockSpec((c, hv), lambda b, i: (b * nc + i, 0))] + rider_specs,
        out_shape=([jax.ShapeDtypeStruct((batch * seq, hv), BF16)]
                   + [jax.ShapeDtypeStruct(w.shape, BF16) for w in riders]),
        scratch_shapes=[pltpu.VMEM((GLA_HEADS, GLA_DV, GLA_DK), F32)],
        compiler_params=_params(("arbitrary", "arbitrary"), 58),
        name="gla",
    )(z, z, z, z, z, wg, bg, go, jnp.asarray(ws, BF16), jnp.asarray(masks), *riders)


MIX_SUBTILE = 128


def _mix_kernel(oa_ref, ob_ref, ga_ref, gb_ref, wa_ref, wb_ref, wm_ref, x_ref, gm_ref, gpost_ref,
                gpre_ref, sc_ref, sh_ref, x1_ref, h2_ref):
    post = gm_ref[0] * gpost_ref[...]
    pre = gpre_ref[...] * (1.0 + sc_ref[0])

    def sub_tile(rows):
        ya = jnp.dot(oa_ref[rows], wa_ref[...], preferred_element_type=F32)
        yb = jnp.dot(ob_ref[rows], wb_ref[...], preferred_element_type=F32)
        yield
        merged = (jax.nn.sigmoid(ga_ref[rows].astype(F32)) * ya
                  + jax.nn.sigmoid(gb_ref[rows].astype(F32)) * yb).astype(BF16)
        yield
        y = jnp.dot(merged, wm_ref[...], preferred_element_type=F32)
        yield
        x1 = x_ref[rows] + _rms(y, post)
        x1_ref[rows] = x1
        h2_ref[rows] = (_rms(x1, pre) + sh_ref[0]).astype(h2_ref.dtype)
        yield

    pending = [sub_tile(slice(r0, r0 + MIX_SUBTILE))
               for r0 in range(0, x_ref.shape[0], MIX_SUBTILE)]
    live = []
    while pending or live:
        if pending:
            live.append(pending.pop(0))
        for gen in list(live):
            if next(gen, "done") == "done":
                live.remove(gen)


def _mix(oa, ob, z, wa, wb, wm, x2, gate_m, g_post, g_pre, scale_f, shift_f, seq):
    t = oa.shape[0]
    tm = 256
    tpb = seq // tm
    row = pl.BlockSpec((tm, D_MODEL), lambda i: (i, 0))
    vec = pl.BlockSpec((1, D_MODEL), lambda i: (0, 0))
    per_batch = pl.BlockSpec((1, 1, D_MODEL), lambda i: (i // tpb, 0, 0))
    resident = lambda w: pl.BlockSpec(w.shape, lambda i: (0, 0), pipeline_mode=pl.Buffered(1))
    return pl.pallas_call(
        _mix_kernel,
        grid=(t // tm,),
        in_specs=[pl.BlockSpec((tm, oa.shape[1]), lambda i: (i, 0)),
                  pl.BlockSpec((tm, ob.shape[1]), lambda i: (i, 0)),
                  pl.BlockSpec((tm, D_MODEL), lambda i: (i, Z_GATE_A // D_MODEL)),
                  pl.BlockSpec((tm, D_MODEL), lambda i: (i, Z_GATE_B // D_MODEL)),
                  resident(wa), resident(wb), resident(wm),
                  row, per_batch, vec, vec, per_batch, per_batch],
        out_specs=[row, row],
        out_shape=[jax.ShapeDtypeStruct((t, D_MODEL), F32),
                   jax.ShapeDtypeStruct((t, D_MODEL), BF16)],
        compiler_params=_params(("arbitrary",), 56),
        name="mix",
    )(oa, ob, z, z, wa, wb, wm, x2, gate_m, g_post, g_pre, scale_f, shift_f)


def _ffn_up_kernel(h_ref, wa_ref, wv_ref, cwa_ref, cwv_ref, cba_ref, cbv_ref, o_ref,
                   w_bf, u_sc, *, tiles_per_batch):
    i = pl.program_id(1)
    tm, tn = o_ref.shape
    head, body, tail = slice(0, SUBLANES), slice(SUBLANES, SUBLANES + tm), slice(tm, tm + SUBLANES)

    @pl.when(i == 0)
    def _():
        w_bf[:, 0:tn] = wa_ref[...].astype(BF16)
        w_bf[:, tn:2 * tn] = wv_ref[...].astype(BF16)

    @pl.when(i % tiles_per_batch == 0)
    def _():
        u_sc[head] = jnp.zeros((SUBLANES, u_sc.shape[1]), F32)

    @pl.when(i % tiles_per_batch != 0)
    def _():
        u_sc[head] = u_sc[tail]

    u_sc[body] = jnp.dot(h_ref[...], w_bf[...], preferred_element_type=F32)

    def conv(cols, cw_ref, cb_ref):
        cw = cw_ref[...]
        acc = cb_ref[...] + u_sc[body, cols] * cw[CONV_WIDTH - 1:CONV_WIDTH]
        for tap in range(1, CONV_WIDTH):
            acc = acc + (u_sc[SUBLANES - tap:SUBLANES - tap + tm, cols]
                         * cw[CONV_WIDTH - 1 - tap:CONV_WIDTH - tap])
        return acc

    a = conv(slice(0, tn), cwa_ref, cba_ref)
    val = conv(slice(tn, 2 * tn), cwv_ref, cbv_ref)
    o_ref[...] = (jax.nn.gelu(a, approximate=True) * val).astype(o_ref.dtype)


def _ffn_up(h2, w_up, conv_w, conv_b, seq):
    t = h2.shape[0]
    tm, tn = 1024, 512
    nj = D_FF // tn
    return pl.pallas_call(
        functools.partial(_ffn_up_kernel, tiles_per_batch=seq // tm),
        grid=(nj, t // tm),
        in_specs=[pl.BlockSpec((tm, D_MODEL), lambda j, i: (i, 0)),
                  pl.BlockSpec((D_MODEL, tn), lambda j, i: (0, j)),
                  pl.BlockSpec((D_MODEL, tn), lambda j, i: (0, nj + j)),
                  pl.BlockSpec((CONV_WIDTH, tn), lambda j, i: (0, j)),
                  pl.BlockSpec((CONV_WIDTH, tn), lambda j, i: (0, nj + j)),
                  pl.BlockSpec((1, tn), lambda j, i: (0, j)),
                  pl.BlockSpec((1, tn), lambda j, i: (0, nj + j))],
        out_specs=pl.BlockSpec((tm, tn), lambda j, i: (i, j)),
        out_shape=jax.ShapeDtypeStruct((t, D_FF), BF16),
        scratch_shapes=[pltpu.VMEM((D_MODEL, 2 * tn), BF16),
                        pltpu.VMEM((tm + SUBLANES, 2 * tn), F32)],
        compiler_params=_params(("arbitrary", "arbitrary"), 56),
        name="ffn_up",
    )(h2, w_up, w_up, conv_w, conv_w, conv_b, conv_b)


def _ffn_down_kernel(g_ref, w_ref, x1_ref, gf_ref, gpost_ref, o_ref):
    f = jnp.dot(g_ref[...], w_ref[...], preferred_element_type=F32)
    o_ref[...] = x1_ref[...] + _rms(f, gf_ref[0] * gpost_ref[...])


def _ffn_down(g, w, x1, gate_f, g_post, seq):
    t = g.shape[0]
    tm = 256
    tpb = seq // tm
    return pl.pallas_call(
        _ffn_down_kernel,
        grid=(t // tm,),
        in_specs=[pl.BlockSpec((tm, D_FF), lambda i: (i, 0)),
                  pl.BlockSpec((D_FF, D_MODEL), lambda i: (0, 0), pipeline_mode=pl.Buffered(1)),
                  pl.BlockSpec((tm, D_MODEL), lambda i: (i, 0)),
                  pl.BlockSpec((1, 1, D_MODEL), lambda i: (i // tpb, 0, 0)),
                  pl.BlockSpec((1, D_MODEL), lambda i: (0, 0))],
        out_specs=pl.BlockSpec((tm, D_MODEL), lambda i: (i, 0)),
        out_shape=jax.ShapeDtypeStruct((t, D_MODEL), F32),
        compiler_params=_params(("arbitrary",), 56),
        name="ffn_down",
    )(g, w, x1, gate_f, g_post)


def _swap_halves(w):
    half = w.shape[-1] // 2
    return jnp.concatenate([w[..., half:], w[..., :half]], axis=-1)


def _w_in_special(w_t):
    half = MLA_ROPE // 2
    k_rope = w_t[_SRC_KROPE:_SRC_KROPE + MLA_ROPE]
    g_lr = w_t[_SRC_GLR:_SRC_GLR + GLA_GATE_RANK]
    pad = jnp.zeros((Z_GQ - Z_GLR - 2 * GLA_GATE_RANK, w_t.shape[1]), w_t.dtype)
    return jnp.concatenate([k_rope, k_rope[half:], k_rope[:half], g_lr, g_lr, pad], axis=0)


def _prep_w_q(w_q_up):
    w = w_q_up.reshape(MLA_Q_RANK, MLA_HEADS, MLA_NOPE + MLA_ROPE)
    pe = w[:, :, MLA_NOPE:]
    w = jnp.concatenate([w, _swap_halves(pe)], axis=-1)
    return jnp.transpose(w, (1, 0, 2)).astype(BF16)


def kernel(x, c, positions, w_ada, b_ada, g_pre_mix, w_in, g_q_lat, w_q_up, g_kv_lat, w_kv_up,
           w_gla_gate_up, b_gla_gate, g_gla_out, w_branch_a, w_branch_b, w_mix_out, g_post_mix,
           g_pre_ffn, w_ffn_up, conv_w, conv_b, w_ffn_down, g_post_ffn):
    batch, seq, _ = x.shape
    depth = w_ada.shape[0]
    t = batch * seq
    row = lambda v: v.reshape(1, -1)

    inv = 1.0 / (ROPE_THETA ** (jnp.arange(0, MLA_ROPE, 2, dtype=F32) / MLA_ROPE))
    cos, sin = _rope_tables(positions, inv)

    c_pad = jnp.zeros((SUBLANES, D_MODEL), F32).at[:batch].set(c)
    x2 = x.reshape(t, D_MODEL)
    for l in range(depth):
        mod = _ada(c_pad, w_ada[l], row(b_ada[l]))[:batch]
        shift_m, scale_m, gate_m, shift_f, scale_f, gate_f = (
            m.reshape(batch, 1, D_MODEL) for m in jnp.split(mod, 6, axis=-1))

        h = _norm_mod(x2, row(g_pre_mix[l]), scale_m, shift_m, seq)
        w_in_t = w_in[l].T
        z = _mm_in(h, w_in_t, _w_in_special(w_in_t))

        q, k, v, w_mix_bf = _mla_proj(z, cos, sin, row(g_q_lat[l]), row(g_kv_lat[l]),
                                      _prep_w_q(w_q_up[l]), w_kv_up[l].astype(BF16),
                                      [w_mix_out[l]], batch, seq)
        o_a, w_a_bf, w_b_bf = _attn(q, k, v, [w_branch_a[l], w_branch_b[l]], batch, seq)

        wg_hi, wg_lo = _split_bf16(w_gla_gate_up[l])
        wg = jnp.zeros((LANES, GLA_HEADS * GLA_DK), BF16)
        wg = wg.at[:GLA_GATE_RANK].set(wg_hi).at[GLA_GATE_RANK:2 * GLA_GATE_RANK].set(wg_lo)
        o_b, w_down_bf = _gla(z, wg, row(b_gla_gate[l]), row(g_gla_out[l]),
                              [w_ffn_down[l]], batch, seq)

        x1, h2 = _mix(o_a, o_b, z, w_a_bf, w_b_bf, w_mix_bf, x2, gate_m, row(g_post_mix[l]),
                      row(g_pre_ffn[l]), scale_f, shift_f, seq)

        g = _ffn_up(h2, w_ffn_up[l], conv_w[l], row(conv_b[l]), seq)
        x2 = _ffn_down(g, w_down_bf, x1, gate_f, row(g_post_ffn[l]), seq)
    return x2.reshape(batch, seq, D_MODEL)
```

```python
import functools

import jax
import jax.numpy as jnp
import numpy as np
from jax import lax
from jax.experimental import pallas as pl
from jax.experimental.pallas import tpu as pltpu

F32 = jnp.float32
BF16 = jnp.bfloat16

D_MODEL = 2048
MLA_HEADS = 8
MLA_Q_RANK = 512
MLA_KV_RANK = 256
MLA_NOPE = 128
MLA_ROPE = 64
MLA_V = 128
ROPE_THETA = 10000.0
GLA_HEADS = 4
GLA_DK = 256
GLA_DV = 512
GLA_GATE_RANK = 16
GLA_TAU = 16.0
D_FF = 5632
CONV_WIDTH = 3
EPS = 1e-6

LANES = 128
SUBLANES = 8
MLA_QK_PAD = 256

Z_QLAT = 0
Z_KVLAT = 512
Z_KROPE = 768
Z_GLR = 896
Z_GQ = 1024
Z_GV = 2048
Z_OG = 4096
Z_GATE_A = 6144
Z_GATE_B = 8192
Z_GK = 10240
Z_WIDTH = 11264

_SRC_KROPE = MLA_Q_RANK + MLA_KV_RANK
_SRC_GQ = _SRC_KROPE + MLA_ROPE
_SRC_GK = _SRC_GQ + GLA_HEADS * GLA_DK
_SRC_GV = _SRC_GK + GLA_HEADS * GLA_DK
_SRC_GLR = _SRC_GV + GLA_HEADS * GLA_DV
_SRC_OG = _SRC_GLR + GLA_GATE_RANK
_SRC_GATE_A = _SRC_OG + GLA_HEADS * GLA_DV
_SRC_GATE_B = _SRC_GATE_A + D_MODEL

GLA_CHUNK = 128
GLA_LEVELS = 7

MIB = 1024 * 1024


def _params(semantics, vmem_mib):
    return pltpu.CompilerParams(dimension_semantics=semantics,
                                vmem_limit_bytes=vmem_mib * MIB)


def _rms(x, g):
    return x * lax.rsqrt(jnp.mean(x * x, axis=-1, keepdims=True) + EPS) * g


RIDER_ROWS = 256


def _rider_specs(arrays, step_of):
    specs, windows, start = [], [], 0
    for a in arrays:
        n = a.shape[0] // RIDER_ROWS
        specs.append(pl.BlockSpec(
            (RIDER_ROWS, a.shape[1]),
            lambda *idx, lo=start, n=n: (jnp.clip(step_of(*idx) - lo, 0, n - 1), 0)))
        windows.append((start, start + n))
        start += n
    return specs, windows, start


def _run_riders(step, windows, src_refs, dst_refs):
    for (lo, hi), src, dst in zip(windows, src_refs, dst_refs):
        @pl.when((step >= lo) & (step < hi))
        def _(src=src, dst=dst):
            dst[...] = src[...].astype(BF16)


def _ada_kernel(c_ref, w_ref, b_ref, o_ref):
    c = c_ref[...]
    ca = c * jax.nn.sigmoid(c)
    o_ref[...] = jnp.dot(ca.astype(BF16), w_ref[...].astype(BF16),
                         preferred_element_type=F32) + b_ref[...]


def _ada(c_pad, w, b):
    n = w.shape[1]
    tn = 1024
    return pl.pallas_call(
        _ada_kernel,
        grid=(n // tn,),
        in_specs=[pl.BlockSpec((SUBLANES, D_MODEL), lambda j: (0, 0)),
                  pl.BlockSpec((D_MODEL, tn), lambda j: (0, j)),
                  pl.BlockSpec((1, tn), lambda j: (0, j))],
        out_specs=pl.BlockSpec((SUBLANES, tn), lambda j: (0, j)),
        out_shape=jax.ShapeDtypeStruct((SUBLANES, n), F32),
        compiler_params=_params(("arbitrary",), 32),
        name="ada",
    )(c_pad, w, b)


def _norm_mod_kernel(x_ref, g_ref, sc_ref, sh_ref, o_ref):
    y = _rms(x_ref[...], g_ref[...] * (1.0 + sc_ref[0]))
    o_ref[...] = (y + sh_ref[0]).astype(o_ref.dtype)


def _norm_mod(x2, g, scale, shift, seq):
    t = x2.shape[0]
    tm = 512
    tpb = seq // tm
    return pl.pallas_call(
        _norm_mod_kernel,
        grid=(t // tm,),
        in_specs=[pl.BlockSpec((tm, D_MODEL), lambda i: (i, 0)),
                  pl.BlockSpec((1, D_MODEL), lambda i: (0, 0)),
                  pl.BlockSpec((1, 1, D_MODEL), lambda i: (i // tpb, 0, 0)),
                  pl.BlockSpec((1, 1, D_MODEL), lambda i: (i // tpb, 0, 0))],
        out_specs=pl.BlockSpec((tm, D_MODEL), lambda i: (i, 0)),
        out_shape=jax.ShapeDtypeStruct((t, D_MODEL), BF16),
        compiler_params=_params(("arbitrary",), 32),
        name="norm_mod",
    )(x2, g, scale, shift)


MM_IN_TN = 1024
MM_IN_SRC_ALIGN = 16
_MM_IN_SRC_ROWS = (0,
                   _SRC_GQ,
                   _SRC_GV, _SRC_GV + MM_IN_TN,
                   _SRC_OG, _SRC_OG + MM_IN_TN,
                   _SRC_GATE_A, _SRC_GATE_A + MM_IN_TN,
                   _SRC_GATE_B, _SRC_GATE_B + MM_IN_TN,
                   _SRC_GK)


def _mm_in_kernel(src_ref, h_ref, w_ref, wsp_ref, o_ref, w_bf):
    del src_ref
    j, i = pl.program_id(0), pl.program_id(1)

    @pl.when(i == 0)
    def _():
        w_bf[...] = w_ref[...].astype(BF16)

    @pl.when((i == 0) & (j == 0))
    def _():
        w_bf[Z_KROPE:Z_GQ] = wsp_ref[...].astype(BF16)

    for r0 in range(0, o_ref.shape[0], MM_IN_SUBTILE):
        rows = slice(r0, r0 + MM_IN_SUBTILE)
        o_ref[rows] = lax.dot_general(h_ref[rows], w_bf[...], (((1,), (1,)), ((), ())),
                                      preferred_element_type=F32).astype(o_ref.dtype)


MM_IN_SUBTILE = 1024


def _mm_in(h, w_t, w_special):
    t = h.shape[0]
    tm, tn = 2048, MM_IN_TN
    grid_spec = pltpu.PrefetchScalarGridSpec(
        num_scalar_prefetch=1,
        grid=(Z_WIDTH // tn, t // tm),
        in_specs=[pl.BlockSpec((tm, D_MODEL), lambda j, i, src: (i, 0)),
                  pl.BlockSpec((pl.Element(tn), pl.Element(D_MODEL)),
                               lambda j, i, src: (src[j] * MM_IN_SRC_ALIGN, 0)),
                  pl.BlockSpec((Z_GQ - Z_KROPE, D_MODEL), lambda j, i, src: (0, 0))],
        out_specs=pl.BlockSpec((tm, tn), lambda j, i, src: (i, j)),
        scratch_shapes=[pltpu.VMEM((tn, D_MODEL), BF16)])
    return pl.pallas_call(
        _mm_in_kernel,
        grid_spec=grid_spec,
        out_shape=jax.ShapeDtypeStruct((t, Z_WIDTH), BF16),
        compiler_params=_params(("arbitrary", "arbitrary"), 56),
        name="mm_in",
    )(jnp.asarray([r // MM_IN_SRC_ALIGN for r in _MM_IN_SRC_ROWS], jnp.int32), h, w_t, w_special)


ROPE_HALF = MLA_ROPE // 2
ROPE_PACK = LANES // ROPE_HALF


def _rope_kernel(pos_ref, inv_ref, cos_ref, sin_ref):
    ang = pos_ref[...].astype(F32) * inv_ref[...]
    cos_ref[...] = jnp.cos(ang)
    sin_ref[...] = jnp.sin(ang)


def _rope_tables(positions, inv):
    t = positions.size
    rows = t // ROPE_PACK
    pos_dense = jnp.repeat(positions.reshape(t), ROPE_HALF).reshape(rows, LANES)
    inv_dense = jnp.tile(inv, ROPE_PACK).reshape(1, LANES)
    tr = 512
    spec = pl.BlockSpec((tr, LANES), lambda i: (i, 0))
    cos, sin = pl.pallas_call(
        _rope_kernel,
        grid=(rows // tr,),
        in_specs=[spec, pl.BlockSpec((1, LANES), lambda i: (0, 0))],
        out_specs=[spec, spec],
        out_shape=[jax.ShapeDtypeStruct((rows, LANES), F32)] * 2,
        compiler_params=_params(("arbitrary",), 32),
        name="rope_tables",
    )(pos_dense, inv_dense)
    return cos.reshape(t, ROPE_HALF), sin.reshape(t, ROPE_HALF)


def _mla_proj_kernel(ql_ref, kvl_ref, kr_ref, cos_ref, sin_ref,
                     gq_ref, gkv_ref, wq_ref, wkv_ref, q_ref, k_ref, v_ref):
    c, s = cos_ref[...], sin_ref[...]
    zeros = jnp.zeros((c.shape[0], LANES - MLA_ROPE), F32)
    cosm = jnp.concatenate([c, c, zeros], axis=1)
    sinm = jnp.concatenate([-s, s, zeros], axis=1)

    def rope(r):
        return r * cosm + pltpu.roll(r, MLA_ROPE, 1) * sinm

    scale = (MLA_NOPE + MLA_ROPE) ** -0.5 * np.log2(np.e)
    qn =_rms(ql_ref[...].astype(F32), gq_ref[...]).astype(BF16)
    for h in range(MLA_HEADS):
        r = jnp.dot(qn, wq_ref[h], preferred_element_type=F32) * scale
        q_ref[0, h, :, 0:MLA_NOPE] = r[:, 0:MLA_NOPE].astype(BF16)
        q_ref[0, h, :, MLA_NOPE:MLA_QK_PAD] = rope(r[:, MLA_NOPE:MLA_QK_PAD]).astype(BF16)

    kvn = _rms(kvl_ref[...].astype(F32), gkv_ref[...]).astype(BF16)
    kv = jnp.dot(kvn, wkv_ref[...], preferred_element_type=F32).astype(BF16)
    kpe = rope(kr_ref[...].astype(F32)).astype(BF16)
    hw = MLA_NOPE + MLA_V
    for h in range(MLA_HEADS):
        k_ref[0, h, :, 0:MLA_NOPE] = kv[:, h * hw:h * hw + MLA_NOPE]
        k_ref[0, h, :, MLA_NOPE:MLA_QK_PAD] = kpe
        v_ref[0, h] = kv[:, h * hw + MLA_NOPE:(h + 1) * hw]


def _mla_proj(z, cos, sin, gq, gkv, wq, wkv, batch, seq):
    tm = 1024
    tpb = seq // tm
    qk_shape = jax.ShapeDtypeStruct((batch, MLA_HEADS, seq, MLA_QK_PAD), BF16)
    v_shape = jax.ShapeDtypeStruct((batch, MLA_HEADS, seq, MLA_V), BF16)
    const2 = lambda i: (0, 0)
    qk_spec = pl.BlockSpec((1, MLA_HEADS, tm, MLA_QK_PAD), lambda i: (i // tpb, 0, i % tpb, 0))
    return pl.pallas_call(
        _mla_proj_kernel,
        grid=(batch * tpb,),
        in_specs=[pl.BlockSpec((tm, MLA_Q_RANK), lambda i: (i, Z_QLAT // MLA_Q_RANK)),
                  pl.BlockSpec((tm, MLA_KV_RANK), lambda i: (i, Z_KVLAT // MLA_KV_RANK)),
                  pl.BlockSpec((tm, LANES), lambda i: (i, Z_KROPE // LANES)),
                  pl.BlockSpec((tm, ROPE_HALF), lambda i: (i, 0)),
                  pl.BlockSpec((tm, ROPE_HALF), lambda i: (i, 0)),
                  pl.BlockSpec((1, MLA_Q_RANK), const2),
                  pl.BlockSpec((1, MLA_KV_RANK), const2),
                  pl.BlockSpec((MLA_HEADS, MLA_Q_RANK, MLA_QK_PAD), lambda i: (0, 0, 0)),
                  pl.BlockSpec((MLA_KV_RANK, MLA_HEADS * (MLA_NOPE + MLA_V)), const2)],
        out_specs=[qk_spec, qk_spec,
                   pl.BlockSpec((1, MLA_HEADS, tm, MLA_V), lambda i: (i // tpb, 0, i % tpb, 0))],
        out_shape=[qk_shape, qk_shape, v_shape],
        compiler_params=_params(("arbitrary",), 48),
        name="mla_proj",
    )(z, z, z, cos, sin, gq, gkv, wq, wkv)


ATTN_KV_BLOCK = 1024
ATTN_Q_BLOCKS = 4
ATTN_DIAG_CHAINS = 2


def _attn_kernel(q_ref, k_ref, v_ref, *refs, rider_windows):
    nr = len(rider_windows)
    rider_src, o_ref, rider_dst = refs[:nr], refs[nr], refs[nr + 1:2 * nr + 1]
    m_sc, l_sc, acc_sc = refs[2 * nr + 1:]
    blk = ATTN_KV_BLOCK
    nt = (((1,), (1,)), ((), ()))
    qi = pl.program_id(2)
    step = ((pl.program_id(0) * pl.num_programs(1) + pl.program_id(1)) * pl.num_programs(2) + qi)
    _run_riders(step, rider_windows, rider_src, rider_dst)
    m_sc[...] = jnp.full(m_sc.shape, -jnp.inf, F32)
    l_sc[...] = jnp.zeros(l_sc.shape, F32)
    acc_sc[...] = jnp.zeros(acc_sc.shape, F32)

    def update(rows, s, v):
        m_prev = m_sc[rows]
        m_new = jnp.maximum(m_prev, jnp.max(s, axis=-1, keepdims=True))
        alpha = jnp.exp2(m_prev - m_new)
        ps = [jnp.exp2(s[:, c * LANES:(c + 1) * LANES] - m_new)
              for c in range(s.shape[1] // LANES)]
        l_sc[rows] = alpha * l_sc[rows] + functools.reduce(lambda a, b: a + b, ps)
        p = jnp.concatenate(ps, axis=1).astype(BF16)
        acc_sc[rows] = alpha * acc_sc[rows] + jnp.dot(p, v, preferred_element_type=F32)
        m_sc[rows] = m_new

    def full_block(rows, ks):
        k = k_ref[0, 0, pl.ds(ks, blk), :]
        v = v_ref[0, 0, pl.ds(ks, blk), :]
        update(rows, lax.dot_general(q_ref[0, 0, rows], k, nt, preferred_element_type=F32), v)

    def diag_block(row0, ks):
        sub = blk // ATTN_DIAG_CHAINS
        for r in range(ATTN_DIAG_CHAINS):
            n = (r + 1) * sub
            rows = slice(row0 + r * sub, row0 + n)
            k = k_ref[0, 0, pl.ds(ks, n), :]
            v = v_ref[0, 0, pl.ds(ks, n), :]
            s = lax.dot_general(q_ref[0, 0, rows], k, nt, preferred_element_type=F32)
            row = lax.broadcasted_iota(jnp.int32, s.shape, 0) + r * sub
            col = lax.broadcasted_iota(jnp.int32, s.shape, 1)
            update(rows, jnp.where(col <= row, s, -jnp.inf), v)

    def body(kb, carry):
        full_block(slice(0, ATTN_Q_BLOCKS * blk), pl.multiple_of(kb * blk, blk))
        return carry

    lax.fori_loop(0, qi * ATTN_Q_BLOCKS, body, 0)

    base = qi * ATTN_Q_BLOCKS * blk
    for a in range(ATTN_Q_BLOCKS):
        for kb in range(a):
            full_block(slice(a * blk, (a + 1) * blk), pl.multiple_of(base + kb * blk, blk))
        diag_block(a * blk, pl.multiple_of(base + a * blk, blk))
    l = jnp.sum(l_sc[...], axis=-1, keepdims=True)
    o_ref[...] = (acc_sc[...] / l).astype(o_ref.dtype)


def _attn(q, k, v, riders, batch, seq):
    blk = ATTN_Q_BLOCKS * ATTN_KV_BLOCK
    nq = seq // blk
    rider_specs, windows, steps = _rider_specs(
        riders, lambda b, h, i: (b * MLA_HEADS + h) * nq + i)
    assert steps <= batch * MLA_HEADS * nq
    return pl.pallas_call(
        functools.partial(_attn_kernel, rider_windows=windows),
        grid=(batch, MLA_HEADS, nq),
        in_specs=[pl.BlockSpec((1, 1, blk, MLA_QK_PAD), lambda b, h, i: (b, h, i, 0)),
                  pl.BlockSpec((1, 1, seq, MLA_QK_PAD), lambda b, h, i: (b, h, 0, 0)),
                  pl.BlockSpec((1, 1, seq, MLA_V), lambda b, h, i: (b, h, 0, 0))] + rider_specs,
        out_specs=[pl.BlockSpec((blk, MLA_V), lambda b, h, i: (b * nq + i, h))] + rider_specs,
        out_shape=([jax.ShapeDtypeStruct((batch * seq, MLA_HEADS * MLA_V), BF16)]
                   + [jax.ShapeDtypeStruct(w.shape, BF16) for w in riders]),
        scratch_shapes=[pltpu.VMEM((blk, LANES), F32), pltpu.VMEM((blk, LANES), F32),
                        pltpu.VMEM((blk, MLA_V), F32)],
        compiler_params=_params(("arbitrary", "arbitrary", "arbitrary"), 58),
        name="attn",
    )(q, k, v, *riders)


def _gla_constants():
    c = GLA_CHUNK
    idx = np.arange(c)
    w = np.zeros((GLA_LEVELS + 2, c, c), np.float32)
    masks = np.zeros((GLA_LEVELS + 1, c, c), np.float32)
    for l in range(GLA_LEVELS):
        half = 1 << l
        blk = idx // (2 * half)
        mid = blk * 2 * half + half - 1
        upper = (idx % (2 * half)) >= half
        t = idx[None, :]
        up_rows = (t > mid[:, None]) & (t <= idx[:, None])
        lo_rows = (t > idx[:, None]) & (t <= mid[:, None])
        w[l] = np.where(upper[:, None], up_rows, lo_rows)
        masks[l] = (blk[:, None] == blk[None, :]) & upper[:, None] & (~upper[None, :])
    w[GLA_LEVELS] = idx[None, :] <= idx[:, None]
    w[GLA_LEVELS + 1] = idx[None, :] > idx[:, None]
    masks[GLA_LEVELS] = np.eye(c)
    w = w.reshape(-1, c)
    return np.concatenate([w, w], axis=1), masks


def _split_bf16(x):
    hi = x.astype(BF16)
    return hi, (x - hi.astype(F32)).astype(BF16)


GLA_CHUNKS_PER_STEP = 2
GLA_STAGGER = 2


def _gla_kernel(q_ref, k_ref, v_ref, lr_ref, og_ref, wg_ref, bg_ref, go_ref, ws_ref, mask_ref,
                *refs, rider_windows):
    nr = len(rider_windows)
    rider_src, o_ref, rider_dst, st_ref = refs[:nr], refs[nr], refs[nr + 1:2 * nr + 1], refs[-1]
    c = GLA_CHUNK
    dk, dv = GLA_DK, GLA_DV
    nt = (((1,), (1,)), ((), ()))
    _run_riders(pl.program_id(0) * pl.num_programs(1) + pl.program_id(1), rider_windows,
                rider_src, rider_dst)

    @pl.when(pl.program_id(1) == 0)
    def _():
        st_ref[...] = jnp.zeros(st_ref.shape, F32)

    def rows(ci):
        return slice(ci * c, (ci + 1) * c)

    def qk(ci, h):
        q = q_ref[rows(ci), h * dk:(h + 1) * dk] * (dk ** -0.5)
        return q, k_ref[rows(ci), h * dk:(h + 1) * dk]

    def gate(ci, h):
        lr = lr_ref[rows(ci)]
        x = (jnp.dot(lr, wg_ref[:, h * dk:(h + 1) * dk], preferred_element_type=F32)
             + bg_ref[:, h * dk:(h + 1) * dk])
        log2_a = (-(jnp.maximum(-x, 0.0) + jnp.log1p(jnp.exp(-jnp.abs(x))))
                  * (np.log2(np.e) / GLA_TAU))
        return jnp.concatenate(_split_bf16(log2_a), axis=0)

    def exponents(log2_a):
        return jnp.dot(ws_ref[...], log2_a, preferred_element_type=F32)

    def level_operands(ci, h, pb):
        q, k = qk(ci, h)
        ops = [(q * pb[l * c:(l + 1) * c], k * pb[l * c:(l + 1) * c]) for l in range(GLA_LEVELS)]
        return ops + [(q, k)]

    def level_scores(ops):
        return [lax.dot_general(a, b, nt, preferred_element_type=F32) for a, b in ops]

    def masked_sum(scores):
        attn = mask_ref[0] * scores[0]
        for l in range(1, GLA_LEVELS + 1):
            attn = attn + mask_ref[l] * scores[l]
        return attn.astype(BF16)

    def mix(ci, h, p, pb, attn):
        q, k = qk(ci, h)
        v = v_ref[rows(ci), h * dv:(h + 1) * dv]
        eb = pb[GLA_LEVELS * c:(GLA_LEVELS + 1) * c]
        ebl = pb[(GLA_LEVELS + 1) * c:(GLA_LEVELS + 2) * c]
        st = st_ref[h]
        o = (lax.dot_general(q * eb, st.astype(BF16), nt, preferred_element_type=F32)
             + jnp.dot(attn, v, preferred_element_type=F32))
        v_t = v.astype(F32).T.astype(BF16)
        decay = p[(GLA_LEVELS + 1) * c - 1:(GLA_LEVELS + 1) * c]
        st_ref[h] = st * decay + jnp.dot(v_t, k * ebl, preferred_element_type=F32)
        return o

    def emit(ci, h, o):
        y = _rms(o, go_ref[...])
        og = og_ref[rows(ci), h * dv:(h + 1) * dv].astype(F32)
        o_ref[rows(ci), h * dv:(h + 1) * dv] = (y * (og * jax.nn.sigmoid(og))).astype(o_ref.dtype)

    heads = range(GLA_HEADS)

    def chunk_stages(ci):
        log2_a = [gate(ci, h) for h in heads]
        yield
        expo = [exponents(x) for x in log2_a]
        p = [jnp.exp2(e) for e in expo]
        pb = [x.astype(BF16) for x in p]
        yield
        operands = [level_operands(ci, h, pb[h]) for h in heads]
        scores = [level_scores(ops) for ops in operands]
        yield
        attn = [masked_sum(s) for s in scores]
        out = [mix(ci, h, p[h], pb[h], attn[h]) for h in heads]
        yield
        for h in heads:
            emit(ci, h, out[h])
        yield

    live = []
    pending = [chunk_stages(ci) for ci in range(GLA_CHUNKS_PER_STEP)]
    tick = 0
    while pending or live:
        if pending and tick % GLA_STAGGER == 0:
            live.append(pending.pop(0))
        for gen in list(live):
            if next(gen, "done") == "done":
                live.remove(gen)
        tick += 1


def _gla(z, wg, bg, go, riders, batch, seq):
    c = GLA_CHUNK * GLA_CHUNKS_PER_STEP
    nc = seq // c
    ws, masks = _gla_constants()
    hk, hv = GLA_HEADS * GLA_DK, GLA_HEADS * GLA_DV
    const2 = lambda b, i: (0, 0)
    rider_specs, windows, steps = _rider_specs(riders, lambda b, i: b * nc + i)
    assert steps <= batch * nc
    return pl.pallas_call(
        functools.partial(_gla_kernel, rider_windows=windows),
        grid=(batch, nc),
        in_specs=[pl.BlockSpec((c, hk), lambda b, i: (b * nc + i, Z_GQ // hk)),
                  pl.BlockSpec((c, hk), lambda b, i: (b * nc + i, Z_GK // hk)),
                  pl.BlockSpec((c, hv), lambda b, i: (b * nc + i, Z_GV // hv)),
                  pl.BlockSpec((c, LANES), lambda b, i: (b * nc + i, Z_GLR // LANES)),
                  pl.BlockSpec((c, hv), lambda b, i: (b * nc + i, Z_OG // hv)),
                  pl.BlockSpec((LANES, hk), const2),
                  pl.BlockSpec((1, hk), const2),
                  pl.BlockSpec((1, GLA_DV), const2),
                  pl.BlockSpec(ws.shape, const2),
                  pl.BlockSpec(masks.shape, lambda b, i: (0, 0, 0))] + rider_specs,
        out_specs=[pl.BlockSpec((c, hv), lambda b, i: (b * nc + i, 0))] + rider_specs,
        out_shape=([jax.ShapeDtypeStruct((batch * seq, hv), BF16)]
                   + [jax.ShapeDtypeStruct(w.shape, BF16) for w in riders]),
        scratch_shapes=[pltpu.VMEM((GLA_HEADS, GLA_DV, GLA_DK), F32)],
        compiler_params=_params(("arbitrary", "arbitrary"), 48),
        name="gla",
    )(z, z, z, z, z, wg, bg, go, jnp.asarray(ws, BF16), jnp.asarray(masks), *riders)


MIX_SUBTILE = 128


def _mix_kernel(oa_ref, ob_ref, ga_ref, gb_ref, wa_ref, wb_ref, wm_ref, x_ref, gm_ref, gpost_ref,
                gpre_ref, sc_ref, sh_ref, x1_ref, h2_ref):
    post = gm_ref[0] * gpost_ref[...]
    pre = gpre_ref[...] * (1.0 + sc_ref[0])

    def sub_tile(rows):
        ya = jnp.dot(oa_ref[rows], wa_ref[...], preferred_element_type=F32)
        yb = jnp.dot(ob_ref[rows], wb_ref[...], preferred_element_type=F32)
        yield
        merged = (jax.nn.sigmoid(ga_ref[rows].astype(F32)) * ya
                  + jax.nn.sigmoid(gb_ref[rows].astype(F32)) * yb).astype(BF16)
        yield
        y = jnp.dot(merged, wm_ref[...], preferred_element_type=F32)
        yield
        x1 = x_ref[rows] + _rms(y, post)
        x1_ref[rows] = x1
        h2_ref[rows] = (_rms(x1, pre) + sh_ref[0]).astype(h2_ref.dtype)
        yield

    pending = [sub_tile(slice(r0, r0 + MIX_SUBTILE))
               for r0 in range(0, x_ref.shape[0], MIX_SUBTILE)]
    live = []
    while pending or live:
        if pending:
            live.append(pending.pop(0))
        for gen in list(live):
            if next(gen, "done") == "done":
                live.remove(gen)


def _mix(oa, ob, z, wa, wb, wm, x2, gate_m, g_post, g_pre, scale_f, shift_f, seq):
    t = oa.shape[0]
    tm = 256
    tpb = seq // tm
    row = pl.BlockSpec((tm, D_MODEL), lambda i: (i, 0))
    vec = pl.BlockSpec((1, D_MODEL), lambda i: (0, 0))
    per_batch = pl.BlockSpec((1, 1, D_MODEL), lambda i: (i // tpb, 0, 0))
    resident = lambda w: pl.BlockSpec(w.shape, lambda i: (0, 0), pipeline_mode=pl.Buffered(1))
    return pl.pallas_call(
        _mix_kernel,
        grid=(t // tm,),
        in_specs=[pl.BlockSpec((tm, oa.shape[1]), lambda i: (i, 0)),
                  pl.BlockSpec((tm, ob.shape[1]), lambda i: (i, 0)),
                  pl.BlockSpec((tm, D_MODEL), lambda i: (i, Z_GATE_A // D_MODEL)),
                  pl.BlockSpec((tm, D_MODEL), lambda i: (i, Z_GATE_B // D_MODEL)),
                  resident(wa), resident(wb), resident(wm),
                  row, per_batch, vec, vec, per_batch, per_batch],
        out_specs=[row, row],
        out_shape=[jax.ShapeDtypeStruct((t, D_MODEL), F32),
                   jax.ShapeDtypeStruct((t, D_MODEL), BF16)],
        compiler_params=_params(("arbitrary",), 56),
        name="mix",
    )(oa, ob, z, z, wa, wb, wm, x2, gate_m, g_post, g_pre, scale_f, shift_f)


def _ffn_up_kernel(h_ref, wa_ref, wv_ref, cwa_ref, cwv_ref, cba_ref, cbv_ref, o_ref,
                   w_bf, u_sc, *, tiles_per_batch):
    i = pl.program_id(1)
    tm, tn = o_ref.shape
    head, body, tail = slice(0, SUBLANES), slice(SUBLANES, SUBLANES + tm), slice(tm, tm + SUBLANES)

    @pl.when(i == 0)
    def _():
        w_bf[:, 0:tn] = wa_ref[...].astype(BF16)
        w_bf[:, tn:2 * tn] = wv_ref[...].astype(BF16)

    @pl.when(i % tiles_per_batch == 0)
    def _():
        u_sc[head] = jnp.zeros((SUBLANES, u_sc.shape[1]), F32)

    @pl.when(i % tiles_per_batch != 0)
    def _():
        u_sc[head] = u_sc[tail]

    def conv(r0, cols, cw_ref, cb_ref):
        cw = cw_ref[...]
        start = SUBLANES + r0
        acc = cb_ref[...] + (u_sc[start:start + FFN_UP_SUBTILE, cols]
                             * cw[CONV_WIDTH - 1:CONV_WIDTH])
        for tap in range(1, CONV_WIDTH):
            acc = acc + (u_sc[start - tap:start - tap + FFN_UP_SUBTILE, cols]
                         * cw[CONV_WIDTH - 1 - tap:CONV_WIDTH - tap])
        return acc

    for r0 in range(0, tm, FFN_UP_SUBTILE):
        rows = slice(r0, r0 + FFN_UP_SUBTILE)
        u_sc[SUBLANES + r0:SUBLANES + r0 + FFN_UP_SUBTILE] = jnp.dot(
            h_ref[rows], w_bf[...], preferred_element_type=F32)
        a = conv(r0, slice(0, tn), cwa_ref, cba_ref)
        val = conv(r0, slice(tn, 2 * tn), cwv_ref, cbv_ref)
        o_ref[rows] = (jax.nn.gelu(a, approximate=True) * val).astype(o_ref.dtype)


FFN_UP_SUBTILE = 1024


def _ffn_up(h2, w_up, conv_w, conv_b, seq):
    t = h2.shape[0]
    tm, tn = 2048, 512
    nj = D_FF // tn
    return pl.pallas_call(
        functools.partial(_ffn_up_kernel, tiles_per_batch=seq // tm),
        grid=(nj, t // tm),
        in_specs=[pl.BlockSpec((tm, D_MODEL), lambda j, i: (i, 0)),
                  pl.BlockSpec((D_MODEL, tn), lambda j, i: (0, j)),
                  pl.BlockSpec((D_MODEL, tn), lambda j, i: (0, nj + j)),
                  pl.BlockSpec((CONV_WIDTH, tn), lambda j, i: (0, j)),
                  pl.BlockSpec((CONV_WIDTH, tn), lambda j, i: (0, nj + j)),
                  pl.BlockSpec((1, tn), lambda j, i: (0, j)),
                  pl.BlockSpec((1, tn), lambda j, i: (0, nj + j))],
        out_specs=pl.BlockSpec((tm, tn), lambda j, i: (i, j)),
        out_shape=jax.ShapeDtypeStruct((t, D_FF), BF16),
        scratch_shapes=[pltpu.VMEM((D_MODEL, 2 * tn), BF16),
                        pltpu.VMEM((tm + SUBLANES, 2 * tn), F32)],
        compiler_params=_params(("arbitrary", "arbitrary"), 60),
        name="ffn_up",
    )(h2, w_up, w_up, conv_w, conv_w, conv_b, conv_b)


def _ffn_down_kernel(g_ref, w_ref, x1_ref, gf_ref, gpost_ref, o_ref):
    f = jnp.dot(g_ref[...], w_ref[...], preferred_element_type=F32)
    o_ref[...] = x1_ref[...] + _rms(f, gf_ref[0] * gpost_ref[...])


def _ffn_down(g, w, x1, gate_f, g_post, seq):
    t = g.shape[0]
    tm = 256
    tpb = seq // tm
    return pl.pallas_call(
        _ffn_down_kernel,
        grid=(t // tm,),
        in_specs=[pl.BlockSpec((tm, D_FF), lambda i: (i, 0)),
                  pl.BlockSpec((D_FF, D_MODEL), lambda i: (0, 0), pipeline_mode=pl.Buffered(1)),
                  pl.BlockSpec((tm, D_MODEL), lambda i: (i, 0)),
                  pl.BlockSpec((1, 1, D_MODEL), lambda i: (i // tpb, 0, 0)),
                  pl.BlockSpec((1, D_MODEL), lambda i: (0, 0))],
        out_specs=pl.BlockSpec((tm, D_MODEL), lambda i: (i, 0)),
        out_shape=jax.ShapeDtypeStruct((t, D_MODEL), F32),
        compiler_params=_params(("arbitrary",), 56),
        name="ffn_down",
    )(g, w, x1, gate_f, g_post)


def _swap_halves(w):
    half = w.shape[-1] // 2
    return jnp.concatenate([w[..., half:], w[..., :half]], axis=-1)


def _w_in_special(w_t):
    half = MLA_ROPE // 2
    k_rope = w_t[_SRC_KROPE:_SRC_KROPE + MLA_ROPE]
    g_lr = w_t[_SRC_GLR:_SRC_GLR + GLA_GATE_RANK]
    pad = jnp.zeros((Z_GQ - Z_GLR - 2 * GLA_GATE_RANK, w_t.shape[1]), w_t.dtype)
    return jnp.concatenate([k_rope, k_rope[half:], k_rope[:half], g_lr, g_lr, pad], axis=0)


def _prep_w_q(w_q_up):
    w = w_q_up.reshape(MLA_Q_RANK, MLA_HEADS, MLA_NOPE + MLA_ROPE)
    pe = w[:, :, MLA_NOPE:]
    w = jnp.concatenate([w, _swap_halves(pe)], axis=-1)
    return jnp.transpose(w, (1, 0, 2)).astype(BF16)


def kernel(x, c, positions, w_ada, b_ada, g_pre_mix, w_in, g_q_lat, w_q_up, g_kv_lat, w_kv_up,
           w_gla_gate_up, b_gla_gate, g_gla_out, w_branch_a, w_branch_b, w_mix_out, g_post_mix,
           g_pre_ffn, w_ffn_up, conv_w, conv_b, w_ffn_down, g_post_ffn):
    batch, seq, _ = x.shape
    depth = w_ada.shape[0]
    t = batch * seq
    row = lambda v: v.reshape(1, -1)

    inv = 1.0 / (ROPE_THETA ** (jnp.arange(0, MLA_ROPE, 2, dtype=F32) / MLA_ROPE))
    cos, sin = _rope_tables(positions, inv)

    c_pad = jnp.zeros((SUBLANES, D_MODEL), F32).at[:batch].set(c)
    x2 = x.reshape(t, D_MODEL)
    for l in range(depth):
        mod = _ada(c_pad, w_ada[l], row(b_ada[l]))[:batch]
        shift_m, scale_m, gate_m, shift_f, scale_f, gate_f = (
            m.reshape(batch, 1, D_MODEL) for m in jnp.split(mod, 6, axis=-1))

        h = _norm_mod(x2, row(g_pre_mix[l]), scale_m, shift_m, seq)
        w_in_t = w_in[l].T
        z = _mm_in(h, w_in_t, _w_in_special(w_in_t))

        q, k, v = _mla_proj(z, cos, sin, row(g_q_lat[l]), row(g_kv_lat[l]),
                            _prep_w_q(w_q_up[l]), w_kv_up[l].astype(BF16), batch, seq)
        o_a, w_a_bf, w_b_bf = _attn(q, k, v, [w_branch_a[l], w_branch_b[l]], batch, seq)

        wg_hi, wg_lo = _split_bf16(w_gla_gate_up[l])
        wg = jnp.zeros((LANES, GLA_HEADS * GLA_DK), BF16)
        wg = wg.at[:GLA_GATE_RANK].set(wg_hi).at[GLA_GATE_RANK:2 * GLA_GATE_RANK].set(wg_lo)
        o_b, w_down_bf, w_mix_bf = _gla(z, wg, row(b_gla_gate[l]), row(g_gla_out[l]),
                                        [w_ffn_down[l], w_mix_out[l]], batch, seq)

        x1, h2 = _mix(o_a, o_b, z, w_a_bf, w_b_bf, w_mix_bf, x2, gate_m, row(g_post_mix[l]),
                      row(g_pre_ffn[l]), scale_f, shift_f, seq)

        g = _ffn_up(h2, w_ffn_up[l], conv_w[l], row(conv_b[l]), seq)
        x2 = _ffn_down(g, w_down_bf, x1, gate_f, row(g_post_ffn[l]), seq)
    return x2.reshape(batch, seq, D_MODEL)
```

```python
import functools

import jax
import jax.numpy as jnp
import numpy as np
from jax import lax
from jax.experimental import pallas as pl
from jax.experimental.pallas import tpu as pltpu

F32 = jnp.float32
BF16 = jnp.bfloat16

D_MODEL = 2048
MLA_HEADS = 8
MLA_Q_RANK = 512
MLA_KV_RANK = 256
MLA_NOPE = 128
MLA_ROPE = 64
MLA_V = 128
ROPE_THETA = 10000.0
GLA_HEADS = 4
GLA_DK = 256
GLA_DV = 512
GLA_GATE_RANK = 16
GLA_TAU = 16.0
D_FF = 5632
CONV_WIDTH = 3
EPS = 1e-6

LANES = 128
SUBLANES = 8
MLA_QK_PAD = 256

Z_QLAT = 0
Z_KVLAT = 512
Z_KROPE = 768
Z_GLR = 896
Z_GQ = 1024
Z_GV = 2048
Z_OG = 4096
Z_GATE_A = 6144
Z_GATE_B = 8192
Z_GK = 10240
Z_WIDTH = 11264

_SRC_KROPE = MLA_Q_RANK + MLA_KV_RANK
_SRC_GQ = _SRC_KROPE + MLA_ROPE
_SRC_GK = _SRC_GQ + GLA_HEADS * GLA_DK
_SRC_GV = _SRC_GK + GLA_HEADS * GLA_DK
_SRC_GLR = _SRC_GV + GLA_HEADS * GLA_DV
_SRC_OG = _SRC_GLR + GLA_GATE_RANK
_SRC_GATE_A = _SRC_OG + GLA_HEADS * GLA_DV
_SRC_GATE_B = _SRC_GATE_A + D_MODEL

GLA_CHUNK = 128
GLA_LEVELS = 7

MIB = 1024 * 1024


def _params(semantics, vmem_mib):
    return pltpu.CompilerParams(dimension_semantics=semantics,
                                vmem_limit_bytes=vmem_mib * MIB)


def _rms(x, g):
    return x * lax.rsqrt(jnp.mean(x * x, axis=-1, keepdims=True) + EPS) * g


RIDER_ROWS = 256


def _rider_specs(arrays, step_of):
    specs, windows, start = [], [], 0
    for a in arrays:
        n = a.shape[0] // RIDER_ROWS
        specs.append(pl.BlockSpec(
            (RIDER_ROWS, a.shape[1]),
            lambda *idx, lo=start, n=n: (jnp.clip(step_of(*idx) - lo, 0, n - 1), 0)))
        windows.append((start, start + n))
        start += n
    return specs, windows, start


def _run_riders(step, windows, src_refs, dst_refs):
    for (lo, hi), src, dst in zip(windows, src_refs, dst_refs):
        @pl.when((step >= lo) & (step < hi))
        def _(src=src, dst=dst):
            dst[...] = src[...].astype(BF16)


def _ada_kernel(c_ref, w_ref, b_ref, o_ref):
    c = c_ref[...]
    ca = c * jax.nn.sigmoid(c)
    o_ref[...] = jnp.dot(ca.astype(BF16), w_ref[...].astype(BF16),
                         preferred_element_type=F32) + b_ref[...]


def _ada(c_pad, w, b):
    n = w.shape[1]
    tn = 1024
    return pl.pallas_call(
        _ada_kernel,
        grid=(n // tn,),
        in_specs=[pl.BlockSpec((SUBLANES, D_MODEL), lambda j: (0, 0)),
                  pl.BlockSpec((D_MODEL, tn), lambda j: (0, j)),
                  pl.BlockSpec((1, tn), lambda j: (0, j))],
        out_specs=pl.BlockSpec((SUBLANES, tn), lambda j: (0, j)),
        out_shape=jax.ShapeDtypeStruct((SUBLANES, n), F32),
        compiler_params=_params(("arbitrary",), 32),
        name="ada",
    )(c_pad, w, b)


def _norm_mod_kernel(x_ref, g_ref, sc_ref, sh_ref, o_ref):
    y = _rms(x_ref[...], g_ref[...] * (1.0 + sc_ref[0]))
    o_ref[...] = (y + sh_ref[0]).astype(o_ref.dtype)


def _norm_mod(x2, g, scale, shift, seq):
    t = x2.shape[0]
    tm = 1024
    tpb = seq // tm
    return pl.pallas_call(
        _norm_mod_kernel,
        grid=(t // tm,),
        in_specs=[pl.BlockSpec((tm, D_MODEL), lambda i: (i, 0)),
                  pl.BlockSpec((1, D_MODEL), lambda i: (0, 0)),
                  pl.BlockSpec((1, 1, D_MODEL), lambda i: (i // tpb, 0, 0)),
                  pl.BlockSpec((1, 1, D_MODEL), lambda i: (i // tpb, 0, 0))],
        out_specs=pl.BlockSpec((tm, D_MODEL), lambda i: (i, 0)),
        out_shape=jax.ShapeDtypeStruct((t, D_MODEL), BF16),
        compiler_params=_params(("arbitrary",), 32),
        name="norm_mod",
    )(x2, g, scale, shift)


MM_IN_TN = 1024
MM_IN_SRC_ALIGN = 16
_MM_IN_SRC_ROWS = (0,
                   _SRC_GQ,
                   _SRC_GV, _SRC_GV + MM_IN_TN,
                   _SRC_OG, _SRC_OG + MM_IN_TN,
                   _SRC_GATE_A, _SRC_GATE_A + MM_IN_TN,
                   _SRC_GATE_B, _SRC_GATE_B + MM_IN_TN,
                   _SRC_GK)


def _mm_in_kernel(src_ref, h_ref, w_ref, wsp_ref, o_ref, w_bf):
    del src_ref
    j, i = pl.program_id(0), pl.program_id(1)

    @pl.when(i == 0)
    def _():
        w_bf[...] = w_ref[...].astype(BF16)

    @pl.when((i == 0) & (j == 0))
    def _():
        w_bf[Z_KROPE:Z_GQ] = wsp_ref[...].astype(BF16)

    for r0 in range(0, o_ref.shape[0], MM_IN_SUBTILE):
        rows = slice(r0, r0 + MM_IN_SUBTILE)
        o_ref[rows] = lax.dot_general(h_ref[rows], w_bf[...], (((1,), (1,)), ((), ())),
                                      preferred_element_type=F32).astype(o_ref.dtype)


MM_IN_SUBTILE = 1024


def _mm_in(h, w_t, w_special):
    t = h.shape[0]
    tm, tn = 2048, MM_IN_TN
    grid_spec = pltpu.PrefetchScalarGridSpec(
        num_scalar_prefetch=1,
        grid=(Z_WIDTH // tn, t // tm),
        in_specs=[pl.BlockSpec((tm, D_MODEL), lambda j, i, src: (i, 0)),
                  pl.BlockSpec((pl.Element(tn), pl.Element(D_MODEL)),
                               lambda j, i, src: (src[j] * MM_IN_SRC_ALIGN, 0)),
                  pl.BlockSpec((Z_GQ - Z_KROPE, D_MODEL), lambda j, i, src: (0, 0))],
        out_specs=pl.BlockSpec((tm, tn), lambda j, i, src: (i, j)),
        scratch_shapes=[pltpu.VMEM((tn, D_MODEL), BF16)])
    return pl.pallas_call(
        _mm_in_kernel,
        grid_spec=grid_spec,
        out_shape=jax.ShapeDtypeStruct((t, Z_WIDTH), BF16),
        compiler_params=_params(("arbitrary", "arbitrary"), 56),
        name="mm_in",
    )(jnp.asarray([r // MM_IN_SRC_ALIGN for r in _MM_IN_SRC_ROWS], jnp.int32), h, w_t, w_special)


ROPE_HALF = MLA_ROPE // 2
ROPE_PACK = LANES // ROPE_HALF


def _rope_kernel(pos_ref, inv_ref, cos_ref, sin_ref):
    ang = pos_ref[...].astype(F32) * inv_ref[...]
    cos_ref[...] = jnp.cos(ang)
    sin_ref[...] = jnp.sin(ang)


def _rope_tables(positions, inv):
    t = positions.size
    rows = t // ROPE_PACK
    pos_dense = jnp.repeat(positions.reshape(t), ROPE_HALF).reshape(rows, LANES)
    inv_dense = jnp.tile(inv, ROPE_PACK).reshape(1, LANES)
    tr = 512
    spec = pl.BlockSpec((tr, LANES), lambda i: (i, 0))
    cos, sin = pl.pallas_call(
        _rope_kernel,
        grid=(rows // tr,),
        in_specs=[spec, pl.BlockSpec((1, LANES), lambda i: (0, 0))],
        out_specs=[spec, spec],
        out_shape=[jax.ShapeDtypeStruct((rows, LANES), F32)] * 2,
        compiler_params=_params(("arbitrary",), 32),
        name="rope_tables",
    )(pos_dense, inv_dense)
    return cos.reshape(t, ROPE_HALF), sin.reshape(t, ROPE_HALF)


def _mla_proj_kernel(ql_ref, kvl_ref, kr_ref, cos_ref, sin_ref,
                     gq_ref, gkv_ref, wq_ref, wkv_ref, q_ref, k_ref, v_ref):
    c, s = cos_ref[...], sin_ref[...]
    zeros = jnp.zeros((c.shape[0], LANES - MLA_ROPE), F32)
    cosm = jnp.concatenate([c, c, zeros], axis=1)
    sinm = jnp.concatenate([-s, s, zeros], axis=1)

    def rope(r):
        return r * cosm + pltpu.roll(r, MLA_ROPE, 1) * sinm

    scale = (MLA_NOPE + MLA_ROPE) ** -0.5 * np.log2(np.e)
    qn =_rms(ql_ref[...].astype(F32), gq_ref[...]).astype(BF16)
    for h in range(MLA_HEADS):
        r = jnp.dot(qn, wq_ref[h], preferred_element_type=F32) * scale
        q_ref[0, h, :, 0:MLA_NOPE] = r[:, 0:MLA_NOPE].astype(BF16)
        q_ref[0, h, :, MLA_NOPE:MLA_QK_PAD] = rope(r[:, MLA_NOPE:MLA_QK_PAD]).astype(BF16)

    kvn = _rms(kvl_ref[...].astype(F32), gkv_ref[...]).astype(BF16)
    kv = jnp.dot(kvn, wkv_ref[...], preferred_element_type=F32).astype(BF16)
    kpe = rope(kr_ref[...].astype(F32)).astype(BF16)
    hw = MLA_NOPE + MLA_V
    for h in range(MLA_HEADS):
        k_ref[0, h, :, 0:MLA_NOPE] = kv[:, h * hw:h * hw + MLA_NOPE]
        k_ref[0, h, :, MLA_NOPE:MLA_QK_PAD] = kpe
        v_ref[0, h] = kv[:, h * hw + MLA_NOPE:(h + 1) * hw]


def _mla_proj(z, cos, sin, gq, gkv, wq, wkv, batch, seq):
    tm = 1024
    tpb = seq // tm
    qk_shape = jax.ShapeDtypeStruct((batch, MLA_HEADS, seq, MLA_QK_PAD), BF16)
    v_shape = jax.ShapeDtypeStruct((batch, MLA_HEADS, seq, MLA_V), BF16)
    const2 = lambda i: (0, 0)
    qk_spec = pl.BlockSpec((1, MLA_HEADS, tm, MLA_QK_PAD), lambda i: (i // tpb, 0, i % tpb, 0))
    return pl.pallas_call(
        _mla_proj_kernel,
        grid=(batch * tpb,),
        in_specs=[pl.BlockSpec((tm, MLA_Q_RANK), lambda i: (i, Z_QLAT // MLA_Q_RANK)),
                  pl.BlockSpec((tm, MLA_KV_RANK), lambda i: (i, Z_KVLAT // MLA_KV_RANK)),
                  pl.BlockSpec((tm, LANES), lambda i: (i, Z_KROPE // LANES)),
                  pl.BlockSpec((tm, ROPE_HALF), lambda i: (i, 0)),
                  pl.BlockSpec((tm, ROPE_HALF), lambda i: (i, 0)),
                  pl.BlockSpec((1, MLA_Q_RANK), const2),
                  pl.BlockSpec((1, MLA_KV_RANK), const2),
                  pl.BlockSpec((MLA_HEADS, MLA_Q_RANK, MLA_QK_PAD), lambda i: (0, 0, 0)),
                  pl.BlockSpec((MLA_KV_RANK, MLA_HEADS * (MLA_NOPE + MLA_V)), const2)],
        out_specs=[qk_spec, qk_spec,
                   pl.BlockSpec((1, MLA_HEADS, tm, MLA_V), lambda i: (i // tpb, 0, i % tpb, 0))],
        out_shape=[qk_shape, qk_shape, v_shape],
        compiler_params=_params(("arbitrary",), 48),
        name="mla_proj",
    )(z, z, z, cos, sin, gq, gkv, wq, wkv)


ATTN_KV_BLOCK = 1024
ATTN_Q_BLOCKS = 4
ATTN_DIAG_CHAINS = 2


def _attn_kernel(q_ref, k_ref, v_ref, *refs, rider_windows):
    nr = len(rider_windows)
    rider_src, o_ref, rider_dst = refs[:nr], refs[nr], refs[nr + 1:2 * nr + 1]
    m_sc, l_sc, acc_sc = refs[2 * nr + 1:]
    blk = ATTN_KV_BLOCK
    nt = (((1,), (1,)), ((), ()))
    qi = pl.program_id(2)
    step = ((pl.program_id(0) * pl.num_programs(1) + pl.program_id(1)) * pl.num_programs(2) + qi)
    _run_riders(step, rider_windows, rider_src, rider_dst)
    m_sc[...] = jnp.full(m_sc.shape, -jnp.inf, F32)
    l_sc[...] = jnp.zeros(l_sc.shape, F32)
    acc_sc[...] = jnp.zeros(acc_sc.shape, F32)

    def update(rows, s, v):
        m_prev = m_sc[rows]
        m_new = jnp.maximum(m_prev, jnp.max(s, axis=-1, keepdims=True))
        alpha = jnp.exp2(m_prev - m_new)
        ps = [jnp.exp2(s[:, c * LANES:(c + 1) * LANES] - m_new)
              for c in range(s.shape[1] // LANES)]
        l_sc[rows] = alpha * l_sc[rows] + functools.reduce(lambda a, b: a + b, ps)
        p = jnp.concatenate(ps, axis=1).astype(BF16)
        acc_sc[rows] = alpha * acc_sc[rows] + jnp.dot(p, v, preferred_element_type=F32)
        m_sc[rows] = m_new

    def full_block(rows, ks):
        k = k_ref[0, 0, pl.ds(ks, blk), :]
        v = v_ref[0, 0, pl.ds(ks, blk), :]
        update(rows, lax.dot_general(q_ref[0, 0, rows], k, nt, preferred_element_type=F32), v)

    def diag_block(row0, ks):
        sub = blk // ATTN_DIAG_CHAINS
        for r in range(ATTN_DIAG_CHAINS):
            n = (r + 1) * sub
            rows = slice(row0 + r * sub, row0 + n)
            k = k_ref[0, 0, pl.ds(ks, n), :]
            v = v_ref[0, 0, pl.ds(ks, n), :]
            s = lax.dot_general(q_ref[0, 0, rows], k, nt, preferred_element_type=F32)
            row = lax.broadcasted_iota(jnp.int32, s.shape, 0) + r * sub
            col = lax.broadcasted_iota(jnp.int32, s.shape, 1)
            update(rows, jnp.where(col <= row, s, -jnp.inf), v)

    def body(kb, carry):
        full_block(slice(0, ATTN_Q_BLOCKS * blk), pl.multiple_of(kb * blk, blk))
        return carry

    lax.fori_loop(0, qi * ATTN_Q_BLOCKS, body, 0)

    base = qi * ATTN_Q_BLOCKS * blk
    for a in range(ATTN_Q_BLOCKS):
        for kb in range(a):
            full_block(slice(a * blk, (a + 1) * blk), pl.multiple_of(base + kb * blk, blk))
        diag_block(a * blk, pl.multiple_of(base + a * blk, blk))
    l = jnp.sum(l_sc[...], axis=-1, keepdims=True)
    o_ref[...] = (acc_sc[...] / l).astype(o_ref.dtype)


def _attn(q, k, v, riders, batch, seq):
    blk = ATTN_Q_BLOCKS * ATTN_KV_BLOCK
    nq = seq // blk
    rider_specs, windows, steps = _rider_specs(
        riders, lambda b, h, i: (b * MLA_HEADS + h) * nq + i)
    assert steps <= batch * MLA_HEADS * nq
    return pl.pallas_call(
        functools.partial(_attn_kernel, rider_windows=windows),
        grid=(batch, MLA_HEADS, nq),
        in_specs=[pl.BlockSpec((1, 1, blk, MLA_QK_PAD), lambda b, h, i: (b, h, i, 0)),
                  pl.BlockSpec((1, 1, seq, MLA_QK_PAD), lambda b, h, i: (b, h, 0, 0)),
                  pl.BlockSpec((1, 1, seq, MLA_V), lambda b, h, i: (b, h, 0, 0))] + rider_specs,
        out_specs=[pl.BlockSpec((blk, MLA_V), lambda b, h, i: (b * nq + i, h))] + rider_specs,
        out_shape=([jax.ShapeDtypeStruct((batch * seq, MLA_HEADS * MLA_V), BF16)]
                   + [jax.ShapeDtypeStruct(w.shape, BF16) for w in riders]),
        scratch_shapes=[pltpu.VMEM((blk, LANES), F32), pltpu.VMEM((blk, LANES), F32),
                        pltpu.VMEM((blk, MLA_V), F32)],
        compiler_params=_params(("arbitrary", "arbitrary", "arbitrary"), 58),
        name="attn",
    )(q, k, v, *riders)


def _gla_constants():
    c = GLA_CHUNK
    idx = np.arange(c)
    w = np.zeros((GLA_LEVELS + 2, c, c), np.float32)
    masks = np.zeros((GLA_LEVELS + 1, c, c), np.float32)
    for l in range(GLA_LEVELS):
        half = 1 << l
        blk = idx // (2 * half)
        mid = blk * 2 * half + half - 1
        upper = (idx % (2 * half)) >= half
        t = idx[None, :]
        up_rows = (t > mid[:, None]) & (t <= idx[:, None])
        lo_rows = (t > idx[:, None]) & (t <= mid[:, None])
        w[l] = np.where(upper[:, None], up_rows, lo_rows)
        masks[l] = (blk[:, None] == blk[None, :]) & upper[:, None] & (~upper[None, :])
    w[GLA_LEVELS] = idx[None, :] <= idx[:, None]
    w[GLA_LEVELS + 1] = idx[None, :] > idx[:, None]
    masks[GLA_LEVELS] = np.eye(c)
    w = w.reshape(-1, c)
    return np.concatenate([w, w], axis=1), masks


def _split_bf16(x):
    hi = x.astype(BF16)
    return hi, (x - hi.astype(F32)).astype(BF16)


GLA_CHUNKS_PER_STEP = 2
GLA_HEAD_GROUP = 2
GLA_STAGGER = 2


def _gla_kernel(q_ref, k_ref, v_ref, lr_ref, og_ref, wg_ref, bg_ref, go_ref, ws_ref, mask_ref,
                *refs, rider_windows):
    nr = len(rider_windows)
    rider_src, o_ref, rider_dst, st_ref = refs[:nr], refs[nr], refs[nr + 1:2 * nr + 1], refs[-1]
    c = GLA_CHUNK
    dk, dv = GLA_DK, GLA_DV
    nt = (((1,), (1,)), ((), ()))
    _run_riders(pl.program_id(0) * pl.num_programs(1) + pl.program_id(1), rider_windows,
                rider_src, rider_dst)

    @pl.when(pl.program_id(1) == 0)
    def _():
        st_ref[...] = jnp.zeros(st_ref.shape, F32)

    def rows(ci):
        return slice(ci * c, (ci + 1) * c)

    def qk(ci, h):
        q = q_ref[rows(ci), h * dk:(h + 1) * dk] * (dk ** -0.5)
        return q, k_ref[rows(ci), h * dk:(h + 1) * dk]

    def gate(ci, h):
        lr = lr_ref[rows(ci)]
        x = (jnp.dot(lr, wg_ref[:, h * dk:(h + 1) * dk], preferred_element_type=F32)
             + bg_ref[:, h * dk:(h + 1) * dk])
        log2_a = (-(jnp.maximum(-x, 0.0) + jnp.log1p(jnp.exp(-jnp.abs(x))))
                  * (np.log2(np.e) / GLA_TAU))
        return jnp.concatenate(_split_bf16(log2_a), axis=0)

    def exponents(log2_a):
        return jnp.dot(ws_ref[...], log2_a, preferred_element_type=F32)

    def level_operands(ci, h, pb):
        q, k = qk(ci, h)
        ops = [(q * pb[l * c:(l + 1) * c], k * pb[l * c:(l + 1) * c]) for l in range(GLA_LEVELS)]
        return ops + [(q, k)]

    def level_scores(ops):
        return [lax.dot_general(a, b, nt, preferred_element_type=F32) for a, b in ops]

    def masked_sum(scores):
        attn = mask_ref[0] * scores[0]
        for l in range(1, GLA_LEVELS + 1):
            attn = attn + mask_ref[l] * scores[l]
        return attn.astype(BF16)

    def mix(ci, h, p, pb, attn):
        q, k = qk(ci, h)
        v = v_ref[rows(ci), h * dv:(h + 1) * dv]
        eb = pb[GLA_LEVELS * c:(GLA_LEVELS + 1) * c]
        ebl = pb[(GLA_LEVELS + 1) * c:(GLA_LEVELS + 2) * c]
        st = st_ref[h]
        o = (lax.dot_general(q * eb, st.astype(BF16), nt, preferred_element_type=F32)
             + jnp.dot(attn, v, preferred_element_type=F32))
        v_t = v.astype(F32).T.astype(BF16)
        decay = p[(GLA_LEVELS + 1) * c - 1:(GLA_LEVELS + 1) * c]
        st_ref[h] = st * decay + jnp.dot(v_t, k * ebl, preferred_element_type=F32)
        return o

    def emit(ci, h, o):
        y = _rms(o, go_ref[...])
        og = og_ref[rows(ci), h * dv:(h + 1) * dv].astype(F32)
        o_ref[rows(ci), h * dv:(h + 1) * dv] = (y * (og * jax.nn.sigmoid(og))).astype(o_ref.dtype)

    def chunk_stages(ci, heads):
        log2_a = {h: gate(ci, h) for h in heads}
        yield
        expo = {h: exponents(log2_a[h]) for h in heads}
        p = {h: jnp.exp2(expo[h]) for h in heads}
        pb = {h: p[h].astype(BF16) for h in heads}
        yield
        operands = {h: level_operands(ci, h, pb[h]) for h in heads}
        scores = {h: level_scores(operands[h]) for h in heads}
        yield
        attn = {h: masked_sum(scores[h]) for h in heads}
        out = {h: mix(ci, h, p[h], pb[h], attn[h]) for h in heads}
        yield
        for h in heads:
            emit(ci, h, out[h])
        yield

    live = []
    pending = [chunk_stages(ci, range(h0, h0 + GLA_HEAD_GROUP))
               for ci in range(GLA_CHUNKS_PER_STEP)
               for h0 in range(0, GLA_HEADS, GLA_HEAD_GROUP)]
    tick = 0
    while pending or live:
        if pending and tick % GLA_STAGGER == 0:
            live.append(pending.pop(0))
        for gen in list(live):
            if next(gen, "done") == "done":
                live.remove(gen)
        tick += 1


def _gla(z, wg, bg, go, riders, batch, seq):
    c = GLA_CHUNK * GLA_CHUNKS_PER_STEP
    nc = seq // c
    ws, masks = _gla_constants()
    hk, hv = GLA_HEADS * GLA_DK, GLA_HEADS * GLA_DV
    const2 = lambda b, i: (0, 0)
    rider_specs, windows, steps = _rider_specs(riders, lambda b, i: b * nc + i)
    assert steps <= batch * nc
    return pl.pallas_call(
        functools.partial(_gla_kernel, rider_windows=windows),
        grid=(batch, nc),
        in_specs=[pl.BlockSpec((c, hk), lambda b, i: (b * nc + i, Z_GQ // hk)),
                  pl.BlockSpec((c, hk), lambda b, i: (b * nc + i, Z_GK // hk)),
                  pl.BlockSpec((c, hv), lambda b, i: (b * nc + i, Z_GV // hv)),
                  pl.BlockSpec((c, LANES), lambda b, i: (b * nc + i, Z_GLR // LANES)),
                  pl.BlockSpec((c, hv), lambda b, i: (b * nc + i, Z_OG // hv)),
                  pl.BlockSpec((LANES, hk), const2),
                  pl.BlockSpec((1, hk), const2),
                  pl.BlockSpec((1, GLA_DV), const2),
                  pl.BlockSpec(ws.shape, const2),
                  pl.BlockSpec(masks.shape, lambda b, i: (0, 0, 0))] + rider_specs,
        out_specs=[pl.BlockSpec((c, hv), lambda b, i: (b * nc + i, 0))] + rider_specs,
        out_shape=([jax.ShapeDtypeStruct((batch * seq, hv), BF16)]
                   + [jax.ShapeDtypeStruct(w.shape, BF16) for w in riders]),
        scratch_shapes=[pltpu.VMEM((GLA_HEADS, GLA_DV, GLA_DK), F32)],
        compiler_params=_params(("arbitrary", "arbitrary"), 48),
        name="gla",
    )(z, z, z, z, z, wg, bg, go, jnp.asarray(ws, BF16), jnp.asarray(masks), *riders)


MIX_SUBTILE = 128


def _mix_kernel(oa_ref, ob_ref, ga_ref, gb_ref, wa_ref, wb_ref, wm_ref, x_ref, gm_ref, gpost_ref,
                gpre_ref, sc_ref, sh_ref, x1_ref, h2_ref):
    post = gm_ref[0] * gpost_ref[...]
    pre = gpre_ref[...] * (1.0 + sc_ref[0])

    def sub_tile(rows):
        ya = jnp.dot(oa_ref[rows], wa_ref[...], preferred_element_type=F32)
        yb = jnp.dot(ob_ref[rows], wb_ref[...], preferred_element_type=F32)
        yield
        merged = (jax.nn.sigmoid(ga_ref[rows].astype(F32)) * ya
                  + jax.nn.sigmoid(gb_ref[rows].astype(F32)) * yb).astype(BF16)
        yield
        y = jnp.dot(merged, wm_ref[...], preferred_element_type=F32)
        yield
        x1 = x_ref[rows] + _rms(y, post)
        x1_ref[rows] = x1
        h2_ref[rows] = (_rms(x1, pre) + sh_ref[0]).astype(h2_ref.dtype)
        yield

    pending = [sub_tile(slice(r0, r0 + MIX_SUBTILE))
               for r0 in range(0, x_ref.shape[0], MIX_SUBTILE)]
    live = []
    while pending or live:
        if pending:
            live.append(pending.pop(0))
        for gen in list(live):
            if next(gen, "done") == "done":
                live.remove(gen)


def _mix(oa, ob, z, wa, wb, wm, x2, gate_m, g_post, g_pre, scale_f, shift_f, seq):
    t = oa.shape[0]
    tm = 256
    tpb = seq // tm
    row = pl.BlockSpec((tm, D_MODEL), lambda i: (i, 0))
    vec = pl.BlockSpec((1, D_MODEL), lambda i: (0, 0))
    per_batch = pl.BlockSpec((1, 1, D_MODEL), lambda i: (i // tpb, 0, 0))
    resident = lambda w: pl.BlockSpec(w.shape, lambda i: (0, 0), pipeline_mode=pl.Buffered(1))
    return pl.pallas_call(
        _mix_kernel,
        grid=(t // tm,),
        in_specs=[pl.BlockSpec((tm, oa.shape[1]), lambda i: (i, 0)),
                  pl.BlockSpec((tm, ob.shape[1]), lambda i: (i, 0)),
                  pl.BlockSpec((tm, D_MODEL), lambda i: (i, Z_GATE_A // D_MODEL)),
                  pl.BlockSpec((tm, D_MODEL), lambda i: (i, Z_GATE_B // D_MODEL)),
                  resident(wa), resident(wb), resident(wm),
                  row, per_batch, vec, vec, per_batch, per_batch],
        out_specs=[row, row],
        out_shape=[jax.ShapeDtypeStruct((t, D_MODEL), F32),
                   jax.ShapeDtypeStruct((t, D_MODEL), BF16)],
        compiler_params=_params(("arbitrary",), 56),
        name="mix",
    )(oa, ob, z, z, wa, wb, wm, x2, gate_m, g_post, g_pre, scale_f, shift_f)


def _ffn_up_kernel(h_ref, wa_ref, wv_ref, cwa_ref, cwv_ref, cba_ref, cbv_ref, o_ref,
                   w_bf, u_sc, *, tiles_per_batch):
    i = pl.program_id(1)
    tm, tn = o_ref.shape
    head, body, tail = slice(0, SUBLANES), slice(SUBLANES, SUBLANES + tm), slice(tm, tm + SUBLANES)

    @pl.when(i == 0)
    def _():
        w_bf[:, 0:tn] = wa_ref[...].astype(BF16)
        w_bf[:, tn:2 * tn] = wv_ref[...].astype(BF16)

    @pl.when(i % tiles_per_batch == 0)
    def _():
        u_sc[head] = jnp.zeros((SUBLANES, u_sc.shape[1]), F32)

    @pl.when(i % tiles_per_batch != 0)
    def _():
        u_sc[head] = u_sc[tail]

    u_sc[body] = jnp.dot(h_ref[...], w_bf[...], preferred_element_type=F32)

    def conv(cols, cw_ref, cb_ref):
        cw = cw_ref[...]
        acc = cb_ref[...] + u_sc[body, cols] * cw[CONV_WIDTH - 1:CONV_WIDTH]
        for tap in range(1, CONV_WIDTH):
            acc = acc + (u_sc[SUBLANES - tap:SUBLANES - tap + tm, cols]
                         * cw[CONV_WIDTH - 1 - tap:CONV_WIDTH - tap])
        return acc

    a = conv(slice(0, tn), cwa_ref, cba_ref)
    val = conv(slice(tn, 2 * tn), cwv_ref, cbv_ref)
    o_ref[...] = (jax.nn.gelu(a, approximate=True) * val).astype(o_ref.dtype)


def _ffn_up(h2, w_up, conv_w, conv_b, seq):
    t = h2.shape[0]
    tm, tn = 1024, 512
    nj = D_FF // tn
    return pl.pallas_call(
        functools.partial(_ffn_up_kernel, tiles_per_batch=seq // tm),
        grid=(nj, t // tm),
        in_specs=[pl.BlockSpec((tm, D_MODEL), lambda j, i: (i, 0)),
                  pl.BlockSpec((D_MODEL, tn), lambda j, i: (0, j)),
                  pl.BlockSpec((D_MODEL, tn), lambda j, i: (0, nj + j)),
                  pl.BlockSpec((CONV_WIDTH, tn), lambda j, i: (0, j)),
                  pl.BlockSpec((CONV_WIDTH, tn), lambda j, i: (0, nj + j)),
                  pl.BlockSpec((1, tn), lambda j, i: (0, j)),
                  pl.BlockSpec((1, tn), lambda j, i: (0, nj + j))],
        out_specs=pl.BlockSpec((tm, tn), lambda j, i: (i, j)),
        out_shape=jax.ShapeDtypeStruct((t, D_FF), BF16),
        scratch_shapes=[pltpu.VMEM((D_MODEL, 2 * tn), BF16),
                        pltpu.VMEM((tm + SUBLANES, 2 * tn), F32)],
        compiler_params=_params(("arbitrary", "arbitrary"), 56),
        name="ffn_up",
    )(h2, w_up, w_up, conv_w, conv_w, conv_b, conv_b)


def _ffn_down_kernel(g_ref, w_ref, x1_ref, gf_ref, gpost_ref, o_ref):
    f = jnp.dot(g_ref[...], w_ref[...], preferred_element_type=F32)
    o_ref[...] = x1_ref[...] + _rms(f, gf_ref[0] * gpost_ref[...])


def _ffn_down(g, w, x1, gate_f, g_post, seq):
    t = g.shape[0]
    tm = 256
    tpb = seq // tm
    return pl.pallas_call(
        _ffn_down_kernel,
        grid=(t // tm,),
        in_specs=[pl.BlockSpec((tm, D_FF), lambda i: (i, 0)),
                  pl.BlockSpec((D_FF, D_MODEL), lambda i: (0, 0), pipeline_mode=pl.Buffered(1)),
                  pl.BlockSpec((tm, D_MODEL), lambda i: (i, 0)),
                  pl.BlockSpec((1, 1, D_MODEL), lambda i: (i // tpb, 0, 0)),
                  pl.BlockSpec((1, D_MODEL), lambda i: (0, 0))],
        out_specs=pl.BlockSpec((tm, D_MODEL), lambda i: (i, 0)),
        out_shape=jax.ShapeDtypeStruct((t, D_MODEL), F32),
        compiler_params=_params(("arbitrary",), 56),
        name="ffn_down",
    )(g, w, x1, gate_f, g_post)


def _swap_halves(w):
    half = w.shape[-1] // 2
    return jnp.concatenate([w[..., half:], w[..., :half]], axis=-1)


def _w_in_special(w_t):
    half = MLA_ROPE // 2
    k_rope = w_t[_SRC_KROPE:_SRC_KROPE + MLA_ROPE]
    g_lr = w_t[_SRC_GLR:_SRC_GLR + GLA_GATE_RANK]
    pad = jnp.zeros((Z_GQ - Z_GLR - 2 * GLA_GATE_RANK, w_t.shape[1]), w_t.dtype)
    return jnp.concatenate([k_rope, k_rope[half:], k_rope[:half], g_lr, g_lr, pad], axis=0)


def _prep_w_q(w_q_up):
    w = w_q_up.reshape(MLA_Q_RANK, MLA_HEADS, MLA_NOPE + MLA_ROPE)
    pe = w[:, :, MLA_NOPE:]
    w = jnp.concatenate([w, _swap_halves(pe)], axis=-1)
    return jnp.transpose(w, (1, 0, 2)).astype(BF16)


def kernel(x, c, positions, w_ada, b_ada, g_pre_mix, w_in, g_q_lat, w_q_up, g_kv_lat, w_kv_up,
           w_gla_gate_up, b_gla_gate, g_gla_out, w_branch_a, w_branch_b, w_mix_out, g_post_mix,
           g_pre_ffn, w_ffn_up, conv_w, conv_b, w_ffn_down, g_post_ffn):
    batch, seq, _ = x.shape
    depth = w_ada.shape[0]
    t = batch * seq
    row = lambda v: v.reshape(1, -1)

    inv = 1.0 / (ROPE_THETA ** (jnp.arange(0, MLA_ROPE, 2, dtype=F32) / MLA_ROPE))
    cos, sin = _rope_tables(positions, inv)

    c_pad = jnp.zeros((SUBLANES, D_MODEL), F32).at[:batch].set(c)
    x2 = x.reshape(t, D_MODEL)
    for l in range(depth):
        mod = _ada(c_pad, w_ada[l], row(b_ada[l]))[:batch]
        shift_m, scale_m, gate_m, shift_f, scale_f, gate_f = (
            m.reshape(batch, 1, D_MODEL) for m in jnp.split(mod, 6, axis=-1))

        h = _norm_mod(x2, row(g_pre_mix[l]), scale_m, shift_m, seq)
        w_in_t = w_in[l].T
        z = _mm_in(h, w_in_t, _w_in_special(w_in_t))

        q, k, v = _mla_proj(z, cos, sin, row(g_q_lat[l]), row(g_kv_lat[l]),
                            _prep_w_q(w_q_up[l]), w_kv_up[l].astype(BF16), batch, seq)
        o_a, w_a_bf, w_b_bf = _attn(q, k, v, [w_branch_a[l], w_branch_b[l]], batch, seq)

        wg_hi, wg_lo = _split_bf16(w_gla_gate_up[l])
        wg = jnp.zeros((LANES, GLA_HEADS * GLA_DK), BF16)
        wg = wg.at[:GLA_GATE_RANK].set(wg_hi).at[GLA_GATE_RANK:2 * GLA_GATE_RANK].set(wg_lo)
        o_b, w_down_bf, w_mix_bf = _gla(z, wg, row(b_gla_gate[l]), row(g_gla_out[l]),
                                        [w_ffn_down[l], w_mix_out[l]], batch, seq)

        x1, h2 = _mix(o_a, o_b, z, w_a_bf, w_b_bf, w_mix_bf, x2, gate_m, row(g_post_mix[l]),
                      row(g_pre_ffn[l]), scale_f, shift_f, seq)

        g = _ffn_up(h2, w_ffn_up[l], conv_w[l], row(conv_b[l]), seq)
        x2 = _ffn_down(g, w_down_bf, x1, gate_f, row(g_post_ffn[l]), seq)
    return x2.reshape(batch, seq, D_MODEL)
```

```python
import functools

import jax
import jax.numpy as jnp
import numpy as np
from jax import lax
from jax.experimental import pallas as pl
from jax.experimental.pallas import tpu as pltpu

F32 = jnp.float32
BF16 = jnp.bfloat16

D_MODEL = 2048
MLA_HEADS = 8
MLA_Q_RANK = 512
MLA_KV_RANK = 256
MLA_NOPE = 128
MLA_ROPE = 64
MLA_V = 128
ROPE_THETA = 10000.0
GLA_HEADS = 4
GLA_DK = 256
GLA_DV = 512
GLA_GATE_RANK = 16
GLA_TAU = 16.0
D_FF = 5632
CONV_WIDTH = 3
EPS = 1e-6

LANES = 128
SUBLANES = 8
MLA_QK_PAD = 256

Z_QLAT = 0
Z_KVLAT = 512
Z_KROPE = 768
Z_GLR = 896
Z_GQ = 1024
Z_GV = 2048
Z_OG = 4096
Z_GATE_A = 6144
Z_GATE_B = 8192
Z_GK = 10240
Z_WIDTH = 11264

_SRC_KROPE = MLA_Q_RANK + MLA_KV_RANK
_SRC_GQ = _SRC_KROPE + MLA_ROPE
_SRC_GK = _SRC_GQ + GLA_HEADS * GLA_DK
_SRC_GV = _SRC_GK + GLA_HEADS * GLA_DK
_SRC_GLR = _SRC_GV + GLA_HEADS * GLA_DV
_SRC_OG = _SRC_GLR + GLA_GATE_RANK
_SRC_GATE_A = _SRC_OG + GLA_HEADS * GLA_DV
_SRC_GATE_B = _SRC_GATE_A + D_MODEL

GLA_CHUNK = 128
GLA_LEVELS = 7

MIB = 1024 * 1024


def _params(semantics, vmem_mib):
    return pltpu.CompilerParams(dimension_semantics=semantics,
                                vmem_limit_bytes=vmem_mib * MIB)


def _rms(x, g):
    return x * lax.rsqrt(jnp.mean(x * x, axis=-1, keepdims=True) + EPS) * g


RIDER_ROWS = 256


def _rider_specs(arrays, step_of):
    specs, windows, start = [], [], 0
    for a in arrays:
        n = a.shape[0] // RIDER_ROWS
        specs.append(pl.BlockSpec(
            (RIDER_ROWS, a.shape[1]),
            lambda *idx, lo=start, n=n: (jnp.clip(step_of(*idx) - lo, 0, n - 1), 0)))
        windows.append((start, start + n))
        start += n
    return specs, windows, start


def _run_riders(step, windows, src_refs, dst_refs):
    del step, windows
    for src, dst in zip(src_refs, dst_refs):
        dst[...] = src[...].astype(BF16)


def _ada_kernel(c_ref, w_ref, b_ref, o_ref):
    c = c_ref[...]
    ca = c * jax.nn.sigmoid(c)
    o_ref[...] = jnp.dot(ca.astype(BF16), w_ref[...].astype(BF16),
                         preferred_element_type=F32) + b_ref[...]


def _ada(c_pad, w, b):
    n = w.shape[1]
    tn = 1024
    return pl.pallas_call(
        _ada_kernel,
        grid=(n // tn,),
        in_specs=[pl.BlockSpec((SUBLANES, D_MODEL), lambda j: (0, 0)),
                  pl.BlockSpec((D_MODEL, tn), lambda j: (0, j)),
                  pl.BlockSpec((1, tn), lambda j: (0, j))],
        out_specs=pl.BlockSpec((SUBLANES, tn), lambda j: (0, j)),
        out_shape=jax.ShapeDtypeStruct((SUBLANES, n), F32),
        compiler_params=_params(("arbitrary",), 32),
        name="ada",
    )(c_pad, w, b)


def _norm_mod_kernel(x_ref, g_ref, sc_ref, sh_ref, o_ref):
    y = _rms(x_ref[...], g_ref[...] * (1.0 + sc_ref[0]))
    o_ref[...] = (y + sh_ref[0]).astype(o_ref.dtype)


def _norm_mod(x2, g, scale, shift, seq):
    t = x2.shape[0]
    tm = 1024
    tpb = seq // tm
    return pl.pallas_call(
        _norm_mod_kernel,
        grid=(t // tm,),
        in_specs=[pl.BlockSpec((tm, D_MODEL), lambda i: (i, 0)),
                  pl.BlockSpec((1, D_MODEL), lambda i: (0, 0)),
                  pl.BlockSpec((1, 1, D_MODEL), lambda i: (i // tpb, 0, 0)),
                  pl.BlockSpec((1, 1, D_MODEL), lambda i: (i // tpb, 0, 0))],
        out_specs=pl.BlockSpec((tm, D_MODEL), lambda i: (i, 0)),
        out_shape=jax.ShapeDtypeStruct((t, D_MODEL), BF16),
        compiler_params=_params(("arbitrary",), 32),
        name="norm_mod",
    )(x2, g, scale, shift)


MM_IN_TN = 1024
MM_IN_SRC_ALIGN = 16
_MM_IN_SRC_ROWS = (0,
                   _SRC_GQ,
                   _SRC_GV, _SRC_GV + MM_IN_TN,
                   _SRC_OG, _SRC_OG + MM_IN_TN,
                   _SRC_GATE_A, _SRC_GATE_A + MM_IN_TN,
                   _SRC_GATE_B, _SRC_GATE_B + MM_IN_TN,
                   _SRC_GK)


def _mm_in_kernel(src_ref, h_ref, w_ref, wsp_ref, o_ref, w_bf):
    del src_ref
    j, i = pl.program_id(0), pl.program_id(1)

    @pl.when(i == 0)
    def _():
        w_bf[...] = w_ref[...].astype(BF16)

    @pl.when((i == 0) & (j == 0))
    def _():
        w_bf[Z_KROPE:Z_GQ] = wsp_ref[...].astype(BF16)

    for r0 in range(0, o_ref.shape[0], MM_IN_SUBTILE):
        rows = slice(r0, r0 + MM_IN_SUBTILE)
        o_ref[rows] = lax.dot_general(h_ref[rows], w_bf[...], (((1,), (1,)), ((), ())),
                                      preferred_element_type=F32).astype(o_ref.dtype)


MM_IN_SUBTILE = 1024


def _mm_in(h, w_t, w_special):
    t = h.shape[0]
    tm, tn = 2048, MM_IN_TN
    grid_spec = pltpu.PrefetchScalarGridSpec(
        num_scalar_prefetch=1,
        grid=(Z_WIDTH // tn, t // tm),
        in_specs=[pl.BlockSpec((tm, D_MODEL), lambda j, i, src: (i, 0)),
                  pl.BlockSpec((pl.Element(tn), pl.Element(D_MODEL)),
                               lambda j, i, src: (src[j] * MM_IN_SRC_ALIGN, 0)),
                  pl.BlockSpec((Z_GQ - Z_KROPE, D_MODEL), lambda j, i, src: (0, 0))],
        out_specs=pl.BlockSpec((tm, tn), lambda j, i, src: (i, j)),
        scratch_shapes=[pltpu.VMEM((tn, D_MODEL), BF16)])
    return pl.pallas_call(
        _mm_in_kernel,
        grid_spec=grid_spec,
        out_shape=jax.ShapeDtypeStruct((t, Z_WIDTH), BF16),
        compiler_params=_params(("arbitrary", "arbitrary"), 56),
        name="mm_in",
    )(jnp.asarray([r // MM_IN_SRC_ALIGN for r in _MM_IN_SRC_ROWS], jnp.int32), h, w_t, w_special)


ROPE_HALF = MLA_ROPE // 2
ROPE_PACK = LANES // ROPE_HALF


def _rope_kernel(pos_ref, inv_ref, cos_ref, sin_ref):
    ang = pos_ref[...].astype(F32) * inv_ref[...]
    cos_ref[...] = jnp.cos(ang)
    sin_ref[...] = jnp.sin(ang)


def _rope_tables(positions, inv):
    t = positions.size
    rows = t // ROPE_PACK
    pos_dense = jnp.repeat(positions.reshape(t), ROPE_HALF).reshape(rows, LANES)
    inv_dense = jnp.tile(inv, ROPE_PACK).reshape(1, LANES)
    tr = 512
    spec = pl.BlockSpec((tr, LANES), lambda i: (i, 0))
    cos, sin = pl.pallas_call(
        _rope_kernel,
        grid=(rows // tr,),
        in_specs=[spec, pl.BlockSpec((1, LANES), lambda i: (0, 0))],
        out_specs=[spec, spec],
        out_shape=[jax.ShapeDtypeStruct((rows, LANES), F32)] * 2,
        compiler_params=_params(("arbitrary",), 32),
        name="rope_tables",
    )(pos_dense, inv_dense)
    return cos.reshape(t, ROPE_HALF), sin.reshape(t, ROPE_HALF)


def _mla_proj_kernel(ql_ref, kvl_ref, kr_ref, cos_ref, sin_ref,
                     gq_ref, gkv_ref, wq_ref, wkv_ref, q_ref, k_ref, v_ref):
    c, s = cos_ref[...], sin_ref[...]
    zeros = jnp.zeros((c.shape[0], LANES - MLA_ROPE), F32)
    cosm = jnp.concatenate([c, c, zeros], axis=1)
    sinm = jnp.concatenate([-s, s, zeros], axis=1)

    def rope(r):
        return r * cosm + pltpu.roll(r, MLA_ROPE, 1) * sinm

    scale = (MLA_NOPE + MLA_ROPE) ** -0.5 * np.log2(np.e)
    qn =_rms(ql_ref[...].astype(F32), gq_ref[...]).astype(BF16)
    for h in range(MLA_HEADS):
        r = jnp.dot(qn, wq_ref[h], preferred_element_type=F32) * scale
        q_ref[0, h, :, 0:MLA_NOPE] = r[:, 0:MLA_NOPE].astype(BF16)
        q_ref[0, h, :, MLA_NOPE:MLA_QK_PAD] = rope(r[:, MLA_NOPE:MLA_QK_PAD]).astype(BF16)

    kvn = _rms(kvl_ref[...].astype(F32), gkv_ref[...]).astype(BF16)
    kv = jnp.dot(kvn, wkv_ref[...], preferred_element_type=F32).astype(BF16)
    kpe = rope(kr_ref[...].astype(F32)).astype(BF16)
    hw = MLA_NOPE + MLA_V
    for h in range(MLA_HEADS):
        k_ref[0, h, :, 0:MLA_NOPE] = kv[:, h * hw:h * hw + MLA_NOPE]
        k_ref[0, h, :, MLA_NOPE:MLA_QK_PAD] = kpe
        v_ref[0, h] = kv[:, h * hw + MLA_NOPE:(h + 1) * hw]


def _mla_proj(z, cos, sin, gq, gkv, wq, wkv, batch, seq):
    tm = 1024
    tpb = seq // tm
    qk_shape = jax.ShapeDtypeStruct((batch, MLA_HEADS, seq, MLA_QK_PAD), BF16)
    v_shape = jax.ShapeDtypeStruct((batch, MLA_HEADS, seq, MLA_V), BF16)
    const2 = lambda i: (0, 0)
    qk_spec = pl.BlockSpec((1, MLA_HEADS, tm, MLA_QK_PAD), lambda i: (i // tpb, 0, i % tpb, 0))
    return pl.pallas_call(
        _mla_proj_kernel,
        grid=(batch * tpb,),
        in_specs=[pl.BlockSpec((tm, MLA_Q_RANK), lambda i: (i, Z_QLAT // MLA_Q_RANK)),
                  pl.BlockSpec((tm, MLA_KV_RANK), lambda i: (i, Z_KVLAT // MLA_KV_RANK)),
                  pl.BlockSpec((tm, LANES), lambda i: (i, Z_KROPE // LANES)),
                  pl.BlockSpec((tm, ROPE_HALF), lambda i: (i, 0)),
                  pl.BlockSpec((tm, ROPE_HALF), lambda i: (i, 0)),
                  pl.BlockSpec((1, MLA_Q_RANK), const2),
                  pl.BlockSpec((1, MLA_KV_RANK), const2),
                  pl.BlockSpec((MLA_HEADS, MLA_Q_RANK, MLA_QK_PAD), lambda i: (0, 0, 0)),
                  pl.BlockSpec((MLA_KV_RANK, MLA_HEADS * (MLA_NOPE + MLA_V)), const2)],
        out_specs=[qk_spec, qk_spec,
                   pl.BlockSpec((1, MLA_HEADS, tm, MLA_V), lambda i: (i // tpb, 0, i % tpb, 0))],
        out_shape=[qk_shape, qk_shape, v_shape],
        compiler_params=_params(("arbitrary",), 48),
        name="mla_proj",
    )(z, z, z, cos, sin, gq, gkv, wq, wkv)


ATTN_KV_BLOCK = 1024
ATTN_Q_BLOCKS = 4
ATTN_DIAG_CHAINS = 2


def _attn_kernel(q_ref, k_ref, v_ref, *refs, rider_windows):
    nr = len(rider_windows)
    rider_src, o_ref, rider_dst = refs[:nr], refs[nr], refs[nr + 1:2 * nr + 1]
    m_sc, l_sc, acc_sc = refs[2 * nr + 1:]
    blk = ATTN_KV_BLOCK
    nt = (((1,), (1,)), ((), ()))
    qi = pl.program_id(2)
    step = ((pl.program_id(0) * pl.num_programs(1) + pl.program_id(1)) * pl.num_programs(2) + qi)
    _run_riders(step, rider_windows, rider_src, rider_dst)
    m_sc[...] = jnp.full(m_sc.shape, -jnp.inf, F32)
    l_sc[...] = jnp.zeros(l_sc.shape, F32)
    acc_sc[...] = jnp.zeros(acc_sc.shape, F32)

    def update(rows, s, v):
        m_prev = m_sc[rows]
        m_new = jnp.maximum(m_prev, jnp.max(s, axis=-1, keepdims=True))
        alpha = jnp.exp2(m_prev - m_new)
        ps = [jnp.exp2(s[:, c * LANES:(c + 1) * LANES] - m_new)
              for c in range(s.shape[1] // LANES)]
        l_sc[rows] = alpha * l_sc[rows] + functools.reduce(lambda a, b: a + b, ps)
        p = jnp.concatenate(ps, axis=1).astype(BF16)
        acc_sc[rows] = alpha * acc_sc[rows] + jnp.dot(p, v, preferred_element_type=F32)
        m_sc[rows] = m_new

    def full_block(rows, ks):
        k = k_ref[0, 0, pl.ds(ks, blk), :]
        v = v_ref[0, 0, pl.ds(ks, blk), :]
        update(rows, lax.dot_general(q_ref[0, 0, rows], k, nt, preferred_element_type=F32), v)

    def diag_block(row0, ks):
        sub = blk // ATTN_DIAG_CHAINS
        for r in range(ATTN_DIAG_CHAINS):
            n = (r + 1) * sub
            rows = slice(row0 + r * sub, row0 + n)
            k = k_ref[0, 0, pl.ds(ks, n), :]
            v = v_ref[0, 0, pl.ds(ks, n), :]
            s = lax.dot_general(q_ref[0, 0, rows], k, nt, preferred_element_type=F32)
            row = lax.broadcasted_iota(jnp.int32, s.shape, 0) + r * sub
            col = lax.broadcasted_iota(jnp.int32, s.shape, 1)
            update(rows, jnp.where(col <= row, s, -jnp.inf), v)

    def body(kb, carry):
        full_block(slice(0, ATTN_Q_BLOCKS * blk), pl.multiple_of(kb * blk, blk))
        return carry

    lax.fori_loop(0, qi * ATTN_Q_BLOCKS, body, 0)

    base = qi * ATTN_Q_BLOCKS * blk
    for a in range(ATTN_Q_BLOCKS):
        for kb in range(a):
            full_block(slice(a * blk, (a + 1) * blk), pl.multiple_of(base + kb * blk, blk))
        diag_block(a * blk, pl.multiple_of(base + a * blk, blk))
    l = jnp.sum(l_sc[...], axis=-1, keepdims=True)
    o_ref[...] = (acc_sc[...] / l).astype(o_ref.dtype)


def _attn(q, k, v, riders, batch, seq):
    blk = ATTN_Q_BLOCKS * ATTN_KV_BLOCK
    nq = seq // blk
    rider_specs, windows, steps = _rider_specs(
        riders, lambda b, h, i: (b * MLA_HEADS + h) * nq + i)
    assert steps <= batch * MLA_HEADS * nq
    return pl.pallas_call(
        functools.partial(_attn_kernel, rider_windows=windows),
        grid=(batch, MLA_HEADS, nq),
        in_specs=[pl.BlockSpec((1, 1, blk, MLA_QK_PAD), lambda b, h, i: (b, h, i, 0)),
                  pl.BlockSpec((1, 1, seq, MLA_QK_PAD), lambda b, h, i: (b, h, 0, 0)),
                  pl.BlockSpec((1, 1, seq, MLA_V), lambda b, h, i: (b, h, 0, 0))] + rider_specs,
        out_specs=[pl.BlockSpec((blk, MLA_V), lambda b, h, i: (b * nq + i, h))] + rider_specs,
        out_shape=([jax.ShapeDtypeStruct((batch * seq, MLA_HEADS * MLA_V), BF16)]
                   + [jax.ShapeDtypeStruct(w.shape, BF16) for w in riders]),
        scratch_shapes=[pltpu.VMEM((blk, LANES), F32), pltpu.VMEM((blk, LANES), F32),
                        pltpu.VMEM((blk, MLA_V), F32)],
        compiler_params=_params(("arbitrary", "arbitrary", "arbitrary"), 58),
        name="attn",
    )(q, k, v, *riders)


def _gla_constants():
    c = GLA_CHUNK
    idx = np.arange(c)
    w = np.zeros((GLA_LEVELS + 2, c, c), np.float32)
    masks = np.zeros((GLA_LEVELS + 1, c, c), np.float32)
    for l in range(GLA_LEVELS):
        half = 1 << l
        blk = idx // (2 * half)
        mid = blk * 2 * half + half - 1
        upper = (idx % (2 * half)) >= half
        t = idx[None, :]
        up_rows = (t > mid[:, None]) & (t <= idx[:, None])
        lo_rows = (t > idx[:, None]) & (t <= mid[:, None])
        w[l] = np.where(upper[:, None], up_rows, lo_rows)
        masks[l] = (blk[:, None] == blk[None, :]) & upper[:, None] & (~upper[None, :])
    w[GLA_LEVELS] = idx[None, :] <= idx[:, None]
    w[GLA_LEVELS + 1] = idx[None, :] > idx[:, None]
    masks[GLA_LEVELS] = np.eye(c)
    w = w.reshape(-1, c)
    return np.concatenate([w, w], axis=1), masks


def _split_bf16(x):
    hi = x.astype(BF16)
    return hi, (x - hi.astype(F32)).astype(BF16)


GLA_CHUNKS_PER_STEP = 2
GLA_HEAD_GROUP = 2
GLA_STAGGER = 2


def _gla_kernel(q_ref, k_ref, v_ref, lr_ref, og_ref, wg_ref, bg_ref, go_ref, ws_ref, mask_ref,
                *refs, rider_windows):
    nr = len(rider_windows)
    rider_src, o_ref, rider_dst, st_ref = refs[:nr], refs[nr], refs[nr + 1:2 * nr + 1], refs[-1]
    c = GLA_CHUNK
    dk, dv = GLA_DK, GLA_DV
    nt = (((1,), (1,)), ((), ()))
    _run_riders(pl.program_id(0) * pl.num_programs(1) + pl.program_id(1), rider_windows,
                rider_src, rider_dst)

    @pl.when(pl.program_id(1) == 0)
    def _():
        st_ref[...] = jnp.zeros(st_ref.shape, F32)

    def rows(ci):
        return slice(ci * c, (ci + 1) * c)

    def qk(ci, h):
        q = q_ref[rows(ci), h * dk:(h + 1) * dk] * (dk ** -0.5)
        return q, k_ref[rows(ci), h * dk:(h + 1) * dk]

    def gate(ci, h):
        lr = lr_ref[rows(ci)]
        x = (jnp.dot(lr, wg_ref[:, h * dk:(h + 1) * dk], preferred_element_type=F32)
             + bg_ref[:, h * dk:(h + 1) * dk])
        log2_a = (-(jnp.maximum(-x, 0.0) + jnp.log1p(jnp.exp(-jnp.abs(x))))
                  * (np.log2(np.e) / GLA_TAU))
        return jnp.concatenate(_split_bf16(log2_a), axis=0)

    def exponents(log2_a):
        return jnp.dot(ws_ref[...], log2_a, preferred_element_type=F32)

    def level_operands(ci, h, pb):
        q, k = qk(ci, h)
        ops = [(q * pb[l * c:(l + 1) * c], k * pb[l * c:(l + 1) * c]) for l in range(GLA_LEVELS)]
        return ops + [(q, k)]

    def level_scores(ops):
        return [lax.dot_general(a, b, nt, preferred_element_type=F32) for a, b in ops]

    def masked_sum(scores):
        attn = mask_ref[0] * scores[0]
        for l in range(1, GLA_LEVELS + 1):
            attn = attn + mask_ref[l] * scores[l]
        return attn.astype(BF16)

    def mix(ci, h, p, pb, attn):
        q, k = qk(ci, h)
        v = v_ref[rows(ci), h * dv:(h + 1) * dv]
        eb = pb[GLA_LEVELS * c:(GLA_LEVELS + 1) * c]
        ebl = pb[(GLA_LEVELS + 1) * c:(GLA_LEVELS + 2) * c]
        st = st_ref[h]
        o = (lax.dot_general(q * eb, st.astype(BF16), nt, preferred_element_type=F32)
             + jnp.dot(attn, v, preferred_element_type=F32))
        v_t = v.astype(F32).T.astype(BF16)
        decay = p[(GLA_LEVELS + 1) * c - 1:(GLA_LEVELS + 1) * c]
        st_ref[h] = st * decay + jnp.dot(v_t, k * ebl, preferred_element_type=F32)
        return o

    def emit(ci, h, o):
        y = _rms(o, go_ref[...])
        og = og_ref[rows(ci), h * dv:(h + 1) * dv].astype(F32)
        o_ref[rows(ci), h * dv:(h + 1) * dv] = (y * (og * jax.nn.sigmoid(og))).astype(o_ref.dtype)

    def chunk_stages(ci, heads):
        log2_a = {h: gate(ci, h) for h in heads}
        yield
        expo = {h: exponents(log2_a[h]) for h in heads}
        p = {h: jnp.exp2(expo[h]) for h in heads}
        pb = {h: p[h].astype(BF16) for h in heads}
        yield
        operands = {h: level_operands(ci, h, pb[h]) for h in heads}
        scores = {h: level_scores(operands[h]) for h in heads}
        yield
        attn = {h: masked_sum(scores[h]) for h in heads}
        out = {h: mix(ci, h, p[h], pb[h], attn[h]) for h in heads}
        yield
        for h in heads:
            emit(ci, h, out[h])
        yield

    live = []
    pending = [chunk_stages(ci, range(h0, h0 + GLA_HEAD_GROUP))
               for ci in range(GLA_CHUNKS_PER_STEP)
               for h0 in range(0, GLA_HEADS, GLA_HEAD_GROUP)]
    tick = 0
    while pending or live:
        if pending and tick % GLA_STAGGER == 0:
            live.append(pending.pop(0))
        for gen in list(live):
            if next(gen, "done") == "done":
                live.remove(gen)
        tick += 1


def _gla(z, wg, bg, go, riders, batch, seq):
    c = GLA_CHUNK * GLA_CHUNKS_PER_STEP
    nc = seq // c
    ws, masks = _gla_constants()
    hk, hv = GLA_HEADS * GLA_DK, GLA_HEADS * GLA_DV
    const2 = lambda b, i: (0, 0)
    rider_specs, windows, steps = _rider_specs(riders, lambda b, i: b * nc + i)
    assert steps <= batch * nc
    return pl.pallas_call(
        functools.partial(_gla_kernel, rider_windows=windows),
        grid=(batch, nc),
        in_specs=[pl.BlockSpec((c, hk), lambda b, i: (b * nc + i, Z_GQ // hk)),
                  pl.BlockSpec((c, hk), lambda b, i: (b * nc + i, Z_GK // hk)),
                  pl.BlockSpec((c, hv), lambda b, i: (b * nc + i, Z_GV // hv)),
                  pl.BlockSpec((c, LANES), lambda b, i: (b * nc + i, Z_GLR // LANES)),
                  pl.BlockSpec((c, hv), lambda b, i: (b * nc + i, Z_OG // hv)),
                  pl.BlockSpec((LANES, hk), const2),
                  pl.BlockSpec((1, hk), const2),
                  pl.BlockSpec((1, GLA_DV), const2),
                  pl.BlockSpec(ws.shape, const2),
                  pl.BlockSpec(masks.shape, lambda b, i: (0, 0, 0))] + rider_specs,
        out_specs=[pl.BlockSpec((c, hv), lambda b, i: (b * nc + i, 0))] + rider_specs,
        out_shape=([jax.ShapeDtypeStruct((batch * seq, hv), BF16)]
                   + [jax.ShapeDtypeStruct(w.shape, BF16) for w in riders]),
        scratch_shapes=[pltpu.VMEM((GLA_HEADS, GLA_DV, GLA_DK), F32)],
        compiler_params=_params(("arbitrary", "arbitrary"), 48),
        name="gla",
    )(z, z, z, z, z, wg, bg, go, jnp.asarray(ws, BF16), jnp.asarray(masks), *riders)


MIX_SUBTILE = 128


def _mix_kernel(oa_ref, ob_ref, ga_ref, gb_ref, wa_ref, wb_ref, wm_ref, x_ref, gm_ref, gpost_ref,
                gpre_ref, sc_ref, sh_ref, x1_ref, h2_ref):
    post = gm_ref[0] * gpost_ref[...]
    pre = gpre_ref[...] * (1.0 + sc_ref[0])

    def sub_tile(rows):
        ya = jnp.dot(oa_ref[rows], wa_ref[...], preferred_element_type=F32)
        yb = jnp.dot(ob_ref[rows], wb_ref[...], preferred_element_type=F32)
        yield
        merged = (jax.nn.sigmoid(ga_ref[rows].astype(F32)) * ya
                  + jax.nn.sigmoid(gb_ref[rows].astype(F32)) * yb).astype(BF16)
        yield
        y = jnp.dot(merged, wm_ref[...], preferred_element_type=F32)
        yield
        x1 = x_ref[rows] + _rms(y, post)
        x1_ref[rows] = x1
        h2_ref[rows] = (_rms(x1, pre) + sh_ref[0]).astype(h2_ref.dtype)
        yield

    pending = [sub_tile(slice(r0, r0 + MIX_SUBTILE))
               for r0 in range(0, x_ref.shape[0], MIX_SUBTILE)]
    live = []
    while pending or live:
        if pending:
            live.append(pending.pop(0))
        for gen in list(live):
            if next(gen, "done") == "done":
                live.remove(gen)


def _mix(oa, ob, z, wa, wb, wm, x2, gate_m, g_post, g_pre, scale_f, shift_f, seq):
    t = oa.shape[0]
    tm = 256
    tpb = seq // tm
    row = pl.BlockSpec((tm, D_MODEL), lambda i: (i, 0))
    vec = pl.BlockSpec((1, D_MODEL), lambda i: (0, 0))
    per_batch = pl.BlockSpec((1, 1, D_MODEL), lambda i: (i // tpb, 0, 0))
    resident = lambda w: pl.BlockSpec(w.shape, lambda i: (0, 0), pipeline_mode=pl.Buffered(1))
    return pl.pallas_call(
        _mix_kernel,
        grid=(t // tm,),
        in_specs=[pl.BlockSpec((tm, oa.shape[1]), lambda i: (i, 0)),
                  pl.BlockSpec((tm, ob.shape[1]), lambda i: (i, 0)),
                  pl.BlockSpec((tm, D_MODEL), lambda i: (i, Z_GATE_A // D_MODEL)),
                  pl.BlockSpec((tm, D_MODEL), lambda i: (i, Z_GATE_B // D_MODEL)),
                  resident(wa), resident(wb), resident(wm),
                  row, per_batch, vec, vec, per_batch, per_batch],
        out_specs=[row, row],
        out_shape=[jax.ShapeDtypeStruct((t, D_MODEL), F32),
                   jax.ShapeDtypeStruct((t, D_MODEL), BF16)],
        compiler_params=_params(("arbitrary",), 56),
        name="mix",
    )(oa, ob, z, z, wa, wb, wm, x2, gate_m, g_post, g_pre, scale_f, shift_f)


def _ffn_up_kernel(h_ref, wa_ref, wv_ref, cwa_ref, cwv_ref, cba_ref, cbv_ref, o_ref,
                   w_bf, u_sc, *, tiles_per_batch):
    i = pl.program_id(1)
    tm, tn = o_ref.shape
    head, body, tail = slice(0, SUBLANES), slice(SUBLANES, SUBLANES + tm), slice(tm, tm + SUBLANES)

    @pl.when(i == 0)
    def _():
        w_bf[:, 0:tn] = wa_ref[...].astype(BF16)
        w_bf[:, tn:2 * tn] = wv_ref[...].astype(BF16)

    @pl.when(i % tiles_per_batch == 0)
    def _():
        u_sc[head] = jnp.zeros((SUBLANES, u_sc.shape[1]), F32)

    @pl.when(i % tiles_per_batch != 0)
    def _():
        u_sc[head] = u_sc[tail]

    u_sc[body] = jnp.dot(h_ref[...], w_bf[...], preferred_element_type=F32)

    def conv(cols, cw_ref, cb_ref):
        cw = cw_ref[...]
        acc = cb_ref[...] + u_sc[body, cols] * cw[CONV_WIDTH - 1:CONV_WIDTH]
        for tap in range(1, CONV_WIDTH):
            acc = acc + (u_sc[SUBLANES - tap:SUBLANES - tap + tm, cols]
                         * cw[CONV_WIDTH - 1 - tap:CONV_WIDTH - tap])
        return acc

    a = conv(slice(0, tn), cwa_ref, cba_ref)
    val = conv(slice(tn, 2 * tn), cwv_ref, cbv_ref)
    o_ref[...] = (jax.nn.gelu(a, approximate=True) * val).astype(o_ref.dtype)


def _ffn_up(h2, w_up, conv_w, conv_b, seq):
    t = h2.shape[0]
    tm, tn = 1024, 512
    nj = D_FF // tn
    return pl.pallas_call(
        functools.partial(_ffn_up_kernel, tiles_per_batch=seq // tm),
        grid=(nj, t // tm),
        in_specs=[pl.BlockSpec((tm, D_MODEL), lambda j, i: (i, 0)),
                  pl.BlockSpec((D_MODEL, tn), lambda j, i: (0, j)),
                  pl.BlockSpec((D_MODEL, tn), lambda j, i: (0, nj + j)),
                  pl.BlockSpec((CONV_WIDTH, tn), lambda j, i: (0, j)),
                  pl.BlockSpec((CONV_WIDTH, tn), lambda j, i: (0, nj + j)),
                  pl.BlockSpec((1, tn), lambda j, i: (0, j)),
                  pl.BlockSpec((1, tn), lambda j, i: (0, nj + j))],
        out_specs=pl.BlockSpec((tm, tn), lambda j, i: (i, j)),
        out_shape=jax.ShapeDtypeStruct((t, D_FF), BF16),
        scratch_shapes=[pltpu.VMEM((D_MODEL, 2 * tn), BF16),
                        pltpu.VMEM((tm + SUBLANES, 2 * tn), F32)],
        compiler_params=_params(("arbitrary", "arbitrary"), 56),
        name="ffn_up",
    )(h2, w_up, w_up, conv_w, conv_w, conv_b, conv_b)


def _ffn_down_kernel(g_ref, w_ref, x1_ref, gf_ref, gpost_ref, o_ref):
    f = jnp.dot(g_ref[...], w_ref[...], preferred_element_type=F32)
    o_ref[...] = x1_ref[...] + _rms(f, gf_ref[0] * gpost_ref[...])


def _ffn_down(g, w, x1, gate_f, g_post, seq):
    t = g.shape[0]
    tm = 256
    tpb = seq // tm
    return pl.pallas_call(
        _ffn_down_kernel,
        grid=(t // tm,),
        in_specs=[pl.BlockSpec((tm, D_FF), lambda i: (i, 0)),
                  pl.BlockSpec((D_FF, D_MODEL), lambda i: (0, 0), pipeline_mode=pl.Buffered(1)),
                  pl.BlockSpec((tm, D_MODEL), lambda i: (i, 0)),
                  pl.BlockSpec((1, 1, D_MODEL), lambda i: (i // tpb, 0, 0)),
                  pl.BlockSpec((1, D_MODEL), lambda i: (0, 0))],
        out_specs=pl.BlockSpec((tm, D_MODEL), lambda i: (i, 0)),
        out_shape=jax.ShapeDtypeStruct((t, D_MODEL), F32),
        compiler_params=_params(("arbitrary",), 56),
        name="ffn_down",
    )(g, w, x1, gate_f, g_post)


def _swap_halves(w):
    half = w.shape[-1] // 2
    return jnp.concatenate([w[..., half:], w[..., :half]], axis=-1)


def _w_in_special(w_t):
    half = MLA_ROPE // 2
    k_rope = w_t[_SRC_KROPE:_SRC_KROPE + MLA_ROPE]
    g_lr = w_t[_SRC_GLR:_SRC_GLR + GLA_GATE_RANK]
    pad = jnp.zeros((Z_GQ - Z_GLR - 2 * GLA_GATE_RANK, w_t.shape[1]), w_t.dtype)
    return jnp.concatenate([k_rope, k_rope[half:], k_rope[:half], g_lr, g_lr, pad], axis=0)


def _prep_w_q(w_q_up):
    w = w_q_up.reshape(MLA_Q_RANK, MLA_HEADS, MLA_NOPE + MLA_ROPE)
    pe = w[:, :, MLA_NOPE:]
    w = jnp.concatenate([w, _swap_halves(pe)], axis=-1)
    return jnp.transpose(w, (1, 0, 2)).astype(BF16)


def kernel(x, c, positions, w_ada, b_ada, g_pre_mix, w_in, g_q_lat, w_q_up, g_kv_lat, w_kv_up,
           w_gla_gate_up, b_gla_gate, g_gla_out, w_branch_a, w_branch_b, w_mix_out, g_post_mix,
           g_pre_ffn, w_ffn_up, conv_w, conv_b, w_ffn_down, g_post_ffn):
    batch, seq, _ = x.shape
    depth = w_ada.shape[0]
    t = batch * seq
    row = lambda v: v.reshape(1, -1)

    inv = 1.0 / (ROPE_THETA ** (jnp.arange(0, MLA_ROPE, 2, dtype=F32) / MLA_ROPE))
    cos, sin = _rope_tables(positions, inv)

    c_pad = jnp.zeros((SUBLANES, D_MODEL), F32).at[:batch].set(c)
    x2 = x.reshape(t, D_MODEL)
    for l in range(depth):
        mod = _ada(c_pad, w_ada[l], row(b_ada[l]))[:batch]
        shift_m, scale_m, gate_m, shift_f, scale_f, gate_f = (
            m.reshape(batch, 1, D_MODEL) for m in jnp.split(mod, 6, axis=-1))

        h = _norm_mod(x2, row(g_pre_mix[l]), scale_m, shift_m, seq)
        w_in_t = w_in[l].T
        z = _mm_in(h, w_in_t, _w_in_special(w_in_t))

        q, k, v = _mla_proj(z, cos, sin, row(g_q_lat[l]), row(g_kv_lat[l]),
                            _prep_w_q(w_q_up[l]), w_kv_up[l].astype(BF16), batch, seq)
        o_a, w_a_bf, w_b_bf = _attn(q, k, v, [w_branch_a[l], w_branch_b[l]], batch, seq)

        wg_hi, wg_lo = _split_bf16(w_gla_gate_up[l])
        wg = jnp.zeros((LANES, GLA_HEADS * GLA_DK), BF16)
        wg = wg.at[:GLA_GATE_RANK].set(wg_hi).at[GLA_GATE_RANK:2 * GLA_GATE_RANK].set(wg_lo)
        o_b, w_down_bf, w_mix_bf = _gla(z, wg, row(b_gla_gate[l]), row(g_gla_out[l]),
                                        [w_ffn_down[l], w_mix_out[l]], batch, seq)

        x1, h2 = _mix(o_a, o_b, z, w_a_bf, w_b_bf, w_mix_bf, x2, gate_m, row(g_post_mix[l]),
                      row(g_pre_ffn[l]), scale_f, shift_f, seq)

        g = _ffn_up(h2, w_ffn_up[l], conv_w[l], row(conv_b[l]), seq)
        x2 = _ffn_down(g, w_down_bf, x1, gate_f, row(g_post_ffn[l]), seq)
    return x2.reshape(batch, seq, D_MODEL)
```

```python
import functools

import jax
import jax.numpy as jnp
import numpy as np
from jax import lax
from jax.experimental import pallas as pl
from jax.experimental.pallas import tpu as pltpu

F32 = jnp.float32
BF16 = jnp.bfloat16

D_MODEL = 2048
MLA_HEADS = 8
MLA_Q_RANK = 512
MLA_KV_RANK = 256
MLA_NOPE = 128
MLA_ROPE = 64
MLA_V = 128
ROPE_THETA = 10000.0
GLA_HEADS = 4
GLA_DK = 256
GLA_DV = 512
GLA_GATE_RANK = 16
GLA_TAU = 16.0
D_FF = 5632
CONV_WIDTH = 3
EPS = 1e-6

LANES = 128
SUBLANES = 8
MLA_QK_PAD = 256
MLA_V_PAD = 256

Z_QLAT = 0
Z_KVLAT = 512
Z_KROPE = 768
Z_GLR = 896
Z_GQ = 1024
Z_GV = 2048
Z_OG = 4096
Z_GATE_A = 6144
Z_GATE_B = 8192
Z_GK = 10240
Z_WIDTH = 11264

_SRC_KROPE = MLA_Q_RANK + MLA_KV_RANK
_SRC_GQ = _SRC_KROPE + MLA_ROPE
_SRC_GK = _SRC_GQ + GLA_HEADS * GLA_DK
_SRC_GV = _SRC_GK + GLA_HEADS * GLA_DK
_SRC_GLR = _SRC_GV + GLA_HEADS * GLA_DV
_SRC_OG = _SRC_GLR + GLA_GATE_RANK
_SRC_GATE_A = _SRC_OG + GLA_HEADS * GLA_DV
_SRC_GATE_B = _SRC_GATE_A + D_MODEL

GLA_CHUNK = 128
GLA_LEVELS = 7

MIB = 1024 * 1024


def _params(semantics, vmem_mib):
    return pltpu.CompilerParams(dimension_semantics=semantics,
                                vmem_limit_bytes=vmem_mib * MIB)


def _rms(x, g):
    return x * lax.rsqrt(jnp.mean(x * x, axis=-1, keepdims=True) + EPS) * g


RIDER_ROWS = 256


def _rider_specs(arrays, step_of):
    specs, windows, start = [], [], 0
    for a in arrays:
        n = a.shape[0] // RIDER_ROWS
        specs.append(pl.BlockSpec(
            (RIDER_ROWS, a.shape[1]),
            lambda *idx, lo=start, n=n: (jnp.clip(step_of(*idx) - lo, 0, n - 1), 0)))
        windows.append((start, start + n))
        start += n
    return specs, windows, start


def _run_riders(step, windows, src_refs, dst_refs):
    for (lo, hi), src, dst in zip(windows, src_refs, dst_refs):
        @pl.when((step >= lo) & (step < hi))
        def _(src=src, dst=dst):
            dst[...] = src[...].astype(BF16)


def _ada_kernel(c_ref, w_ref, b_ref, o_ref):
    c = c_ref[...]
    ca = c * jax.nn.sigmoid(c)
    o_ref[...] = jnp.dot(ca.astype(BF16), w_ref[...].astype(BF16),
                         preferred_element_type=F32) + b_ref[...]


def _ada(c_pad, w, b):
    n = w.shape[1]
    tn = 1024
    return pl.pallas_call(
        _ada_kernel,
        grid=(n // tn,),
        in_specs=[pl.BlockSpec((SUBLANES, D_MODEL), lambda j: (0, 0)),
                  pl.BlockSpec((D_MODEL, tn), lambda j: (0, j)),
                  pl.BlockSpec((1, tn), lambda j: (0, j))],
        out_specs=pl.BlockSpec((SUBLANES, tn), lambda j: (0, j)),
        out_shape=jax.ShapeDtypeStruct((SUBLANES, n), F32),
        compiler_params=_params(("arbitrary",), 32),
        name="ada",
    )(c_pad, w, b)


def _norm_mod_kernel(x_ref, g_ref, sc_ref, sh_ref, o_ref):
    y = _rms(x_ref[...], g_ref[...] * (1.0 + sc_ref[0]))
    o_ref[...] = (y + sh_ref[0]).astype(o_ref.dtype)


def _norm_mod(x2, g, scale, shift, seq):
    t = x2.shape[0]
    tm = 1024
    tpb = seq // tm
    return pl.pallas_call(
        _norm_mod_kernel,
        grid=(t // tm,),
        in_specs=[pl.BlockSpec((tm, D_MODEL), lambda i: (i, 0)),
                  pl.BlockSpec((1, D_MODEL), lambda i: (0, 0)),
                  pl.BlockSpec((1, 1, D_MODEL), lambda i: (i // tpb, 0, 0)),
                  pl.BlockSpec((1, 1, D_MODEL), lambda i: (i // tpb, 0, 0))],
        out_specs=pl.BlockSpec((tm, D_MODEL), lambda i: (i, 0)),
        out_shape=jax.ShapeDtypeStruct((t, D_MODEL), BF16),
        compiler_params=_params(("arbitrary",), 32),
        name="norm_mod",
    )(x2, g, scale, shift)


MM_IN_TN = 1024
MM_IN_SRC_ALIGN = 16
_MM_IN_SRC_ROWS = (0,
                   _SRC_GQ,
                   _SRC_GV, _SRC_GV + MM_IN_TN,
                   _SRC_OG, _SRC_OG + MM_IN_TN,
                   _SRC_GATE_A, _SRC_GATE_A + MM_IN_TN,
                   _SRC_GATE_B, _SRC_GATE_B + MM_IN_TN,
                   _SRC_GK)


def _mm_in_kernel(src_ref, h_ref, w_ref, wsp_ref, o_ref, w_bf):
    del src_ref
    j, i = pl.program_id(0), pl.program_id(1)

    @pl.when(i == 0)
    def _():
        w_bf[...] = w_ref[...].astype(BF16)

    @pl.when((i == 0) & (j == 0))
    def _():
        w_bf[Z_KROPE:Z_GQ] = wsp_ref[...].astype(BF16)

    for r0 in range(0, o_ref.shape[0], MM_IN_SUBTILE):
        rows = slice(r0, r0 + MM_IN_SUBTILE)
        o_ref[rows] = lax.dot_general(h_ref[rows], w_bf[...], (((1,), (1,)), ((), ())),
                                      preferred_element_type=F32).astype(o_ref.dtype)


MM_IN_SUBTILE = 1024


def _mm_in(h, w_t, w_special):
    t = h.shape[0]
    tm, tn = 2048, MM_IN_TN
    grid_spec = pltpu.PrefetchScalarGridSpec(
        num_scalar_prefetch=1,
        grid=(Z_WIDTH // tn, t // tm),
        in_specs=[pl.BlockSpec((tm, D_MODEL), lambda j, i, src: (i, 0)),
                  pl.BlockSpec((pl.Element(tn), pl.Element(D_MODEL)),
                               lambda j, i, src: (src[j] * MM_IN_SRC_ALIGN, 0)),
                  pl.BlockSpec((Z_GQ - Z_KROPE, D_MODEL), lambda j, i, src: (0, 0))],
        out_specs=pl.BlockSpec((tm, tn), lambda j, i, src: (i, j)),
        scratch_shapes=[pltpu.VMEM((tn, D_MODEL), BF16)])
    return pl.pallas_call(
        _mm_in_kernel,
        grid_spec=grid_spec,
        out_shape=jax.ShapeDtypeStruct((t, Z_WIDTH), BF16),
        compiler_params=_params(("arbitrary", "arbitrary"), 56),
        name="mm_in",
    )(jnp.asarray([r // MM_IN_SRC_ALIGN for r in _MM_IN_SRC_ROWS], jnp.int32), h, w_t, w_special)


ROPE_HALF = MLA_ROPE // 2
ROPE_PACK = LANES // ROPE_HALF


def _rope_kernel(pos_ref, inv_ref, cos_ref, sin_ref):
    ang = pos_ref[...].astype(F32) * inv_ref[...]
    cos_ref[...] = jnp.cos(ang)
    sin_ref[...] = jnp.sin(ang)


def _rope_tables(positions, inv):
    t = positions.size
    rows = t // ROPE_PACK
    pos_dense = jnp.repeat(positions.reshape(t), ROPE_HALF).reshape(rows, LANES)
    inv_dense = jnp.tile(inv, ROPE_PACK).reshape(1, LANES)
    tr = 512
    spec = pl.BlockSpec((tr, LANES), lambda i: (i, 0))
    cos, sin = pl.pallas_call(
        _rope_kernel,
        grid=(rows // tr,),
        in_specs=[spec, pl.BlockSpec((1, LANES), lambda i: (0, 0))],
        out_specs=[spec, spec],
        out_shape=[jax.ShapeDtypeStruct((rows, LANES), F32)] * 2,
        compiler_params=_params(("arbitrary",), 32),
        name="rope_tables",
    )(pos_dense, inv_dense)
    return cos.reshape(t, ROPE_HALF), sin.reshape(t, ROPE_HALF)


def _mla_proj_kernel(ql_ref, kvl_ref, kr_ref, cos_ref, sin_ref,
                     gq_ref, gkv_ref, wq_ref, wkv_ref, q_ref, k_ref, v_ref):
    c, s = cos_ref[...], sin_ref[...]
    zeros = jnp.zeros((c.shape[0], LANES - MLA_ROPE), F32)
    cosm = jnp.concatenate([c, c, zeros], axis=1)
    sinm = jnp.concatenate([-s, s, zeros], axis=1)

    def rope(r):
        return r * cosm + pltpu.roll(r, MLA_ROPE, 1) * sinm

    scale = (MLA_NOPE + MLA_ROPE) ** -0.5 * np.log2(np.e)
    qn =_rms(ql_ref[...].astype(F32), gq_ref[...]).astype(BF16)
    for h in range(MLA_HEADS):
        r = jnp.dot(qn, wq_ref[h], preferred_element_type=F32) * scale
        q_ref[0, h, :, 0:MLA_NOPE] = r[:, 0:MLA_NOPE].astype(BF16)
        q_ref[0, h, :, MLA_NOPE:MLA_QK_PAD] = rope(r[:, MLA_NOPE:MLA_QK_PAD]).astype(BF16)

    kvn = _rms(kvl_ref[...].astype(F32), gkv_ref[...]).astype(BF16)
    kv = jnp.dot(kvn, wkv_ref[...], preferred_element_type=F32).astype(BF16)
    kpe = rope(kr_ref[...].astype(F32)).astype(BF16)
    hw = MLA_NOPE + MLA_V
    ones = jnp.ones((kv.shape[0], MLA_V_PAD - MLA_V), BF16)
    for h in range(MLA_HEADS):
        k_ref[0, h, :, 0:MLA_NOPE] = kv[:, h * hw:h * hw + MLA_NOPE]
        k_ref[0, h, :, MLA_NOPE:MLA_QK_PAD] = kpe
        v_ref[0, h, :, 0:MLA_V] = kv[:, h * hw + MLA_NOPE:(h + 1) * hw]
        v_ref[0, h, :, MLA_V:MLA_V_PAD] = ones


def _mla_proj(z, cos, sin, gq, gkv, wq, wkv, batch, seq):
    tm = 1024
    tpb = seq // tm
    qk_shape = jax.ShapeDtypeStruct((batch, MLA_HEADS, seq, MLA_QK_PAD), BF16)
    v_shape = jax.ShapeDtypeStruct((batch, MLA_HEADS, seq, MLA_V_PAD), BF16)
    const2 = lambda i: (0, 0)
    qk_spec = pl.BlockSpec((1, MLA_HEADS, tm, MLA_QK_PAD), lambda i: (i // tpb, 0, i % tpb, 0))
    return pl.pallas_call(
        _mla_proj_kernel,
        grid=(batch * tpb,),
        in_specs=[pl.BlockSpec((tm, MLA_Q_RANK), lambda i: (i, Z_QLAT // MLA_Q_RANK)),
                  pl.BlockSpec((tm, MLA_KV_RANK), lambda i: (i, Z_KVLAT // MLA_KV_RANK)),
                  pl.BlockSpec((tm, LANES), lambda i: (i, Z_KROPE // LANES)),
                  pl.BlockSpec((tm, ROPE_HALF), lambda i: (i, 0)),
                  pl.BlockSpec((tm, ROPE_HALF), lambda i: (i, 0)),
                  pl.BlockSpec((1, MLA_Q_RANK), const2),
                  pl.BlockSpec((1, MLA_KV_RANK), const2),
                  pl.BlockSpec((MLA_HEADS, MLA_Q_RANK, MLA_QK_PAD), lambda i: (0, 0, 0)),
                  pl.BlockSpec((MLA_KV_RANK, MLA_HEADS * (MLA_NOPE + MLA_V)), const2)],
        out_specs=[qk_spec, qk_spec,
                   pl.BlockSpec((1, MLA_HEADS, tm, MLA_V_PAD), lambda i: (i // tpb, 0, i % tpb, 0))],
        out_shape=[qk_shape, qk_shape, v_shape],
        compiler_params=_params(("arbitrary",), 48),
        name="mla_proj",
    )(z, z, z, cos, sin, gq, gkv, wq, wkv)


ATTN_KV_BLOCK = 1024
ATTN_Q_BLOCKS = 4
ATTN_DIAG_CHAINS = 2


def _attn_kernel(q_ref, k_ref, v_ref, *refs, rider_windows):
    nr = len(rider_windows)
    rider_src, o_ref, rider_dst = refs[:nr], refs[nr], refs[nr + 1:2 * nr + 1]
    m_sc, acc_sc = refs[2 * nr + 1:]
    blk = ATTN_KV_BLOCK
    nt = (((1,), (1,)), ((), ()))
    qi = pl.program_id(2)
    step = ((pl.program_id(0) * pl.num_programs(1) + pl.program_id(1)) * pl.num_programs(2) + qi)
    _run_riders(step, rider_windows, rider_src, rider_dst)
    m_sc[...] = jnp.full(m_sc.shape, -jnp.inf, F32)
    acc_sc[...] = jnp.zeros(acc_sc.shape, F32)

    def update(rows, s, v):
        m_prev = m_sc[rows]
        m_new = jnp.maximum(m_prev, jnp.max(s, axis=-1, keepdims=True))
        alpha = jnp.exp2(m_prev - m_new)
        p = jnp.concatenate([jnp.exp2(s[:, c * LANES:(c + 1) * LANES] - m_new)
                             for c in range(s.shape[1] // LANES)], axis=1).astype(BF16)
        acc_sc[rows] = (jnp.concatenate([alpha, alpha], axis=1) * acc_sc[rows]
                        + jnp.dot(p, v, preferred_element_type=F32))
        m_sc[rows] = m_new

    def full_block(rows, ks):
        k = k_ref[0, 0, pl.ds(ks, blk), :]
        v = v_ref[0, 0, pl.ds(ks, blk), :]
        update(rows, lax.dot_general(q_ref[0, 0, rows], k, nt, preferred_element_type=F32), v)

    def diag_block(row0, ks):
        sub = blk // ATTN_DIAG_CHAINS
        for r in range(ATTN_DIAG_CHAINS):
            n = (r + 1) * sub
            rows = slice(row0 + r * sub, row0 + n)
            k = k_ref[0, 0, pl.ds(ks, n), :]
            v = v_ref[0, 0, pl.ds(ks, n), :]
            s = lax.dot_general(q_ref[0, 0, rows], k, nt, preferred_element_type=F32)
            row = lax.broadcasted_iota(jnp.int32, s.shape, 0) + r * sub
            col = lax.broadcasted_iota(jnp.int32, s.shape, 1)
            update(rows, jnp.where(col <= row, s, -jnp.inf), v)

    def body(kb, carry):
        full_block(slice(0, ATTN_Q_BLOCKS * blk), pl.multiple_of(kb * blk, blk))
        return carry

    lax.fori_loop(0, qi * ATTN_Q_BLOCKS, body, 0)

    base = qi * ATTN_Q_BLOCKS * blk
    for a in range(ATTN_Q_BLOCKS):
        for kb in range(a):
            full_block(slice(a * blk, (a + 1) * blk), pl.multiple_of(base + kb * blk, blk))
        diag_block(a * blk, pl.multiple_of(base + a * blk, blk))
    o_ref[...] = (acc_sc[:, 0:MLA_V] / acc_sc[:, MLA_V:MLA_V_PAD]).astype(o_ref.dtype)


def _attn(q, k, v, riders, batch, seq):
    blk = ATTN_Q_BLOCKS * ATTN_KV_BLOCK
    nq = seq // blk
    rider_specs, windows, steps = _rider_specs(
        riders, lambda b, h, i: (b * MLA_HEADS + h) * nq + i)
    assert steps <= batch * MLA_HEADS * nq
    return pl.pallas_call(
        functools.partial(_attn_kernel, rider_windows=windows),
        grid=(batch, MLA_HEADS, nq),
        in_specs=[pl.BlockSpec((1, 1, blk, MLA_QK_PAD), lambda b, h, i: (b, h, i, 0)),
                  pl.BlockSpec((1, 1, seq, MLA_QK_PAD), lambda b, h, i: (b, h, 0, 0)),
                  pl.BlockSpec((1, 1, seq, MLA_V_PAD), lambda b, h, i: (b, h, 0, 0))] + rider_specs,
        out_specs=[pl.BlockSpec((blk, MLA_V), lambda b, h, i: (b * nq + i, h))] + rider_specs,
        out_shape=([jax.ShapeDtypeStruct((batch * seq, MLA_HEADS * MLA_V), BF16)]
                   + [jax.ShapeDtypeStruct(w.shape, BF16) for w in riders]),
        scratch_shapes=[pltpu.VMEM((blk, LANES), F32), pltpu.VMEM((blk, MLA_V_PAD), F32)],
        compiler_params=_params(("arbitrary", "arbitrary", "arbitrary"), 58),
        name="attn",
    )(q, k, v, *riders)


def _gla_constants():
    c = GLA_CHUNK
    idx = np.arange(c)
    w = np.zeros((GLA_LEVELS + 2, c, c), np.float32)
    masks = np.zeros((GLA_LEVELS + 1, c, c), np.float32)
    for l in range(GLA_LEVELS):
        half = 1 << l
        blk = idx // (2 * half)
        mid = blk * 2 * half + half - 1
        upper = (idx % (2 * half)) >= half
        t = idx[None, :]
        up_rows = (t > mid[:, None]) & (t <= idx[:, None])
        lo_rows = (t > idx[:, None]) & (t <= mid[:, None])
        w[l] = np.where(upper[:, None], up_rows, lo_rows)
        masks[l] = (blk[:, None] == blk[None, :]) & upper[:, None] & (~upper[None, :])
    w[GLA_LEVELS] = idx[None, :] <= idx[:, None]
    w[GLA_LEVELS + 1] = idx[None, :] > idx[:, None]
    masks[GLA_LEVELS] = np.eye(c)
    w = w.reshape(-1, c)
    return np.concatenate([w, w], axis=1), masks


def _split_bf16(x):
    hi = x.astype(BF16)
    return hi, (x - hi.astype(F32)).astype(BF16)


GLA_CHUNKS_PER_STEP = 2
GLA_HEAD_GROUP = 2
GLA_STAGGER = 2


def _gla_kernel(q_ref, k_ref, v_ref, lr_ref, og_ref, wg_ref, bg_ref, go_ref, ws_ref, mask_ref,
                *refs, rider_windows):
    nr = len(rider_windows)
    rider_src, o_ref, rider_dst, st_ref = refs[:nr], refs[nr], refs[nr + 1:2 * nr + 1], refs[-1]
    c = GLA_CHUNK
    dk, dv = GLA_DK, GLA_DV
    nt = (((1,), (1,)), ((), ()))
    _run_riders(pl.program_id(0) * pl.num_programs(1) + pl.program_id(1), rider_windows,
                rider_src, rider_dst)

    @pl.when(pl.program_id(1) == 0)
    def _():
        st_ref[...] = jnp.zeros(st_ref.shape, F32)

    def rows(ci):
        return slice(ci * c, (ci + 1) * c)

    def qk(ci, h):
        q = q_ref[rows(ci), h * dk:(h + 1) * dk] * (dk ** -0.5)
        return q, k_ref[rows(ci), h * dk:(h + 1) * dk]

    def gate(ci, h):
        lr = lr_ref[rows(ci)]
        x = (jnp.dot(lr, wg_ref[:, h * dk:(h + 1) * dk], preferred_element_type=F32)
             + bg_ref[:, h * dk:(h + 1) * dk])
        log2_a = (-(jnp.maximum(-x, 0.0) + jnp.log1p(jnp.exp(-jnp.abs(x))))
                  * (np.log2(np.e) / GLA_TAU))
        return jnp.concatenate(_split_bf16(log2_a), axis=0)

    def exponents(log2_a):
        return jnp.dot(ws_ref[...], log2_a, preferred_element_type=F32)

    def level_operands(ci, h, pb):
        q, k = qk(ci, h)
        ops = [(q * pb[l * c:(l + 1) * c], k * pb[l * c:(l + 1) * c]) for l in range(GLA_LEVELS)]
        return ops + [(q, k)]

    def level_scores(ops):
        return [lax.dot_general(a, b, nt, preferred_element_type=F32) for a, b in ops]

    def masked_sum(scores):
        attn = mask_ref[0] * scores[0]
        for l in range(1, GLA_LEVELS + 1):
            attn = attn + mask_ref[l] * scores[l]
        return attn.astype(BF16)

    def mix(ci, h, p, pb, attn):
        q, k = qk(ci, h)
        v = v_ref[rows(ci), h * dv:(h + 1) * dv]
        eb = pb[GLA_LEVELS * c:(GLA_LEVELS + 1) * c]
        ebl = pb[(GLA_LEVELS + 1) * c:(GLA_LEVELS + 2) * c]
        st = st_ref[h]
        o = (lax.dot_general(q * eb, st.astype(BF16), nt, preferred_element_type=F32)
             + jnp.dot(attn, v, preferred_element_type=F32))
        v_t = v.astype(F32).T.astype(BF16)
        decay = p[(GLA_LEVELS + 1) * c - 1:(GLA_LEVELS + 1) * c]
        st_ref[h] = st * decay + jnp.dot(v_t, k * ebl, preferred_element_type=F32)
        return o

    def emit(ci, h, o):
        y = _rms(o, go_ref[...])
        og = og_ref[rows(ci), h * dv:(h + 1) * dv].astype(F32)
        o_ref[rows(ci), h * dv:(h + 1) * dv] = (y * (og * jax.nn.sigmoid(og))).astype(o_ref.dtype)

    def chunk_stages(ci, heads):
        log2_a = {h: gate(ci, h) for h in heads}
        yield
        expo = {h: exponents(log2_a[h]) for h in heads}
        p = {h: jnp.exp2(expo[h]) for h in heads}
        pb = {h: p[h].astype(BF16) for h in heads}
        yield
        operands = {h: level_operands(ci, h, pb[h]) for h in heads}
        scores = {h: level_scores(operands[h]) for h in heads}
        yield
        attn = {h: masked_sum(scores[h]) for h in heads}
        out = {h: mix(ci, h, p[h], pb[h], attn[h]) for h in heads}
        yield
        for h in heads:
            emit(ci, h, out[h])
        yield

    live = []
    pending = [chunk_stages(ci, range(h0, h0 + GLA_HEAD_GROUP))
               for ci in range(GLA_CHUNKS_PER_STEP)
               for h0 in range(0, GLA_HEADS, GLA_HEAD_GROUP)]
    tick = 0
    while pending or live:
        if pending and tick % GLA_STAGGER == 0:
            live.append(pending.pop(0))
        for gen in list(live):
            if next(gen, "done") == "done":
                live.remove(gen)
        tick += 1


def _gla(z, wg, bg, go, riders, batch, seq):
    c = GLA_CHUNK * GLA_CHUNKS_PER_STEP
    nc = seq // c
    ws, masks = _gla_constants()
    hk, hv = GLA_HEADS * GLA_DK, GLA_HEADS * GLA_DV
    const2 = lambda b, i: (0, 0)
    rider_specs, windows, steps = _rider_specs(riders, lambda b, i: b * nc + i)
    assert steps <= batch * nc
    return pl.pallas_call(
        functools.partial(_gla_kernel, rider_windows=windows),
        grid=(batch, nc),
        in_specs=[pl.BlockSpec((c, hk), lambda b, i: (b * nc + i, Z_GQ // hk)),
                  pl.BlockSpec((c, hk), lambda b, i: (b * nc + i, Z_GK // hk)),
                  pl.BlockSpec((c, hv), lambda b, i: (b * nc + i, Z_GV // hv)),
                  pl.BlockSpec((c, LANES), lambda b, i: (b * nc + i, Z_GLR // LANES)),
                  pl.BlockSpec((c, hv), lambda b, i: (b * nc + i, Z_OG // hv)),
                  pl.BlockSpec((LANES, hk), const2),
                  pl.BlockSpec((1, hk), const2),
                  pl.BlockSpec((1, GLA_DV), const2),
                  pl.BlockSpec(ws.shape, const2),
                  pl.BlockSpec(masks.shape, lambda b, i: (0, 0, 0))] + rider_specs,
        out_specs=[pl.BlockSpec((c, hv), lambda b, i: (b * nc + i, 0))] + rider_specs,
        out_shape=([jax.ShapeDtypeStruct((batch * seq, hv), BF16)]
                   + [jax.ShapeDtypeStruct(w.shape, BF16) for w in riders]),
        scratch_shapes=[pltpu.VMEM((GLA_HEADS, GLA_DV, GLA_DK), F32)],
        compiler_params=_params(("arbitrary", "arbitrary"), 48),
        name="gla",
    )(z, z, z, z, z, wg, bg, go, jnp.asarray(ws, BF16), jnp.asarray(masks), *riders)


MIX_SUBTILE = 128


def _mix_kernel(oa_ref, ob_ref, ga_ref, gb_ref, wa_ref, wb_ref, wm_ref, x_ref, gm_ref, gpost_ref,
                gpre_ref, sc_ref, sh_ref, x1_ref, h2_ref):
    post = gm_ref[0] * gpost_ref[...]
    pre = gpre_ref[...] * (1.0 + sc_ref[0])

    def sub_tile(rows):
        ya = jnp.dot(oa_ref[rows], wa_ref[...], preferred_element_type=F32)
        yb = jnp.dot(ob_ref[rows], wb_ref[...], preferred_element_type=F32)
        yield
        merged = (jax.nn.sigmoid(ga_ref[rows].astype(F32)) * ya
                  + jax.nn.sigmoid(gb_ref[rows].astype(F32)) * yb).astype(BF16)
        yield
        y = jnp.dot(merged, wm_ref[...], preferred_element_type=F32)
        yield
        x1 = x_ref[rows] + _rms(y, post)
        x1_ref[rows] = x1
        h2_ref[rows] = (_rms(x1, pre) + sh_ref[0]).astype(h2_ref.dtype)
        yield

    pending = [sub_tile(slice(r0, r0 + MIX_SUBTILE))
               for r0 in range(0, x_ref.shape[0], MIX_SUBTILE)]
    live = []
    while pending or live:
        if pending:
            live.append(pending.pop(0))
        for gen in list(live):
            if next(gen, "done") == "done":
                live.remove(gen)


def _mix(oa, ob, z, wa, wb, wm, x2, gate_m, g_post, g_pre, scale_f, shift_f, seq):
    t = oa.shape[0]
    tm = 256
    tpb = seq // tm
    row = pl.BlockSpec((tm, D_MODEL), lambda i: (i, 0))
    vec = pl.BlockSpec((1, D_MODEL), lambda i: (0, 0))
    per_batch = pl.BlockSpec((1, 1, D_MODEL), lambda i: (i // tpb, 0, 0))
    resident = lambda w: pl.BlockSpec(w.shape, lambda i: (0, 0), pipeline_mode=pl.Buffered(1))
    return pl.pallas_call(
        _mix_kernel,
        grid=(t // tm,),
        in_specs=[pl.BlockSpec((tm, oa.shape[1]), lambda i: (i, 0)),
                  pl.BlockSpec((tm, ob.shape[1]), lambda i: (i, 0)),
                  pl.BlockSpec((tm, D_MODEL), lambda i: (i, Z_GATE_A // D_MODEL)),
                  pl.BlockSpec((tm, D_MODEL), lambda i: (i, Z_GATE_B // D_MODEL)),
                  resident(wa), resident(wb), resident(wm),
                  row, per_batch, vec, vec, per_batch, per_batch],
        out_specs=[row, row],
        out_shape=[jax.ShapeDtypeStruct((t, D_MODEL), F32),
                   jax.ShapeDtypeStruct((t, D_MODEL), BF16)],
        compiler_params=_params(("arbitrary",), 56),
        name="mix",
    )(oa, ob, z, z, wa, wb, wm, x2, gate_m, g_post, g_pre, scale_f, shift_f)


def _ffn_up_kernel(h_ref, wa_ref, wv_ref, cwa_ref, cwv_ref, cba_ref, cbv_ref, o_ref,
                   w_bf, u_sc, *, tiles_per_batch):
    i = pl.program_id(1)
    tm, tn = o_ref.shape
    head, body, tail = slice(0, SUBLANES), slice(SUBLANES, SUBLANES + tm), slice(tm, tm + SUBLANES)

    @pl.when(i == 0)
    def _():
        w_bf[:, 0:tn] = wa_ref[...].astype(BF16)
        w_bf[:, tn:2 * tn] = wv_ref[...].astype(BF16)

    @pl.when(i % tiles_per_batch == 0)
    def _():
        u_sc[head] = jnp.zeros((SUBLANES, u_sc.shape[1]), F32)

    @pl.when(i % tiles_per_batch != 0)
    def _():
        u_sc[head] = u_sc[tail]

    u_sc[body] = jnp.dot(h_ref[...], w_bf[...], preferred_element_type=F32)

    def conv(cols, cw_ref, cb_ref):
        cw = cw_ref[...]
        acc = cb_ref[...] + u_sc[body, cols] * cw[CONV_WIDTH - 1:CONV_WIDTH]
        for tap in range(1, CONV_WIDTH):
            acc = acc + (u_sc[SUBLANES - tap:SUBLANES - tap + tm, cols]
                         * cw[CONV_WIDTH - 1 - tap:CONV_WIDTH - tap])
        return acc

    a = conv(slice(0, tn), cwa_ref, cba_ref)
    val = conv(slice(tn, 2 * tn), cwv_ref, cbv_ref)
    o_ref[...] = (jax.nn.gelu(a, approximate=True) * val).astype(o_ref.dtype)


def _ffn_up(h2, w_up, conv_w, conv_b, seq):
    t = h2.shape[0]
    tm, tn = 1024, 512
    nj = D_FF // tn
    return pl.pallas_call(
        functools.partial(_ffn_up_kernel, tiles_per_batch=seq // tm),
        grid=(nj, t // tm),
        in_specs=[pl.BlockSpec((tm, D_MODEL), lambda j, i: (i, 0)),
                  pl.BlockSpec((D_MODEL, tn), lambda j, i: (0, j)),
                  pl.BlockSpec((D_MODEL, tn), lambda j, i: (0, nj + j)),
                  pl.BlockSpec((CONV_WIDTH, tn), lambda j, i: (0, j)),
                  pl.BlockSpec((CONV_WIDTH, tn), lambda j, i: (0, nj + j)),
                  pl.BlockSpec((1, tn), lambda j, i: (0, j)),
                  pl.BlockSpec((1, tn), lambda j, i: (0, nj + j))],
        out_specs=pl.BlockSpec((tm, tn), lambda j, i: (i, j)),
        out_shape=jax.ShapeDtypeStruct((t, D_FF), BF16),
        scratch_shapes=[pltpu.VMEM((D_MODEL, 2 * tn), BF16),
                        pltpu.VMEM((tm + SUBLANES, 2 * tn), F32)],
        compiler_params=_params(("arbitrary", "arbitrary"), 56),
        name="ffn_up",
    )(h2, w_up, w_up, conv_w, conv_w, conv_b, conv_b)


def _ffn_down_kernel(g_ref, w_ref, x1_ref, gf_ref, gpost_ref, o_ref):
    f = jnp.dot(g_ref[...], w_ref[...], preferred_element_type=F32)
    o_ref[...] = x1_ref[...] + _rms(f, gf_ref[0] * gpost_ref[...])


def _ffn_down(g, w, x1, gate_f, g_post, seq):
    t = g.shape[0]
    tm = 256
    tpb = seq // tm
    return pl.pallas_call(
        _ffn_down_kernel,
        grid=(t // tm,),
        in_specs=[pl.BlockSpec((tm, D_FF), lambda i: (i, 0)),
                  pl.BlockSpec((D_FF, D_MODEL), lambda i: (0, 0), pipeline_mode=pl.Buffered(1)),
                  pl.BlockSpec((tm, D_MODEL), lambda i: (i, 0)),
                  pl.BlockSpec((1, 1, D_MODEL), lambda i: (i // tpb, 0, 0)),
                  pl.BlockSpec((1, D_MODEL), lambda i: (0, 0))],
        out_specs=pl.BlockSpec((tm, D_MODEL), lambda i: (i, 0)),
        out_shape=jax.ShapeDtypeStruct((t, D_MODEL), F32),
        compiler_params=_params(("arbitrary",), 56),
        name="ffn_down",
    )(g, w, x1, gate_f, g_post)


def _swap_halves(w):
    half = w.shape[-1] // 2
    return jnp.concatenate([w[..., half:], w[..., :half]], axis=-1)


def _w_in_special(w_t):
    half = MLA_ROPE // 2
    k_rope = w_t[_SRC_KROPE:_SRC_KROPE + MLA_ROPE]
    g_lr = w_t[_SRC_GLR:_SRC_GLR + GLA_GATE_RANK]
    pad = jnp.zeros((Z_GQ - Z_GLR - 2 * GLA_GATE_RANK, w_t.shape[1]), w_t.dtype)
    return jnp.concatenate([k_rope, k_rope[half:], k_rope[:half], g_lr, g_lr, pad], axis=0)


def _prep_w_q(w_q_up):
    w = w_q_up.reshape(MLA_Q_RANK, MLA_HEADS, MLA_NOPE + MLA_ROPE)
    pe = w[:, :, MLA_NOPE:]
    w = jnp.concatenate([w, _swap_halves(pe)], axis=-1)
    return jnp.transpose(w, (1, 0, 2)).astype(BF16)


def kernel(x, c, positions, w_ada, b_ada, g_pre_mix, w_in, g_q_lat, w_q_up, g_kv_lat, w_kv_up,
           w_gla_gate_up, b_gla_gate, g_gla_out, w_branch_a, w_branch_b, w_mix_out, g_post_mix,
           g_pre_ffn, w_ffn_up, conv_w, conv_b, w_ffn_down, g_post_ffn):
    batch, seq, _ = x.shape
    depth = w_ada.shape[0]
    t = batch * seq
    row = lambda v: v.reshape(1, -1)

    inv = 1.0 / (ROPE_THETA ** (jnp.arange(0, MLA_ROPE, 2, dtype=F32) / MLA_ROPE))
    cos, sin = _rope_tables(positions, inv)

    c_pad = jnp.zeros((SUBLANES, D_MODEL), F32).at[:batch].set(c)
    x2 = x.reshape(t, D_MODEL)
    for l in range(depth):
        mod = _ada(c_pad, w_ada[l], row(b_ada[l]))[:batch]
        shift_m, scale_m, gate_m, shift_f, scale_f, gate_f = (
            m.reshape(batch, 1, D_MODEL) for m in jnp.split(mod, 6, axis=-1))

        h = _norm_mod(x2, row(g_pre_mix[l]), scale_m, shift_m, seq)
        w_in_t = w_in[l].T
        z = _mm_in(h, w_in_t, _w_in_special(w_in_t))

        q, k, v = _mla_proj(z, cos, sin, row(g_q_lat[l]), row(g_kv_lat[l]),
                            _prep_w_q(w_q_up[l]), w_kv_up[l].astype(BF16), batch, seq)
        o_a, w_a_bf, w_b_bf = _attn(q, k, v, [w_branch_a[l], w_branch_b[l]], batch, seq)

        wg_hi, wg_lo = _split_bf16(w_gla_gate_up[l])
        wg = jnp.zeros((LANES, GLA_HEADS * GLA_DK), BF16)
        wg = wg.at[:GLA_GATE_RANK].set(wg_hi).at[GLA_GATE_RANK:2 * GLA_GATE_RANK].set(wg_lo)
        o_b, w_down_bf, w_mix_bf = _gla(z, wg, row(b_gla_gate[l]), row(g_gla_out[l]),
                                        [w_ffn_down[l], w_mix_out[l]], batch, seq)

        x1, h2 = _mix(o_a, o_b, z, w_a_bf, w_b_bf, w_mix_bf, x2, gate_m, row(g_post_mix[l]),
                      row(g_pre_ffn[l]), scale_f, shift_f, seq)

        g = _ffn_up(h2, w_ffn_up[l], conv_w[l], row(conv_b[l]), seq)
        x2 = _ffn_down(g, w_down_bf, x1, gate_f, row(g_post_ffn[l]), seq)
    return x2.reshape(batch, seq, D_MODEL)
```
